```python
import math
import jax, jax.numpy as jnp
from jax import lax
import numpy as np

D_MODEL = 1024
BATCH = 8
SEQ = 2048
DEPTH = 2

D_MIX = 2 * D_MODEL
EPS = 1e-6
MLSTM_HEADS = 4
MLSTM_DV = D_MIX // 2
MLSTM_DV_HEAD = MLSTM_DV // MLSTM_HEADS
MLSTM_DK_HEAD = MLSTM_DV_HEAD // 2
MLSTM_DK = MLSTM_HEADS * MLSTM_DK_HEAD
MLSTM_CHUNK = 64
CONV_WIDTH = 4
S5_WIDTH = D_MIX // 2
S5_GROUP = 16
S5_GROUPS = S5_WIDTH // S5_GROUP
S5_STATE = 64
S5_DT_MIN = 1e-3
S5_DT_MAX = 1e-1
GLA_HEADS = 4
GLA_DV = D_MIX
GLA_DK = D_MIX // 2
GLA_DV_HEAD = GLA_DV // GLA_HEADS
GLA_DK_HEAD = GLA_DK // GLA_HEADS
GLA_GATE_RANK = 16
GLA_TAU = 16.0
GLA_CHUNK = 64
EVEN_SIZES = (MLSTM_DK, MLSTM_DK, MLSTM_DV, MLSTM_DV, MLSTM_HEADS, MLSTM_HEADS, S5_WIDTH, D_MIX)
EVEN_IN = sum(EVEN_SIZES)
ODD_SIZES = (GLA_DK, GLA_DK, GLA_DV, D_MIX, GLA_GATE_RANK)
ODD_IN = sum(ODD_SIZES)
N_EVEN = (DEPTH + 1) // 2
N_ODD = DEPTH // 2

kernel_name = "hybrid_mlstm_s5_gla_trunk"


def _split(t, sizes):
    idx = [int(v) for v in np.cumsum(sizes)[:-1]]
    return jnp.split(t, idx, axis=-1)


def rmsnorm(x, g):
    xf = x.astype(jnp.float32)
    y = xf * lax.rsqrt(jnp.mean(xf * xf, axis=-1, keepdims=True) + EPS)
    return (y * g.astype(jnp.float32)).astype(x.dtype)


def headwise_rmsnorm(h, g, n_heads):
    b, s, w = h.shape
    hf = h.astype(jnp.float32).reshape(b, s, n_heads, w // n_heads)
    hf = hf * lax.rsqrt(jnp.mean(hf * hf, axis=-1, keepdims=True) + EPS)
    return hf.reshape(b, s, w) * g.astype(jnp.float32)


def causal_dwconv(x, w, bias):
    c = x.shape[-1]
    y = lax.conv_general_dilated(
        x.astype(jnp.float32), w.astype(jnp.float32)[:, None, :],
        window_strides=(1,), padding=[(CONV_WIDTH - 1, 0)],
        dimension_numbers=('NWC', 'WIO', 'NWC'), feature_group_count=c)
    return y + bias.astype(jnp.float32)


def mlstm_chunkwise(q, k, v, i_pre, f_pre):
    b_, s_, h_, dk = q.shape
    dv = v.shape[-1]
    L = MLSTM_CHUNK
    nc = s_ // L
    q = q * (dk ** -0.5)
    logf = jax.nn.log_sigmoid(f_pre)
    qc = q.reshape(b_, nc, L, h_, dk)
    kc = k.reshape(b_, nc, L, h_, dk)
    vc = v.reshape(b_, nc, L, h_, dv)
    ic = i_pre.reshape(b_, nc, L, h_)
    bcum = jnp.cumsum(logf.reshape(b_, nc, L, h_), axis=2)
    g = bcum[:, :, -1]
    causal = jnp.tril(jnp.ones((L, L), dtype=bool))
    dmat = bcum[:, :, :, None, :] - bcum[:, :, None, :, :] + ic[:, :, None, :, :]
    dmat = jnp.where(causal[None, None, :, :, None], dmat, -jnp.inf)
    a = g[:, :, None, :] - bcum + ic
    m_loc = jnp.max(a, axis=2)
    w_loc = jnp.exp(a - m_loc[:, :, None, :])
    c_loc = jnp.einsum('bclh,bclhk,bclhv->bchkv', w_loc, kc, vc)
    n_loc = jnp.einsum('bclh,bclhk->bchk', w_loc, kc)

    def step(carry, inp):
        c_st, n_st, m_st = carry
        g_c, m_l, c_l, n_l = inp
        m_new = jnp.maximum(g_c + m_st, m_l)
        s_prev = jnp.exp(g_c + m_st - m_new)
        s_loc = jnp.exp(m_l - m_new)
        c_new = s_prev[..., None, None] * c_st + s_loc[..., None, None] * c_l
        n_new = s_prev[..., None] * n_st + s_loc[..., None] * n_l
        return (c_new, n_new, m_new), (c_st, n_st, m_st)

    init = (jnp.zeros((b_, h_, dk, dv), jnp.float32),
            jnp.zeros((b_, h_, dk), jnp.float32),
            jnp.zeros((b_, h_), jnp.float32))
    xs = (jnp.moveaxis(g, 1, 0), jnp.moveaxis(m_loc, 1, 0),
          jnp.moveaxis(c_loc, 1, 0), jnp.moveaxis(n_loc, 1, 0))
    _, (c_prev, n_prev, m_prev) = lax.scan(step, init, xs)
    c_prev = jnp.moveaxis(c_prev, 0, 1)
    n_prev = jnp.moveaxis(n_prev, 0, 1)
    m_prev = jnp.moveaxis(m_prev, 0, 1)
    inter_log = bcum + m_prev[:, :, None, :]
    m_t = jnp.maximum(inter_log, jnp.max(dmat, axis=3))
    w_intra = jnp.exp(dmat - m_t[:, :, :, None, :])
    w_inter = jnp.exp(inter_log - m_t)
    sc = w_intra * jnp.einsum('bcthk,bcshk->bctsh', qc, kc)
    num = (jnp.einsum('bctsh,bcshv->bcthv', sc, vc)
           + w_inter[..., None] * jnp.einsum('bcthk,bchkv->bcthv', qc, c_prev))
    den = jnp.sum(sc, axis=3) + w_inter * jnp.einsum('bcthk,bchk->bcth', qc, n_prev)
    h = num / jnp.maximum(jnp.abs(den), jnp.exp(-m_t))[..., None]
    return h.reshape(b_, s_, h_, dv)


def s5_branch(u, lam_re, lam_im, log_dt, b_re, b_im, c_re, c_im, d_skip, glu_w, glu_b):
    f32 = jnp.float32
    b_, s_, _ = u.shape
    uf = u.astype(f32)
    lam = lax.complex(lam_re.astype(f32), lam_im.astype(f32))
    dt = jnp.exp(log_dt.astype(f32))[:, None]
    a_bar = jnp.exp(lam * dt)
    b_bar = lax.complex(b_re.astype(f32), b_im.astype(f32)) * ((a_bar - 1.0) / lam)[..., None]
    ug = uf.reshape(b_, s_, S5_GROUPS, S5_GROUP)
    bu = jnp.einsum('bsgc,gpc->bsgp', ug, b_bar)
    a_seq = jnp.broadcast_to(a_bar, (1, s_, S5_GROUPS, S5_STATE))

    def combine(e1, e2):
        a1, x1 = e1
        a2, x2 = e2
        return a2 * a1, a2 * x1 + x2

    _, states = lax.associative_scan(combine, (a_seq, bu), axis=1)
    y = (jnp.einsum('bsgp,gcp->bsgc', jnp.real(states), c_re.astype(f32))
         - jnp.einsum('bsgp,gcp->bsgc', jnp.imag(states), c_im.astype(f32)))
    y = y.reshape(b_, s_, S5_WIDTH) + d_skip.astype(f32) * uf
    y = jax.nn.gelu(y)
    return y * jax.nn.sigmoid(y @ glu_w.astype(f32) + glu_b.astype(f32))


def gla_chunkwise(q, k, v, log_alpha):
    b_, s_, h_, dk = q.shape
    dv = v.shape[-1]
    L = GLA_CHUNK
    nc = s_ // L
    q = q * (dk ** -0.5)
    qc = q.reshape(b_, nc, L, h_, dk)
    kc = k.reshape(b_, nc, L, h_, dk)
    vc = v.reshape(b_, nc, L, h_, dv)
    bcum = jnp.cumsum(log_alpha.reshape(b_, nc, L, h_, dk), axis=2)
    g = bcum[:, :, -1]
    q_t = qc * jnp.exp(bcum)
    k_t = kc * jnp.exp(-bcum)
    causal = jnp.tril(jnp.ones((L, L), dtype=bool))
    attn = jnp.einsum('bcthk,bcshk->bchts', q_t, k_t)
    attn = jnp.where(causal[None, None, None], attn, 0.0)
    o_intra = jnp.einsum('bchts,bcshv->bcthv', attn, vc)
    k_end = kc * jnp.exp(g[:, :, None] - bcum)
    upd = jnp.einsum('bcshk,bcshv->bchkv', k_end, vc)

    def step(st, inp):
        g_c, u_c = inp
        return jnp.exp(g_c)[..., None] * st + u_c, st

    _, s_prev = lax.scan(step, jnp.zeros((b_, h_, dk, dv), jnp.float32),
                         (jnp.moveaxis(g, 1, 0), jnp.moveaxis(upd, 1, 0)))
    s_prev = jnp.moveaxis(s_prev, 0, 1)
    o_inter = jnp.einsum('bcthk,bchkv->bcthv', q_t, s_prev)
    return (o_intra + o_inter).reshape(b_, s_, h_, dv)


def even_layer(h, w_in, conv_w, conv_b, i_bias, f_bias, head_g,
               lam_re, lam_im, log_dt, b_re, b_im, c_re, c_im, d_skip, glu_w, glu_b, w_out):
    f32 = jnp.float32
    b_, s_, _ = h.shape
    proj = h @ w_in
    q, k, v, o, i_pre, f_pre, u, z = _split(proj, EVEN_SIZES)
    qk = jax.nn.silu(causal_dwconv(jnp.concatenate([q, k], axis=-1), conv_w, conv_b))
    q, k = qk[..., :MLSTM_DK], qk[..., MLSTM_DK:]
    hd = lambda t, dh: t.astype(f32).reshape(b_, s_, MLSTM_HEADS, dh)
    h_a = mlstm_chunkwise(hd(q, MLSTM_DK_HEAD), hd(k, MLSTM_DK_HEAD), hd(v, MLSTM_DV_HEAD),
                          i_pre.astype(f32) + i_bias.astype(f32),
                          f_pre.astype(f32) + f_bias.astype(f32))
    h_a = jax.nn.sigmoid(o.astype(f32)) * h_a.reshape(b_, s_, MLSTM_DV)
    h_a = headwise_rmsnorm(h_a, head_g, MLSTM_HEADS)
    h_b = s5_branch(u, lam_re, lam_im, log_dt, b_re, b_im, c_re, c_im, d_skip, glu_w, glu_b)
    y = jnp.concatenate([h_a, h_b], axis=-1) * jax.nn.silu(z.astype(f32))
    return y.astype(h.dtype) @ w_out


def odd_layer(h, w_in, w_alpha, b_alpha, head_g, w_out):
    f32 = jnp.float32
    b_, s_, _ = h.shape
    proj = h @ w_in
    q, k, v, z, r = _split(proj, ODD_SIZES)
    log_alpha = jax.nn.log_sigmoid(r.astype(f32) @ w_alpha.astype(f32)
                                   + b_alpha.astype(f32)) / GLA_TAU
    hd = lambda t, dh: t.astype(f32).reshape(b_, s_, GLA_HEADS, dh)
    o_c = gla_chunkwise(hd(q, GLA_DK_HEAD), hd(k, GLA_DK_HEAD), hd(v, GLA_DV_HEAD),
                        hd(log_alpha, GLA_DK_HEAD))
    o_c = headwise_rmsnorm(o_c.reshape(b_, s_, GLA_DV), head_g, GLA_HEADS)
    y = o_c * jax.nn.silu(z.astype(f32))
    return y.astype(h.dtype) @ w_out


def setup_inputs(seed: int = 0) -> dict:
    key = jax.random.key(seed)
    ks = jax.random.split(key, 26)
    f32 = jnp.float32

    def nrm(k, shape, scale):
        return scale * jax.random.normal(k, shape, f32)

    ne, no = N_EVEN, N_ODD
    gp = (ne, S5_GROUPS, S5_STATE)
    n_idx = jnp.arange(S5_STATE, dtype=f32)
    return {
        "x": nrm(ks[0], (BATCH, SEQ, D_MODEL), 1.0),
        "norm_g": 1.0 + nrm(ks[1], (DEPTH, D_MODEL), 0.01),
        "final_norm_g": 1.0 + nrm(ks[2], (D_MODEL,), 0.01),
        "ev_w_in": nrm(ks[3], (ne, D_MODEL, EVEN_IN), D_MODEL ** -0.5),
        "ev_conv_w": nrm(ks[4], (ne, CONV_WIDTH, 2 * MLSTM_DK), CONV_WIDTH ** -0.5),
        "ev_conv_b": nrm(ks[5], (ne, 2 * MLSTM_DK), 0.01),
        "ev_i_bias": nrm(ks[6], (ne, MLSTM_HEADS), 0.1),
        "ev_f_bias": jnp.linspace(3.0, 6.0, MLSTM_HEADS, dtype=f32) + nrm(ks[7], (ne, MLSTM_HEADS), 0.1),
        "ev_head_g": 1.0 + nrm(ks[8], (ne, MLSTM_DV), 0.01),
        "s5_lam_re": -0.5 + nrm(ks[9], gp, 0.01),
        "s5_lam_im": jnp.pi * n_idx + nrm(ks[10], gp, 0.01),
        "s5_log_dt": jax.random.uniform(ks[11], (ne, S5_GROUPS), f32,
                                        math.log(S5_DT_MIN), math.log(S5_DT_MAX)),
        "s5_b_re": nrm(ks[12], (ne, S5_GROUPS, S5_STATE, S5_GROUP), (2 * S5_GROUP) ** -0.5),
        "s5_b_im": nrm(ks[13], (ne, S5_GROUPS, S5_STATE, S5_GROUP), (2 * S5_GROUP) ** -0.5),
        "s5_c_re": nrm(ks[14], (ne, S5_GROUPS, S5_GROUP, S5_STATE), (2 * S5_STATE) ** -0.5),
        "s5_c_im": nrm(ks[15], (ne, S5_GROUPS, S5_GROUP, S5_STATE), (2 * S5_STATE) ** -0.5),
        "s5_d": nrm(ks[16], (ne, S5_WIDTH), 0.5),
        "s5_glu_w": nrm(ks[17], (ne, S5_WIDTH, S5_WIDTH), S5_WIDTH ** -0.5),
        "s5_glu_b": nrm(ks[18], (ne, S5_WIDTH), 0.01),
        "ev_w_out": nrm(ks[19], (ne, D_MIX, D_MODEL), D_MIX ** -0.5),
        "od_w_in": nrm(ks[20], (no, D_MODEL, ODD_IN), D_MODEL ** -0.5),
        "gla_w_alpha": nrm(ks[21], (no, GLA_GATE_RANK, GLA_DK), GLA_GATE_RANK ** -0.5),
        "gla_b_alpha": nrm(ks[22], (no, GLA_DK), 0.1),
        "gla_head_g": 1.0 + nrm(ks[23], (no, GLA_DV), 0.01),
        "od_w_out": nrm(ks[24], (no, D_MIX, D_MODEL), D_MIX ** -0.5),
    }


def reference(x, norm_g, final_norm_g, ev_w_in, ev_conv_w, ev_conv_b, ev_i_bias, ev_f_bias,
              ev_head_g, s5_lam_re, s5_lam_im, s5_log_dt, s5_b_re, s5_b_im, s5_c_re, s5_c_im,
              s5_d, s5_glu_w, s5_glu_b, ev_w_out, od_w_in, gla_w_alpha, gla_b_alpha,
              gla_head_g, od_w_out):
    for layer in range(DEPTH):
        hn = rmsnorm(x, norm_g[layer])
        j = layer // 2
        if layer % 2 == 0:
            y = even_layer(hn, ev_w_in[j], ev_conv_w[j], ev_conv_b[j], ev_i_bias[j], ev_f_bias[j],
                           ev_head_g[j], s5_lam_re[j], s5_lam_im[j], s5_log_dt[j], s5_b_re[j],
                           s5_b_im[j], s5_c_re[j], s5_c_im[j], s5_d[j], s5_glu_w[j], s5_glu_b[j],
                           ev_w_out[j])
        else:
            y = odd_layer(hn, od_w_in[j], gla_w_alpha[j], gla_b_alpha[j], gla_head_g[j], od_w_out[j])
        x = x + y.astype(x.dtype)
    return rmsnorm(x, final_norm_g)
```

```python
import functools

import jax
import jax.numpy as jnp
from jax import lax
from jax.experimental import pallas as pl
from jax.experimental.pallas import tpu as pltpu

F32 = jnp.float32
BF16 = jnp.bfloat16

EPS = 1e-6
D_MODEL = 1024
D_MIX = 2 * D_MODEL
M_HEADS = 4
M_DK = 128
M_DV = 256
M_QK = 2 * M_HEADS * M_DK
CONV_WIDTH = 4
M_CHUNK = 256
S5_GROUPS = 64
S5_GROUP = 16
S5_STATE = 64
S5_BLK = 16
S5_ROWW = S5_BLK * S5_GROUP
S5_GB = 2
G_HEADS = 4
G_DK = 256
G_DV = 512
G_TAU = 16.0
G_CHUNK = 128
G_RANK_PAD = 128

LANES = 128
SUBLANES = 8
TOKEN_TILE = 512
PROJ_TN = 256
VMEM_LIMIT = 56 * 1024 * 1024


def _cparams(n_grid):
    return pltpu.CompilerParams(
        dimension_semantics=("arbitrary",) * n_grid, vmem_limit_bytes=VMEM_LIMIT)


def _log_sigmoid(x):
    return jnp.minimum(x, 0.0) - jnp.log1p(jnp.exp(-jnp.abs(x)))


def _silu(x):
    return x * jax.nn.sigmoid(x)


def _split_hi_lo(x):
    hi = x.astype(BF16)
    lo = (x - hi.astype(F32)).astype(BF16)
    return hi, lo


def _dot(a, b):
    return jnp.dot(a, b, preferred_element_type=F32)


def _dot_nt(a, b):
    return lax.dot_general(a, b, (((1,), (1,)), ((), ())), preferred_element_type=F32)


def _tri_ones(n, lower):
    row = lax.broadcasted_iota(jnp.int32, (n, n), 0)
    col = lax.broadcasted_iota(jnp.int32, (n, n), 1)
    keep = (row >= col) if lower else (row <= col)
    return jnp.where(keep, 1.0, 0.0).astype(BF16)


def _norm_proj_body(x_ref, g_ref, w_ref, wg_ref, *out_refs):
    x = x_ref[...]
    hn = x * lax.rsqrt(jnp.mean(x * x, axis=-1, keepdims=True) + EPS) * g_ref[...]
    hb = hn.astype(BF16)
    off = 0
    for o_ref in out_refs[:-1]:
        n = o_ref.shape[1]
        for j in range(0, n, PROJ_TN):
            o_ref[:, j:j + PROJ_TN] = _dot(hb, w_ref[:, off + j:off + j + PROJ_TN]).astype(o_ref.dtype)
        off += n
    out_refs[-1][...] = _dot(hb, wg_ref[...])


def _norm_proj(xf, g, wm, wg, splits):
    t, d = xf.shape
    n_main = wm.shape[1]
    assert sum(splits) == n_main and t % TOKEN_TILE == 0
    out_shape = [jax.ShapeDtypeStruct((t, n), BF16) for n in splits]
    out_shape.append(jax.ShapeDtypeStruct((t, LANES), F32))
    out_specs = [pl.BlockSpec((TOKEN_TILE, n), lambda i: (i, 0)) for n in splits]
    out_specs.append(pl.BlockSpec((TOKEN_TILE, LANES), lambda i: (i, 0)))
    return pl.pallas_call(
        _norm_proj_body,
        grid=(t // TOKEN_TILE,),
        in_specs=[
            pl.BlockSpec((TOKEN_TILE, d), lambda i: (i, 0)),
            pl.BlockSpec((1, d), lambda i: (0, 0)),
            pl.BlockSpec((d, n_main), lambda i: (0, 0), pipeline_mode=pl.Buffered(1)),
            pl.BlockSpec((d, LANES), lambda i: (0, 0)),
        ],
        out_specs=out_specs,
        out_shape=out_shape,
        compiler_params=_cparams(1),
        name="norm_proj",
    )(xf, g.reshape(1, d), wm, wg)


def _mlstm_body(qk_ref, v_ref, o_ref, z_ref, gates_ref, convw_ref, convb_ref, gbias_ref,
                headg_ref, out_ref, qkext_ref, c_ref, n_ref, m_ref):
    L = M_CHUNK

    @pl.when(pl.program_id(1) == 0)
    def _init():
        qkext_ref[0:SUBLANES, :] = jnp.zeros((SUBLANES, M_QK), F32)
        c_ref[...] = jnp.zeros_like(c_ref)
        n_ref[...] = jnp.zeros_like(n_ref)
        m_ref[...] = jnp.zeros_like(m_ref)

    qkext_ref[SUBLANES:SUBLANES + L, :] = qk_ref[...].astype(F32)
    first = SUBLANES - (CONV_WIDTH - 1)
    acc = convb_ref[...] + convw_ref[0:1, :] * qkext_ref[first:first + L, :]
    for j in range(1, CONV_WIDTH):
        acc = acc + convw_ref[j:j + 1, :] * qkext_ref[first + j:first + j + L, :]
    qkc = _silu(acc)
    qkext_ref[0:SUBLANES, :] = qkext_ref[L:L + SUBLANES, :]

    gt = gates_ref[...] + gbias_ref[...]
    lane = lax.broadcasted_iota(jnp.int32, gt.shape, 1)
    gl = jnp.where(lane >= M_HEADS, _log_sigmoid(gt), gt)
    gl_t = gl.T
    hi, lo = _split_hi_lo(gl)
    tril = _tri_ones(L, True)
    bcol = _dot(tril, hi) + _dot(tril, lo)
    hi_t, lo_t = _split_hi_lo(gl_t[0:2 * SUBLANES, :])
    triu = _tri_ones(L, False)
    brow = _dot(hi_t, triu) + _dot(lo_t, triu)

    row = lax.broadcasted_iota(jnp.int32, (L, L), 0)
    col = lax.broadcasted_iota(jnp.int32, (L, L), 1)
    causal = row >= col

    for h in range(M_HEADS):
        ks = slice(h * M_DK, (h + 1) * M_DK)
        ks2 = slice(M_HEADS * M_DK + h * M_DK, M_HEADS * M_DK + (h + 1) * M_DK)
        vs = slice(h * M_DV, (h + 1) * M_DV)
        fl = M_HEADS + h
        b_t = bcol[:, fl:fl + 1]
        i_t = gl[:, h:h + 1]
        b_s = brow[fl:fl + 1, :]
        i_s = gl_t[h:h + 1, :]
        g = bcol[L - 1:L, fl:fl + 1]
        m_prev = m_ref[h, 0:1, 0:1]
        c_prev = c_ref[h]
        n_prev = n_ref[h, 0:1, :]

        dmat = jnp.where(causal, b_t - b_s + i_s, -jnp.inf)
        inter = b_t + m_prev
        m_t = jnp.maximum(inter, jnp.max(dmat, axis=1, keepdims=True))
        w_intra = jnp.exp(dmat - m_t)
        w_inter = jnp.exp(inter - m_t)

        q_h = qkc[:, ks] * (M_DK ** -0.5)
        k_h = qkc[:, ks2]
        qb = q_h.astype(BF16)
        kb = k_h.astype(BF16)
        v_h = v_ref[:, vs]
        sc = w_intra * _dot_nt(qb, kb)
        num = _dot(sc.astype(BF16), v_h) + w_inter * _dot(qb, c_prev.astype(BF16))
        den = (jnp.sum(sc, axis=1, keepdims=True)
               + w_inter * jnp.sum(q_h * n_prev, axis=1, keepdims=True))
        hh = num / jnp.maximum(jnp.abs(den), jnp.exp(-m_t))

        a = g - b_t + i_t
        m_loc = jnp.max(a, axis=0, keepdims=True)
        w_loc = jnp.exp(a - m_loc)
        kw = k_h * w_loc
        c_loc = _dot(kw.T.astype(BF16), v_h)
        n_loc = jnp.sum(kw, axis=0, keepdims=True)
        m_new = jnp.maximum(g + m_prev, m_loc)
        s_prev = jnp.exp(g + m_prev - m_new)
        s_loc = jnp.exp(m_loc - m_new)
        c_ref[h] = s_prev * c_prev + s_loc * c_loc
        n_ref[h] = jnp.broadcast_to(s_prev * n_prev + s_loc * n_loc, (SUBLANES, M_DK))
        m_ref[h] = jnp.broadcast_to(m_new, (SUBLANES, LANES))

        og = jax.nn.sigmoid(o_ref[:, vs].astype(F32)) * hh
        hn = og * lax.rsqrt(jnp.mean(og * og, axis=-1, keepdims=True) + EPS) * headg_ref[:, vs]
        out_ref[:, vs] = (hn * _silu(z_ref[:, vs].astype(F32))).astype(BF16)


def _mlstm(qk, v, o, z, gates, conv_w, conv_b, gbias, head_g, batch, seq):
    t = batch * seq
    L = M_CHUNK
    nc = seq // L
    dv = M_HEADS * M_DV
    tok = lambda b, c: (b * nc + c, 0)
    const = lambda b, c: (0, 0)
    return pl.pallas_call(
        _mlstm_body,
        grid=(batch, nc),
        in_specs=[
            pl.BlockSpec((L, M_QK), tok),
            pl.BlockSpec((L, dv), tok),
            pl.BlockSpec((L, dv), tok),
            pl.BlockSpec((L, dv), tok),
            pl.BlockSpec((L, LANES), tok),
            pl.BlockSpec((CONV_WIDTH, M_QK), const),
            pl.BlockSpec((1, M_QK), const),
            pl.BlockSpec((1, LANES), const),
            pl.BlockSpec((1, dv), const),
        ],
        out_specs=pl.BlockSpec((L, dv), tok),
        out_shape=jax.ShapeDtypeStruct((t, dv), BF16),
        scratch_shapes=[
            pltpu.VMEM((L + SUBLANES, M_QK), F32),
            pltpu.VMEM((M_HEADS, M_DK, M_DV), F32),
            pltpu.VMEM((M_HEADS, SUBLANES, M_DK), F32),
            pltpu.VMEM((M_HEADS, SUBLANES, LANES), F32),
        ],
        compiler_params=_cparams(2),
        name="mlstm",
    )(qk, v, o, z, gates, conv_w, conv_b.reshape(1, M_QK), gbias, head_g.reshape(1, dv))


def _s5_param_body(lamr_ref, lami_ref, ldt_ref, btr_ref, bti_ref, cr_ref, ci_ref, d_ref,
                   mt_ref, pc_ref, qt_ref, ab_ref):
    lr = lamr_ref[0]
    li = lami_ref[0]
    dt = jnp.exp(ldt_ref[0])
    valid = lax.broadcasted_iota(jnp.int32, lr.shape, 1) < S5_STATE
    zr = lr * dt
    th = li * dt
    er = jnp.exp(zr)
    ar = er * jnp.cos(th)
    ai = er * jnp.sin(th)
    den = jnp.where(valid, lr * lr + li * li, 1.0)
    beta_r = jnp.where(valid, ((ar - 1.0) * lr + ai * li) / den, 0.0)
    beta_i = jnp.where(valid, (ai * lr - (ar - 1.0) * li) / den, 0.0)
    btr = btr_ref[0]
    bti = bti_ref[0]
    bbr = (btr * beta_r - bti * beta_i)[None]
    bbi = (btr * beta_i + bti * beta_r)[None]

    jj = lax.broadcasted_iota(jnp.int32, (S5_BLK, 1, LANES), 0).astype(F32)

    def powers(j):
        e = jnp.exp(j * zr[None])
        return e * jnp.cos(j * th[None]), e * jnp.sin(j * th[None])

    p0r, p0i = powers(jj)
    p1r, p1i = powers(jj + 1.0)
    prr, pri = powers((S5_BLK - 1.0) - jj)
    cr = cr_ref[0][None]
    ci = ci_ref[0][None]
    n = S5_ROWW
    ca0r = (cr * p0r - ci * p0i).reshape(n, LANES)
    ca0i = (cr * p0i + ci * p0r).reshape(n, LANES)
    ca1r = (cr * p1r - ci * p1i).reshape(n, LANES)
    ca1i = (cr * p1i + ci * p1r).reshape(n, LANES)
    qt_ref[0, :, 0:LANES] = ca1r.astype(BF16)
    qt_ref[0, :, LANES:2 * LANES] = (-ca1i).astype(BF16)
    pc_ref[0, :, 0:LANES] = (prr * bbr - pri * bbi).reshape(n, LANES).astype(BF16)
    pc_ref[0, :, LANES:2 * LANES] = (prr * bbi + pri * bbr).reshape(n, LANES).astype(BF16)

    shape3 = (S5_BLK, S5_GROUP, LANES)
    btile_r = jnp.broadcast_to(bbr, shape3).reshape(n, LANES)
    btile_i = jnp.broadcast_to(bbi, shape3).reshape(n, LANES)
    hp = lax.Precision.HIGHEST
    kt = (lax.dot_general(ca0r, btile_r, (((1,), (1,)), ((), ())), precision=hp,
                          preferred_element_type=F32)
          - lax.dot_general(ca0i, btile_i, (((1,), (1,)), ((), ())), precision=hp,
                            preferred_element_type=F32))
    row = lax.broadcasted_iota(jnp.int32, (n, n), 0)
    col = lax.broadcasted_iota(jnp.int32, (n, n), 1)
    colblk = lax.shift_right_logical(col, 4)
    mt = jnp.where(colblk == 0, kt, 0.0)
    for s in range(1, S5_BLK):
        sh = jnp.concatenate([jnp.zeros((S5_GROUP * s, n), F32), kt[:n - S5_GROUP * s, :]], axis=0)
        mt = jnp.where(colblk == s, sh, mt)
    mt = mt + jnp.where(row == col, d_ref[0], 0.0)
    mt_ref[0] = mt.astype(BF16)

    e16 = jnp.exp(float(S5_BLK) * zr)
    ab_ref[0, :, 0:LANES] = jnp.broadcast_to(e16 * jnp.cos(float(S5_BLK) * th), (SUBLANES, LANES))
    ab_ref[0, :, LANES:2 * LANES] = jnp.broadcast_to(e16 * jnp.sin(float(S5_BLK) * th), (SUBLANES, LANES))


def _s5_params(lam_re, lam_im, log_dt, b_re, b_im, c_re, c_im, d_skip):
    g = S5_GROUPS
    padl = lambda a: jnp.pad(a, [(0, 0)] * (a.ndim - 1) + [(0, LANES - a.shape[-1])])
    lamr = padl(lam_re).reshape(g, 1, LANES)
    lami = padl(lam_im).reshape(g, 1, LANES)
    ldt = jnp.broadcast_to(log_dt.reshape(g, 1, 1), (g, 1, LANES))
    btr = padl(jnp.swapaxes(b_re, 1, 2))
    bti = padl(jnp.swapaxes(b_im, 1, 2))
    cr = padl(c_re)
    ci = padl(c_im)
    dt = jnp.tile(d_skip.reshape(g, 1, S5_GROUP), (1, 1, S5_BLK))
    n = S5_ROWW
    vec = lambda r: pl.BlockSpec((1, r, LANES), lambda i: (i, 0, 0))
    mat = pl.BlockSpec((1, n, n), lambda i: (i, 0, 0))
    return pl.pallas_call(
        _s5_param_body,
        grid=(g,),
        in_specs=[vec(1), vec(1), vec(1), vec(S5_GROUP), vec(S5_GROUP), vec(S5_GROUP), vec(S5_GROUP),
                  pl.BlockSpec((1, 1, n), lambda i: (i, 0, 0))],
        out_specs=[mat, mat, mat, pl.BlockSpec((1, SUBLANES, n), lambda i: (i, 0, 0))],
        out_shape=[jax.ShapeDtypeStruct((g, n, n), BF16)] * 3
        + [jax.ShapeDtypeStruct((g, SUBLANES, n), F32)],
        compiler_params=_cparams(1),
        name="s5_params",
    )(lamr, lami, ldt, btr, bti, cr, ci, dt)


def _gelu_tanh(x):
    return 0.5 * x * (1.0 + jnp.tanh(0.7978845608028654 * (x + 0.044715 * (x * x * x))))


def _s5_core_body(u_ref, mt_ref, pc_ref, qt_ref, ab_ref, y_ref, x_ref, xp_ref, *, n_blocks, batch):
    for gi in range(S5_GB):
        x_ref[gi] = _dot(u_ref[gi], pc_ref[gi])
    ar = [ab_ref[gi, :, 0:LANES] for gi in range(S5_GB)]
    ai = [ab_ref[gi, :, LANES:2 * LANES] for gi in range(S5_GB)]
    xr = [jnp.zeros((batch, LANES), F32) for _ in range(S5_GB)]
    xi = [jnp.zeros((batch, LANES), F32) for _ in range(S5_GB)]
    for blk in range(n_blocks):
        rows = slice(blk * batch, (blk + 1) * batch)
        for gi in range(S5_GB):
            xp_ref[gi, rows, 0:LANES] = xr[gi]
            xp_ref[gi, rows, LANES:2 * LANES] = xi[gi]
            nr = ar[gi] * xr[gi] - ai[gi] * xi[gi] + x_ref[gi, rows, 0:LANES]
            ni = ar[gi] * xi[gi] + ai[gi] * xr[gi] + x_ref[gi, rows, LANES:2 * LANES]
            xr[gi], xi[gi] = nr, ni
    for gi in range(S5_GB):
        y = _dot_nt(u_ref[gi], mt_ref[gi]) + _dot_nt(xp_ref[gi].astype(BF16), qt_ref[gi])
        y_ref[gi] = _gelu_tanh(y).astype(BF16)


def _s5_core(ug, mt, pc, qt, ab, batch):
    g, rows, n = ug.shape
    assert batch == SUBLANES and g % S5_GB == 0
    blk3 = lambda r: pl.BlockSpec((S5_GB, r, n), lambda i: (i, 0, 0))
    return pl.pallas_call(
        functools.partial(_s5_core_body, n_blocks=rows // batch, batch=batch),
        grid=(g // S5_GB,),
        in_specs=[blk3(rows), blk3(n), blk3(n), blk3(n), blk3(SUBLANES)],
        out_specs=blk3(rows),
        out_shape=jax.ShapeDtypeStruct((g, rows, n), BF16),
        scratch_shapes=[pltpu.VMEM((S5_GB, rows, n), F32), pltpu.VMEM((S5_GB, rows, n), F32)],
        compiler_params=_cparams(1),
        name="s5_core",
    )(ug, mt, pc, qt, ab)


def _even_out_body(yb_ref, ya_ref, zb_ref, x_ref, gluw_ref, glub_ref, wout_ref, out_ref):
    yg = yb_ref[...]
    s = _dot(yg, gluw_ref[...]) + glub_ref[...]
    hb = yg.astype(F32) * jax.nn.sigmoid(s)
    yb = (hb * _silu(zb_ref[...].astype(F32))).astype(BF16)
    half = ya_ref.shape[1]
    out_ref[...] = (x_ref[...] + _dot(ya_ref[...], wout_ref[0:half, :])
                    + _dot(yb, wout_ref[half:2 * half, :]))


def _even_out(yb, ya, z, xf, glu_w, glu_b, w_out):
    t, d = xf.shape
    half = ya.shape[1]
    tok = lambda i: (i, 0)
    const = lambda i: (0, 0)
    return pl.pallas_call(
        _even_out_body,
        grid=(t // TOKEN_TILE,),
        in_specs=[
            pl.BlockSpec((TOKEN_TILE, half), tok),
            pl.BlockSpec((TOKEN_TILE, half), tok),
            pl.BlockSpec((TOKEN_TILE, half), lambda i: (i, 1)),
            pl.BlockSpec((TOKEN_TILE, d), tok),
            pl.BlockSpec((half, half), const),
            pl.BlockSpec((1, half), const),
            pl.BlockSpec((2 * half, d), const),
        ],
        out_specs=pl.BlockSpec((TOKEN_TILE, d), tok),
        out_shape=jax.ShapeDtypeStruct((t, d), F32),
        compiler_params=_cparams(1),
        name="even_out",
    )(yb, ya, z, xf, glu_w, glu_b.reshape(1, half), w_out)


def _gla_body(q_ref, k_ref, v_ref, z_ref, r_ref, wa_ref, ba_ref, hg_ref, out_ref, s_ref):
    L = G_CHUNK

    @pl.when(pl.program_id(1) == 0)
    def _init():
        s_ref[...] = jnp.zeros_like(s_ref)

    rh, rl = _split_hi_lo(r_ref[...])
    wh, wl = _split_hi_lo(wa_ref[...])
    pre = _dot(rh, wh) + _dot(rh, wl) + _dot(rl, wh) + ba_ref[...]
    la = _log_sigmoid(pre) * (1.0 / G_TAU)
    lh, ll = _split_hi_lo(la)
    tril = _tri_ones(L, True)
    bc = _dot(tril, lh) + _dot(tril, ll)

    row = lax.broadcasted_iota(jnp.int32, (L, L), 0)
    col = lax.broadcasted_iota(jnp.int32, (L, L), 1)
    causal = row >= col
    mid = L // 2 - 1

    for h in range(G_HEADS):
        ks = slice(h * G_DK, (h + 1) * G_DK)
        vs = slice(h * G_DV, (h + 1) * G_DV)
        b = bc[:, ks]
        bm = b[mid:mid + 1, :]
        g = b[L - 1:L, :]
        e1 = jnp.exp(b - bm)
        e2 = jnp.exp(bm - b)
        qt = q_ref[:, ks].astype(F32) * (G_DK ** -0.5) * e1
        kt = k_ref[:, ks].astype(F32) * e2
        attn = jnp.where(causal, _dot_nt(qt.astype(BF16), kt.astype(BF16)), 0.0)
        v_h = v_ref[:, vs]
        s_prev = s_ref[h]
        qi = qt * jnp.exp(bm)
        o = _dot(attn.astype(BF16), v_h) + _dot(qi.astype(BF16), s_prev.astype(BF16))
        ke = kt * jnp.exp(g - bm)
        g_col = jnp.broadcast_to(jnp.exp(g), (LANES, G_DK)).T[:, 0:1]
        s_ref[h] = g_col * s_prev + _dot(ke.T.astype(BF16), v_h)
        on = o * lax.rsqrt(jnp.mean(o * o, axis=-1, keepdims=True) + EPS) * hg_ref[:, vs]
        out_ref[:, vs] = (on * _silu(z_ref[:, vs].astype(F32))).astype(BF16)


def _gla(q, k, v, z, r, w_alpha, b_alpha, head_g, batch, seq):
    t = batch * seq
    L = G_CHUNK
    nc = seq // L
    dk = G_HEADS * G_DK
    dv = G_HEADS * G_DV
    tok = lambda b, c: (b * nc + c, 0)
    const = lambda b, c: (0, 0)
    return pl.pallas_call(
        _gla_body,
        grid=(batch, nc),
        in_specs=[
            pl.BlockSpec((L, dk), tok),
            pl.BlockSpec((L, dk), tok),
            pl.BlockSpec((L, dv), tok),
            pl.BlockSpec((L, dv), tok),
            pl.BlockSpec((L, G_RANK_PAD), tok),
            pl.BlockSpec((G_RANK_PAD, dk), const),
            pl.BlockSpec((1, dk), const),
            pl.BlockSpec((1, dv), const),
        ],
        out_specs=pl.BlockSpec((L, dv), tok),
        out_shape=jax.ShapeDtypeStruct((t, dv), BF16),
        scratch_shapes=[pltpu.VMEM((G_HEADS, G_DK, G_DV), F32)],
        compiler_params=_cparams(2),
        name="gla",
    )(q, k, v, z, r, w_alpha, b_alpha.reshape(1, dk), head_g.reshape(1, dv))


def _odd_out_body(y_ref, x_ref, wout_ref, g_ref, out_ref):
    x = x_ref[...] + _dot(y_ref[...], wout_ref[...])
    out_ref[...] = x * lax.rsqrt(jnp.mean(x * x, axis=-1, keepdims=True) + EPS) * g_ref[...]


def _odd_out(y, x1, w_out, g):
    t, d = x1.shape
    dm = y.shape[1]
    tok = lambda i: (i, 0)
    const = lambda i: (0, 0)
    return pl.pallas_call(
        _odd_out_body,
        grid=(t // TOKEN_TILE,),
        in_specs=[
            pl.BlockSpec((TOKEN_TILE, dm), tok),
            pl.BlockSpec((TOKEN_TILE, d), tok),
            pl.BlockSpec((dm, d), const),
            pl.BlockSpec((1, d), const),
        ],
        out_specs=pl.BlockSpec((TOKEN_TILE, d), tok),
        out_shape=jax.ShapeDtypeStruct((t, d), F32),
        compiler_params=_cparams(1),
        name="odd_out",
    )(y, x1, w_out, g.reshape(1, d))


def kernel(x, norm_g, final_norm_g, ev_w_in, ev_conv_w, ev_conv_b, ev_i_bias, ev_f_bias, ev_head_g,
           s5_lam_re, s5_lam_im, s5_log_dt, s5_b_re, s5_b_im, s5_c_re, s5_c_im, s5_d, s5_glu_w,
           s5_glu_b, ev_w_out, od_w_in, gla_w_alpha, gla_b_alpha, gla_head_g, od_w_out):
    batch, seq, d = x.shape
    t = batch * seq
    xf = x.reshape(t, d)
    padc = lambda a: jnp.pad(a, ((0, 0), (0, LANES - a.shape[1])))

    w = ev_w_in[0]
    g0 = 2 * M_HEADS * M_DK + 2 * M_HEADS * M_DV
    g1 = g0 + 2 * M_HEADS
    wm = jnp.concatenate([w[:, :g0], w[:, g1:]], axis=1).astype(BF16)
    wg = padc(w[:, g0:g1]).astype(BF16)
    half = D_MIX // 2
    qk, v, o, u, z, gates = _norm_proj(xf, norm_g[0], wm, wg, (M_QK, half, half, half, D_MIX))
    gbias = padc(jnp.concatenate([ev_i_bias[0], ev_f_bias[0]]).reshape(1, 2 * M_HEADS))
    ya = _mlstm(qk, v, o, z, gates, ev_conv_w[0], ev_conv_b[0], gbias, ev_head_g[0], batch, seq)

    mt, pc, qt, ab = _s5_params(s5_lam_re[0], s5_lam_im[0], s5_log_dt[0], s5_b_re[0], s5_b_im[0],
                                s5_c_re[0], s5_c_im[0], s5_d[0])
    nblk = seq // S5_BLK
    ug = (u.reshape(batch, nblk, S5_BLK, S5_GROUPS, S5_GROUP)
          .transpose(3, 1, 0, 2, 4).reshape(S5_GROUPS, nblk * batch, S5_ROWW))
    yg = _s5_core(ug, mt, pc, qt, ab, batch)
    yb = (yg.reshape(S5_GROUPS, nblk, batch, S5_BLK, S5_GROUP)
          .transpose(2, 1, 3, 0, 4).reshape(t, half))
    x1 = _even_out(yb, ya, z, xf, s5_glu_w[0].astype(BF16), s5_glu_b[0], ev_w_out[0].astype(BF16))

    w = od_w_in[0]
    dk = G_HEADS * G_DK
    n_main = 2 * dk + 2 * D_MIX
    q, k, v, z, r = _norm_proj(xf=x1, g=norm_g[1], wm=w[:, :n_main].astype(BF16),
                               wg=padc(w[:, n_main:]).astype(BF16), splits=(dk, dk, D_MIX, D_MIX))
    wa = jnp.pad(gla_w_alpha[0], ((0, G_RANK_PAD - gla_w_alpha.shape[1]), (0, 0)))
    y = _gla(q, k, v, z, r, wa, gla_b_alpha[0], gla_head_g[0], batch, seq)
    out = _odd_out(y, x1, od_w_out[0].astype(BF16), final_norm_g)
    return out.reshape(batch, seq, d)
```

```python
import jax
import jax.numpy as jnp
from jax import lax
from jax.experimental import pallas as pl
from jax.experimental.pallas import tpu as pltpu

F32 = jnp.float32
BF16 = jnp.bfloat16

EPS = 1e-6
D_MODEL = 1024
D_MIX = 2 * D_MODEL
M_HEADS = 4
M_DK = 128
M_DV = 256
M_QK = 2 * M_HEADS * M_DK
CONV_WIDTH = 4
M_CHUNK = 256
S5_GROUP = 16
S5_STATE = 64
S5_BLK = 8
S5_CHUNK = 1024
G_HEADS = 4
G_DK = 256
G_DV = 512
G_TAU = 16.0
G_CHUNK = 128
G_RANK_PAD = 128

LANES = 128
SUBLANES = 8
S5_GPL = LANES // S5_GROUP
S5_SW = S5_GPL * S5_STATE
TOKEN_TILE = 512
PROJ_TN = 256
VMEM_LIMIT = 56 * 1024 * 1024


def _cparams(n_grid):
    return pltpu.CompilerParams(
        dimension_semantics=("arbitrary",) * n_grid, vmem_limit_bytes=VMEM_LIMIT)


def _log_sigmoid(x):
    return jnp.minimum(x, 0.0) - jnp.log1p(jnp.exp(-jnp.abs(x)))


def _silu(x):
    return x * jax.nn.sigmoid(x)


def _split_hi_lo(x):
    hi = x.astype(BF16)
    lo = (x - hi.astype(F32)).astype(BF16)
    return hi, lo


def _dot(a, b):
    return jnp.dot(a, b, preferred_element_type=F32)


def _dot_nt(a, b, precision=None):
    return lax.dot_general(a, b, (((1,), (1,)), ((), ())), precision=precision,
                           preferred_element_type=F32)


def _tri_ones(n, lower):
    row = lax.broadcasted_iota(jnp.int32, (n, n), 0)
    col = lax.broadcasted_iota(jnp.int32, (n, n), 1)
    keep = (row >= col) if lower else (row <= col)
    return jnp.where(keep, 1.0, 0.0).astype(BF16)


def _norm_proj_body(x_ref, g_ref, w_ref, wg_ref, *out_refs):
    x = x_ref[...]
    hn = x * lax.rsqrt(jnp.mean(x * x, axis=-1, keepdims=True) + EPS) * g_ref[...]
    hb = hn.astype(BF16)
    off = 0
    for o_ref in out_refs[:-1]:
        n = o_ref.shape[1]
        for j in range(0, n, PROJ_TN):
            o_ref[:, j:j + PROJ_TN] = _dot(hb, w_ref[:, off + j:off + j + PROJ_TN]).astype(o_ref.dtype)
        off += n
    out_refs[-1][...] = _dot(hb, wg_ref[...])


def _norm_proj(xf, g, wm, wg, splits, dtypes):
    t, d = xf.shape
    n_main = wm.shape[1]
    assert sum(splits) == n_main and t % TOKEN_TILE == 0
    out_shape = [jax.ShapeDtypeStruct((t, n), dt) for n, dt in zip(splits, dtypes)]
    out_shape.append(jax.ShapeDtypeStruct((t, LANES), F32))
    out_specs = [pl.BlockSpec((TOKEN_TILE, n), lambda i: (i, 0)) for n in splits]
    out_specs.append(pl.BlockSpec((TOKEN_TILE, LANES), lambda i: (i, 0)))
    return pl.pallas_call(
        _norm_proj_body,
        grid=(t // TOKEN_TILE,),
        in_specs=[
            pl.BlockSpec((TOKEN_TILE, d), lambda i: (i, 0)),
            pl.BlockSpec((1, d), lambda i: (0, 0)),
            pl.BlockSpec((d, n_main), lambda i: (0, 0), pipeline_mode=pl.Buffered(1)),
            pl.BlockSpec((d, LANES), lambda i: (0, 0)),
        ],
        out_specs=out_specs,
        out_shape=out_shape,
        compiler_params=_cparams(1),
        name="norm_proj",
    )(xf, g.reshape(1, d), wm, wg)


def _mlstm_body(qk_ref, v_ref, o_ref, z_ref, gates_ref, convw_ref, convb_ref, gbias_ref,
                headg_ref, out_ref, qkext_ref, c_ref, n_ref, m_ref):
    L = M_CHUNK

    @pl.when(pl.program_id(1) == 0)
    def _init():
        qkext_ref[0:SUBLANES, :] = jnp.zeros((SUBLANES, M_QK), F32)
        c_ref[...] = jnp.zeros_like(c_ref)
        n_ref[...] = jnp.zeros_like(n_ref)
        m_ref[...] = jnp.zeros_like(m_ref)

    qkext_ref[SUBLANES:SUBLANES + L, :] = qk_ref[...].astype(F32)
    first = SUBLANES - (CONV_WIDTH - 1)
    acc = convb_ref[...] + convw_ref[0:1, :] * qkext_ref[first:first + L, :]
    for j in range(1, CONV_WIDTH):
        acc = acc + convw_ref[j:j + 1, :] * qkext_ref[first + j:first + j + L, :]
    qkc = _silu(acc)
    qkext_ref[0:SUBLANES, :] = qkext_ref[L:L + SUBLANES, :]

    gt = gates_ref[...] + gbias_ref[...]
    lane = lax.broadcasted_iota(jnp.int32, gt.shape, 1)
    gl = jnp.where(lane >= M_HEADS, _log_sigmoid(gt), gt)
    gl_t = gl.T
    hi, lo = _split_hi_lo(gl)
    tril = _tri_ones(L, True)
    bcol = _dot(tril, hi) + _dot(tril, lo)
    hi_t, lo_t = _split_hi_lo(gl_t[0:2 * SUBLANES, :])
    triu = _tri_ones(L, False)
    brow = _dot(hi_t, triu) + _dot(lo_t, triu)

    row = lax.broadcasted_iota(jnp.int32, (L, L), 0)
    col = lax.broadcasted_iota(jnp.int32, (L, L), 1)
    causal = row >= col

    for h in range(M_HEADS):
        ks = slice(h * M_DK, (h + 1) * M_DK)
        ks2 = slice(M_HEADS * M_DK + h * M_DK, M_HEADS * M_DK + (h + 1) * M_DK)
        vs = slice(h * M_DV, (h + 1) * M_DV)
        fl = M_HEADS + h
        b_t = bcol[:, fl:fl + 1]
        i_t = gl[:, h:h + 1]
        b_s = brow[fl:fl + 1, :]
        i_s = gl_t[h:h + 1, :]
        g = bcol[L - 1:L, fl:fl + 1]
        m_prev = m_ref[h, 0:1, 0:1]
        c_prev = c_ref[h]
        n_prev = n_ref[h, 0:1, :]

        dmat = jnp.where(causal, b_t - b_s + i_s, -jnp.inf)
        inter = b_t + m_prev
        m_t = jnp.maximum(inter, jnp.max(dmat, axis=1, keepdims=True))
        w_intra = jnp.exp(dmat - m_t)
        w_inter = jnp.exp(inter - m_t)

        q_h = qkc[:, ks] * (M_DK ** -0.5)
        k_h = qkc[:, ks2]
        qb = q_h.astype(BF16)
        kb = k_h.astype(BF16)
        v_h = v_ref[:, vs]
        sc = w_intra * _dot_nt(qb, kb)
        num = _dot(sc.astype(BF16), v_h) + w_inter * _dot(qb, c_prev.astype(BF16))
        den = (jnp.sum(sc, axis=1, keepdims=True)
               + w_inter * jnp.sum(q_h * n_prev, axis=1, keepdims=True))
        hh = num / jnp.maximum(jnp.abs(den), jnp.exp(-m_t))

        a = g - b_t + i_t
        m_loc = jnp.max(a, axis=0, keepdims=True)
        w_loc = jnp.exp(a - m_loc)
        kw = k_h * w_loc
        c_loc = _dot(kw.T.astype(BF16), v_h)
        n_loc = jnp.sum(kw, axis=0, keepdims=True)
        m_new = jnp.maximum(g + m_prev, m_loc)
        s_prev = jnp.exp(g + m_prev - m_new)
        s_loc = jnp.exp(m_loc - m_new)
        c_ref[h] = s_prev * c_prev + s_loc * c_loc
        n_ref[h] = jnp.broadcast_to(s_prev * n_prev + s_loc * n_loc, (SUBLANES, M_DK))
        m_ref[h] = jnp.broadcast_to(m_new, (SUBLANES, LANES))

        og = jax.nn.sigmoid(o_ref[:, vs].astype(F32)) * hh
        hn = og * lax.rsqrt(jnp.mean(og * og, axis=-1, keepdims=True) + EPS) * headg_ref[:, vs]
        out_ref[:, vs] = (hn * _silu(z_ref[:, vs].astype(F32))).astype(BF16)


def _mlstm(qk, v, o, z, gates, conv_w, conv_b, gbias, head_g, batch, seq):
    t = batch * seq
    L = M_CHUNK
    nc = seq // L
    dv = M_HEADS * M_DV
    tok = lambda b, c: (b * nc + c, 0)
    const = lambda b, c: (0, 0)
    return pl.pallas_call(
        _mlstm_body,
        grid=(batch, nc),
        in_specs=[
            pl.BlockSpec((L, M_QK), tok),
            pl.BlockSpec((L, dv), tok),
            pl.BlockSpec((L, dv), tok),
            pl.BlockSpec((L, dv), tok),
            pl.BlockSpec((L, LANES), tok),
            pl.BlockSpec((CONV_WIDTH, M_QK), const),
            pl.BlockSpec((1, M_QK), const),
            pl.BlockSpec((1, LANES), const),
            pl.BlockSpec((1, dv), const),
        ],
        out_specs=pl.BlockSpec((L, dv), tok),
        out_shape=jax.ShapeDtypeStruct((t, dv), BF16),
        scratch_shapes=[
            pltpu.VMEM((L + SUBLANES, M_QK), F32),
            pltpu.VMEM((M_HEADS, M_DK, M_DV), F32),
            pltpu.VMEM((M_HEADS, SUBLANES, M_DK), F32),
            pltpu.VMEM((M_HEADS, SUBLANES, LANES), F32),
        ],
        compiler_params=_cparams(2),
        name="mlstm",
    )(qk, v, o, z, gates, conv_w, conv_b.reshape(1, M_QK), gbias, head_g.reshape(1, dv))


def _gelu_tanh(x):
    return 0.5 * x * (1.0 + jnp.tanh(0.7978845608028654 * (x + 0.044715 * (x * x * x))))


def _s5_build_operators(lamr_ref, lami_ref, ldt_ref, btr_ref, bti_ref, ctr_ref, cti_ref, d_ref,
                        pbig_ref, qbig_ref, mbig_ref, a_ref):
    lr = lamr_ref[0]
    li = lami_ref[0]
    dt = jnp.exp(ldt_ref[0])
    zr = lr * dt
    th = li * dt
    er = jnp.exp(zr)
    ar = er * jnp.cos(th)
    ai = er * jnp.sin(th)
    den = lr * lr + li * li
    beta_r = ((ar - 1.0) * lr + ai * li) / den
    beta_i = (ai * lr - (ar - 1.0) * li) / den
    btr = btr_ref[0]
    bti = bti_ref[0]
    bbr = btr * beta_r - bti * beta_i
    bbi = btr * beta_i + bti * beta_r
    ctr = ctr_ref[0]
    cti = cti_ref[0]

    row_g = lax.shift_right_logical(lax.broadcasted_iota(jnp.int32, (LANES, S5_SW), 0), 4)
    lane_g = lax.shift_right_logical(lax.broadcasted_iota(jnp.int32, (LANES, S5_SW), 1), 6)
    same_group = row_g == lane_g

    def expand(x16):
        return jnp.where(same_group, jnp.concatenate([x16] * S5_GPL, axis=0), 0.0)

    def power(k):
        e = jnp.exp(float(k) * zr)
        return e * jnp.cos(float(k) * th), e * jnp.sin(float(k) * th)

    for s in range(S5_BLK):
        rows = slice(s * LANES, (s + 1) * LANES)
        pr, pi = power(S5_BLK - 1 - s)
        pbig_ref[rows, 0:S5_SW] = expand(pr * bbr - pi * bbi).astype(BF16)
        pbig_ref[rows, S5_SW:2 * S5_SW] = expand(pr * bbi + pi * bbr).astype(BF16)
        pr, pi = power(s + 1)
        qbig_ref[rows, 0:S5_SW] = expand(ctr * pr - cti * pi).astype(BF16)
        qbig_ref[rows, S5_SW:2 * S5_SW] = expand(-(ctr * pi + cti * pr)).astype(BF16)

    cb = jnp.concatenate([expand(ctr), expand(-cti)], axis=1)
    r128 = lax.broadcasted_iota(jnp.int32, (LANES, LANES), 0)
    c128 = lax.broadcasted_iota(jnp.int32, (LANES, LANES), 1)
    zero_blk = jnp.zeros((LANES, LANES), BF16)
    for lag in range(S5_BLK):
        pr, pi = power(lag)
        ab = jnp.concatenate([expand(pr * bbr - pi * bbi), expand(pr * bbi + pi * bbr)], axis=1)
        v = _dot_nt(ab, cb, precision=lax.Precision.HIGHEST)
        if lag == 0:
            v = v + jnp.where(r128 == c128, d_ref[0], 0.0)
        vb = v.astype(BF16)
        for s in range(S5_BLK - lag):
            t = s + lag
            mbig_ref[s * LANES:(s + 1) * LANES, t * LANES:(t + 1) * LANES] = vb
            if lag > 0:
                mbig_ref[t * LANES:(t + 1) * LANES, s * LANES:(s + 1) * LANES] = zero_blk

    pr, pi = power(S5_BLK)
    a_ref[:, 0:S5_SW] = jnp.broadcast_to(pr, (SUBLANES, S5_SW))
    a_ref[:, S5_SW:2 * S5_SW] = jnp.broadcast_to(pi, (SUBLANES, S5_SW))


def _s5_body(u_ref, lamr_ref, lami_ref, ldt_ref, btr_ref, bti_ref, ctr_ref, cti_ref, d_ref, y_ref,
             pbig_ref, qbig_ref, mbig_ref, a_ref, ucat_ref, x_ref, xp_ref, st_ref):
    nblk = S5_CHUNK // S5_BLK
    rows = nblk * SUBLANES

    @pl.when(pl.program_id(1) == 0)
    def _setup():
        _s5_build_operators(lamr_ref, lami_ref, ldt_ref, btr_ref, bti_ref, ctr_ref, cti_ref, d_ref,
                            pbig_ref, qbig_ref, mbig_ref, a_ref)
        st_ref[...] = jnp.zeros_like(st_ref)

    for s in range(S5_BLK):
        piece = u_ref[pl.ds(s, nblk, stride=S5_BLK), :, :]
        ucat_ref[:, s * LANES:(s + 1) * LANES] = piece.reshape(rows, LANES).astype(BF16)
    x_ref[...] = _dot(ucat_ref[...], pbig_ref[...])

    ar = a_ref[:, 0:S5_SW]
    ai = a_ref[:, S5_SW:2 * S5_SW]
    xr = st_ref[:, 0:S5_SW]
    xi = st_ref[:, S5_SW:2 * S5_SW]
    for blk in range(nblk):
        r = slice(blk * SUBLANES, (blk + 1) * SUBLANES)
        xp_ref[r, 0:S5_SW] = xr
        xp_ref[r, S5_SW:2 * S5_SW] = xi
        nr = ar * xr - ai * xi + x_ref[r, 0:S5_SW]
        ni = ar * xi + ai * xr + x_ref[r, S5_SW:2 * S5_SW]
        xr, xi = nr, ni
    st_ref[:, 0:S5_SW] = xr
    st_ref[:, S5_SW:2 * S5_SW] = xi

    xpb = xp_ref[...].astype(BF16)
    width = 2 * LANES
    for nb in range(S5_BLK // 2):
        kk = (2 * nb + 2) * LANES
        cols = slice(nb * width, (nb + 1) * width)
        y = _dot(ucat_ref[:, 0:kk], mbig_ref[0:kk, cols]) + _dot_nt(xpb, qbig_ref[cols, :])
        y = _gelu_tanh(y)
        for tt in range(2):
            t = 2 * nb + tt
            y_ref[pl.ds(t, nblk, stride=S5_BLK), :, :] = (
                y[:, tt * LANES:(tt + 1) * LANES].reshape(nblk, SUBLANES, LANES))


def _s5(u3, lam_re, lam_im, log_dt, b_re, b_im, c_re, c_im, d_skip):
    seq, batch, width = u3.shape
    nlb = width // LANES
    assert batch == SUBLANES and seq % S5_CHUNK == 0
    lamr = lam_re.reshape(nlb, 1, S5_SW)
    lami = lam_im.reshape(nlb, 1, S5_SW)
    ldt = jnp.repeat(log_dt, S5_STATE).reshape(nlb, 1, S5_SW)
    bt = lambda b: b.reshape(nlb, S5_GPL, S5_STATE, S5_GROUP).transpose(0, 3, 1, 2).reshape(nlb, S5_GROUP, S5_SW)
    ct = lambda c: c.reshape(nlb, S5_GPL, S5_GROUP, S5_STATE).transpose(0, 2, 1, 3).reshape(nlb, S5_GROUP, S5_SW)
    par = lambda r, w: pl.BlockSpec((1, r, w), lambda i, j: (i, 0, 0))
    kdim = S5_BLK * LANES
    rows = (S5_CHUNK // S5_BLK) * SUBLANES
    io = pl.BlockSpec((S5_CHUNK, batch, LANES), lambda i, j: (j, 0, i))
    return pl.pallas_call(
        _s5_body,
        grid=(nlb, seq // S5_CHUNK),
        in_specs=[io, par(1, S5_SW), par(1, S5_SW), par(1, S5_SW), par(S5_GROUP, S5_SW),
                  par(S5_GROUP, S5_SW), par(S5_GROUP, S5_SW), par(S5_GROUP, S5_SW), par(1, LANES)],
        out_specs=io,
        out_shape=jax.ShapeDtypeStruct((seq, batch, width), F32),
        scratch_shapes=[
            pltpu.VMEM((kdim, 2 * S5_SW), BF16),
            pltpu.VMEM((kdim, 2 * S5_SW), BF16),
            pltpu.VMEM((kdim, kdim), BF16),
            pltpu.VMEM((SUBLANES, 2 * S5_SW), F32),
            pltpu.VMEM((rows, kdim), BF16),
            pltpu.VMEM((rows, 2 * S5_SW), F32),
            pltpu.VMEM((rows, 2 * S5_SW), F32),
            pltpu.VMEM((SUBLANES, 2 * S5_SW), F32),
        ],
        compiler_params=_cparams(2),
        name="s5",
    )(u3, lamr, lami, ldt, bt(b_re), bt(b_im), ct(c_re), ct(c_im), d_skip.reshape(nlb, 1, LANES))


def _even_out_body(yb_ref, ya_ref, zb_ref, x_ref, gluw_ref, glub_ref, wout_ref, out_ref):
    yg = yb_ref[...]
    s = _dot(yg, gluw_ref[...]) + glub_ref[...]
    hb = yg.astype(F32) * jax.nn.sigmoid(s)
    yb = (hb * _silu(zb_ref[...].astype(F32))).astype(BF16)
    half = ya_ref.shape[1]
    out_ref[...] = (x_ref[...] + _dot(ya_ref[...], wout_ref[0:half, :])
                    + _dot(yb, wout_ref[half:2 * half, :]))


def _even_out(yb, ya, z, xf, glu_w, glu_b, w_out):
    t, d = xf.shape
    half = ya.shape[1]
    tok = lambda i: (i, 0)
    const = lambda i: (0, 0)
    return pl.pallas_call(
        _even_out_body,
        grid=(t // TOKEN_TILE,),
        in_specs=[
            pl.BlockSpec((TOKEN_TILE, half), tok),
            pl.BlockSpec((TOKEN_TILE, half), tok),
            pl.BlockSpec((TOKEN_TILE, half), lambda i: (i, 1)),
            pl.BlockSpec((TOKEN_TILE, d), tok),
            pl.BlockSpec((half, half), const),
            pl.BlockSpec((1, half), const),
            pl.BlockSpec((2 * half, d), const),
        ],
        out_specs=pl.BlockSpec((TOKEN_TILE, d), tok),
        out_shape=jax.ShapeDtypeStruct((t, d), F32),
        compiler_params=_cparams(1),
        name="even_out",
    )(yb, ya, z, xf, glu_w, glu_b.reshape(1, half), w_out)


def _gla_body(q_ref, k_ref, v_ref, z_ref, r_ref, wa_ref, ba_ref, hg_ref, out_ref, s_ref):
    L = G_CHUNK

    @pl.when(pl.program_id(1) == 0)
    def _init():
        s_ref[...] = jnp.zeros_like(s_ref)

    rh, rl = _split_hi_lo(r_ref[...])
    wh, wl = _split_hi_lo(wa_ref[...])
    pre = _dot(rh, wh) + _dot(rh, wl) + _dot(rl, wh) + ba_ref[...]
    la = _log_sigmoid(pre) * (1.0 / G_TAU)
    lh, ll = _split_hi_lo(la)
    tril = _tri_ones(L, True)
    bc = _dot(tril, lh) + _dot(tril, ll)

    row = lax.broadcasted_iota(jnp.int32, (L, L), 0)
    col = lax.broadcasted_iota(jnp.int32, (L, L), 1)
    causal = row >= col
    mid = L // 2 - 1

    for h in range(G_HEADS):
        ks = slice(h * G_DK, (h + 1) * G_DK)
        vs = slice(h * G_DV, (h + 1) * G_DV)
        b = bc[:, ks]
        bm = b[mid:mid + 1, :]
        g = b[L - 1:L, :]
        e1 = jnp.exp(b - bm)
        e2 = jnp.exp(bm - b)
        qt = q_ref[:, ks].astype(F32) * (G_DK ** -0.5) * e1
        kt = k_ref[:, ks].astype(F32) * e2
        attn = jnp.where(causal, _dot_nt(qt.astype(BF16), kt.astype(BF16)), 0.0)
        v_h = v_ref[:, vs]
        s_prev = s_ref[h]
        qi = qt * jnp.exp(bm)
        o = _dot(attn.astype(BF16), v_h) + _dot(qi.astype(BF16), s_prev.astype(BF16))
        ke = kt * jnp.exp(g - bm)
        g_col = jnp.broadcast_to(jnp.exp(g), (LANES, G_DK)).T[:, 0:1]
        s_ref[h] = g_col * s_prev + _dot(ke.T.astype(BF16), v_h)
        on = o * lax.rsqrt(jnp.mean(o * o, axis=-1, keepdims=True) + EPS) * hg_ref[:, vs]
        out_ref[:, vs] = (on * _silu(z_ref[:, vs].astype(F32))).astype(BF16)


def _gla(q, k, v, z, r, w_alpha, b_alpha, head_g, batch, seq):
    t = batch * seq
    L = G_CHUNK
    nc = seq // L
    dk = G_HEADS * G_DK
    dv = G_HEADS * G_DV
    tok = lambda b, c: (b * nc + c, 0)
    const = lambda b, c: (0, 0)
    return pl.pallas_call(
        _gla_body,
        grid=(batch, nc),
        in_specs=[
            pl.BlockSpec((L, dk), tok),
            pl.BlockSpec((L, dk), tok),
            pl.BlockSpec((L, dv), tok),
            pl.BlockSpec((L, dv), tok),
            pl.BlockSpec((L, G_RANK_PAD), tok),
            pl.BlockSpec((G_RANK_PAD, dk), const),
            pl.BlockSpec((1, dk), const),
            pl.BlockSpec((1, dv), const),
        ],
        out_specs=pl.BlockSpec((L, dv), tok),
        out_shape=jax.ShapeDtypeStruct((t, dv), BF16),
        scratch_shapes=[pltpu.VMEM((G_HEADS, G_DK, G_DV), F32)],
        compiler_params=_cparams(2),
        name="gla",
    )(q, k, v, z, r, w_alpha, b_alpha.reshape(1, dk), head_g.reshape(1, dv))


def _odd_out_body(y_ref, x_ref, wout_ref, g_ref, out_ref):
    x = x_ref[...] + _dot(y_ref[...], wout_ref[...])
    out_ref[...] = x * lax.rsqrt(jnp.mean(x * x, axis=-1, keepdims=True) + EPS) * g_ref[...]


def _odd_out(y, x1, w_out, g):
    t, d = x1.shape
    dm = y.shape[1]
    tok = lambda i: (i, 0)
    const = lambda i: (0, 0)
    return pl.pallas_call(
        _odd_out_body,
        grid=(t // TOKEN_TILE,),
        in_specs=[
            pl.BlockSpec((TOKEN_TILE, dm), tok),
            pl.BlockSpec((TOKEN_TILE, d), tok),
            pl.BlockSpec((dm, d), const),
            pl.BlockSpec((1, d), const),
        ],
        out_specs=pl.BlockSpec((TOKEN_TILE, d), tok),
        out_shape=jax.ShapeDtypeStruct((t, d), F32),
        compiler_params=_cparams(1),
        name="odd_out",
    )(y, x1, w_out, g.reshape(1, d))


def kernel(x, norm_g, final_norm_g, ev_w_in, ev_conv_w, ev_conv_b, ev_i_bias, ev_f_bias, ev_head_g,
           s5_lam_re, s5_lam_im, s5_log_dt, s5_b_re, s5_b_im, s5_c_re, s5_c_im, s5_d, s5_glu_w,
           s5_glu_b, ev_w_out, od_w_in, gla_w_alpha, gla_b_alpha, gla_head_g, od_w_out):
    batch, seq, d = x.shape
    t = batch * seq
    xf = x.reshape(t, d)
    padc = lambda a: jnp.pad(a, ((0, 0), (0, LANES - a.shape[1])))

    w = ev_w_in[0]
    g0 = 2 * M_HEADS * M_DK + 2 * M_HEADS * M_DV
    g1 = g0 + 2 * M_HEADS
    wm = jnp.concatenate([w[:, :g0], w[:, g1:]], axis=1).astype(BF16)
    wg = padc(w[:, g0:g1]).astype(BF16)
    half = D_MIX // 2
    qk, v, o, u, z, gates = _norm_proj(xf, norm_g[0], wm, wg, (M_QK, half, half, half, D_MIX),
                                       (BF16, BF16, BF16, F32, BF16))
    gbias = padc(jnp.concatenate([ev_i_bias[0], ev_f_bias[0]]).reshape(1, 2 * M_HEADS))
    ya = _mlstm(qk, v, o, z, gates, ev_conv_w[0], ev_conv_b[0], gbias, ev_head_g[0], batch, seq)

    u3 = u.reshape(batch, seq, half).transpose(1, 0, 2)
    y3 = _s5(u3, s5_lam_re[0], s5_lam_im[0], s5_log_dt[0], s5_b_re[0], s5_b_im[0],
             s5_c_re[0], s5_c_im[0], s5_d[0])
    yb = y3.transpose(1, 0, 2).reshape(t, half).astype(BF16)
    x1 = _even_out(yb, ya, z, xf, s5_glu_w[0].astype(BF16), s5_glu_b[0], ev_w_out[0].astype(BF16))

    w = od_w_in[0]
    dk = G_HEADS * G_DK
    n_main = 2 * dk + 2 * D_MIX
    q, k, v, z, r = _norm_proj(x1, norm_g[1], w[:, :n_main].astype(BF16), padc(w[:, n_main:]).astype(BF16),
                               (dk, dk, D_MIX, D_MIX), (BF16,) * 4)
    wa = jnp.pad(gla_w_alpha[0], ((0, G_RANK_PAD - gla_w_alpha.shape[1]), (0, 0)))
    y = _gla(q, k, v, z, r, wa, gla_b_alpha[0], gla_head_g[0], batch, seq)
    out = _odd_out(y, x1, od_w_out[0].astype(BF16), final_norm_g)
    return out.reshape(batch, seq, d)
```

```python
import functools

import jax
import jax.numpy as jnp
from jax import lax
from jax.experimental import pallas as pl
from jax.experimental.pallas import tpu as pltpu

F32 = jnp.float32
BF16 = jnp.bfloat16

EPS = 1e-6
D_MODEL = 1024
D_MIX = 2 * D_MODEL
M_HEADS = 4
M_DK = 128
M_DV = 256
M_QK = 2 * M_HEADS * M_DK
CONV_WIDTH = 4
M_CHUNK = 256
S5_GROUP = 16
S5_STATE = 64
S5_BLK = 8
S5_CHUNK = 1024
G_HEADS = 4
G_DK = 256
G_DV = 512
G_TAU = 16.0
G_CHUNK = 128
G_RANK_PAD = 128

LANES = 128
SUBLANES = 8
HALO = 16
S5_GPL = LANES // S5_GROUP
S5_SW = S5_GPL * S5_STATE
TOKEN_TILE = 512
PROJ_TN = 256
VMEM_LIMIT = 56 * 1024 * 1024


def _cparams(n_grid):
    return pltpu.CompilerParams(
        dimension_semantics=("arbitrary",) * n_grid, vmem_limit_bytes=VMEM_LIMIT)


def _log_sigmoid(x):
    return jnp.minimum(x, 0.0) - jnp.log(1.0 + jnp.exp(-jnp.abs(x)))


def _silu(x):
    return x * jax.nn.sigmoid(x)


def _split_hi_lo(x):
    hi = x.astype(BF16)
    lo = (x - hi.astype(F32)).astype(BF16)
    return hi, lo


def _dot(a, b):
    return jnp.dot(a, b, preferred_element_type=F32)


def _dot_nt(a, b, precision=None):
    return lax.dot_general(a, b, (((1,), (1,)), ((), ())), precision=precision,
                           preferred_element_type=F32)


def _tri_ones(n, lower):
    row = lax.broadcasted_iota(jnp.int32, (n, n), 0)
    col = lax.broadcasted_iota(jnp.int32, (n, n), 1)
    keep = (row >= col) if lower else (row <= col)
    return jnp.where(keep, 1.0, 0.0).astype(BF16)


def _rmsnorm(x, g):
    return x * lax.rsqrt(jnp.mean(x * x, axis=-1, keepdims=True) + EPS) * g


def _project(hb, w_ref, out_refs, col0):
    off = col0
    for o_ref in out_refs:
        n = o_ref.shape[1]
        for j in range(0, n, PROJ_TN):
            o_ref[:, j:j + PROJ_TN] = _dot(hb, w_ref[:, off + j:off + j + PROJ_TN]).astype(o_ref.dtype)
        off += n


def _proj_even_body(x_ref, xh_ref, g_ref, w_ref, wg_ref, convw_ref, convb_ref,
                    qk_ref, v_ref, o_ref, u_ref, z_ref, gates_ref, ext_ref, *, tiles_per_seq):
    hb = _rmsnorm(x_ref[...], g_ref[...]).astype(BF16)
    hh = _rmsnorm(xh_ref[...], g_ref[...]).astype(BF16)
    seq_start = lax.rem(pl.program_id(0), tiles_per_seq) == 0
    base = HALO - (CONV_WIDTH - 1)
    lane = lax.broadcasted_iota(jnp.int32, (1, PROJ_TN), 1)
    for j in range(0, M_QK, PROJ_TN):
        cols = slice(j, j + PROJ_TN)
        wj = w_ref[:, cols]
        ext_ref[HALO:HALO + TOKEN_TILE, :] = _dot(hb, wj)
        ext_ref[0:HALO, :] = jnp.where(seq_start, 0.0, _dot(hh, wj))
        acc = convb_ref[:, cols] + convw_ref[0:1, cols] * ext_ref[base:base + TOKEN_TILE, :]
        for i in range(1, CONV_WIDTH):
            acc = acc + convw_ref[i:i + 1, cols] * ext_ref[base + i:base + i + TOKEN_TILE, :]
        scale = jnp.where(lane + j < M_HEADS * M_DK, M_DK ** -0.5, 1.0)
        qk_ref[:, cols] = (_silu(acc) * scale).astype(BF16)
    _project(hb, w_ref, (v_ref, o_ref, u_ref, z_ref), M_QK)
    gates_ref[...] = _dot(hb, wg_ref[...])


def _proj_even(xf, g, wm, wg, conv_w, conv_b, seq):
    t, d = xf.shape
    half = D_MIX // 2
    assert seq % TOKEN_TILE == 0 and TOKEN_TILE % HALO == 0
    tok = lambda i: (i, 0)
    const = lambda i: (0, 0)
    per_halo = TOKEN_TILE // HALO
    widths = (M_QK, half, half, half, D_MIX)
    dtypes = (BF16, BF16, BF16, F32, BF16)
    out_shape = [jax.ShapeDtypeStruct((t, n), dt) for n, dt in zip(widths, dtypes)]
    out_shape.append(jax.ShapeDtypeStruct((t, LANES), F32))
    out_specs = [pl.BlockSpec((TOKEN_TILE, n), tok) for n in widths]
    out_specs.append(pl.BlockSpec((TOKEN_TILE, LANES), tok))
    return pl.pallas_call(
        functools.partial(_proj_even_body, tiles_per_seq=seq // TOKEN_TILE),
        grid=(t // TOKEN_TILE,),
        in_specs=[
            pl.BlockSpec((TOKEN_TILE, d), tok),
            pl.BlockSpec((HALO, d), lambda i: (jnp.maximum(i * per_halo - 1, 0), 0)),
            pl.BlockSpec((1, d), const),
            pl.BlockSpec((d, wm.shape[1]), const, pipeline_mode=pl.Buffered(1)),
            pl.BlockSpec((d, LANES), const),
            pl.BlockSpec((CONV_WIDTH, M_QK), const),
            pl.BlockSpec((1, M_QK), const),
        ],
        out_specs=out_specs,
        out_shape=out_shape,
        scratch_shapes=[pltpu.VMEM((HALO + TOKEN_TILE, PROJ_TN), F32)],
        compiler_params=_cparams(1),
        name="proj_even",
    )(xf, xf, g.reshape(1, d), wm, wg, conv_w, conv_b.reshape(1, M_QK))


def _mlstm_body(qk_ref, v_ref, gates_ref, gbias_ref, out_ref, c_ref, n_ref, m_ref):
    L = M_CHUNK

    @pl.when(pl.program_id(1) == 0)
    def _init():
        c_ref[...] = jnp.zeros_like(c_ref)
        n_ref[...] = jnp.zeros_like(n_ref)
        m_ref[...] = jnp.zeros_like(m_ref)

    gt = gates_ref[...] + gbias_ref[...]
    lane = lax.broadcasted_iota(jnp.int32, gt.shape, 1)
    gl = jnp.where(lane >= M_HEADS, _log_sigmoid(gt), gt)
    gl_t = gl.T
    hi, lo = _split_hi_lo(gl)
    tril = _tri_ones(L, True)
    bcol = _dot(tril, hi) + _dot(tril, lo)
    hi_t, lo_t = _split_hi_lo(gl_t[0:2 * SUBLANES, :])
    triu = _tri_ones(L, False)
    brow = _dot(hi_t, triu) + _dot(lo_t, triu)

    row = lax.broadcasted_iota(jnp.int32, (L, L), 0)
    col = lax.broadcasted_iota(jnp.int32, (L, L), 1)
    causal = row >= col

    for h in range(M_HEADS):
        ks = slice(h * M_DK, (h + 1) * M_DK)
        ks2 = slice(M_HEADS * M_DK + h * M_DK, M_HEADS * M_DK + (h + 1) * M_DK)
        vs = slice(h * M_DV, (h + 1) * M_DV)
        fl = M_HEADS + h
        b_t = bcol[:, fl:fl + 1]
        i_t = gl[:, h:h + 1]
        b_s = brow[fl:fl + 1, :]
        i_s = gl_t[h:h + 1, :]
        g = bcol[L - 1:L, fl:fl + 1]
        m_prev = m_ref[h, 0:1, 0:1]
        c_prev = c_ref[h]
        n_prev = n_ref[h, 0:1, :]

        dmat = jnp.where(causal, b_t - b_s + i_s, -jnp.inf)
        inter = b_t + m_prev
        m_t = jnp.maximum(inter, jnp.max(dmat, axis=1, keepdims=True))
        w_intra = jnp.exp(dmat - m_t)
        w_inter = jnp.exp(inter - m_t)

        qb = qk_ref[:, ks]
        kb = qk_ref[:, ks2]
        v_h = v_ref[:, vs]
        sc = w_intra * _dot_nt(qb, kb)
        num = _dot(sc.astype(BF16), v_h) + w_inter * _dot(qb, c_prev.astype(BF16))
        den = (jnp.sum(sc, axis=1, keepdims=True)
               + w_inter * jnp.sum(qb.astype(F32) * n_prev, axis=1, keepdims=True))
        out_ref[:, vs] = (num / jnp.maximum(jnp.abs(den), jnp.exp(-m_t))).astype(BF16)

        a = g - b_t + i_t
        m_loc = jnp.max(a, axis=0, keepdims=True)
        w_loc = jnp.exp(a - m_loc)
        kw = kb.astype(F32) * w_loc
        c_loc = _dot(kw.T.astype(BF16), v_h)
        n_loc = jnp.sum(kw, axis=0, keepdims=True)
        m_new = jnp.maximum(g + m_prev, m_loc)
        s_prev = jnp.exp(g + m_prev - m_new)
        s_loc = jnp.exp(m_loc - m_new)
        c_ref[h] = s_prev * c_prev + s_loc * c_loc
        n_ref[h] = jnp.broadcast_to(s_prev * n_prev + s_loc * n_loc, (SUBLANES, M_DK))
        m_ref[h] = jnp.broadcast_to(m_new, (SUBLANES, LANES))


def _mlstm(qk, v, gates, gbias, batch, seq):
    t = batch * seq
    L = M_CHUNK
    nc = seq // L
    dv = M_HEADS * M_DV
    tok = lambda b, c: (b * nc + c, 0)
    const = lambda b, c: (0, 0)
    return pl.pallas_call(
        _mlstm_body,
        grid=(batch, nc),
        in_specs=[
            pl.BlockSpec((L, M_QK), tok),
            pl.BlockSpec((L, dv), tok),
            pl.BlockSpec((L, LANES), tok),
            pl.BlockSpec((1, LANES), const),
        ],
        out_specs=pl.BlockSpec((L, dv), tok),
        out_shape=jax.ShapeDtypeStruct((t, dv), BF16),
        scratch_shapes=[
            pltpu.VMEM((M_HEADS, M_DK, M_DV), F32),
            pltpu.VMEM((M_HEADS, SUBLANES, M_DK), F32),
            pltpu.VMEM((M_HEADS, SUBLANES, LANES), F32),
        ],
        compiler_params=_cparams(2),
        name="mlstm",
    )(qk, v, gates, gbias)


def _gelu_tanh(x):
    return 0.5 * x * (1.0 + jnp.tanh(0.7978845608028654 * (x + 0.044715 * (x * x * x))))


def _s5_build_operators(lamr_ref, lami_ref, ldt_ref, btr_ref, bti_ref, ctr_ref, cti_ref, d_ref,
                        pbig_ref, qbig_ref, mbig_ref, a_ref):
    lr = lamr_ref[0]
    li = lami_ref[0]
    dt = jnp.exp(ldt_ref[0])
    zr = lr * dt
    th = li * dt
    er = jnp.exp(zr)
    ar = er * jnp.cos(th)
    ai = er * jnp.sin(th)
    den = lr * lr + li * li
    beta_r = ((ar - 1.0) * lr + ai * li) / den
    beta_i = (ai * lr - (ar - 1.0) * li) / den
    btr = btr_ref[0]
    bti = bti_ref[0]
    bbr = btr * beta_r - bti * beta_i
    bbi = btr * beta_i + bti * beta_r
    ctr = ctr_ref[0]
    cti = cti_ref[0]

    row_g = lax.shift_right_logical(lax.broadcasted_iota(jnp.int32, (LANES, S5_SW), 0), 4)
    lane_g = lax.shift_right_logical(lax.broadcasted_iota(jnp.int32, (LANES, S5_SW), 1), 6)
    same_group = row_g == lane_g

    def expand(x16):
        return jnp.where(same_group, jnp.concatenate([x16] * S5_GPL, axis=0), 0.0)

    def power(k):
        e = jnp.exp(float(k) * zr)
        return e * jnp.cos(float(k) * th), e * jnp.sin(float(k) * th)

    for s in range(S5_BLK):
        rows = slice(s * LANES, (s + 1) * LANES)
        pr, pi = power(S5_BLK - 1 - s)
        pbig_ref[rows, 0:S5_SW] = expand(pr * bbr - pi * bbi).astype(BF16)
        pbig_ref[rows, S5_SW:2 * S5_SW] = expand(pr * bbi + pi * bbr).astype(BF16)
        pr, pi = power(s + 1)
        qbig_ref[rows, 0:S5_SW] = expand(ctr * pr - cti * pi).astype(BF16)
        qbig_ref[rows, S5_SW:2 * S5_SW] = expand(-(ctr * pi + cti * pr)).astype(BF16)

    cb = jnp.concatenate([expand(ctr), expand(-cti)], axis=1)
    r128 = lax.broadcasted_iota(jnp.int32, (LANES, LANES), 0)
    c128 = lax.broadcasted_iota(jnp.int32, (LANES, LANES), 1)
    zero_blk = jnp.zeros((LANES, LANES), BF16)
    for lag in range(S5_BLK):
        pr, pi = power(lag)
        ab = jnp.concatenate([expand(pr * bbr - pi * bbi), expand(pr * bbi + pi * bbr)], axis=1)
        v = _dot_nt(ab, cb, precision=lax.Precision.HIGHEST)
        if lag == 0:
            v = v + jnp.where(r128 == c128, d_ref[0], 0.0)
        vb = v.astype(BF16)
        for s in range(S5_BLK - lag):
            t = s + lag
            mbig_ref[s * LANES:(s + 1) * LANES, t * LANES:(t + 1) * LANES] = vb
            if lag > 0:
                mbig_ref[t * LANES:(t + 1) * LANES, s * LANES:(s + 1) * LANES] = zero_blk

    pr, pi = power(S5_BLK)
    a_ref[:, 0:S5_SW] = jnp.broadcast_to(pr, (SUBLANES, S5_SW))
    a_ref[:, S5_SW:2 * S5_SW] = jnp.broadcast_to(pi, (SUBLANES, S5_SW))


def _s5_body(u_ref, lamr_ref, lami_ref, ldt_ref, btr_ref, bti_ref, ctr_ref, cti_ref, d_ref, y_ref,
             pbig_ref, qbig_ref, mbig_ref, a_ref, ucat_ref, x_ref, xp_ref, st_ref):
    nblk = S5_CHUNK // S5_BLK
    rows = nblk * SUBLANES

    @pl.when(pl.program_id(1) == 0)
    def _setup():
        _s5_build_operators(lamr_ref, lami_ref, ldt_ref, btr_ref, bti_ref, ctr_ref, cti_ref, d_ref,
                            pbig_ref, qbig_ref, mbig_ref, a_ref)
        st_ref[...] = jnp.zeros_like(st_ref)

    for s in range(S5_BLK):
        piece = u_ref[pl.ds(s, nblk, stride=S5_BLK), :, :]
        ucat_ref[:, s * LANES:(s + 1) * LANES] = piece.reshape(rows, LANES).astype(BF16)
    x_ref[...] = _dot(ucat_ref[...], pbig_ref[...])

    ar = a_ref[:, 0:S5_SW]
    ai = a_ref[:, S5_SW:2 * S5_SW]
    xr = st_ref[:, 0:S5_SW]
    xi = st_ref[:, S5_SW:2 * S5_SW]
    for blk in range(nblk):
        r = slice(blk * SUBLANES, (blk + 1) * SUBLANES)
        xp_ref[r, 0:S5_SW] = xr
        xp_ref[r, S5_SW:2 * S5_SW] = xi
        nr = ar * xr - ai * xi + x_ref[r, 0:S5_SW]
        ni = ar * xi + ai * xr + x_ref[r, S5_SW:2 * S5_SW]
        xr, xi = nr, ni
    st_ref[:, 0:S5_SW] = xr
    st_ref[:, S5_SW:2 * S5_SW] = xi

    xpb = xp_ref[...].astype(BF16)
    width = 2 * LANES
    for nb in range(S5_BLK // 2):
        kk = (2 * nb + 2) * LANES
        cols = slice(nb * width, (nb + 1) * width)
        y = _dot(ucat_ref[:, 0:kk], mbig_ref[0:kk, cols]) + _dot_nt(xpb, qbig_ref[cols, :])
        y = _gelu_tanh(y)
        for tt in range(2):
            t = 2 * nb + tt
            y_ref[pl.ds(t, nblk, stride=S5_BLK), :, :] = (
                y[:, tt * LANES:(tt + 1) * LANES].reshape(nblk, SUBLANES, LANES))


def _s5(u3, lam_re, lam_im, log_dt, b_re, b_im, c_re, c_im, d_skip):
    seq, batch, width = u3.shape
    nlb = width // LANES
    assert batch == SUBLANES and seq % S5_CHUNK == 0
    lamr = lam_re.reshape(nlb, 1, S5_SW)
    lami = lam_im.reshape(nlb, 1, S5_SW)
    ldt = jnp.repeat(log_dt, S5_STATE).reshape(nlb, 1, S5_SW)
    bt = lambda b: b.reshape(nlb, S5_GPL, S5_STATE, S5_GROUP).transpose(0, 3, 1, 2).reshape(nlb, S5_GROUP, S5_SW)
    ct = lambda c: c.reshape(nlb, S5_GPL, S5_GROUP, S5_STATE).transpose(0, 2, 1, 3).reshape(nlb, S5_GROUP, S5_SW)
    par = lambda r, w: pl.BlockSpec((1, r, w), lambda i, j: (i, 0, 0))
    kdim = S5_BLK * LANES
    rows = (S5_CHUNK // S5_BLK) * SUBLANES
    io = pl.BlockSpec((S5_CHUNK, batch, LANES), lambda i, j: (j, 0, i))
    return pl.pallas_call(
        _s5_body,
        grid=(nlb, seq // S5_CHUNK),
        in_specs=[io, par(1, S5_SW), par(1, S5_SW), par(1, S5_SW), par(S5_GROUP, S5_SW),
                  par(S5_GROUP, S5_SW), par(S5_GROUP, S5_SW), par(S5_GROUP, S5_SW), par(1, LANES)],
        out_specs=io,
        out_shape=jax.ShapeDtypeStruct((seq, batch, width), F32),
        scratch_shapes=[
            pltpu.VMEM((kdim, 2 * S5_SW), BF16),
            pltpu.VMEM((kdim, 2 * S5_SW), BF16),
            pltpu.VMEM((kdim, kdim), BF16),
            pltpu.VMEM((SUBLANES, 2 * S5_SW), F32),
            pltpu.VMEM((rows, kdim), BF16),
            pltpu.VMEM((rows, 2 * S5_SW), F32),
            pltpu.VMEM((rows, 2 * S5_SW), F32),
            pltpu.VMEM((SUBLANES, 2 * S5_SW), F32),
        ],
        compiler_params=_cparams(2),
        name="s5",
    )(u3, lamr, lami, ldt, bt(b_re), bt(b_im), ct(c_re), ct(c_im), d_skip.reshape(nlb, 1, LANES))


def _even_out_body(yb_ref, h_ref, o_ref, z_ref, x_ref, hg_ref, gluw_ref, glub_ref, wout_ref,
                   out_ref, y_ref):
    half = h_ref.shape[1]
    for h in range(M_HEADS):
        vs = slice(h * M_DV, (h + 1) * M_DV)
        og = jax.nn.sigmoid(o_ref[:, vs].astype(F32)) * h_ref[:, vs].astype(F32)
        y_ref[:, vs] = (_rmsnorm(og, hg_ref[:, vs]) * _silu(z_ref[:, vs].astype(F32))).astype(BF16)
    yg = yb_ref[...]
    s = _dot(yg, gluw_ref[...]) + glub_ref[...]
    hb = yg.astype(F32) * jax.nn.sigmoid(s)
    y_ref[:, half:2 * half] = (hb * _silu(z_ref[:, half:2 * half].astype(F32))).astype(BF16)
    out_ref[...] = x_ref[...] + _dot(y_ref[...], wout_ref[...])


def _even_out(yb, hm, o, z, xf, head_g, glu_w, glu_b, w_out):
    t, d = xf.shape
    half = hm.shape[1]
    tok = lambda i: (i, 0)
    const = lambda i: (0, 0)
    return pl.pallas_call(
        _even_out_body,
        grid=(t // TOKEN_TILE,),
        in_specs=[
            pl.BlockSpec((TOKEN_TILE, half), tok),
            pl.BlockSpec((TOKEN_TILE, half), tok),
            pl.BlockSpec((TOKEN_TILE, half), tok),
            pl.BlockSpec((TOKEN_TILE, 2 * half), tok),
            pl.BlockSpec((TOKEN_TILE, d), tok),
            pl.BlockSpec((1, half), const),
            pl.BlockSpec((half, half), const),
            pl.BlockSpec((1, half), const),
            pl.BlockSpec((2 * half, d), const),
        ],
        out_specs=pl.BlockSpec((TOKEN_TILE, d), tok),
        out_shape=jax.ShapeDtypeStruct((t, d), F32),
        scratch_shapes=[pltpu.VMEM((TOKEN_TILE, 2 * half), BF16)],
        compiler_params=_cparams(1),
        name="even_out",
    )(yb, hm, o, z, xf, head_g.reshape(1, half), glu_w, glu_b.reshape(1, half), w_out)


def _proj_odd_body(x_ref, g_ref, w_ref, wg_ref, wa_ref, ba_ref, q_ref, k_ref, v_ref, z_ref, bc_ref):
    hb = _rmsnorm(x_ref[...], g_ref[...]).astype(BF16)
    _project(hb, w_ref, (q_ref, k_ref, v_ref, z_ref), 0)
    rh, rl = _split_hi_lo(_dot(hb, wg_ref[...]))
    tril = _tri_ones(G_CHUNK, True)
    for j in range(0, bc_ref.shape[1], PROJ_TN):
        cols = slice(j, j + PROJ_TN)
        wh, wl = _split_hi_lo(wa_ref[:, cols])
        pre = _dot(rh, wh) + _dot(rh, wl) + _dot(rl, wh) + ba_ref[:, cols]
        lh, ll = _split_hi_lo(_log_sigmoid(pre) * (1.0 / G_TAU))
        for c in range(0, TOKEN_TILE, G_CHUNK):
            rows = slice(c, c + G_CHUNK)
            bc_ref[rows, cols] = _dot(tril, lh[rows]) + _dot(tril, ll[rows])


def _proj_odd(xf, g, wm, wg, w_alpha, b_alpha):
    t, d = xf.shape
    dk = G_HEADS * G_DK
    assert TOKEN_TILE % G_CHUNK == 0
    tok = lambda i: (i, 0)
    const = lambda i: (0, 0)
    widths = (dk, dk, D_MIX, D_MIX, dk)
    dtypes = (BF16, BF16, BF16, BF16, F32)
    return pl.pallas_call(
        _proj_odd_body,
        grid=(t // TOKEN_TILE,),
        in_specs=[
            pl.BlockSpec((TOKEN_TILE, d), tok),
            pl.BlockSpec((1, d), const),
            pl.BlockSpec((d, wm.shape[1]), const, pipeline_mode=pl.Buffered(1)),
            pl.BlockSpec((d, G_RANK_PAD), const),
            pl.BlockSpec((G_RANK_PAD, dk), const),
            pl.BlockSpec((1, dk), const),
        ],
        out_specs=[pl.BlockSpec((TOKEN_TILE, n), tok) for n in widths],
        out_shape=[jax.ShapeDtypeStruct((t, n), dt) for n, dt in zip(widths, dtypes)],
        compiler_params=_cparams(1),
        name="proj_odd",
    )(xf, g.reshape(1, d), wm, wg, w_alpha, b_alpha.reshape(1, dk))


def _gla_body(q_ref, k_ref, v_ref, bc_ref, out_ref, s_ref):
    L = G_CHUNK

    @pl.when(pl.program_id(1) == 0)
    def _init():
        s_ref[...] = jnp.zeros_like(s_ref)

    row = lax.broadcasted_iota(jnp.int32, (L, L), 0)
    col = lax.broadcasted_iota(jnp.int32, (L, L), 1)
    causal = row >= col
    mid = L // 2 - 1

    for h in range(G_HEADS):
        ks = slice(h * G_DK, (h + 1) * G_DK)
        vs = slice(h * G_DV, (h + 1) * G_DV)
        b = bc_ref[:, ks]
        bm = b[mid:mid + 1, :]
        g = b[L - 1:L, :]
        e1 = jnp.exp(b - bm)
        e2 = jnp.exp(bm - b)
        qt = q_ref[:, ks].astype(F32) * (G_DK ** -0.5) * e1
        kt = k_ref[:, ks].astype(F32) * e2
        attn = jnp.where(causal, _dot_nt(qt.astype(BF16), kt.astype(BF16)), 0.0)
        v_h = v_ref[:, vs]
        s_prev = s_ref[h]
        qi = qt * jnp.exp(bm)
        o = _dot(attn.astype(BF16), v_h) + _dot(qi.astype(BF16), s_prev.astype(BF16))
        out_ref[:, vs] = o.astype(BF16)
        ke = kt * jnp.exp(g - bm)
        g_col = jnp.broadcast_to(jnp.exp(g), (LANES, G_DK)).T[:, 0:1]
        s_ref[h] = g_col * s_prev + _dot(ke.T.astype(BF16), v_h)


def _gla(q, k, v, bc, batch, seq):
    t = batch * seq
    L = G_CHUNK
    nc = seq // L
    dk = G_HEADS * G_DK
    dv = G_HEADS * G_DV
    tok = lambda b, c: (b * nc + c, 0)
    return pl.pallas_call(
        _gla_body,
        grid=(batch, nc),
        in_specs=[
            pl.BlockSpec((L, dk), tok),
            pl.BlockSpec((L, dk), tok),
            pl.BlockSpec((L, dv), tok),
            pl.BlockSpec((L, dk), tok),
        ],
        out_specs=pl.BlockSpec((L, dv), tok),
        out_shape=jax.ShapeDtypeStruct((t, dv), BF16),
        scratch_shapes=[pltpu.VMEM((G_HEADS, G_DK, G_DV), F32)],
        compiler_params=_cparams(2),
        name="gla",
    )(q, k, v, bc)


def _odd_out_body(o_ref, z_ref, x_ref, hg_ref, wout_ref, g_ref, out_ref, y_ref):
    for h in range(G_HEADS):
        vs = slice(h * G_DV, (h + 1) * G_DV)
        on = _rmsnorm(o_ref[:, vs].astype(F32), hg_ref[:, vs])
        y_ref[:, vs] = (on * _silu(z_ref[:, vs].astype(F32))).astype(BF16)
    x = x_ref[...] + _dot(y_ref[...], wout_ref[...])
    out_ref[...] = _rmsnorm(x, g_ref[...])


def _odd_out(o, z, x1, head_g, w_out, g):
    t, d = x1.shape
    dm = o.shape[1]
    tok = lambda i: (i, 0)
    const = lambda i: (0, 0)
    return pl.pallas_call(
        _odd_out_body,
        grid=(t // TOKEN_TILE,),
        in_specs=[
            pl.BlockSpec((TOKEN_TILE, dm), tok),
            pl.BlockSpec((TOKEN_TILE, dm), tok),
            pl.BlockSpec((TOKEN_TILE, d), tok),
            pl.BlockSpec((1, dm), const),
            pl.BlockSpec((dm, d), const),
            pl.BlockSpec((1, d), const),
        ],
        out_specs=pl.BlockSpec((TOKEN_TILE, d), tok),
        out_shape=jax.ShapeDtypeStruct((t, d), F32),
        scratch_shapes=[pltpu.VMEM((TOKEN_TILE, dm), BF16)],
        compiler_params=_cparams(1),
        name="odd_out",
    )(o, z, x1, head_g.reshape(1, dm), w_out, g.reshape(1, d))


def kernel(x, norm_g, final_norm_g, ev_w_in, ev_conv_w, ev_conv_b, ev_i_bias, ev_f_bias, ev_head_g,
           s5_lam_re, s5_lam_im, s5_log_dt, s5_b_re, s5_b_im, s5_c_re, s5_c_im, s5_d, s5_glu_w,
           s5_glu_b, ev_w_out, od_w_in, gla_w_alpha, gla_b_alpha, gla_head_g, od_w_out):
    batch, seq, d = x.shape
    t = batch * seq
    xf = x.reshape(t, d)
    padc = lambda a: jnp.pad(a, ((0, 0), (0, LANES - a.shape[1])))

    w = ev_w_in[0]
    g0 = 2 * M_HEADS * M_DK + 2 * M_HEADS * M_DV
    g1 = g0 + 2 * M_HEADS
    wm = jnp.concatenate([w[:, :g0], w[:, g1:]], axis=1).astype(BF16)
    wg = padc(w[:, g0:g1]).astype(BF16)
    half = D_MIX // 2
    qk, v, o, u, z, gates = _proj_even(xf, norm_g[0], wm, wg, ev_conv_w[0], ev_conv_b[0], seq)
    gbias = padc(jnp.concatenate([ev_i_bias[0], ev_f_bias[0]]).reshape(1, 2 * M_HEADS))
    hm = _mlstm(qk, v, gates, gbias, batch, seq)

    u3 = u.reshape(batch, seq, half).transpose(1, 0, 2)
    y3 = _s5(u3, s5_lam_re[0], s5_lam_im[0], s5_log_dt[0], s5_b_re[0], s5_b_im[0],
             s5_c_re[0], s5_c_im[0], s5_d[0])
    yb = y3.transpose(1, 0, 2).reshape(t, half).astype(BF16)
    x1 = _even_out(yb, hm, o, z, xf, ev_head_g[0], s5_glu_w[0].astype(BF16), s5_glu_b[0],
                   ev_w_out[0].astype(BF16))

    w = od_w_in[0]
    n_main = 2 * G_HEADS * G_DK + 2 * D_MIX
    wa = jnp.pad(gla_w_alpha[0], ((0, G_RANK_PAD - gla_w_alpha.shape[1]), (0, 0)))
    q, k, v, z, bc = _proj_odd(x1, norm_g[1], w[:, :n_main].astype(BF16), padc(w[:, n_main:]).astype(BF16),
                               wa, gla_b_alpha[0])
    og = _gla(q, k, v, bc, batch, seq)
    out = _odd_out(og, z, x1, gla_head_g[0], od_w_out[0].astype(BF16), final_norm_g)
    return out.reshape(batch, seq, d)
```

```python
import functools

import jax
import jax.numpy as jnp
from jax import lax
from jax.experimental import pallas as pl
from jax.experimental.pallas import tpu as pltpu

F32 = jnp.float32
BF16 = jnp.bfloat16

EPS = 1e-6
D_MODEL = 1024
D_MIX = 2 * D_MODEL
M_HEADS = 4
M_DK = 128
M_DV = 256
M_QK = 2 * M_HEADS * M_DK
CONV_WIDTH = 4
M_CHUNK = 256
M_SUB = 2
S5_GROUP = 16
S5_STATE = 64
S5_BLK = 8
S5_CHUNK = 1024
G_HEADS = 4
G_DK = 256
G_DV = 512
G_TAU = 16.0
G_CHUNK = 128
G_SUB = 2
G_RANK_PAD = 128

LANES = 128
SUBLANES = 8
HALO = 16
S5_GPL = LANES // S5_GROUP
S5_SW = S5_GPL * S5_STATE
S5_XROWS = S5_CHUNK // S5_BLK + SUBLANES
TOKEN_TILE = 512
PROJ_TN = 256
VMEM_LIMIT = 56 * 1024 * 1024


def _cparams(n_grid):
    return pltpu.CompilerParams(
        dimension_semantics=("arbitrary",) * n_grid, vmem_limit_bytes=VMEM_LIMIT)


def _log_sigmoid(x):
    return jnp.minimum(x, 0.0) - jnp.log(1.0 + jnp.exp(-jnp.abs(x)))


def _silu(x):
    return x * jax.nn.sigmoid(x)


def _split_hi_lo(x):
    hi = x.astype(BF16)
    lo = (x - hi.astype(F32)).astype(BF16)
    return hi, lo


def _dot(a, b):
    return jnp.dot(a, b, preferred_element_type=F32)


def _dot_nt(a, b, precision=None):
    return lax.dot_general(a, b, (((1,), (1,)), ((), ())), precision=precision,
                           preferred_element_type=F32)


def _tri_ones(n, lower):
    row = lax.broadcasted_iota(jnp.int32, (n, n), 0)
    col = lax.broadcasted_iota(jnp.int32, (n, n), 1)
    keep = (row >= col) if lower else (row <= col)
    return jnp.where(keep, 1.0, 0.0).astype(BF16)


def _rmsnorm(x, g):
    return x * lax.rsqrt(jnp.mean(x * x, axis=-1, keepdims=True) + EPS) * g


def _project(hb, w_ref, out_refs, col0):
    off = col0
    for o_ref in out_refs:
        n = o_ref.shape[1]
        for j in range(0, n, PROJ_TN):
            o_ref[:, j:j + PROJ_TN] = _dot(hb, w_ref[:, off + j:off + j + PROJ_TN]).astype(o_ref.dtype)
        off += n


def _proj_even_body(x_ref, xh_ref, g_ref, w_ref, wg_ref, convw_ref, convb_ref,
                    qk_ref, v_ref, o_ref, u_ref, z_ref, gates_ref, ext_ref, *, tiles_per_seq):
    hb = _rmsnorm(x_ref[...], g_ref[...]).astype(BF16)
    hh = _rmsnorm(xh_ref[...], g_ref[...]).astype(BF16)
    seq_start = lax.rem(pl.program_id(0), tiles_per_seq) == 0
    base = HALO - (CONV_WIDTH - 1)
    lane = lax.broadcasted_iota(jnp.int32, (1, PROJ_TN), 1)
    for j in range(0, M_QK, PROJ_TN):
        cols = slice(j, j + PROJ_TN)
        wj = w_ref[:, cols]
        ext_ref[HALO:HALO + TOKEN_TILE, :] = _dot(hb, wj)
        ext_ref[0:HALO, :] = jnp.where(seq_start, 0.0, _dot(hh, wj))
        acc = convb_ref[:, cols] + convw_ref[0:1, cols] * ext_ref[base:base + TOKEN_TILE, :]
        for i in range(1, CONV_WIDTH):
            acc = acc + convw_ref[i:i + 1, cols] * ext_ref[base + i:base + i + TOKEN_TILE, :]
        scale = jnp.where(lane + j < M_HEADS * M_DK, M_DK ** -0.5, 1.0)
        qk_ref[:, cols] = (_silu(acc) * scale).astype(BF16)
    _project(hb, w_ref, (v_ref, o_ref, u_ref, z_ref), M_QK)
    gates_ref[...] = _dot(hb, wg_ref[...])


def _proj_even(xf, g, wm, wg, conv_w, conv_b, seq):
    t, d = xf.shape
    half = D_MIX // 2
    assert seq % TOKEN_TILE == 0 and TOKEN_TILE % HALO == 0
    tok = lambda i: (i, 0)
    const = lambda i: (0, 0)
    per_halo = TOKEN_TILE // HALO
    widths = (M_QK, half, half, half, D_MIX)
    dtypes = (BF16, BF16, BF16, F32, BF16)
    out_shape = [jax.ShapeDtypeStruct((t, n), dt) for n, dt in zip(widths, dtypes)]
    out_shape.append(jax.ShapeDtypeStruct((t, LANES), F32))
    out_specs = [pl.BlockSpec((TOKEN_TILE, n), tok) for n in widths]
    out_specs.append(pl.BlockSpec((TOKEN_TILE, LANES), tok))
    return pl.pallas_call(
        functools.partial(_proj_even_body, tiles_per_seq=seq // TOKEN_TILE),
        grid=(t // TOKEN_TILE,),
        in_specs=[
            pl.BlockSpec((TOKEN_TILE, d), tok),
            pl.BlockSpec((HALO, d), lambda i: (jnp.maximum(i * per_halo - 1, 0), 0)),
            pl.BlockSpec((1, d), const),
            pl.BlockSpec((d, wm.shape[1]), const, pipeline_mode=pl.Buffered(1)),
            pl.BlockSpec((d, LANES), const),
            pl.BlockSpec((CONV_WIDTH, M_QK), const),
            pl.BlockSpec((1, M_QK), const),
        ],
        out_specs=out_specs,
        out_shape=out_shape,
        scratch_shapes=[pltpu.VMEM((HALO + TOKEN_TILE, PROJ_TN), F32)],
        compiler_params=_cparams(1),
        name="proj_even",
    )(xf, xf, g.reshape(1, d), wm, wg, conv_w, conv_b.reshape(1, M_QK))


def _mlstm_body(qk_ref, v_ref, gates_ref, gbias_ref, out_ref, c_ref, n_ref, m_ref):
    @pl.when(pl.program_id(1) == 0)
    def _init():
        c_ref[...] = jnp.zeros_like(c_ref)
        n_ref[...] = jnp.zeros_like(n_ref)
        m_ref[...] = jnp.zeros_like(m_ref)

    for sub in range(M_SUB):
        _mlstm_chunk(slice(sub * M_CHUNK, (sub + 1) * M_CHUNK), qk_ref, v_ref, gates_ref, gbias_ref,
                     out_ref, c_ref, n_ref, m_ref)


def _mlstm_chunk(r, qk_ref, v_ref, gates_ref, gbias_ref, out_ref, c_ref, n_ref, m_ref):
    L = M_CHUNK
    gt = gates_ref[r, :] + gbias_ref[...]
    lane = lax.broadcasted_iota(jnp.int32, gt.shape, 1)
    gl = jnp.where(lane >= M_HEADS, _log_sigmoid(gt), gt)
    gl_t = gl.T
    bcol = _dot(_tri_ones(L, True), gl.astype(BF16))
    brow = _dot(gl_t[0:2 * SUBLANES, :].astype(BF16), _tri_ones(L, False))

    row = lax.broadcasted_iota(jnp.int32, (L, L), 0)
    col = lax.broadcasted_iota(jnp.int32, (L, L), 1)
    causal = row >= col

    for h in range(M_HEADS):
        ks = slice(h * M_DK, (h + 1) * M_DK)
        ks2 = slice(M_HEADS * M_DK + h * M_DK, M_HEADS * M_DK + (h + 1) * M_DK)
        vs = slice(h * M_DV, (h + 1) * M_DV)
        fl = M_HEADS + h
        b_t = bcol[:, fl:fl + 1]
        i_t = gl[:, h:h + 1]
        b_s = brow[fl:fl + 1, :]
        i_s = gl_t[h:h + 1, :]
        g = bcol[L - 1:L, fl:fl + 1]
        m_prev = m_ref[h, 0:1, 0:1]
        c_prev = c_ref[h]
        n_prev = n_ref[h, 0:1, :]

        dmat = jnp.where(causal, b_t - b_s + i_s, -jnp.inf)
        inter = b_t + m_prev
        m_t = jnp.maximum(inter, jnp.max(dmat, axis=1, keepdims=True))
        w_intra = jnp.exp(dmat - m_t)
        w_inter = jnp.exp(inter - m_t)

        qb = qk_ref[r, ks]
        kb = qk_ref[r, ks2]
        v_h = v_ref[r, vs]
        sc = w_intra * _dot_nt(qb, kb)
        num = _dot(sc.astype(BF16), v_h) + w_inter * _dot(qb, c_prev.astype(BF16))
        den = (jnp.sum(sc, axis=1, keepdims=True)
               + w_inter * jnp.sum(qb.astype(F32) * n_prev, axis=1, keepdims=True))
        out_ref[r, vs] = (num / jnp.maximum(jnp.abs(den), jnp.exp(-m_t))).astype(BF16)

        a = g - b_t + i_t
        m_loc = jnp.max(a, axis=0, keepdims=True)
        w_loc = jnp.exp(a - m_loc)
        kw = kb.astype(F32) * w_loc
        c_loc = _dot(kw.T.astype(BF16), v_h)
        n_loc = jnp.sum(kw, axis=0, keepdims=True)
        m_new = jnp.maximum(g + m_prev, m_loc)
        s_prev = jnp.exp(g + m_prev - m_new)
        s_loc = jnp.exp(m_loc - m_new)
        c_ref[h] = s_prev * c_prev + s_loc * c_loc
        n_ref[h] = jnp.broadcast_to(s_prev * n_prev + s_loc * n_loc, (SUBLANES, M_DK))
        m_ref[h] = jnp.broadcast_to(m_new, (SUBLANES, LANES))


def _mlstm(qk, v, gates, gbias, batch, seq):
    t = batch * seq
    L = M_CHUNK * M_SUB
    nc = seq // L
    dv = M_HEADS * M_DV
    tok = lambda b, c: (b * nc + c, 0)
    const = lambda b, c: (0, 0)
    return pl.pallas_call(
        _mlstm_body,
        grid=(batch, nc),
        in_specs=[
            pl.BlockSpec((L, M_QK), tok),
            pl.BlockSpec((L, dv), tok),
            pl.BlockSpec((L, LANES), tok),
            pl.BlockSpec((1, LANES), const),
        ],
        out_specs=pl.BlockSpec((L, dv), tok),
        out_shape=jax.ShapeDtypeStruct((t, dv), BF16),
        scratch_shapes=[
            pltpu.VMEM((M_HEADS, M_DK, M_DV), F32),
            pltpu.VMEM((M_HEADS, SUBLANES, M_DK), F32),
            pltpu.VMEM((M_HEADS, SUBLANES, LANES), F32),
        ],
        compiler_params=_cparams(2),
        name="mlstm",
    )(qk, v, gates, gbias)


def _gelu_tanh(x):
    return 0.5 * x * (1.0 + jnp.tanh(0.7978845608028654 * (x + 0.044715 * (x * x * x))))


def _s5_build_operators(lamr_ref, lami_ref, ldt_ref, btr_ref, bti_ref, ctr_ref, cti_ref, d_ref,
                        pbig_ref, qbig_ref, mbig_ref, a_ref):
    lr = lamr_ref[0]
    li = lami_ref[0]
    dt = jnp.exp(ldt_ref[0])
    zr = lr * dt
    th = li * dt
    er = jnp.exp(zr)
    ar = er * jnp.cos(th)
    ai = er * jnp.sin(th)
    den = lr * lr + li * li
    beta_r = ((ar - 1.0) * lr + ai * li) / den
    beta_i = (ai * lr - (ar - 1.0) * li) / den
    btr = btr_ref[0]
    bti = bti_ref[0]
    bbr = btr * beta_r - bti * beta_i
    bbi = btr * beta_i + bti * beta_r
    ctr = ctr_ref[0]
    cti = cti_ref[0]

    row_g = lax.shift_right_logical(lax.broadcasted_iota(jnp.int32, (LANES, S5_SW), 0), 4)
    lane_g = lax.shift_right_logical(lax.broadcasted_iota(jnp.int32, (LANES, S5_SW), 1), 6)
    same_group = row_g == lane_g

    def expand(x16):
        return jnp.where(same_group, jnp.concatenate([x16] * S5_GPL, axis=0), 0.0)

    def power(k):
        e = jnp.exp(float(k) * zr)
        return e * jnp.cos(float(k) * th), e * jnp.sin(float(k) * th)

    for s in range(S5_BLK):
        rows = slice(s * LANES, (s + 1) * LANES)
        pr, pi = power(S5_BLK - 1 - s)
        pbig_ref[rows, 0:S5_SW] = expand(pr * bbr - pi * bbi).astype(BF16)
        pbig_ref[rows, S5_SW:2 * S5_SW] = expand(pr * bbi + pi * bbr).astype(BF16)
        pr, pi = power(s + 1)
        qbig_ref[rows, 0:S5_SW] = expand(ctr * pr - cti * pi).astype(BF16)
        qbig_ref[rows, S5_SW:2 * S5_SW] = expand(-(ctr * pi + cti * pr)).astype(BF16)

    cb = jnp.concatenate([expand(ctr), expand(-cti)], axis=1)
    r128 = lax.broadcasted_iota(jnp.int32, (LANES, LANES), 0)
    c128 = lax.broadcasted_iota(jnp.int32, (LANES, LANES), 1)
    zero_blk = jnp.zeros((LANES, LANES), BF16)
    for lag in range(S5_BLK):
        pr, pi = power(lag)
        ab = jnp.concatenate([expand(pr * bbr - pi * bbi), expand(pr * bbi + pi * bbr)], axis=1)
        v = _dot_nt(ab, cb, precision=lax.Precision.HIGHEST)
        if lag == 0:
            v = v + jnp.where(r128 == c128, d_ref[0], 0.0)
        vb = v.astype(BF16)
        for s in range(S5_BLK - lag):
            t = s + lag
            mbig_ref[s * LANES:(s + 1) * LANES, t * LANES:(t + 1) * LANES] = vb
            if lag > 0:
                mbig_ref[t * LANES:(t + 1) * LANES, s * LANES:(s + 1) * LANES] = zero_blk

    pr, pi = power(S5_BLK)
    a_ref[:, 0:S5_SW] = jnp.broadcast_to(pr, (SUBLANES, S5_SW))
    a_ref[:, S5_SW:2 * S5_SW] = jnp.broadcast_to(pi, (SUBLANES, S5_SW))


def _s5_body(u_ref, lamr_ref, lami_ref, ldt_ref, btr_ref, bti_ref, ctr_ref, cti_ref, d_ref, y_ref,
             pbig_ref, qbig_ref, mbig_ref, a_ref, ucat_ref, x_ref, xp_ref, st_ref):
    nblk = S5_CHUNK // S5_BLK
    batch = u_ref.shape[0]

    @pl.when(pl.program_id(1) == 0)
    def _setup():
        _s5_build_operators(lamr_ref, lami_ref, ldt_ref, btr_ref, bti_ref, ctr_ref, cti_ref, d_ref,
                            pbig_ref, qbig_ref, mbig_ref, a_ref)
        st_ref[...] = jnp.zeros_like(st_ref)

    for b in range(batch):
        for s in range(S5_BLK):
            piece = u_ref[b, pl.ds(s, nblk, stride=S5_BLK), :]
            ucat_ref[b * nblk:(b + 1) * nblk, s * LANES:(s + 1) * LANES] = piece.astype(BF16)
    nslab = 2 * S5_SW // LANES
    half = nslab // 2
    xloc = _dot(ucat_ref[...], pbig_ref[...])
    for c in range(nslab):
        for b in range(batch):
            x_ref[c, b * S5_XROWS:b * S5_XROWS + nblk, :] = (
                xloc[b * nblk:(b + 1) * nblk, c * LANES:(c + 1) * LANES])

    lanes = lambda ref, c: ref[:, c * LANES:(c + 1) * LANES]
    ar = [lanes(a_ref, c) for c in range(half)]
    ai = [lanes(a_ref, half + c) for c in range(half)]
    xr = [lanes(st_ref, c) for c in range(half)]
    xi = [lanes(st_ref, half + c) for c in range(half)]
    for blk in range(nblk):
        r = pl.ds(blk, batch, stride=S5_XROWS)
        for c in range(half):
            xp_ref[c, r, :] = xr[c]
            xp_ref[half + c, r, :] = xi[c]
            nr = ar[c] * xr[c] - ai[c] * xi[c] + x_ref[c, r, :]
            ni = ar[c] * xi[c] + ai[c] * xr[c] + x_ref[half + c, r, :]
            xr[c], xi[c] = nr, ni
    for c in range(half):
        st_ref[:, c * LANES:(c + 1) * LANES] = xr[c]
        st_ref[:, (half + c) * LANES:(half + c + 1) * LANES] = xi[c]

    xpb = jnp.concatenate(
        [jnp.concatenate([xp_ref[c, b * S5_XROWS:b * S5_XROWS + nblk, :].astype(BF16)
                          for c in range(nslab)], axis=1) for b in range(batch)], axis=0)
    width = 2 * LANES
    for nb in range(S5_BLK // 2):
        kk = (2 * nb + 2) * LANES
        cols = slice(nb * width, (nb + 1) * width)
        y = _dot(ucat_ref[:, 0:kk], mbig_ref[0:kk, cols]) + _dot_nt(xpb, qbig_ref[cols, :])
        y = _gelu_tanh(y)
        for tt in range(2):
            t = 2 * nb + tt
            for b in range(batch):
                y_ref[b, pl.ds(t, nblk, stride=S5_BLK), :] = (
                    y[b * nblk:(b + 1) * nblk, tt * LANES:(tt + 1) * LANES])


def _s5(u3, lam_re, lam_im, log_dt, b_re, b_im, c_re, c_im, d_skip):
    batch, seq, width = u3.shape
    nlb = width // LANES
    assert batch == SUBLANES and seq % S5_CHUNK == 0
    lamr = lam_re.reshape(nlb, 1, S5_SW)
    lami = lam_im.reshape(nlb, 1, S5_SW)
    ldt = jnp.repeat(log_dt, S5_STATE).reshape(nlb, 1, S5_SW)
    bt = lambda b: b.reshape(nlb, S5_GPL, S5_STATE, S5_GROUP).transpose(0, 3, 1, 2).reshape(nlb, S5_GROUP, S5_SW)
    ct = lambda c: c.reshape(nlb, S5_GPL, S5_GROUP, S5_STATE).transpose(0, 2, 1, 3).reshape(nlb, S5_GROUP, S5_SW)
    par = lambda r, w: pl.BlockSpec((1, r, w), lambda i, j: (i, 0, 0))
    kdim = S5_BLK * LANES
    rows = (S5_CHUNK // S5_BLK) * batch
    io = pl.BlockSpec((batch, S5_CHUNK, LANES), lambda i, j: (0, j, i))
    return pl.pallas_call(
        _s5_body,
        grid=(nlb, seq // S5_CHUNK),
        in_specs=[io, par(1, S5_SW), par(1, S5_SW), par(1, S5_SW), par(S5_GROUP, S5_SW),
                  par(S5_GROUP, S5_SW), par(S5_GROUP, S5_SW), par(S5_GROUP, S5_SW), par(1, LANES)],
        out_specs=io,
        out_shape=jax.ShapeDtypeStruct((batch, seq, width), F32),
        scratch_shapes=[
            pltpu.VMEM((kdim, 2 * S5_SW), BF16),
            pltpu.VMEM((kdim, 2 * S5_SW), BF16),
            pltpu.VMEM((kdim, kdim), BF16),
            pltpu.VMEM((SUBLANES, 2 * S5_SW), F32),
            pltpu.VMEM((rows, kdim), BF16),
            pltpu.VMEM((2 * S5_SW // LANES, batch * S5_XROWS, LANES), F32),
            pltpu.VMEM((2 * S5_SW // LANES, batch * S5_XROWS, LANES), F32),
            pltpu.VMEM((SUBLANES, 2 * S5_SW), F32),
        ],
        compiler_params=_cparams(2),
        name="s5",
    )(u3, lamr, lami, ldt, bt(b_re), bt(b_im), ct(c_re), ct(c_im), d_skip.reshape(nlb, 1, LANES))


def _even_out_body(yb_ref, h_ref, o_ref, z_ref, x_ref, hg_ref, gluw_ref, glub_ref, wout_ref,
                   out_ref, y_ref):
    half = h_ref.shape[1]
    for h in range(M_HEADS):
        vs = slice(h * M_DV, (h + 1) * M_DV)
        og = jax.nn.sigmoid(o_ref[:, vs].astype(F32)) * h_ref[:, vs].astype(F32)
        y_ref[:, vs] = (_rmsnorm(og, hg_ref[:, vs]) * _silu(z_ref[:, vs].astype(F32))).astype(BF16)
    yg = yb_ref[...]
    s = _dot(yg.astype(BF16), gluw_ref[...]) + glub_ref[...]
    hb = yg * jax.nn.sigmoid(s)
    y_ref[:, half:2 * half] = (hb * _silu(z_ref[:, half:2 * half].astype(F32))).astype(BF16)
    out_ref[...] = x_ref[...] + _dot(y_ref[...], wout_ref[...])


def _even_out(yb, hm, o, z, xf, head_g, glu_w, glu_b, w_out):
    t, d = xf.shape
    half = hm.shape[1]
    tok = lambda i: (i, 0)
    const = lambda i: (0, 0)
    return pl.pallas_call(
        _even_out_body,
        grid=(t // TOKEN_TILE,),
        in_specs=[
            pl.BlockSpec((TOKEN_TILE, half), tok),
            pl.BlockSpec((TOKEN_TILE, half), tok),
            pl.BlockSpec((TOKEN_TILE, half), tok),
            pl.BlockSpec((TOKEN_TILE, 2 * half), tok),
            pl.BlockSpec((TOKEN_TILE, d), tok),
            pl.BlockSpec((1, half), const),
            pl.BlockSpec((half, half), const),
            pl.BlockSpec((1, half), const),
            pl.BlockSpec((2 * half, d), const),
        ],
        out_specs=pl.BlockSpec((TOKEN_TILE, d), tok),
        out_shape=jax.ShapeDtypeStruct((t, d), F32),
        scratch_shapes=[pltpu.VMEM((TOKEN_TILE, 2 * half), BF16)],
        compiler_params=_cparams(1),
        name="even_out",
    )(yb, hm, o, z, xf, head_g.reshape(1, half), glu_w, glu_b.reshape(1, half), w_out)


def _proj_odd_body(x_ref, g_ref, w_ref, wg_ref, wa_ref, ba_ref, q_ref, k_ref, v_ref, z_ref, bc_ref):
    hb = _rmsnorm(x_ref[...], g_ref[...]).astype(BF16)
    _project(hb, w_ref, (q_ref, k_ref, v_ref, z_ref), 0)
    rb = _dot(hb, wg_ref[...]).astype(BF16)
    tril = _tri_ones(G_CHUNK, True)
    for j in range(0, bc_ref.shape[1], PROJ_TN):
        cols = slice(j, j + PROJ_TN)
        pre = _dot(rb, wa_ref[:, cols].astype(BF16)) + ba_ref[:, cols]
        la = (_log_sigmoid(pre) * (1.0 / G_TAU)).astype(BF16)
        for c in range(0, TOKEN_TILE, G_CHUNK):
            rows = slice(c, c + G_CHUNK)
            bc_ref[rows, cols] = _dot(tril, la[rows])


def _proj_odd(xf, g, wm, wg, w_alpha, b_alpha):
    t, d = xf.shape
    dk = G_HEADS * G_DK
    assert TOKEN_TILE % G_CHUNK == 0
    tok = lambda i: (i, 0)
    const = lambda i: (0, 0)
    widths = (dk, dk, D_MIX, D_MIX, dk)
    dtypes = (BF16, BF16, BF16, BF16, F32)
    return pl.pallas_call(
        _proj_odd_body,
        grid=(t // TOKEN_TILE,),
        in_specs=[
            pl.BlockSpec((TOKEN_TILE, d), tok),
            pl.BlockSpec((1, d), const),
            pl.BlockSpec((d, wm.shape[1]), const, pipeline_mode=pl.Buffered(1)),
            pl.BlockSpec((d, G_RANK_PAD), const),
            pl.BlockSpec((G_RANK_PAD, dk), const),
            pl.BlockSpec((1, dk), const),
        ],
        out_specs=[pl.BlockSpec((TOKEN_TILE, n), tok) for n in widths],
        out_shape=[jax.ShapeDtypeStruct((t, n), dt) for n, dt in zip(widths, dtypes)],
        compiler_params=_cparams(1),
        name="proj_odd",
    )(xf, g.reshape(1, d), wm, wg, w_alpha, b_alpha.reshape(1, dk))


def _gla_body(q_ref, k_ref, v_ref, bc_ref, out_ref, s_ref):
    L = G_CHUNK

    @pl.when(pl.program_id(1) == 0)
    def _init():
        s_ref[...] = jnp.zeros_like(s_ref)

    row = lax.broadcasted_iota(jnp.int32, (L, L), 0)
    col = lax.broadcasted_iota(jnp.int32, (L, L), 1)
    causal = row >= col
    mid = L // 2 - 1

    for sub, h in [(sub, h) for sub in range(G_SUB) for h in range(G_HEADS)]:
        r = slice(sub * L, (sub + 1) * L)
        ks = slice(h * G_DK, (h + 1) * G_DK)
        vs = slice(h * G_DV, (h + 1) * G_DV)
        b = bc_ref[r, ks]
        bm = b[mid:mid + 1, :]
        g = b[L - 1:L, :]
        e1 = jnp.exp(b - bm)
        e2 = jnp.exp(bm - b)
        qt = q_ref[r, ks].astype(F32) * (G_DK ** -0.5) * e1
        kt = k_ref[r, ks].astype(F32) * e2
        attn = jnp.where(causal, _dot_nt(qt.astype(BF16), kt.astype(BF16)), 0.0)
        v_h = v_ref[r, vs]
        s_prev = s_ref[h]
        qi = qt * jnp.exp(bm)
        o = _dot(attn.astype(BF16), v_h) + _dot(qi.astype(BF16), s_prev.astype(BF16))
        out_ref[r, vs] = o.astype(BF16)
        ke = kt * jnp.exp(g - bm)
        g_col = jnp.broadcast_to(jnp.exp(g), (LANES, G_DK)).T[:, 0:1]
        s_ref[h] = g_col * s_prev + _dot(ke.T.astype(BF16), v_h)


def _gla(q, k, v, bc, batch, seq):
    t = batch * seq
    L = G_CHUNK * G_SUB
    nc = seq // L
    dk = G_HEADS * G_DK
    dv = G_HEADS * G_DV
    tok = lambda b, c: (b * nc + c, 0)
    return pl.pallas_call(
        _gla_body,
        grid=(batch, nc),
        in_specs=[
            pl.BlockSpec((L, dk), tok),
            pl.BlockSpec((L, dk), tok),
            pl.BlockSpec((L, dv), tok),
            pl.BlockSpec((L, dk), tok),
        ],
        out_specs=pl.BlockSpec((L, dv), tok),
        out_shape=jax.ShapeDtypeStruct((t, dv), BF16),
        scratch_shapes=[pltpu.VMEM((G_HEADS, G_DK, G_DV), F32)],
        compiler_params=_cparams(2),
        name="gla",
    )(q, k, v, bc)


def _odd_out_body(o_ref, z_ref, x_ref, hg_ref, wout_ref, g_ref, out_ref, y_ref):
    for h in range(G_HEADS):
        vs = slice(h * G_DV, (h + 1) * G_DV)
        on = _rmsnorm(o_ref[:, vs].astype(F32), hg_ref[:, vs])
        y_ref[:, vs] = (on * _silu(z_ref[:, vs].astype(F32))).astype(BF16)
    x = x_ref[...] + _dot(y_ref[...], wout_ref[...])
    out_ref[...] = _rmsnorm(x, g_ref[...])


def _odd_out(o, z, x1, head_g, w_out, g):
    t, d = x1.shape
    dm = o.shape[1]
    tok = lambda i: (i, 0)
    const = lambda i: (0, 0)
    return pl.pallas_call(
        _odd_out_body,
        grid=(t // TOKEN_TILE,),
        in_specs=[
            pl.BlockSpec((TOKEN_TILE, dm), tok),
            pl.BlockSpec((TOKEN_TILE, dm), tok),
            pl.BlockSpec((TOKEN_TILE, d), tok),
            pl.BlockSpec((1, dm), const),
            pl.BlockSpec((dm, d), const),
            pl.BlockSpec((1, d), const),
        ],
        out_specs=pl.BlockSpec((TOKEN_TILE, d), tok),
        out_shape=jax.ShapeDtypeStruct((t, d), F32),
        scratch_shapes=[pltpu.VMEM((TOKEN_TILE, dm), BF16)],
        compiler_params=_cparams(1),
        name="odd_out",
    )(o, z, x1, head_g.reshape(1, dm), w_out, g.reshape(1, d))


def kernel(x, norm_g, final_norm_g, ev_w_in, ev_conv_w, ev_conv_b, ev_i_bias, ev_f_bias, ev_head_g,
           s5_lam_re, s5_lam_im, s5_log_dt, s5_b_re, s5_b_im, s5_c_re, s5_c_im, s5_d, s5_glu_w,
           s5_glu_b, ev_w_out, od_w_in, gla_w_alpha, gla_b_alpha, gla_head_g, od_w_out):
    batch, seq, d = x.shape
    t = batch * seq
    xf = x.reshape(t, d)
    padc = lambda a: jnp.pad(a, ((0, 0), (0, LANES - a.shape[1])))

    w = ev_w_in[0]
    g0 = 2 * M_HEADS * M_DK + 2 * M_HEADS * M_DV
    g1 = g0 + 2 * M_HEADS
    wm = jnp.concatenate([w[:, :g0], w[:, g1:]], axis=1).astype(BF16)
    wg = padc(w[:, g0:g1]).astype(BF16)
    half = D_MIX // 2
    qk, v, o, u, z, gates = _proj_even(xf, norm_g[0], wm, wg, ev_conv_w[0], ev_conv_b[0], seq)
    gbias = padc(jnp.concatenate([ev_i_bias[0], ev_f_bias[0]]).reshape(1, 2 * M_HEADS))
    hm = _mlstm(qk, v, gates, gbias, batch, seq)

    y3 = _s5(u.reshape(batch, seq, half), s5_lam_re[0], s5_lam_im[0], s5_log_dt[0], s5_b_re[0],
             s5_b_im[0], s5_c_re[0], s5_c_im[0], s5_d[0])
    yb = y3.reshape(t, half)
    x1 = _even_out(yb, hm, o, z, xf, ev_head_g[0], s5_glu_w[0].astype(BF16), s5_glu_b[0],
                   ev_w_out[0].astype(BF16))

    w = od_w_in[0]
    n_main = 2 * G_HEADS * G_DK + 2 * D_MIX
    wa = jnp.pad(gla_w_alpha[0], ((0, G_RANK_PAD - gla_w_alpha.shape[1]), (0, 0)))
    q, k, v, z, bc = _proj_odd(x1, norm_g[1], w[:, :n_main].astype(BF16), padc(w[:, n_main:]).astype(BF16),
                               wa, gla_b_alpha[0])
    og = _gla(q, k, v, bc, batch, seq)
    out = _odd_out(og, z, x1, gla_head_g[0], od_w_out[0].astype(BF16), final_norm_g)
    return out.reshape(batch, seq, d)
```

```python
import functools

import jax
import jax.numpy as jnp
from jax import lax
from jax.experimental import pallas as pl
from jax.experimental.pallas import tpu as pltpu

F32 = jnp.float32
BF16 = jnp.bfloat16

EPS = 1e-6
D_MODEL = 1024
D_MIX = 2 * D_MODEL
M_HEADS = 4
M_DK = 128
M_DV = 256
M_QK = 2 * M_HEADS * M_DK
CONV_WIDTH = 4
M_CHUNK = 256
M_SUB = 2
S5_GROUP = 16
S5_STATE = 64
S5_BLK = 8
S5_CHUNK = 1024
G_HEADS = 4
G_DK = 256
G_DV = 512
G_TAU = 16.0
G_CHUNK = 128
G_SUB = 2
G_RANK_PAD = 128

LANES = 128
SUBLANES = 8
HALO = 16
S5_GPL = LANES // S5_GROUP
S5_SW = S5_GPL * S5_STATE
S5_XROWS = S5_CHUNK // S5_BLK + SUBLANES
TOKEN_TILE = 512
PROJ_TN = 256
VMEM_LIMIT = 56 * 1024 * 1024


def _cparams(n_grid):
    return pltpu.CompilerParams(
        dimension_semantics=("arbitrary",) * n_grid, vmem_limit_bytes=VMEM_LIMIT)


def _log_sigmoid(x):
    return jnp.minimum(x, 0.0) - jnp.log(1.0 + jnp.exp(-jnp.abs(x)))


def _silu(x):
    return x * jax.nn.sigmoid(x)


def _split_hi_lo(x):
    hi = x.astype(BF16)
    lo = (x - hi.astype(F32)).astype(BF16)
    return hi, lo


def _dot(a, b):
    return jnp.dot(a, b, preferred_element_type=F32)


def _dot_nt(a, b, precision=None):
    return lax.dot_general(a, b, (((1,), (1,)), ((), ())), precision=precision,
                           preferred_element_type=F32)


def _tri_ones(n, lower):
    row = lax.broadcasted_iota(jnp.int32, (n, n), 0)
    col = lax.broadcasted_iota(jnp.int32, (n, n), 1)
    keep = (row >= col) if lower else (row <= col)
    return jnp.where(keep, 1.0, 0.0).astype(BF16)


def _rmsnorm(x, g):
    return x * lax.rsqrt(jnp.mean(x * x, axis=-1, keepdims=True) + EPS) * g


def _project(hb, w_ref, out_refs, col0):
    off = col0
    for o_ref in out_refs:
        n = o_ref.shape[1]
        for j in range(0, n, PROJ_TN):
            o_ref[:, j:j + PROJ_TN] = _dot(hb, w_ref[:, off + j:off + j + PROJ_TN]).astype(o_ref.dtype)
        off += n


def _proj_even_body(x_ref, xh_ref, g_ref, w_ref, wb_ref, wg_ref, convw_ref, convb_ref,
                    qk_ref, v_ref, o_ref, u_ref, z_ref, gates_ref, ext_ref, *, tiles_per_seq):
    hb = _rmsnorm(x_ref[...], g_ref[...]).astype(BF16)
    hh = _rmsnorm(xh_ref[...], g_ref[...]).astype(BF16)
    seq_start = lax.rem(pl.program_id(0), tiles_per_seq) == 0
    base = HALO - (CONV_WIDTH - 1)
    lane = lax.broadcasted_iota(jnp.int32, (1, PROJ_TN), 1)
    for j in range(0, M_QK, PROJ_TN):
        cols = slice(j, j + PROJ_TN)
        wj = w_ref[:, cols]
        ext_ref[HALO:HALO + TOKEN_TILE, :] = _dot(hb, wj)
        ext_ref[0:HALO, :] = jnp.where(seq_start, 0.0, _dot(hh, wj))
        acc = convb_ref[:, cols] + convw_ref[0:1, cols] * ext_ref[base:base + TOKEN_TILE, :]
        for i in range(1, CONV_WIDTH):
            acc = acc + convw_ref[i:i + 1, cols] * ext_ref[base + i:base + i + TOKEN_TILE, :]
        scale = jnp.where(lane + j < M_HEADS * M_DK, M_DK ** -0.5, 1.0)
        qk_ref[:, cols] = (_silu(acc) * scale).astype(BF16)
    _project(hb, w_ref, (v_ref, o_ref), M_QK)
    _project(hb, wb_ref, (u_ref, z_ref), 0)
    gates_ref[...] = _dot(hb, wg_ref[...])


def _proj_even(xf, g, wm, wb, wg, conv_w, conv_b, seq):
    t, d = xf.shape
    half = D_MIX // 2
    assert seq % TOKEN_TILE == 0 and TOKEN_TILE % HALO == 0
    tok = lambda i: (i, 0)
    const = lambda i: (0, 0)
    per_halo = TOKEN_TILE // HALO
    widths = (M_QK, half, half, half, D_MIX)
    dtypes = (BF16, BF16, BF16, F32, BF16)
    out_shape = [jax.ShapeDtypeStruct((t, n), dt) for n, dt in zip(widths, dtypes)]
    ngate = wg.shape[1]
    out_shape.append(jax.ShapeDtypeStruct((t, ngate), F32))
    out_specs = [pl.BlockSpec((TOKEN_TILE, n), tok) for n in widths]
    out_specs.append(pl.BlockSpec((TOKEN_TILE, ngate), tok))
    return pl.pallas_call(
        functools.partial(_proj_even_body, tiles_per_seq=seq // TOKEN_TILE),
        grid=(t // TOKEN_TILE,),
        in_specs=[
            pl.BlockSpec((TOKEN_TILE, d), tok),
            pl.BlockSpec((HALO, d), lambda i: (jnp.maximum(i * per_halo - 1, 0), 0)),
            pl.BlockSpec((1, d), const),
            pl.BlockSpec((d, wm.shape[1]), const, pipeline_mode=pl.Buffered(1)),
            pl.BlockSpec((d, wb.shape[1]), const, pipeline_mode=pl.Buffered(1)),
            pl.BlockSpec((d, ngate), const),
            pl.BlockSpec((CONV_WIDTH, M_QK), const),
            pl.BlockSpec((1, M_QK), const),
        ],
        out_specs=out_specs,
        out_shape=out_shape,
        scratch_shapes=[pltpu.VMEM((HALO + TOKEN_TILE, PROJ_TN), F32)],
        compiler_params=_cparams(1),
        name="proj_even",
    )(xf, xf, g.reshape(1, d), wm, wb, wg, conv_w, conv_b.reshape(1, M_QK))


def _mlstm_body(qk_ref, v_ref, gates_ref, gbias_ref, out_ref, c_ref, n_ref, m_ref):
    @pl.when(pl.program_id(1) == 0)
    def _init():
        c_ref[...] = jnp.zeros_like(c_ref)
        n_ref[...] = jnp.zeros_like(n_ref)
        m_ref[...] = jnp.zeros_like(m_ref)

    for sub in range(M_SUB):
        _mlstm_chunk(slice(sub * M_CHUNK, (sub + 1) * M_CHUNK), qk_ref, v_ref, gates_ref, gbias_ref,
                     out_ref, c_ref, n_ref, m_ref)


def _mlstm_chunk(r, qk_ref, v_ref, gates_ref, gbias_ref, out_ref, c_ref, n_ref, m_ref):
    L = M_CHUNK
    gt = gates_ref[r, :] + gbias_ref[...]
    ipre = gt[:, 0:LANES]
    logf = _log_sigmoid(gt[:, LANES:2 * LANES]).astype(BF16)
    b = _dot(_tri_ones(L, True), logf)
    w = ipre - b
    rows = lax.broadcasted_iota(jnp.int32, (L, LANES), 0)
    cm = w
    k = 1
    while k < L:
        cm = jnp.maximum(cm, jnp.where(rows >= k, pltpu.roll(cm, k, axis=0), -jnp.inf))
        k *= 2

    sel_r = lax.broadcasted_iota(jnp.int32, (LANES, M_HEADS * LANES), 0)
    sel_c = lax.broadcasted_iota(jnp.int32, (LANES, M_HEADS * LANES), 1)
    spread = jnp.where(sel_r == lax.shift_right_logical(sel_c, 7), 1.0, 0.0).astype(BF16)

    def replicate(x):
        hi, lo = _split_hi_lo(x)
        return _dot(hi, spread) + _dot(lo, spread)

    b_rep = replicate(b)
    w_rep = replicate(w)
    cm_rep = replicate(cm)
    pick_r = lax.broadcasted_iota(jnp.int32, (M_HEADS * SUBLANES, LANES), 0)
    pick_c = lax.broadcasted_iota(jnp.int32, (M_HEADS * SUBLANES, LANES), 1)
    pick = jnp.where(lax.shift_right_logical(pick_r, 3) == pick_c, 1.0, 0.0).astype(BF16)
    w_hi, w_lo = _split_hi_lo(w)
    w_row = _dot_nt(pick, w_hi) + _dot_nt(pick, w_lo)

    trow = lax.broadcasted_iota(jnp.int32, (L, LANES), 0)
    tcol = lax.broadcasted_iota(jnp.int32, (L, LANES), 1)
    ones = jnp.ones((L, LANES), BF16)

    for h in range(M_HEADS):
        ks = slice(h * M_DK, (h + 1) * M_DK)
        ks2 = slice(M_HEADS * M_DK + h * M_DK, M_HEADS * M_DK + (h + 1) * M_DK)
        vs = slice(h * M_DV, (h + 1) * M_DV)
        hs = slice(h * LANES, (h + 1) * LANES)
        m_prev = m_ref[h, 0:1, :]
        c_prev = c_ref[h]
        n_prev = n_ref[h]
        big_m = jnp.maximum(m_prev, cm_rep[:, hs])
        w_inter = jnp.exp(m_prev - big_m)
        wr = w_row[h * SUBLANES:h * SUBLANES + 1, :]

        qb = qk_ref[r, ks]
        kb = qk_ref[r, ks2]
        v_h = v_ref[r, vs]
        s = _dot_nt(qb, kb)
        sc = jnp.concatenate(
            [jnp.where(trow >= tcol + j, jnp.exp(wr[:, j:j + LANES] - big_m), 0.0) * s[:, j:j + LANES]
             for j in range(0, L, LANES)], axis=1).astype(BF16)
        q_c = _dot(qb, c_prev.astype(BF16))
        den = _dot(sc, ones) + w_inter * _dot(qb, n_prev.astype(BF16))
        inv = 1.0 / jnp.maximum(jnp.abs(den), jnp.exp(-(b_rep[:, hs] + big_m)))
        num = _dot(sc, v_h)
        out_ref[r, vs] = jnp.concatenate(
            [(num[:, j:j + LANES] + w_inter * q_c[:, j:j + LANES]) * inv for j in range(0, M_DV, LANES)],
            axis=1).astype(BF16)

        g = b_rep[L - 1:L, hs]
        cm_last = cm_rep[L - 1:L, hs]
        m_last = big_m[L - 1:L, :]
        kw_t = (kb.astype(F32) * jnp.exp(w_rep[:, hs] - cm_last)).T.astype(BF16)
        s_prev = jnp.exp(m_prev - m_last)
        s_loc = jnp.exp(cm_last - m_last)
        c_ref[h] = (jnp.concatenate([s_prev] * (M_DV // LANES), axis=1) * c_prev
                    + jnp.concatenate([s_loc] * (M_DV // LANES), axis=1) * _dot(kw_t, v_h))
        n_ref[h] = s_prev * n_prev + s_loc * _dot(kw_t, ones)
        m_ref[h] = jnp.broadcast_to(g + m_last, (SUBLANES, LANES))


def _mlstm(qk, v, gates, gbias, batch, seq):
    t = batch * seq
    L = M_CHUNK * M_SUB
    nc = seq // L
    dv = M_HEADS * M_DV
    tok = lambda b, c: (b * nc + c, 0)
    const = lambda b, c: (0, 0)
    return pl.pallas_call(
        _mlstm_body,
        grid=(batch, nc),
        in_specs=[
            pl.BlockSpec((L, M_QK), tok),
            pl.BlockSpec((L, dv), tok),
            pl.BlockSpec((L, 2 * LANES), tok),
            pl.BlockSpec((1, 2 * LANES), const),
        ],
        out_specs=pl.BlockSpec((L, dv), tok),
        out_shape=jax.ShapeDtypeStruct((t, dv), BF16),
        scratch_shapes=[
            pltpu.VMEM((M_HEADS, M_DK, M_DV), F32),
            pltpu.VMEM((M_HEADS, M_DK, LANES), F32),
            pltpu.VMEM((M_HEADS, SUBLANES, LANES), F32),
        ],
        compiler_params=_cparams(2),
        name="mlstm",
    )(qk, v, gates, gbias)


def _gelu_tanh(x):
    return 0.5 * x * (1.0 + jnp.tanh(0.7978845608028654 * (x + 0.044715 * (x * x * x))))


def _s5_build_operators(lamr_ref, lami_ref, ldt_ref, btr_ref, bti_ref, ctr_ref, cti_ref, d_ref,
                        pbig_ref, qbig_ref, mbig_ref, a_ref):
    lr = lamr_ref[0]
    li = lami_ref[0]
    dt = jnp.exp(ldt_ref[0])
    zr = lr * dt
    th = li * dt
    er = jnp.exp(zr)
    ar = er * jnp.cos(th)
    ai = er * jnp.sin(th)
    den = lr * lr + li * li
    beta_r = ((ar - 1.0) * lr + ai * li) / den
    beta_i = (ai * lr - (ar - 1.0) * li) / den
    btr = btr_ref[0]
    bti = bti_ref[0]
    bbr = btr * beta_r - bti * beta_i
    bbi = btr * beta_i + bti * beta_r
    ctr = ctr_ref[0]
    cti = cti_ref[0]

    row_g = lax.shift_right_logical(lax.broadcasted_iota(jnp.int32, (LANES, S5_SW), 0), 4)
    lane_g = lax.shift_right_logical(lax.broadcasted_iota(jnp.int32, (LANES, S5_SW), 1), 6)
    same_group = row_g == lane_g

    def expand(x16):
        return jnp.where(same_group, jnp.concatenate([x16] * S5_GPL, axis=0), 0.0)

    def power(k):
        e = jnp.exp(float(k) * zr)
        return e * jnp.cos(float(k) * th), e * jnp.sin(float(k) * th)

    for s in range(S5_BLK):
        rows = slice(s * LANES, (s + 1) * LANES)
        pr, pi = power(S5_BLK - 1 - s)
        pbig_ref[rows, 0:S5_SW] = expand(pr * bbr - pi * bbi).astype(BF16)
        pbig_ref[rows, S5_SW:2 * S5_SW] = expand(pr * bbi + pi * bbr).astype(BF16)
        pr, pi = power(s + 1)
        qbig_ref[rows, 0:S5_SW] = expand(ctr * pr - cti * pi).astype(BF16)
        qbig_ref[rows, S5_SW:2 * S5_SW] = expand(-(ctr * pi + cti * pr)).astype(BF16)

    cb = jnp.concatenate([expand(ctr), expand(-cti)], axis=1)
    r128 = lax.broadcasted_iota(jnp.int32, (LANES, LANES), 0)
    c128 = lax.broadcasted_iota(jnp.int32, (LANES, LANES), 1)
    zero_blk = jnp.zeros((LANES, LANES), BF16)
    for lag in range(S5_BLK):
        pr, pi = power(lag)
        ab = jnp.concatenate([expand(pr * bbr - pi * bbi), expand(pr * bbi + pi * bbr)], axis=1)
        v = _dot_nt(ab, cb, precision=lax.Precision.HIGHEST)
        if lag == 0:
            v = v + jnp.where(r128 == c128, d_ref[0], 0.0)
        vb = v.astype(BF16)
        for s in range(S5_BLK - lag):
            t = s + lag
            mbig_ref[s * LANES:(s + 1) * LANES, t * LANES:(t + 1) * LANES] = vb
            if lag > 0:
                mbig_ref[t * LANES:(t + 1) * LANES, s * LANES:(s + 1) * LANES] = zero_blk

    pr, pi = power(S5_BLK)
    a_ref[:, 0:S5_SW] = jnp.broadcast_to(pr, (SUBLANES, S5_SW))
    a_ref[:, S5_SW:2 * S5_SW] = jnp.broadcast_to(pi, (SUBLANES, S5_SW))


def _s5_body(u_ref, lamr_ref, lami_ref, ldt_ref, btr_ref, bti_ref, ctr_ref, cti_ref, d_ref, y_ref,
             pbig_ref, qbig_ref, mbig_ref, a_ref, ucat_ref, x_ref, xp_ref, st_ref):
    nblk = S5_CHUNK // S5_BLK
    batch = u_ref.shape[0]

    @pl.when(pl.program_id(1) == 0)
    def _setup():
        _s5_build_operators(lamr_ref, lami_ref, ldt_ref, btr_ref, bti_ref, ctr_ref, cti_ref, d_ref,
                            pbig_ref, qbig_ref, mbig_ref, a_ref)
        st_ref[...] = jnp.zeros_like(st_ref)

    for b in range(batch):
        for s in range(S5_BLK):
            piece = u_ref[b, pl.ds(s, nblk, stride=S5_BLK), :]
            ucat_ref[b * nblk:(b + 1) * nblk, s * LANES:(s + 1) * LANES] = piece.astype(BF16)
    nslab = 2 * S5_SW // LANES
    half = nslab // 2
    xloc = _dot(ucat_ref[...], pbig_ref[...])
    for c in range(nslab):
        for b in range(batch):
            x_ref[c, b * S5_XROWS:b * S5_XROWS + nblk, :] = (
                xloc[b * nblk:(b + 1) * nblk, c * LANES:(c + 1) * LANES])

    lanes = lambda ref, c: ref[:, c * LANES:(c + 1) * LANES]
    ar = [lanes(a_ref, c) for c in range(half)]
    ai = [lanes(a_ref, half + c) for c in range(half)]
    xr = [lanes(st_ref, c) for c in range(half)]
    xi = [lanes(st_ref, half + c) for c in range(half)]
    for blk in range(nblk):
        r = pl.ds(blk, batch, stride=S5_XROWS)
        for c in range(half):
            xp_ref[c, r, :] = xr[c]
            xp_ref[half + c, r, :] = xi[c]
            nr = ar[c] * xr[c] - ai[c] * xi[c] + x_ref[c, r, :]
            ni = ar[c] * xi[c] + ai[c] * xr[c] + x_ref[half + c, r, :]
            xr[c], xi[c] = nr, ni
    for c in range(half):
        st_ref[:, c * LANES:(c + 1) * LANES] = xr[c]
        st_ref[:, (half + c) * LANES:(half + c + 1) * LANES] = xi[c]

    xpb = jnp.concatenate(
        [jnp.concatenate([xp_ref[c, b * S5_XROWS:b * S5_XROWS + nblk, :].astype(BF16)
                          for c in range(nslab)], axis=1) for b in range(batch)], axis=0)
    width = 2 * LANES
    for nb in range(S5_BLK // 2):
        kk = (2 * nb + 2) * LANES
        cols = slice(nb * width, (nb + 1) * width)
        y = _dot(ucat_ref[:, 0:kk], mbig_ref[0:kk, cols]) + _dot_nt(xpb, qbig_ref[cols, :])
        y = _gelu_tanh(y)
        for tt in range(2):
            t = 2 * nb + tt
            for b in range(batch):
                y_ref[b, pl.ds(t, nblk, stride=S5_BLK), :] = (
                    y[b * nblk:(b + 1) * nblk, tt * LANES:(tt + 1) * LANES])


def _s5(u3, lam_re, lam_im, log_dt, b_re, b_im, c_re, c_im, d_skip):
    batch, seq, width = u3.shape
    nlb = width // LANES
    assert batch == SUBLANES and seq % S5_CHUNK == 0
    lamr = lam_re.reshape(nlb, 1, S5_SW)
    lami = lam_im.reshape(nlb, 1, S5_SW)
    ldt = jnp.repeat(log_dt, S5_STATE).reshape(nlb, 1, S5_SW)
    bt = lambda b: b.reshape(nlb, S5_GPL, S5_STATE, S5_GROUP).transpose(0, 3, 1, 2).reshape(nlb, S5_GROUP, S5_SW)
    ct = lambda c: c.reshape(nlb, S5_GPL, S5_GROUP, S5_STATE).transpose(0, 2, 1, 3).reshape(nlb, S5_GROUP, S5_SW)
    par = lambda r, w: pl.BlockSpec((1, r, w), lambda i, j: (i, 0, 0))
    kdim = S5_BLK * LANES
    rows = (S5_CHUNK // S5_BLK) * batch
    io = pl.BlockSpec((batch, S5_CHUNK, LANES), lambda i, j: (0, j, i))
    return pl.pallas_call(
        _s5_body,
        grid=(nlb, seq // S5_CHUNK),
        in_specs=[io, par(1, S5_SW), par(1, S5_SW), par(1, S5_SW), par(S5_GROUP, S5_SW),
                  par(S5_GROUP, S5_SW), par(S5_GROUP, S5_SW), par(S5_GROUP, S5_SW), par(1, LANES)],
        out_specs=io,
        out_shape=jax.ShapeDtypeStruct((batch, seq, width), F32),
        scratch_shapes=[
            pltpu.VMEM((kdim, 2 * S5_SW), BF16),
            pltpu.VMEM((kdim, 2 * S5_SW), BF16),
            pltpu.VMEM((kdim, kdim), BF16),
            pltpu.VMEM((SUBLANES, 2 * S5_SW), F32),
            pltpu.VMEM((rows, kdim), BF16),
            pltpu.VMEM((2 * S5_SW // LANES, batch * S5_XROWS, LANES), F32),
            pltpu.VMEM((2 * S5_SW // LANES, batch * S5_XROWS, LANES), F32),
            pltpu.VMEM((SUBLANES, 2 * S5_SW), F32),
        ],
        compiler_params=_cparams(2),
        name="s5",
    )(u3, lamr, lami, ldt, bt(b_re), bt(b_im), ct(c_re), ct(c_im), d_skip.reshape(nlb, 1, LANES))


def _even_out_body(yb_ref, h_ref, o_ref, z_ref, x_ref, hg_ref, gluw_ref, glub_ref, wout_ref,
                   out_ref, y_ref):
    half = h_ref.shape[1]
    for h in range(M_HEADS):
        vs = slice(h * M_DV, (h + 1) * M_DV)
        og = jax.nn.sigmoid(o_ref[:, vs].astype(F32)) * h_ref[:, vs].astype(F32)
        y_ref[:, vs] = (_rmsnorm(og, hg_ref[:, vs]) * _silu(z_ref[:, vs].astype(F32))).astype(BF16)
    yg = yb_ref[...]
    s = _dot(yg.astype(BF16), gluw_ref[...]) + glub_ref[...]
    hb = yg * jax.nn.sigmoid(s)
    y_ref[:, half:2 * half] = (hb * _silu(z_ref[:, half:2 * half].astype(F32))).astype(BF16)
    out_ref[...] = x_ref[...] + _dot(y_ref[...], wout_ref[...])


def _even_out(yb, hm, o, z, xf, head_g, glu_w, glu_b, w_out):
    t, d = xf.shape
    half = hm.shape[1]
    tok = lambda i: (i, 0)
    const = lambda i: (0, 0)
    return pl.pallas_call(
        _even_out_body,
        grid=(t // TOKEN_TILE,),
        in_specs=[
            pl.BlockSpec((TOKEN_TILE, half), tok),
            pl.BlockSpec((TOKEN_TILE, half), tok),
            pl.BlockSpec((TOKEN_TILE, half), tok),
            pl.BlockSpec((TOKEN_TILE, 2 * half), tok),
            pl.BlockSpec((TOKEN_TILE, d), tok),
            pl.BlockSpec((1, half), const),
            pl.BlockSpec((half, half), const),
            pl.BlockSpec((1, half), const),
            pl.BlockSpec((2 * half, d), const),
        ],
        out_specs=pl.BlockSpec((TOKEN_TILE, d), tok),
        out_shape=jax.ShapeDtypeStruct((t, d), F32),
        scratch_shapes=[pltpu.VMEM((TOKEN_TILE, 2 * half), BF16)],
        compiler_params=_cparams(1),
        name="even_out",
    )(yb, hm, o, z, xf, head_g.reshape(1, half), glu_w, glu_b.reshape(1, half), w_out)


def _proj_odd_body(x_ref, g_ref, w_ref, wg_ref, wa_ref, ba_ref, q_ref, k_ref, v_ref, z_ref, bc_ref):
    hb = _rmsnorm(x_ref[...], g_ref[...]).astype(BF16)
    _project(hb, w_ref, (q_ref, k_ref, v_ref, z_ref), 0)
    rb = _dot(hb, wg_ref[...]).astype(BF16)
    tril = _tri_ones(G_CHUNK, True)
    for j in range(0, bc_ref.shape[1], PROJ_TN):
        cols = slice(j, j + PROJ_TN)
        pre = _dot(rb, wa_ref[:, cols].astype(BF16)) + ba_ref[:, cols]
        la = (_log_sigmoid(pre) * (1.0 / G_TAU)).astype(BF16)
        for c in range(0, TOKEN_TILE, G_CHUNK):
            rows = slice(c, c + G_CHUNK)
            bc_ref[rows, cols] = _dot(tril, la[rows])


def _proj_odd(xf, g, wm, wg, w_alpha, b_alpha):
    t, d = xf.shape
    dk = G_HEADS * G_DK
    assert TOKEN_TILE % G_CHUNK == 0
    tok = lambda i: (i, 0)
    const = lambda i: (0, 0)
    widths = (dk, dk, D_MIX, D_MIX, dk)
    dtypes = (BF16, BF16, BF16, BF16, F32)
    return pl.pallas_call(
        _proj_odd_body,
        grid=(t // TOKEN_TILE,),
        in_specs=[
            pl.BlockSpec((TOKEN_TILE, d), tok),
            pl.BlockSpec((1, d), const),
            pl.BlockSpec((d, wm.shape[1]), const, pipeline_mode=pl.Buffered(1)),
            pl.BlockSpec((d, G_RANK_PAD), const),
            pl.BlockSpec((G_RANK_PAD, dk), const),
            pl.BlockSpec((1, dk), const),
        ],
        out_specs=[pl.BlockSpec((TOKEN_TILE, n), tok) for n in widths],
        out_shape=[jax.ShapeDtypeStruct((t, n), dt) for n, dt in zip(widths, dtypes)],
        compiler_params=_cparams(1),
        name="proj_odd",
    )(xf, g.reshape(1, d), wm, wg, w_alpha, b_alpha.reshape(1, dk))


def _gla_body(q_ref, k_ref, v_ref, bc_ref, out_ref, s_ref):
    L = G_CHUNK

    @pl.when(pl.program_id(1) == 0)
    def _init():
        s_ref[...] = jnp.zeros_like(s_ref)

    row = lax.broadcasted_iota(jnp.int32, (L, L), 0)
    col = lax.broadcasted_iota(jnp.int32, (L, L), 1)
    causal = row >= col
    mid = L // 2 - 1

    for sub, h in [(sub, h) for sub in range(G_SUB) for h in range(G_HEADS)]:
        r = slice(sub * L, (sub + 1) * L)
        ks = slice(h * G_DK, (h + 1) * G_DK)
        vs = slice(h * G_DV, (h + 1) * G_DV)
        b = bc_ref[r, ks]
        bm = b[mid:mid + 1, :]
        g = b[L - 1:L, :]
        e1 = jnp.exp(b - bm)
        e2 = jnp.exp(bm - b)
        qt = q_ref[r, ks].astype(F32) * (G_DK ** -0.5) * e1
        kt = k_ref[r, ks].astype(F32) * e2
        attn = jnp.where(causal, _dot_nt(qt.astype(BF16), kt.astype(BF16)), 0.0)
        v_h = v_ref[r, vs]
        s_prev = s_ref[h]
        qi = qt * jnp.exp(bm)
        o = _dot(attn.astype(BF16), v_h) + _dot(qi.astype(BF16), s_prev.astype(BF16))
        out_ref[r, vs] = o.astype(BF16)
        ke = kt * jnp.exp(g - bm)
        g_col = jnp.broadcast_to(jnp.exp(g), (LANES, G_DK)).T[:, 0:1]
        s_ref[h] = g_col * s_prev + _dot(ke.T.astype(BF16), v_h)


def _gla(q, k, v, bc, batch, seq):
    t = batch * seq
    L = G_CHUNK * G_SUB
    nc = seq // L
    dk = G_HEADS * G_DK
    dv = G_HEADS * G_DV
    tok = lambda b, c: (b * nc + c, 0)
    return pl.pallas_call(
        _gla_body,
        grid=(batch, nc),
        in_specs=[
            pl.BlockSpec((L, dk), tok),
            pl.BlockSpec((L, dk), tok),
            pl.BlockSpec((L, dv), tok),
            pl.BlockSpec((L, dk), tok),
        ],
        out_specs=pl.BlockSpec((L, dv), tok),
        out_shape=jax.ShapeDtypeStruct((t, dv), BF16),
        scratch_shapes=[pltpu.VMEM((G_HEADS, G_DK, G_DV), F32)],
        compiler_params=_cparams(2),
        name="gla",
    )(q, k, v, bc)


def _odd_out_body(o_ref, z_ref, x_ref, hg_ref, wout_ref, g_ref, out_ref, y_ref):
    for h in range(G_HEADS):
        vs = slice(h * G_DV, (h + 1) * G_DV)
        on = _rmsnorm(o_ref[:, vs].astype(F32), hg_ref[:, vs])
        y_ref[:, vs] = (on * _silu(z_ref[:, vs].astype(F32))).astype(BF16)
    x = x_ref[...] + _dot(y_ref[...], wout_ref[...])
    out_ref[...] = _rmsnorm(x, g_ref[...])


def _odd_out(o, z, x1, head_g, w_out, g):
    t, d = x1.shape
    dm = o.shape[1]
    tok = lambda i: (i, 0)
    const = lambda i: (0, 0)
    return pl.pallas_call(
        _odd_out_body,
        grid=(t // TOKEN_TILE,),
        in_specs=[
            pl.BlockSpec((TOKEN_TILE, dm), tok),
            pl.BlockSpec((TOKEN_TILE, dm), tok),
            pl.BlockSpec((TOKEN_TILE, d), tok),
            pl.BlockSpec((1, dm), const),
            pl.BlockSpec((dm, d), const),
            pl.BlockSpec((1, d), const),
        ],
        out_specs=pl.BlockSpec((TOKEN_TILE, d), tok),
        out_shape=jax.ShapeDtypeStruct((t, d), F32),
        scratch_shapes=[pltpu.VMEM((TOKEN_TILE, dm), BF16)],
        compiler_params=_cparams(1),
        name="odd_out",
    )(o, z, x1, head_g.reshape(1, dm), w_out, g.reshape(1, d))


def kernel(x, norm_g, final_norm_g, ev_w_in, ev_conv_w, ev_conv_b, ev_i_bias, ev_f_bias, ev_head_g,
           s5_lam_re, s5_lam_im, s5_log_dt, s5_b_re, s5_b_im, s5_c_re, s5_c_im, s5_d, s5_glu_w,
           s5_glu_b, ev_w_out, od_w_in, gla_w_alpha, gla_b_alpha, gla_head_g, od_w_out):
    batch, seq, d = x.shape
    t = batch * seq
    xf = x.reshape(t, d)
    padc = lambda a: jnp.pad(a, ((0, 0), (0, LANES - a.shape[1])))

    w = ev_w_in.reshape(ev_w_in.shape[1:])
    g0 = 2 * M_HEADS * M_DK + 2 * M_HEADS * M_DV
    gi = g0 + M_HEADS
    gf = gi + M_HEADS
    wg = jnp.concatenate([padc(w[:, g0:gi]), padc(w[:, gi:gf])], axis=1).astype(BF16)
    half = D_MIX // 2
    qk, v, o, u, z, gates = _proj_even(xf, norm_g[0], w[:, :g0].astype(BF16), w[:, gf:].astype(BF16), wg,
                                       ev_conv_w[0], ev_conv_b[0], seq)
    gbias = jnp.concatenate([padc(ev_i_bias), padc(ev_f_bias)], axis=1)
    hm = _mlstm(qk, v, gates, gbias, batch, seq)

    y3 = _s5(u.reshape(batch, seq, half), s5_lam_re[0], s5_lam_im[0], s5_log_dt[0], s5_b_re[0],
             s5_b_im[0], s5_c_re[0], s5_c_im[0], s5_d[0])
    yb = y3.reshape(t, half)
    x1 = _even_out(yb, hm, o, z, xf, ev_head_g[0], s5_glu_w[0].astype(BF16), s5_glu_b[0],
                   ev_w_out[0].astype(BF16))

    w = od_w_in.reshape(od_w_in.shape[1:])
    n_main = 2 * G_HEADS * G_DK + 2 * D_MIX
    wa = jnp.pad(gla_w_alpha[0], ((0, G_RANK_PAD - gla_w_alpha.shape[1]), (0, 0)))
    q, k, v, z, bc = _proj_odd(x1, norm_g[1], w[:, :n_main].astype(BF16), padc(w[:, n_main:]).astype(BF16),
                               wa, gla_b_alpha[0])
    og = _gla(q, k, v, bc, batch, seq)
    out = _odd_out(og, z, x1, gla_head_g[0], od_w_out[0].astype(BF16), final_norm_g)
    return out.reshape(batch, seq, d)
```

```python
import functools

import jax
import jax.numpy as jnp
from jax import lax
from jax.experimental import pallas as pl
from jax.experimental.pallas import tpu as pltpu

F32 = jnp.float32
BF16 = jnp.bfloat16

EPS = 1e-6
D_MODEL = 1024
D_MIX = 2 * D_MODEL
M_HEADS = 4
M_DK = 128
M_DV = 256
M_QK = 2 * M_HEADS * M_DK
CONV_WIDTH = 4
M_CHUNK = 256
M_SUB = 2
S5_GROUP = 16
S5_STATE = 64
S5_BLK = 8
S5_CHUNK = 1024
G_HEADS = 4
G_DK = 256
G_DV = 512
G_TAU = 16.0
G_CHUNK = 128
G_SUB = 2
G_RANK_PAD = 128

LANES = 128
SUBLANES = 8
HALO = 16
S5_GPL = LANES // S5_GROUP
S5_SW = S5_GPL * S5_STATE
S5_XROWS = S5_CHUNK // S5_BLK + SUBLANES
TOKEN_TILE = 512
PROJ_TN = 256
VMEM_LIMIT = 56 * 1024 * 1024


def _cparams(n_grid):
    return pltpu.CompilerParams(
        dimension_semantics=("arbitrary",) * n_grid, vmem_limit_bytes=VMEM_LIMIT)


def _log_sigmoid(x):
    return jnp.minimum(x, 0.0) - jnp.log(1.0 + jnp.exp(-jnp.abs(x)))


def _silu(x):
    return x * jax.nn.sigmoid(x)


def _split_hi_lo(x):
    hi = x.astype(BF16)
    lo = (x - hi.astype(F32)).astype(BF16)
    return hi, lo


def _dot(a, b):
    return jnp.dot(a, b, preferred_element_type=F32)


def _dot_nt(a, b, precision=None):
    return lax.dot_general(a, b, (((1,), (1,)), ((), ())), precision=precision,
                           preferred_element_type=F32)


def _tri_ones(n, lower):
    row = lax.broadcasted_iota(jnp.int32, (n, n), 0)
    col = lax.broadcasted_iota(jnp.int32, (n, n), 1)
    keep = (row >= col) if lower else (row <= col)
    return jnp.where(keep, 1.0, 0.0).astype(BF16)


def _rmsnorm(x, g):
    return x * lax.rsqrt(jnp.mean(x * x, axis=-1, keepdims=True) + EPS) * g


def _project(hb, w_ref, out_refs, col0):
    off = col0
    for o_ref in out_refs:
        n = o_ref.shape[1]
        for j in range(0, n, PROJ_TN):
            o_ref[:, j:j + PROJ_TN] = _dot(hb, w_ref[:, off + j:off + j + PROJ_TN]).astype(o_ref.dtype)
        off += n


def _proj_even_body(x_ref, xh_ref, g_ref, w_ref, wb_ref, wg_ref, convw_ref, convb_ref,
                    qk_ref, v_ref, o_ref, u_ref, z_ref, gates_ref, ext_ref, *, tiles_per_seq):
    hb = _rmsnorm(x_ref[...], g_ref[...]).astype(BF16)
    hh = _rmsnorm(xh_ref[...], g_ref[...]).astype(BF16)
    seq_start = lax.rem(pl.program_id(0), tiles_per_seq) == 0
    base = HALO - (CONV_WIDTH - 1)
    lane = lax.broadcasted_iota(jnp.int32, (1, PROJ_TN), 1)
    for j in range(0, M_QK, PROJ_TN):
        cols = slice(j, j + PROJ_TN)
        wj = w_ref[:, cols]
        ext_ref[HALO:HALO + TOKEN_TILE, :] = _dot(hb, wj)
        ext_ref[0:HALO, :] = jnp.where(seq_start, 0.0, _dot(hh, wj))
        acc = convb_ref[:, cols] + convw_ref[0:1, cols] * ext_ref[base:base + TOKEN_TILE, :]
        for i in range(1, CONV_WIDTH):
            acc = acc + convw_ref[i:i + 1, cols] * ext_ref[base + i:base + i + TOKEN_TILE, :]
        scale = jnp.where(lane + j < M_HEADS * M_DK, M_DK ** -0.5, 1.0)
        qk_ref[:, cols] = (_silu(acc) * scale).astype(BF16)
    _project(hb, w_ref, (v_ref, o_ref), M_QK)
    _project(hb, wb_ref, (u_ref, z_ref), 0)
    gates_ref[...] = _dot(hb, wg_ref[...])


def _proj_even(xf, g, wm, wb, wg, conv_w, conv_b, seq):
    t, d = xf.shape
    half = D_MIX // 2
    assert seq % TOKEN_TILE == 0 and TOKEN_TILE % HALO == 0
    tok = lambda i: (i, 0)
    const = lambda i: (0, 0)
    per_halo = TOKEN_TILE // HALO
    widths = (M_QK, half, half, half, D_MIX)
    dtypes = (BF16, BF16, BF16, F32, BF16)
    out_shape = [jax.ShapeDtypeStruct((t, n), dt) for n, dt in zip(widths, dtypes)]
    ngate = wg.shape[1]
    out_shape.append(jax.ShapeDtypeStruct((t, ngate), F32))
    out_specs = [pl.BlockSpec((TOKEN_TILE, n), tok) for n in widths]
    out_specs.append(pl.BlockSpec((TOKEN_TILE, ngate), tok))
    return pl.pallas_call(
        functools.partial(_proj_even_body, tiles_per_seq=seq // TOKEN_TILE),
        grid=(t // TOKEN_TILE,),
        in_specs=[
            pl.BlockSpec((TOKEN_TILE, d), tok),
            pl.BlockSpec((HALO, d), lambda i: (jnp.maximum(i * per_halo - 1, 0), 0)),
            pl.BlockSpec((1, d), const),
            pl.BlockSpec((d, wm.shape[1]), const, pipeline_mode=pl.Buffered(1)),
            pl.BlockSpec((d, wb.shape[1]), const, pipeline_mode=pl.Buffered(1)),
            pl.BlockSpec((d, ngate), const),
            pl.BlockSpec((CONV_WIDTH, M_QK), const),
            pl.BlockSpec((1, M_QK), const),
        ],
        out_specs=out_specs,
        out_shape=out_shape,
        scratch_shapes=[pltpu.VMEM((HALO + TOKEN_TILE, PROJ_TN), F32)],
        compiler_params=_cparams(1),
        name="proj_even",
    )(xf, xf, g.reshape(1, d), wm, wb, wg, conv_w, conv_b.reshape(1, M_QK))


def _mlstm_body(qk_ref, v_ref, gates_ref, gbias_ref, out_ref, c_ref, n_ref, m_ref):
    @pl.when(pl.program_id(1) == 0)
    def _init():
        c_ref[...] = jnp.zeros_like(c_ref)
        n_ref[...] = jnp.zeros_like(n_ref)
        m_ref[...] = jnp.zeros_like(m_ref)

    for sub in range(M_SUB):
        _mlstm_chunk(slice(sub * M_CHUNK, (sub + 1) * M_CHUNK), qk_ref, v_ref, gates_ref, gbias_ref,
                     out_ref, c_ref, n_ref, m_ref)


def _mlstm_chunk(r, qk_ref, v_ref, gates_ref, gbias_ref, out_ref, c_ref, n_ref, m_ref):
    L = M_CHUNK
    gt = gates_ref[r, :] + gbias_ref[...]
    ipre = gt[:, 0:LANES]
    logf = _log_sigmoid(gt[:, LANES:2 * LANES]).astype(BF16)
    b = _dot(_tri_ones(L, True), logf)
    w = ipre - b
    rows = lax.broadcasted_iota(jnp.int32, (L, LANES), 0)
    cm = w
    k = 1
    while k < L:
        cm = jnp.maximum(cm, jnp.where(rows >= k, pltpu.roll(cm, k, axis=0), -jnp.inf))
        k *= 2

    sel_r = lax.broadcasted_iota(jnp.int32, (LANES, M_HEADS * LANES), 0)
    sel_c = lax.broadcasted_iota(jnp.int32, (LANES, M_HEADS * LANES), 1)
    spread = jnp.where(sel_r == lax.shift_right_logical(sel_c, 7), 1.0, 0.0).astype(BF16)

    def replicate(x):
        hi, lo = _split_hi_lo(x)
        return _dot(hi, spread) + _dot(lo, spread)

    b_rep = replicate(b)
    w_rep = replicate(w)
    cm_rep = replicate(cm)
    pick_r = lax.broadcasted_iota(jnp.int32, (M_HEADS * SUBLANES, LANES), 0)
    pick_c = lax.broadcasted_iota(jnp.int32, (M_HEADS * SUBLANES, LANES), 1)
    pick = jnp.where(lax.shift_right_logical(pick_r, 3) == pick_c, 1.0, 0.0).astype(BF16)
    w_hi, w_lo = _split_hi_lo(w)
    w_row = _dot_nt(pick, w_hi) + _dot_nt(pick, w_lo)

    trow = lax.broadcasted_iota(jnp.int32, (L, LANES), 0)
    tcol = lax.broadcasted_iota(jnp.int32, (L, LANES), 1)
    ones = jnp.ones((L, LANES), BF16)

    for h in range(M_HEADS):
        ks = slice(h * M_DK, (h + 1) * M_DK)
        ks2 = slice(M_HEADS * M_DK + h * M_DK, M_HEADS * M_DK + (h + 1) * M_DK)
        vs = slice(h * M_DV, (h + 1) * M_DV)
        hs = slice(h * LANES, (h + 1) * LANES)
        m_prev = m_ref[h, 0:1, :]
        c_prev = c_ref[h]
        n_prev = n_ref[h]
        big_m = jnp.maximum(m_prev, cm_rep[:, hs])
        w_inter = jnp.exp(m_prev - big_m)
        wr = w_row[h * SUBLANES:h * SUBLANES + 1, :]

        qb = qk_ref[r, ks]
        kb = qk_ref[r, ks2]
        v_h = v_ref[r, vs]
        s = _dot_nt(qb, kb)
        sc = jnp.concatenate(
            [jnp.where(trow >= tcol + j, jnp.exp(wr[:, j:j + LANES] - big_m), 0.0) * s[:, j:j + LANES]
             for j in range(0, L, LANES)], axis=1).astype(BF16)
        q_c = _dot(qb, c_prev.astype(BF16))
        den = _dot(sc, ones) + w_inter * _dot(qb, n_prev.astype(BF16))
        inv = 1.0 / jnp.maximum(jnp.abs(den), jnp.exp(-(b_rep[:, hs] + big_m)))
        num = _dot(sc, v_h)
        out_ref[r, vs] = jnp.concatenate(
            [(num[:, j:j + LANES] + w_inter * q_c[:, j:j + LANES]) * inv for j in range(0, M_DV, LANES)],
            axis=1).astype(BF16)

        g = b_rep[L - 1:L, hs]
        cm_last = cm_rep[L - 1:L, hs]
        m_last = big_m[L - 1:L, :]
        kw_t = (kb.astype(F32) * jnp.exp(w_rep[:, hs] - cm_last)).T.astype(BF16)
        s_prev = jnp.exp(m_prev - m_last)
        s_loc = jnp.exp(cm_last - m_last)
        c_ref[h] = (jnp.concatenate([s_prev] * (M_DV // LANES), axis=1) * c_prev
                    + jnp.concatenate([s_loc] * (M_DV // LANES), axis=1) * _dot(kw_t, v_h))
        n_ref[h] = s_prev * n_prev + s_loc * _dot(kw_t, ones)
        m_ref[h] = jnp.broadcast_to(g + m_last, (SUBLANES, LANES))


def _mlstm(qk, v, gates, gbias, batch, seq):
    t = batch * seq
    L = M_CHUNK * M_SUB
    nc = seq // L
    dv = M_HEADS * M_DV
    tok = lambda b, c: (b * nc + c, 0)
    const = lambda b, c: (0, 0)
    return pl.pallas_call(
        _mlstm_body,
        grid=(batch, nc),
        in_specs=[
            pl.BlockSpec((L, M_QK), tok),
            pl.BlockSpec((L, dv), tok),
            pl.BlockSpec((L, 2 * LANES), tok),
            pl.BlockSpec((1, 2 * LANES), const),
        ],
        out_specs=pl.BlockSpec((L, dv), tok),
        out_shape=jax.ShapeDtypeStruct((t, dv), BF16),
        scratch_shapes=[
            pltpu.VMEM((M_HEADS, M_DK, M_DV), F32),
            pltpu.VMEM((M_HEADS, M_DK, LANES), F32),
            pltpu.VMEM((M_HEADS, SUBLANES, LANES), F32),
        ],
        compiler_params=_cparams(2),
        name="mlstm",
    )(qk, v, gates, gbias)


def _gelu_tanh(x):
    return 0.5 * x * (1.0 + jnp.tanh(0.7978845608028654 * (x + 0.044715 * (x * x * x))))


def _s5_build_operators(lamr_ref, lami_ref, ldt_ref, btr_ref, bti_ref, ctr_ref, cti_ref, d_ref,
                        pbig_ref, qbig_ref, mbig_ref, a_ref):
    lr = lamr_ref[0]
    li = lami_ref[0]
    dt = jnp.exp(ldt_ref[0])
    zr = lr * dt
    th = li * dt
    er = jnp.exp(zr)
    ar = er * jnp.cos(th)
    ai = er * jnp.sin(th)
    den = lr * lr + li * li
    beta_r = ((ar - 1.0) * lr + ai * li) / den
    beta_i = (ai * lr - (ar - 1.0) * li) / den
    btr = btr_ref[0]
    bti = bti_ref[0]
    bbr = btr * beta_r - bti * beta_i
    bbi = btr * beta_i + bti * beta_r
    ctr = ctr_ref[0]
    cti = cti_ref[0]

    row_g = lax.shift_right_logical(lax.broadcasted_iota(jnp.int32, (LANES, S5_SW), 0), 4)
    lane_g = lax.shift_right_logical(lax.broadcasted_iota(jnp.int32, (LANES, S5_SW), 1), 6)
    same_group = row_g == lane_g

    def expand(x16):
        return jnp.where(same_group, jnp.concatenate([x16] * S5_GPL, axis=0), 0.0)

    def power(k):
        e = jnp.exp(float(k) * zr)
        return e * jnp.cos(float(k) * th), e * jnp.sin(float(k) * th)

    for s in range(S5_BLK):
        rows = slice(s * LANES, (s + 1) * LANES)
        pr, pi = power(S5_BLK - 1 - s)
        pbig_ref[rows, 0:S5_SW] = expand(pr * bbr - pi * bbi).astype(BF16)
        pbig_ref[rows, S5_SW:2 * S5_SW] = expand(pr * bbi + pi * bbr).astype(BF16)
        pr, pi = power(s + 1)
        qbig_ref[rows, 0:S5_SW] = expand(ctr * pr - cti * pi).astype(BF16)
        qbig_ref[rows, S5_SW:2 * S5_SW] = expand(-(ctr * pi + cti * pr)).astype(BF16)

    cb = jnp.concatenate([expand(ctr), expand(-cti)], axis=1)
    r128 = lax.broadcasted_iota(jnp.int32, (LANES, LANES), 0)
    c128 = lax.broadcasted_iota(jnp.int32, (LANES, LANES), 1)
    zero_blk = jnp.zeros((LANES, LANES), BF16)
    for lag in range(S5_BLK):
        pr, pi = power(lag)
        ab = jnp.concatenate([expand(pr * bbr - pi * bbi), expand(pr * bbi + pi * bbr)], axis=1)
        v = _dot_nt(ab, cb, precision=lax.Precision.HIGHEST)
        if lag == 0:
            v = v + jnp.where(r128 == c128, d_ref[0], 0.0)
        vb = v.astype(BF16)
        for s in range(S5_BLK - lag):
            t = s + lag
            mbig_ref[s * LANES:(s + 1) * LANES, t * LANES:(t + 1) * LANES] = vb
            if lag > 0:
                mbig_ref[t * LANES:(t + 1) * LANES, s * LANES:(s + 1) * LANES] = zero_blk

    pr, pi = power(S5_BLK)
    a_ref[:, 0:S5_SW] = jnp.broadcast_to(pr, (SUBLANES, S5_SW))
    a_ref[:, S5_SW:2 * S5_SW] = jnp.broadcast_to(pi, (SUBLANES, S5_SW))


def _s5_body(u_ref, lamr_ref, lami_ref, ldt_ref, btr_ref, bti_ref, ctr_ref, cti_ref, d_ref, y_ref,
             pbig_ref, qbig_ref, mbig_ref, a_ref, ucat_ref, x_ref, xp_ref, st_ref):
    nblk = S5_CHUNK // S5_BLK
    batch = u_ref.shape[0]

    @pl.when(pl.program_id(1) == 0)
    def _setup():
        _s5_build_operators(lamr_ref, lami_ref, ldt_ref, btr_ref, bti_ref, ctr_ref, cti_ref, d_ref,
                            pbig_ref, qbig_ref, mbig_ref, a_ref)
        st_ref[...] = jnp.zeros_like(st_ref)

    for b in range(batch):
        for s in range(S5_BLK):
            piece = u_ref[b, pl.ds(s, nblk, stride=S5_BLK), :]
            ucat_ref[b * nblk:(b + 1) * nblk, s * LANES:(s + 1) * LANES] = piece.astype(BF16)
    nslab = 2 * S5_SW // LANES
    half = nslab // 2
    xloc = _dot(ucat_ref[...], pbig_ref[...])
    for c in range(nslab):
        for b in range(batch):
            x_ref[c, b * S5_XROWS:b * S5_XROWS + nblk, :] = (
                xloc[b * nblk:(b + 1) * nblk, c * LANES:(c + 1) * LANES])

    lanes = lambda ref, c: ref[:, c * LANES:(c + 1) * LANES]
    ar = [lanes(a_ref, c) for c in range(half)]
    ai = [lanes(a_ref, half + c) for c in range(half)]
    xr = [lanes(st_ref, c) for c in range(half)]
    xi = [lanes(st_ref, half + c) for c in range(half)]
    for blk in range(nblk):
        r = pl.ds(blk, batch, stride=S5_XROWS)
        for c in range(half):
            xp_ref[c, r, :] = xr[c]
            xp_ref[half + c, r, :] = xi[c]
            nr = ar[c] * xr[c] - ai[c] * xi[c] + x_ref[c, r, :]
            ni = ar[c] * xi[c] + ai[c] * xr[c] + x_ref[half + c, r, :]
            xr[c], xi[c] = nr, ni
    for c in range(half):
        st_ref[:, c * LANES:(c + 1) * LANES] = xr[c]
        st_ref[:, (half + c) * LANES:(half + c + 1) * LANES] = xi[c]

    xpb = jnp.concatenate(
        [jnp.concatenate([xp_ref[c, b * S5_XROWS:b * S5_XROWS + nblk, :].astype(BF16)
                          for c in range(nslab)], axis=1) for b in range(batch)], axis=0)
    width = 2 * LANES
    for nb in range(S5_BLK // 2):
        kk = (2 * nb + 2) * LANES
        cols = slice(nb * width, (nb + 1) * width)
        y = _dot(ucat_ref[:, 0:kk], mbig_ref[0:kk, cols]) + _dot_nt(xpb, qbig_ref[cols, :])
        y = _gelu_tanh(y)
        for tt in range(2):
            t = 2 * nb + tt
            for b in range(batch):
                y_ref[b, pl.ds(t, nblk, stride=S5_BLK), :] = (
                    y[b * nblk:(b + 1) * nblk, tt * LANES:(tt + 1) * LANES])


def _s5(u3, lam_re, lam_im, log_dt, b_re, b_im, c_re, c_im, d_skip):
    batch, seq, width = u3.shape
    nlb = width // LANES
    assert batch == SUBLANES and seq % S5_CHUNK == 0
    lamr = lam_re.reshape(nlb, 1, S5_SW)
    lami = lam_im.reshape(nlb, 1, S5_SW)
    ldt = jnp.repeat(log_dt, S5_STATE).reshape(nlb, 1, S5_SW)
    bt = lambda b: b.reshape(nlb, S5_GPL, S5_STATE, S5_GROUP).transpose(0, 3, 1, 2).reshape(nlb, S5_GROUP, S5_SW)
    ct = lambda c: c.reshape(nlb, S5_GPL, S5_GROUP, S5_STATE).transpose(0, 2, 1, 3).reshape(nlb, S5_GROUP, S5_SW)
    par = lambda r, w: pl.BlockSpec((1, r, w), lambda i, j: (i, 0, 0))
    kdim = S5_BLK * LANES
    rows = (S5_CHUNK // S5_BLK) * batch
    io = pl.BlockSpec((batch, S5_CHUNK, LANES), lambda i, j: (0, j, i))
    return pl.pallas_call(
        _s5_body,
        grid=(nlb, seq // S5_CHUNK),
        in_specs=[io, par(1, S5_SW), par(1, S5_SW), par(1, S5_SW), par(S5_GROUP, S5_SW),
                  par(S5_GROUP, S5_SW), par(S5_GROUP, S5_SW), par(S5_GROUP, S5_SW), par(1, LANES)],
        out_specs=io,
        out_shape=jax.ShapeDtypeStruct((batch, seq, width), F32),
        scratch_shapes=[
            pltpu.VMEM((kdim, 2 * S5_SW), BF16),
            pltpu.VMEM((kdim, 2 * S5_SW), BF16),
            pltpu.VMEM((kdim, kdim), BF16),
            pltpu.VMEM((SUBLANES, 2 * S5_SW), F32),
            pltpu.VMEM((rows, kdim), BF16),
            pltpu.VMEM((2 * S5_SW // LANES, batch * S5_XROWS, LANES), F32),
            pltpu.VMEM((2 * S5_SW // LANES, batch * S5_XROWS, LANES), F32),
            pltpu.VMEM((SUBLANES, 2 * S5_SW), F32),
        ],
        compiler_params=_cparams(2),
        name="s5",
    )(u3, lamr, lami, ldt, bt(b_re), bt(b_im), ct(c_re), ct(c_im), d_skip.reshape(nlb, 1, LANES))


def _even_out_body(yb_ref, h_ref, o_ref, z_ref, x_ref, hg_ref, gluw_ref, glub_ref, wout_ref,
                   out_ref, y_ref):
    half = h_ref.shape[1]
    for h in range(M_HEADS):
        vs = slice(h * M_DV, (h + 1) * M_DV)
        og = jax.nn.sigmoid(o_ref[:, vs].astype(F32)) * h_ref[:, vs].astype(F32)
        y_ref[:, vs] = (_rmsnorm(og, hg_ref[:, vs]) * _silu(z_ref[:, vs].astype(F32))).astype(BF16)
    yg = yb_ref[...]
    s = _dot(yg.astype(BF16), gluw_ref[...]) + glub_ref[...]
    hb = yg * jax.nn.sigmoid(s)
    y_ref[:, half:2 * half] = (hb * _silu(z_ref[:, half:2 * half].astype(F32))).astype(BF16)
    out_ref[...] = x_ref[...] + _dot(y_ref[...], wout_ref[...])


def _even_out(yb, hm, o, z, xf, head_g, glu_w, glu_b, w_out):
    t, d = xf.shape
    half = hm.shape[1]
    tok = lambda i: (i, 0)
    const = lambda i: (0, 0)
    return pl.pallas_call(
        _even_out_body,
        grid=(t // TOKEN_TILE,),
        in_specs=[
            pl.BlockSpec((TOKEN_TILE, half), tok),
            pl.BlockSpec((TOKEN_TILE, half), tok),
            pl.BlockSpec((TOKEN_TILE, half), tok),
            pl.BlockSpec((TOKEN_TILE, 2 * half), tok),
            pl.BlockSpec((TOKEN_TILE, d), tok),
            pl.BlockSpec((1, half), const),
            pl.BlockSpec((half, half), const),
            pl.BlockSpec((1, half), const),
            pl.BlockSpec((2 * half, d), const),
        ],
        out_specs=pl.BlockSpec((TOKEN_TILE, d), tok),
        out_shape=jax.ShapeDtypeStruct((t, d), F32),
        scratch_shapes=[pltpu.VMEM((TOKEN_TILE, 2 * half), BF16)],
        compiler_params=_cparams(1),
        name="even_out",
    )(yb, hm, o, z, xf, head_g.reshape(1, half), glu_w, glu_b.reshape(1, half), w_out)


def _proj_odd_body(x_ref, g_ref, w_ref, wg_ref, wa_ref, ba_ref, q_ref, k_ref, v_ref, z_ref, bc_ref):
    hb = _rmsnorm(x_ref[...], g_ref[...]).astype(BF16)
    _project(hb, w_ref, (q_ref, k_ref, v_ref, z_ref), 0)
    rb = _dot(hb, wg_ref[...]).astype(BF16)
    tril = _tri_ones(G_CHUNK, True)
    for j in range(0, bc_ref.shape[1], PROJ_TN):
        cols = slice(j, j + PROJ_TN)
        pre = _dot(rb, wa_ref[:, cols].astype(BF16)) + ba_ref[:, cols]
        la = (_log_sigmoid(pre) * (1.0 / G_TAU)).astype(BF16)
        for c in range(0, TOKEN_TILE, G_CHUNK):
            rows = slice(c, c + G_CHUNK)
            bc_ref[rows, cols] = _dot(tril, la[rows])


def _proj_odd(xf, g, wm, wg, w_alpha, b_alpha):
    t, d = xf.shape
    dk = G_HEADS * G_DK
    assert TOKEN_TILE % G_CHUNK == 0
    tok = lambda i: (i, 0)
    const = lambda i: (0, 0)
    widths = (dk, dk, D_MIX, D_MIX, dk)
    dtypes = (BF16, BF16, BF16, BF16, F32)
    return pl.pallas_call(
        _proj_odd_body,
        grid=(t // TOKEN_TILE,),
        in_specs=[
            pl.BlockSpec((TOKEN_TILE, d), tok),
            pl.BlockSpec((1, d), const),
            pl.BlockSpec((d, wm.shape[1]), const, pipeline_mode=pl.Buffered(1)),
            pl.BlockSpec((d, G_RANK_PAD), const),
            pl.BlockSpec((G_RANK_PAD, dk), const),
            pl.BlockSpec((1, dk), const),
        ],
        out_specs=[pl.BlockSpec((TOKEN_TILE, n), tok) for n in widths],
        out_shape=[jax.ShapeDtypeStruct((t, n), dt) for n, dt in zip(widths, dtypes)],
        compiler_params=_cparams(1),
        name="proj_odd",
    )(xf, g.reshape(1, d), wm, wg, w_alpha, b_alpha.reshape(1, dk))


def _gla_body(q_ref, k_ref, v_ref, bc_ref, out_ref, s_ref):
    L = G_CHUNK

    @pl.when(pl.program_id(1) == 0)
    def _init():
        s_ref[...] = jnp.zeros_like(s_ref)

    row = lax.broadcasted_iota(jnp.int32, (L, L), 0)
    col = lax.broadcasted_iota(jnp.int32, (L, L), 1)
    causal = row >= col
    mid = L // 2 - 1

    for sub, h in [(sub, h) for sub in range(G_SUB) for h in range(G_HEADS)]:
        r = slice(sub * L, (sub + 1) * L)
        ks = slice(h * G_DK, (h + 1) * G_DK)
        vs = slice(h * G_DV, (h + 1) * G_DV)
        b = bc_ref[r, ks]
        bm = b[mid:mid + 1, :]
        g = b[L - 1:L, :]
        e1 = jnp.exp(b - bm)
        e2 = jnp.exp(bm - b)
        qt = q_ref[r, ks].astype(F32) * (G_DK ** -0.5) * e1
        kt = k_ref[r, ks].astype(F32) * e2
        attn = jnp.where(causal, _dot_nt(qt.astype(BF16), kt.astype(BF16)), 0.0)
        v_h = v_ref[r, vs]
        s_prev = s_ref[h]
        qi = qt * jnp.exp(bm)
        o = _dot(attn.astype(BF16), v_h) + _dot(qi.astype(BF16), s_prev.astype(BF16))
        out_ref[r, vs] = o.astype(BF16)
        ke = kt * jnp.exp(g - bm)
        g_col = jnp.broadcast_to(jnp.exp(g), (LANES, G_DK)).T[:, 0:1]
        s_ref[h] = g_col * s_prev + _dot(ke.T.astype(BF16), v_h)


def _gla(q, k, v, bc, batch, seq):
    t = batch * seq
    L = G_CHUNK * G_SUB
    nc = seq // L
    dk = G_HEADS * G_DK
    dv = G_HEADS * G_DV
    tok = lambda b, c: (b * nc + c, 0)
    return pl.pallas_call(
        _gla_body,
        grid=(batch, nc),
        in_specs=[
            pl.BlockSpec((L, dk), tok),
            pl.BlockSpec((L, dk), tok),
            pl.BlockSpec((L, dv), tok),
            pl.BlockSpec((L, dk), tok),
        ],
        out_specs=pl.BlockSpec((L, dv), tok),
        out_shape=jax.ShapeDtypeStruct((t, dv), BF16),
        scratch_shapes=[pltpu.VMEM((G_HEADS, G_DK, G_DV), F32)],
        compiler_params=_cparams(2),
        name="gla",
    )(q, k, v, bc)


def _odd_out_body(o_ref, z_ref, x_ref, hg_ref, wout_ref, g_ref, out_ref, y_ref):
    for h in range(G_HEADS):
        vs = slice(h * G_DV, (h + 1) * G_DV)
        on = _rmsnorm(o_ref[:, vs].astype(F32), hg_ref[:, vs])
        y_ref[:, vs] = (on * _silu(z_ref[:, vs].astype(F32))).astype(BF16)
    x = x_ref[...] + _dot(y_ref[...], wout_ref[...])
    out_ref[...] = _rmsnorm(x, g_ref[...])


def _odd_out(o, z, x1, head_g, w_out, g):
    t, d = x1.shape
    dm = o.shape[1]
    tok = lambda i: (i, 0)
    const = lambda i: (0, 0)
    return pl.pallas_call(
        _odd_out_body,
        grid=(t // TOKEN_TILE,),
        in_specs=[
            pl.BlockSpec((TOKEN_TILE, dm), tok),
            pl.BlockSpec((TOKEN_TILE, dm), tok),
            pl.BlockSpec((TOKEN_TILE, d), tok),
            pl.BlockSpec((1, dm), const),
            pl.BlockSpec((dm, d), const),
            pl.BlockSpec((1, d), const),
        ],
        out_specs=pl.BlockSpec((TOKEN_TILE, d), tok),
        out_shape=jax.ShapeDtypeStruct((t, d), F32),
        scratch_shapes=[pltpu.VMEM((TOKEN_TILE, dm), BF16)],
        compiler_params=_cparams(1),
        name="odd_out",
    )(o, z, x1, head_g.reshape(1, dm), w_out, g.reshape(1, d))


def _layer1_body(x_ref, g_ref, w_ref, wg_ref, wa_ref, ba_ref, hg_ref, wout_ref, gf_ref, out_ref,
                 q_s, k_s, v_s, z_s, bc_s, y_s, s_ref, *, tiles_per_seq):
    L = G_CHUNK

    @pl.when(lax.rem(pl.program_id(0), tiles_per_seq) == 0)
    def _init():
        s_ref[...] = jnp.zeros_like(s_ref)

    _proj_odd_body(x_ref, g_ref, w_ref, wg_ref, wa_ref, ba_ref, q_s, k_s, v_s, z_s, bc_s)

    row = lax.broadcasted_iota(jnp.int32, (L, L), 0)
    col = lax.broadcasted_iota(jnp.int32, (L, L), 1)
    causal = row >= col
    mid = L // 2 - 1
    for c, h in [(c, h) for c in range(0, TOKEN_TILE, L) for h in range(G_HEADS)]:
        r = slice(c, c + L)
        ks = slice(h * G_DK, (h + 1) * G_DK)
        vs = slice(h * G_DV, (h + 1) * G_DV)
        b = bc_s[r, ks]
        bm = b[mid:mid + 1, :]
        g = b[L - 1:L, :]
        d = b - bm
        qt = q_s[r, ks].astype(F32) * jnp.exp(d)
        kt = k_s[r, ks].astype(F32) * jnp.exp(-d)
        attn = jnp.where(causal, _dot_nt(qt.astype(BF16), kt.astype(BF16)), 0.0)
        v_h = v_s[r, vs]
        s_prev = s_ref[h]
        qi = qt * jnp.exp(bm)
        o = _dot(attn.astype(BF16), v_h) + _dot(qi.astype(BF16), s_prev.astype(BF16))
        y_s[r, vs] = (_rmsnorm(o, hg_ref[:, vs]) * _silu(z_s[r, vs].astype(F32))).astype(BF16)
        ke = kt * jnp.exp(g - bm)
        g_col = jnp.broadcast_to(jnp.exp(g), (LANES, G_DK)).T[:, 0:1]
        s_ref[h] = g_col * s_prev + _dot(ke.T.astype(BF16), v_h)

    x = x_ref[...] + _dot(y_s[...], wout_ref[...])
    out_ref[...] = _rmsnorm(x, gf_ref[...])


def _layer1(x1, g, wm, wg, w_alpha, b_alpha, head_g, w_out, gf, seq):
    t, d = x1.shape
    dk = G_HEADS * G_DK
    assert seq % TOKEN_TILE == 0 and TOKEN_TILE % G_CHUNK == 0
    tok = lambda i: (i, 0)
    const = lambda i: (0, 0)
    resident = lambda shape: pl.BlockSpec(shape, const, pipeline_mode=pl.Buffered(1))
    return pl.pallas_call(
        functools.partial(_layer1_body, tiles_per_seq=seq // TOKEN_TILE),
        grid=(t // TOKEN_TILE,),
        in_specs=[
            pl.BlockSpec((TOKEN_TILE, d), tok),
            pl.BlockSpec((1, d), const),
            resident((d, wm.shape[1])),
            pl.BlockSpec((d, G_RANK_PAD), const),
            pl.BlockSpec((G_RANK_PAD, dk), const),
            pl.BlockSpec((1, dk), const),
            pl.BlockSpec((1, D_MIX), const),
            resident((D_MIX, d)),
            pl.BlockSpec((1, d), const),
        ],
        out_specs=pl.BlockSpec((TOKEN_TILE, d), tok),
        out_shape=jax.ShapeDtypeStruct((t, d), F32),
        scratch_shapes=[
            pltpu.VMEM((TOKEN_TILE, dk), BF16),
            pltpu.VMEM((TOKEN_TILE, dk), BF16),
            pltpu.VMEM((TOKEN_TILE, D_MIX), BF16),
            pltpu.VMEM((TOKEN_TILE, D_MIX), BF16),
            pltpu.VMEM((TOKEN_TILE, dk), F32),
            pltpu.VMEM((TOKEN_TILE, D_MIX), BF16),
            pltpu.VMEM((G_HEADS, G_DK, G_DV), F32),
        ],
        compiler_params=_cparams(1),
        name="layer1",
    )(x1, g.reshape(1, d), wm, wg, w_alpha, b_alpha.reshape(1, dk), head_g.reshape(1, D_MIX), w_out,
      gf.reshape(1, d))


def kernel(x, norm_g, final_norm_g, ev_w_in, ev_conv_w, ev_conv_b, ev_i_bias, ev_f_bias, ev_head_g,
           s5_lam_re, s5_lam_im, s5_log_dt, s5_b_re, s5_b_im, s5_c_re, s5_c_im, s5_d, s5_glu_w,
           s5_glu_b, ev_w_out, od_w_in, gla_w_alpha, gla_b_alpha, gla_head_g, od_w_out):
    batch, seq, d = x.shape
    t = batch * seq
    xf = x.reshape(t, d)
    padc = lambda a: jnp.pad(a, ((0, 0), (0, LANES - a.shape[1])))

    w = ev_w_in.reshape(ev_w_in.shape[1:])
    g0 = 2 * M_HEADS * M_DK + 2 * M_HEADS * M_DV
    gi = g0 + M_HEADS
    gf = gi + M_HEADS
    wg = jnp.concatenate([padc(w[:, g0:gi]), padc(w[:, gi:gf])], axis=1).astype(BF16)
    half = D_MIX // 2
    qk, v, o, u, z, gates = _proj_even(xf, norm_g[0], w[:, :g0].astype(BF16), w[:, gf:].astype(BF16), wg,
                                       ev_conv_w[0], ev_conv_b[0], seq)
    gbias = jnp.concatenate([padc(ev_i_bias), padc(ev_f_bias)], axis=1)
    hm = _mlstm(qk, v, gates, gbias, batch, seq)

    y3 = _s5(u.reshape(batch, seq, half), s5_lam_re[0], s5_lam_im[0], s5_log_dt[0], s5_b_re[0],
             s5_b_im[0], s5_c_re[0], s5_c_im[0], s5_d[0])
    yb = y3.reshape(t, half)
    x1 = _even_out(yb, hm, o, z, xf, ev_head_g[0], s5_glu_w[0].astype(BF16), s5_glu_b[0],
                   ev_w_out[0].astype(BF16))

    w = od_w_in.reshape(od_w_in.shape[1:])
    n_main = 2 * G_HEADS * G_DK + 2 * D_MIX
    wa = jnp.pad(gla_w_alpha[0], ((0, G_RANK_PAD - gla_w_alpha.shape[1]), (0, 0)))
    dk = G_HEADS * G_DK
    colscale = jnp.where(jnp.arange(n_main) < dk, G_DK ** -0.5, 1.0).astype(F32)
    wm = (w[:, :n_main] * colscale).astype(BF16)
    out = _layer1(x1, norm_g[1], wm, padc(w[:, n_main:]).astype(BF16), wa, gla_b_alpha[0],
                  gla_head_g[0], od_w_out[0].astype(BF16), final_norm_g, seq)
    return out.reshape(batch, seq, d)
```

```python
import functools

import jax
import jax.numpy as jnp
from jax import lax
from jax.experimental import pallas as pl
from jax.experimental.pallas import tpu as pltpu

F32 = jnp.float32
BF16 = jnp.bfloat16

EPS = 1e-6
D_MODEL = 1024
D_MIX = 2 * D_MODEL
M_HEADS = 4
M_DK = 128
M_DV = 256
M_QK = 2 * M_HEADS * M_DK
CONV_WIDTH = 4
M_CHUNK = 256
M_SUB = 2
S5_GROUP = 16
S5_STATE = 64
S5_BLK = 8
S5_CHUNK = 1024
G_HEADS = 4
G_DK = 256
G_DV = 512
G_TAU = 16.0
G_CHUNK = 128
G_SUB = 2
G_RANK_PAD = 128

LANES = 128
SUBLANES = 8
HALO = 16
S5_GPL = LANES // S5_GROUP
S5_SW = S5_GPL * S5_STATE
S5_XROWS = S5_CHUNK // S5_BLK + SUBLANES
TOKEN_TILE = 512
PROJ_TN = 256
VMEM_LIMIT = 56 * 1024 * 1024


def _cparams(n_grid):
    return pltpu.CompilerParams(
        dimension_semantics=("arbitrary",) * n_grid, vmem_limit_bytes=VMEM_LIMIT)


def _log_sigmoid(x):
    return jnp.minimum(x, 0.0) - jnp.log(1.0 + jnp.exp(-jnp.abs(x)))


def _silu(x):
    return x * jax.nn.sigmoid(x)


def _split_hi_lo(x):
    hi = x.astype(BF16)
    lo = (x - hi.astype(F32)).astype(BF16)
    return hi, lo


def _dot(a, b):
    return jnp.dot(a, b, preferred_element_type=F32)


def _dot_nt(a, b, precision=None):
    return lax.dot_general(a, b, (((1,), (1,)), ((), ())), precision=precision,
                           preferred_element_type=F32)


def _tri_ones(n, lower):
    row = lax.broadcasted_iota(jnp.int32, (n, n), 0)
    col = lax.broadcasted_iota(jnp.int32, (n, n), 1)
    keep = (row >= col) if lower else (row <= col)
    return jnp.where(keep, 1.0, 0.0).astype(BF16)


def _rmsnorm(x, g):
    return x * lax.rsqrt(jnp.mean(x * x, axis=-1, keepdims=True) + EPS) * g


def _project(hb, w_ref, out_refs, col0):
    off = col0
    for o_ref in out_refs:
        n = o_ref.shape[1]
        for j in range(0, n, PROJ_TN):
            o_ref[:, j:j + PROJ_TN] = _dot(hb, w_ref[:, off + j:off + j + PROJ_TN]).astype(o_ref.dtype)
        off += n


CAST_ROWS = 128


def _cast_body(wt_ref, *out_refs, starts, narrow, scaled_cols, scale):
    for o_ref, c0 in zip(out_refs[:-1], starts):
        n = o_ref.shape[1]
        val = wt_ref[c0:c0 + n, :]
        if scaled_cols and c0 == 0:
            col = lax.broadcasted_iota(jnp.int32, (n, 1), 0)
            val = val * jnp.where(col < scaled_cols, scale, 1.0)
        o_ref[...] = val.T.astype(BF16)
    lane = lax.broadcasted_iota(jnp.int32, (1, LANES), 1)
    for blk, (c0, n) in enumerate(narrow):
        start = (c0 // SUBLANES) * SUBLANES
        take = -(-(c0 - start + n) // SUBLANES) * SUBLANES
        rows = jnp.concatenate([wt_ref[start:start + take, :], jnp.zeros((LANES - take, CAST_ROWS), F32)], axis=0)
        slab = rows.T
        if c0 != start:
            slab = pltpu.roll(slab, LANES - (c0 - start), axis=1)
        out_refs[-1][:, blk * LANES:(blk + 1) * LANES] = jnp.where(lane < n, slab, 0.0).astype(BF16)


def _cast_weights(wt, starts, widths, narrow, scaled_cols=0, scale=1.0):
    cols, rows = wt.shape
    assert CAST_ROWS == LANES
    widths = tuple(widths) + (len(narrow) * LANES,)
    return pl.pallas_call(
        functools.partial(_cast_body, starts=starts, narrow=narrow, scaled_cols=scaled_cols, scale=scale),
        grid=(rows // CAST_ROWS,),
        in_specs=[pl.BlockSpec((cols, CAST_ROWS), lambda i: (0, i))],
        out_specs=[pl.BlockSpec((CAST_ROWS, n), lambda i: (i, 0)) for n in widths],
        out_shape=[jax.ShapeDtypeStruct((rows, n), BF16) for n in widths],
        compiler_params=_cparams(1),
        name="cast_weights",
    )(wt)


def _proj_even_body(x_ref, xh_ref, g_ref, w_ref, wb_ref, wg_ref, convw_ref, convb_ref,
                    qk_ref, v_ref, o_ref, u_ref, z_ref, gates_ref, ext_ref, *, tiles_per_seq):
    hb = _rmsnorm(x_ref[...], g_ref[...]).astype(BF16)
    hh = _rmsnorm(xh_ref[...], g_ref[...]).astype(BF16)
    seq_start = lax.rem(pl.program_id(0), tiles_per_seq) == 0
    base = HALO - (CONV_WIDTH - 1)
    lane = lax.broadcasted_iota(jnp.int32, (1, PROJ_TN), 1)
    for j in range(0, M_QK, PROJ_TN):
        cols = slice(j, j + PROJ_TN)
        wj = w_ref[:, cols]
        ext_ref[HALO:HALO + TOKEN_TILE, :] = _dot(hb, wj)
        ext_ref[0:HALO, :] = jnp.where(seq_start, 0.0, _dot(hh, wj))
        acc = convb_ref[:, cols] + convw_ref[0:1, cols] * ext_ref[base:base + TOKEN_TILE, :]
        for i in range(1, CONV_WIDTH):
            acc = acc + convw_ref[i:i + 1, cols] * ext_ref[base + i:base + i + TOKEN_TILE, :]
        scale = jnp.where(lane + j < M_HEADS * M_DK, M_DK ** -0.5, 1.0)
        qk_ref[:, cols] = (_silu(acc) * scale).astype(BF16)
    _project(hb, w_ref, (v_ref, o_ref), M_QK)
    _project(hb, wb_ref, (u_ref, z_ref), 0)
    gates_ref[...] = _dot(hb, wg_ref[...])


def _proj_even(xf, g, wm, wb, wg, conv_w, conv_b, seq):
    t, d = xf.shape
    half = D_MIX // 2
    assert seq % TOKEN_TILE == 0 and TOKEN_TILE % HALO == 0
    tok = lambda i: (i, 0)
    const = lambda i: (0, 0)
    per_halo = TOKEN_TILE // HALO
    widths = (M_QK, half, half, half, D_MIX)
    dtypes = (BF16, BF16, BF16, F32, BF16)
    out_shape = [jax.ShapeDtypeStruct((t, n), dt) for n, dt in zip(widths, dtypes)]
    ngate = wg.shape[1]
    out_shape.append(jax.ShapeDtypeStruct((t, ngate), F32))
    out_specs = [pl.BlockSpec((TOKEN_TILE, n), tok) for n in widths]
    out_specs.append(pl.BlockSpec((TOKEN_TILE, ngate), tok))
    return pl.pallas_call(
        functools.partial(_proj_even_body, tiles_per_seq=seq // TOKEN_TILE),
        grid=(t // TOKEN_TILE,),
        in_specs=[
            pl.BlockSpec((TOKEN_TILE, d), tok),
            pl.BlockSpec((HALO, d), lambda i: (jnp.maximum(i * per_halo - 1, 0), 0)),
            pl.BlockSpec((1, d), const),
            pl.BlockSpec((d, wm.shape[1]), const, pipeline_mode=pl.Buffered(1)),
            pl.BlockSpec((d, wb.shape[1]), const, pipeline_mode=pl.Buffered(1)),
            pl.BlockSpec((d, ngate), const),
            pl.BlockSpec((CONV_WIDTH, M_QK), const),
            pl.BlockSpec((1, M_QK), const),
        ],
        out_specs=out_specs,
        out_shape=out_shape,
        scratch_shapes=[pltpu.VMEM((HALO + TOKEN_TILE, PROJ_TN), F32)],
        compiler_params=_cparams(1),
        name="proj_even",
    )(xf, xf, g.reshape(1, d), wm, wb, wg, conv_w, conv_b.reshape(1, M_QK))


def _mlstm_body(qk_ref, v_ref, gates_ref, gbias_ref, out_ref, c_ref, n_ref, m_ref):
    @pl.when(pl.program_id(1) == 0)
    def _init():
        c_ref[...] = jnp.zeros_like(c_ref)
        n_ref[...] = jnp.zeros_like(n_ref)
        m_ref[...] = jnp.zeros_like(m_ref)

    for sub in range(M_SUB):
        _mlstm_chunk(slice(sub * M_CHUNK, (sub + 1) * M_CHUNK), qk_ref, v_ref, gates_ref, gbias_ref,
                     out_ref, c_ref, n_ref, m_ref)


def _mlstm_chunk(r, qk_ref, v_ref, gates_ref, gbias_ref, out_ref, c_ref, n_ref, m_ref):
    L = M_CHUNK
    gt = gates_ref[r, :] + gbias_ref[...]
    ipre = gt[:, 0:LANES]
    logf = _log_sigmoid(gt[:, LANES:2 * LANES]).astype(BF16)
    b = _dot(_tri_ones(L, True), logf)
    w = ipre - b
    rows = lax.broadcasted_iota(jnp.int32, (L, LANES), 0)
    cm = w
    k = 1
    while k < L:
        cm = jnp.maximum(cm, jnp.where(rows >= k, pltpu.roll(cm, k, axis=0), -jnp.inf))
        k *= 2

    sel_r = lax.broadcasted_iota(jnp.int32, (LANES, M_HEADS * LANES), 0)
    sel_c = lax.broadcasted_iota(jnp.int32, (LANES, M_HEADS * LANES), 1)
    spread = jnp.where(sel_r == lax.shift_right_logical(sel_c, 7), 1.0, 0.0).astype(BF16)

    def replicate(x):
        hi, lo = _split_hi_lo(x)
        return _dot(hi, spread) + _dot(lo, spread)

    b_rep = replicate(b)
    w_rep = replicate(w)
    cm_rep = replicate(cm)
    pick_r = lax.broadcasted_iota(jnp.int32, (M_HEADS * SUBLANES, LANES), 0)
    pick_c = lax.broadcasted_iota(jnp.int32, (M_HEADS * SUBLANES, LANES), 1)
    pick = jnp.where(lax.shift_right_logical(pick_r, 3) == pick_c, 1.0, 0.0).astype(BF16)
    w_hi, w_lo = _split_hi_lo(w)
    w_row = _dot_nt(pick, w_hi) + _dot_nt(pick, w_lo)

    trow = lax.broadcasted_iota(jnp.int32, (L, LANES), 0)
    tcol = lax.broadcasted_iota(jnp.int32, (L, LANES), 1)
    ones = jnp.ones((L, LANES), BF16)

    for h in range(M_HEADS):
        ks = slice(h * M_DK, (h + 1) * M_DK)
        ks2 = slice(M_HEADS * M_DK + h * M_DK, M_HEADS * M_DK + (h + 1) * M_DK)
        vs = slice(h * M_DV, (h + 1) * M_DV)
        hs = slice(h * LANES, (h + 1) * LANES)
        m_prev = m_ref[h, 0:1, :]
        c_prev = c_ref[h]
        n_prev = n_ref[h]
        big_m = jnp.maximum(m_prev, cm_rep[:, hs])
        w_inter = jnp.exp(m_prev - big_m)
        wr = w_row[h * SUBLANES:h * SUBLANES + 1, :]

        qb = qk_ref[r, ks]
        kb = qk_ref[r, ks2]
        v_h = v_ref[r, vs]
        s = _dot_nt(qb, kb)
        sc = jnp.concatenate(
            [jnp.where(trow >= tcol + j, jnp.exp(wr[:, j:j + LANES] - big_m), 0.0) * s[:, j:j + LANES]
             for j in range(0, L, LANES)], axis=1).astype(BF16)
        q_c = _dot(qb, c_prev.astype(BF16))
        den = _dot(sc, ones) + w_inter * _dot(qb, n_prev.astype(BF16))
        inv = 1.0 / jnp.maximum(jnp.abs(den), jnp.exp(-(b_rep[:, hs] + big_m)))
        num = _dot(sc, v_h)
        out_ref[r, vs] = jnp.concatenate(
            [(num[:, j:j + LANES] + w_inter * q_c[:, j:j + LANES]) * inv for j in range(0, M_DV, LANES)],
            axis=1).astype(BF16)

        g = b_rep[L - 1:L, hs]
        cm_last = cm_rep[L - 1:L, hs]
        m_last = big_m[L - 1:L, :]
        kw_t = (kb.astype(F32) * jnp.exp(w_rep[:, hs] - cm_last)).T.astype(BF16)
        s_prev = jnp.exp(m_prev - m_last)
        s_loc = jnp.exp(cm_last - m_last)
        c_ref[h] = (jnp.concatenate([s_prev] * (M_DV // LANES), axis=1) * c_prev
                    + jnp.concatenate([s_loc] * (M_DV // LANES), axis=1) * _dot(kw_t, v_h))
        n_ref[h] = s_prev * n_prev + s_loc * _dot(kw_t, ones)
        m_ref[h] = jnp.broadcast_to(g + m_last, (SUBLANES, LANES))


def _mlstm(qk, v, gates, gbias, batch, seq):
    t = batch * seq
    L = M_CHUNK * M_SUB
    nc = seq // L
    dv = M_HEADS * M_DV
    tok = lambda b, c: (b * nc + c, 0)
    const = lambda b, c: (0, 0)
    return pl.pallas_call(
        _mlstm_body,
        grid=(batch, nc),
        in_specs=[
            pl.BlockSpec((L, M_QK), tok),
            pl.BlockSpec((L, dv), tok),
            pl.BlockSpec((L, 2 * LANES), tok),
            pl.BlockSpec((1, 2 * LANES), const),
        ],
        out_specs=pl.BlockSpec((L, dv), tok),
        out_shape=jax.ShapeDtypeStruct((t, dv), BF16),
        scratch_shapes=[
            pltpu.VMEM((M_HEADS, M_DK, M_DV), F32),
            pltpu.VMEM((M_HEADS, M_DK, LANES), F32),
            pltpu.VMEM((M_HEADS, SUBLANES, LANES), F32),
        ],
        compiler_params=_cparams(2),
        name="mlstm",
    )(qk, v, gates, gbias)


def _gelu_tanh(x):
    return 0.5 * x * (1.0 + jnp.tanh(0.7978845608028654 * (x + 0.044715 * (x * x * x))))


def _s5_build_operators(lamr_ref, lami_ref, ldt_ref, btr_ref, bti_ref, ctr_ref, cti_ref, d_ref,
                        pbig_ref, qbig_ref, mbig_ref, a_ref):
    lr = lamr_ref[0]
    li = lami_ref[0]
    dt = jnp.exp(ldt_ref[0])
    zr = lr * dt
    th = li * dt
    er = jnp.exp(zr)
    ar = er * jnp.cos(th)
    ai = er * jnp.sin(th)
    den = lr * lr + li * li
    beta_r = ((ar - 1.0) * lr + ai * li) / den
    beta_i = (ai * lr - (ar - 1.0) * li) / den
    btr = btr_ref[0]
    bti = bti_ref[0]
    bbr = btr * beta_r - bti * beta_i
    bbi = btr * beta_i + bti * beta_r
    ctr = ctr_ref[0]
    cti = cti_ref[0]

    row_g = lax.shift_right_logical(lax.broadcasted_iota(jnp.int32, (LANES, S5_SW), 0), 4)
    lane_g = lax.shift_right_logical(lax.broadcasted_iota(jnp.int32, (LANES, S5_SW), 1), 6)
    same_group = row_g == lane_g

    def expand(x16):
        return jnp.where(same_group, jnp.concatenate([x16] * S5_GPL, axis=0), 0.0)

    def power(k):
        e = jnp.exp(float(k) * zr)
        return e * jnp.cos(float(k) * th), e * jnp.sin(float(k) * th)

    for s in range(S5_BLK):
        rows = slice(s * LANES, (s + 1) * LANES)
        pr, pi = power(S5_BLK - 1 - s)
        pbig_ref[rows, 0:S5_SW] = expand(pr * bbr - pi * bbi).astype(BF16)
        pbig_ref[rows, S5_SW:2 * S5_SW] = expand(pr * bbi + pi * bbr).astype(BF16)
        pr, pi = power(s + 1)
        qbig_ref[rows, 0:S5_SW] = expand(ctr * pr - cti * pi).astype(BF16)
        qbig_ref[rows, S5_SW:2 * S5_SW] = expand(-(ctr * pi + cti * pr)).astype(BF16)

    cb = jnp.concatenate([expand(ctr), expand(-cti)], axis=1)
    r128 = lax.broadcasted_iota(jnp.int32, (LANES, LANES), 0)
    c128 = lax.broadcasted_iota(jnp.int32, (LANES, LANES), 1)
    zero_blk = jnp.zeros((LANES, LANES), BF16)
    ab_lags = []
    for lag in range(S5_BLK):
        pr, pi = power(lag)
        ab_lags.append(jnp.concatenate([expand(pr * bbr - pi * bbi), expand(pr * bbi + pi * bbr)], axis=1))
    ab_hi, ab_lo = _split_hi_lo(jnp.concatenate(ab_lags, axis=0))
    cb_hi, cb_lo = _split_hi_lo(cb)
    v_lags = _dot_nt(ab_hi, cb_hi) + _dot_nt(ab_hi, cb_lo) + _dot_nt(ab_lo, cb_hi)
    for lag in range(S5_BLK):
        v = v_lags[lag * LANES:(lag + 1) * LANES, :]
        if lag == 0:
            v = v + jnp.where(r128 == c128, d_ref[0], 0.0)
        vb = v.astype(BF16)
        for s in range(S5_BLK - lag):
            t = s + lag
            mbig_ref[s * LANES:(s + 1) * LANES, t * LANES:(t + 1) * LANES] = vb
            if lag > 0:
                mbig_ref[t * LANES:(t + 1) * LANES, s * LANES:(s + 1) * LANES] = zero_blk

    pr, pi = power(S5_BLK)
    a_ref[:, 0:S5_SW] = jnp.broadcast_to(pr, (SUBLANES, S5_SW))
    a_ref[:, S5_SW:2 * S5_SW] = jnp.broadcast_to(pi, (SUBLANES, S5_SW))


def _s5_body(u_ref, lamr_ref, lami_ref, ldt_ref, btr_ref, bti_ref, ctr_ref, cti_ref, d_ref, y_ref,
             pbig_ref, qbig_ref, mbig_ref, a_ref, ucat_ref, x_ref, xp_ref, st_ref):
    nblk = S5_CHUNK // S5_BLK
    batch = u_ref.shape[0]

    @pl.when(pl.program_id(1) == 0)
    def _setup():
        _s5_build_operators(lamr_ref, lami_ref, ldt_ref, btr_ref, bti_ref, ctr_ref, cti_ref, d_ref,
                            pbig_ref, qbig_ref, mbig_ref, a_ref)
        st_ref[...] = jnp.zeros_like(st_ref)

    for b in range(batch):
        for s in range(S5_BLK):
            piece = u_ref[b, pl.ds(s, nblk, stride=S5_BLK), :]
            ucat_ref[b * nblk:(b + 1) * nblk, s * LANES:(s + 1) * LANES] = piece.astype(BF16)
    nslab = 2 * S5_SW // LANES
    half = nslab // 2
    xloc = _dot(ucat_ref[...], pbig_ref[...])
    for c in range(nslab):
        for b in range(batch):
            x_ref[c, b * S5_XROWS:b * S5_XROWS + nblk, :] = (
                xloc[b * nblk:(b + 1) * nblk, c * LANES:(c + 1) * LANES])

    lanes = lambda ref, c: ref[:, c * LANES:(c + 1) * LANES]
    ar = [lanes(a_ref, c) for c in range(half)]
    ai = [lanes(a_ref, half + c) for c in range(half)]
    xr = [lanes(st_ref, c) for c in range(half)]
    xi = [lanes(st_ref, half + c) for c in range(half)]
    for blk in range(nblk):
        r = pl.ds(blk, batch, stride=S5_XROWS)
        for c in range(half):
            xp_ref[c, r, :] = xr[c]
            xp_ref[half + c, r, :] = xi[c]
            nr = ar[c] * xr[c] - ai[c] * xi[c] + x_ref[c, r, :]
            ni = ar[c] * xi[c] + ai[c] * xr[c] + x_ref[half + c, r, :]
            xr[c], xi[c] = nr, ni
    for c in range(half):
        st_ref[:, c * LANES:(c + 1) * LANES] = xr[c]
        st_ref[:, (half + c) * LANES:(half + c + 1) * LANES] = xi[c]

    xpb = jnp.concatenate(
        [jnp.concatenate([xp_ref[c, b * S5_XROWS:b * S5_XROWS + nblk, :].astype(BF16)
                          for c in range(nslab)], axis=1) for b in range(batch)], axis=0)
    width = 2 * LANES
    for nb in range(S5_BLK // 2):
        kk = (2 * nb + 2) * LANES
        cols = slice(nb * width, (nb + 1) * width)
        y = _dot(ucat_ref[:, 0:kk], mbig_ref[0:kk, cols]) + _dot_nt(xpb, qbig_ref[cols, :])
        y = _gelu_tanh(y)
        for tt in range(2):
            t = 2 * nb + tt
            for b in range(batch):
                y_ref[b, pl.ds(t, nblk, stride=S5_BLK), :] = (
                    y[b * nblk:(b + 1) * nblk, tt * LANES:(tt + 1) * LANES])


def _s5(u3, lam_re, lam_im, log_dt, b_re, b_im, c_re, c_im, d_skip):
    batch, seq, width = u3.shape
    nlb = width // LANES
    assert batch == SUBLANES and seq % S5_CHUNK == 0
    lamr = lam_re.reshape(nlb, 1, S5_SW)
    lami = lam_im.reshape(nlb, 1, S5_SW)
    ldt = jnp.repeat(log_dt, S5_STATE).reshape(nlb, 1, S5_SW)
    bt = lambda b: b.reshape(nlb, S5_GPL, S5_STATE, S5_GROUP).transpose(0, 3, 1, 2).reshape(nlb, S5_GROUP, S5_SW)
    ct = lambda c: c.reshape(nlb, S5_GPL, S5_GROUP, S5_STATE).transpose(0, 2, 1, 3).reshape(nlb, S5_GROUP, S5_SW)
    par = lambda r, w: pl.BlockSpec((1, r, w), lambda i, j: (i, 0, 0))
    kdim = S5_BLK * LANES
    rows = (S5_CHUNK // S5_BLK) * batch
    io = pl.BlockSpec((batch, S5_CHUNK, LANES), lambda i, j: (0, j, i))
    return pl.pallas_call(
        _s5_body,
        grid=(nlb, seq // S5_CHUNK),
        in_specs=[io, par(1, S5_SW), par(1, S5_SW), par(1, S5_SW), par(S5_GROUP, S5_SW),
                  par(S5_GROUP, S5_SW), par(S5_GROUP, S5_SW), par(S5_GROUP, S5_SW), par(1, LANES)],
        out_specs=io,
        out_shape=jax.ShapeDtypeStruct((batch, seq, width), F32),
        scratch_shapes=[
            pltpu.VMEM((kdim, 2 * S5_SW), BF16),
            pltpu.VMEM((kdim, 2 * S5_SW), BF16),
            pltpu.VMEM((kdim, kdim), BF16),
            pltpu.VMEM((SUBLANES, 2 * S5_SW), F32),
            pltpu.VMEM((rows, kdim), BF16),
            pltpu.VMEM((2 * S5_SW // LANES, batch * S5_XROWS, LANES), F32),
            pltpu.VMEM((2 * S5_SW // LANES, batch * S5_XROWS, LANES), F32),
            pltpu.VMEM((SUBLANES, 2 * S5_SW), F32),
        ],
        compiler_params=_cparams(2),
        name="s5",
    )(u3, lamr, lami, ldt, bt(b_re), bt(b_im), ct(c_re), ct(c_im), d_skip.reshape(nlb, 1, LANES))


def _even_out_body(yb_ref, h_ref, o_ref, z_ref, x_ref, hg_ref, gluw_ref, glub_ref, wout_ref,
                   out_ref, y_ref):
    half = h_ref.shape[1]
    for h in range(M_HEADS):
        vs = slice(h * M_DV, (h + 1) * M_DV)
        og = jax.nn.sigmoid(o_ref[:, vs].astype(F32)) * h_ref[:, vs].astype(F32)
        y_ref[:, vs] = (_rmsnorm(og, hg_ref[:, vs]) * _silu(z_ref[:, vs].astype(F32))).astype(BF16)
    yg = yb_ref[...]
    s = _dot(yg.astype(BF16), gluw_ref[...]) + glub_ref[...]
    hb = yg * jax.nn.sigmoid(s)
    y_ref[:, half:2 * half] = (hb * _silu(z_ref[:, half:2 * half].astype(F32))).astype(BF16)
    out_ref[...] = x_ref[...] + _dot(y_ref[...], wout_ref[...])


def _even_out(yb, hm, o, z, xf, head_g, glu_w, glu_b, w_out):
    t, d = xf.shape
    half = hm.shape[1]
    tok = lambda i: (i, 0)
    const = lambda i: (0, 0)
    return pl.pallas_call(
        _even_out_body,
        grid=(t // TOKEN_TILE,),
        in_specs=[
            pl.BlockSpec((TOKEN_TILE, half), tok),
            pl.BlockSpec((TOKEN_TILE, half), tok),
            pl.BlockSpec((TOKEN_TILE, half), tok),
            pl.BlockSpec((TOKEN_TILE, 2 * half), tok),
            pl.BlockSpec((TOKEN_TILE, d), tok),
            pl.BlockSpec((1, half), const),
            pl.BlockSpec((half, half), const),
            pl.BlockSpec((1, half), const),
            pl.BlockSpec((2 * half, d), const),
        ],
        out_specs=pl.BlockSpec((TOKEN_TILE, d), tok),
        out_shape=jax.ShapeDtypeStruct((t, d), F32),
        scratch_shapes=[pltpu.VMEM((TOKEN_TILE, 2 * half), BF16)],
        compiler_params=_cparams(1),
        name="even_out",
    )(yb, hm, o, z, xf, head_g.reshape(1, half), glu_w, glu_b.reshape(1, half), w_out)


def _proj_odd_body(x_ref, g_ref, w_ref, wg_ref, wa_ref, ba_ref, q_ref, k_ref, v_ref, z_ref, bc_ref):
    hb = _rmsnorm(x_ref[...], g_ref[...]).astype(BF16)
    _project(hb, w_ref, (q_ref, k_ref, v_ref, z_ref), 0)
    rb = _dot(hb, wg_ref[...]).astype(BF16)
    tril = _tri_ones(G_CHUNK, True)
    for j in range(0, bc_ref.shape[1], PROJ_TN):
        cols = slice(j, j + PROJ_TN)
        pre = _dot(rb, wa_ref[:, cols].astype(BF16)) + ba_ref[:, cols]
        la = (_log_sigmoid(pre) * (1.0 / G_TAU)).astype(BF16)
        for c in range(0, TOKEN_TILE, G_CHUNK):
            rows = slice(c, c + G_CHUNK)
            bc_ref[rows, cols] = _dot(tril, la[rows])


def _proj_odd(xf, g, wm, wg, w_alpha, b_alpha):
    t, d = xf.shape
    dk = G_HEADS * G_DK
    assert TOKEN_TILE % G_CHUNK == 0
    tok = lambda i: (i, 0)
    const = lambda i: (0, 0)
    widths = (dk, dk, D_MIX, D_MIX, dk)
    dtypes = (BF16, BF16, BF16, BF16, F32)
    return pl.pallas_call(
        _proj_odd_body,
        grid=(t // TOKEN_TILE,),
        in_specs=[
            pl.BlockSpec((TOKEN_TILE, d), tok),
            pl.BlockSpec((1, d), const),
            pl.BlockSpec((d, wm.shape[1]), const, pipeline_mode=pl.Buffered(1)),
            pl.BlockSpec((d, G_RANK_PAD), const),
            pl.BlockSpec((G_RANK_PAD, dk), const),
            pl.BlockSpec((1, dk), const),
        ],
        out_specs=[pl.BlockSpec((TOKEN_TILE, n), tok) for n in widths],
        out_shape=[jax.ShapeDtypeStruct((t, n), dt) for n, dt in zip(widths, dtypes)],
        compiler_params=_cparams(1),
        name="proj_odd",
    )(xf, g.reshape(1, d), wm, wg, w_alpha, b_alpha.reshape(1, dk))


def _gla_body(q_ref, k_ref, v_ref, bc_ref, out_ref, s_ref):
    L = G_CHUNK

    @pl.when(pl.program_id(1) == 0)
    def _init():
        s_ref[...] = jnp.zeros_like(s_ref)

    row = lax.broadcasted_iota(jnp.int32, (L, L), 0)
    col = lax.broadcasted_iota(jnp.int32, (L, L), 1)
    causal = row >= col
    mid = L // 2 - 1

    for sub, h in [(sub, h) for sub in range(G_SUB) for h in range(G_HEADS)]:
        r = slice(sub * L, (sub + 1) * L)
        ks = slice(h * G_DK, (h + 1) * G_DK)
        vs = slice(h * G_DV, (h + 1) * G_DV)
        b = bc_ref[r, ks]
        bm = b[mid:mid + 1, :]
        g = b[L - 1:L, :]
        e1 = jnp.exp(b - bm)
        e2 = jnp.exp(bm - b)
        qt = q_ref[r, ks].astype(F32) * (G_DK ** -0.5) * e1
        kt = k_ref[r, ks].astype(F32) * e2
        attn = jnp.where(causal, _dot_nt(qt.astype(BF16), kt.astype(BF16)), 0.0)
        v_h = v_ref[r, vs]
        s_prev = s_ref[h]
        qi = qt * jnp.exp(bm)
        o = _dot(attn.astype(BF16), v_h) + _dot(qi.astype(BF16), s_prev.astype(BF16))
        out_ref[r, vs] = o.astype(BF16)
        ke = kt * jnp.exp(g - bm)
        g_col = jnp.broadcast_to(jnp.exp(g), (LANES, G_DK)).T[:, 0:1]
        s_ref[h] = g_col * s_prev + _dot(ke.T.astype(BF16), v_h)


def _gla(q, k, v, bc, batch, seq):
    t = batch * seq
    L = G_CHUNK * G_SUB
    nc = seq // L
    dk = G_HEADS * G_DK
    dv = G_HEADS * G_DV
    tok = lambda b, c: (b * nc + c, 0)
    return pl.pallas_call(
        _gla_body,
        grid=(batch, nc),
        in_specs=[
            pl.BlockSpec((L, dk), tok),
            pl.BlockSpec((L, dk), tok),
            pl.BlockSpec((L, dv), tok),
            pl.BlockSpec((L, dk), tok),
        ],
        out_specs=pl.BlockSpec((L, dv), tok),
        out_shape=jax.ShapeDtypeStruct((t, dv), BF16),
        scratch_shapes=[pltpu.VMEM((G_HEADS, G_DK, G_DV), F32)],
        compiler_params=_cparams(2),
        name="gla",
    )(q, k, v, bc)


def _odd_out_body(o_ref, z_ref, x_ref, hg_ref, wout_ref, g_ref, out_ref, y_ref):
    for h in range(G_HEADS):
        vs = slice(h * G_DV, (h + 1) * G_DV)
        on = _rmsnorm(o_ref[:, vs].astype(F32), hg_ref[:, vs])
        y_ref[:, vs] = (on * _silu(z_ref[:, vs].astype(F32))).astype(BF16)
    x = x_ref[...] + _dot(y_ref[...], wout_ref[...])
    out_ref[...] = _rmsnorm(x, g_ref[...])


def _odd_out(o, z, x1, head_g, w_out, g):
    t, d = x1.shape
    dm = o.shape[1]
    tok = lambda i: (i, 0)
    const = lambda i: (0, 0)
    return pl.pallas_call(
        _odd_out_body,
        grid=(t // TOKEN_TILE,),
        in_specs=[
            pl.BlockSpec((TOKEN_TILE, dm), tok),
            pl.BlockSpec((TOKEN_TILE, dm), tok),
            pl.BlockSpec((TOKEN_TILE, d), tok),
            pl.BlockSpec((1, dm), const),
            pl.BlockSpec((dm, d), const),
            pl.BlockSpec((1, d), const),
        ],
        out_specs=pl.BlockSpec((TOKEN_TILE, d), tok),
        out_shape=jax.ShapeDtypeStruct((t, d), F32),
        scratch_shapes=[pltpu.VMEM((TOKEN_TILE, dm), BF16)],
        compiler_params=_cparams(1),
        name="odd_out",
    )(o, z, x1, head_g.reshape(1, dm), w_out, g.reshape(1, d))


def _layer1_body(x_ref, g_ref, w_ref, wg_ref, wa_ref, ba_ref, hg_ref, wout_ref, gf_ref, out_ref,
                 q_s, k_s, v_s, z_s, bc_s, y_s, s_ref, *, tiles_per_seq):
    L = G_CHUNK

    @pl.when(lax.rem(pl.program_id(0), tiles_per_seq) == 0)
    def _init():
        s_ref[...] = jnp.zeros_like(s_ref)

    _proj_odd_body(x_ref, g_ref, w_ref, wg_ref, wa_ref, ba_ref, q_s, k_s, v_s, z_s, bc_s)

    row = lax.broadcasted_iota(jnp.int32, (L, L), 0)
    col = lax.broadcasted_iota(jnp.int32, (L, L), 1)
    causal = row >= col
    mid = L // 2 - 1
    for c, h in [(c, h) for c in range(0, TOKEN_TILE, L) for h in range(G_HEADS)]:
        r = slice(c, c + L)
        ks = slice(h * G_DK, (h + 1) * G_DK)
        vs = slice(h * G_DV, (h + 1) * G_DV)
        b = bc_s[r, ks]
        bm = b[mid:mid + 1, :]
        g = b[L - 1:L, :]
        d = b - bm
        qt = q_s[r, ks].astype(F32) * jnp.exp(d)
        kt = k_s[r, ks].astype(F32) * jnp.exp(-d)
        attn = jnp.where(causal, _dot_nt(qt.astype(BF16), kt.astype(BF16)), 0.0)
        v_h = v_s[r, vs]
        s_prev = s_ref[h]
        qi = qt * jnp.exp(bm)
        o = _dot(attn.astype(BF16), v_h) + _dot(qi.astype(BF16), s_prev.astype(BF16))
        y_s[r, vs] = (_rmsnorm(o, hg_ref[:, vs]) * _silu(z_s[r, vs].astype(F32))).astype(BF16)
        ke = kt * jnp.exp(g - bm)
        g_col = jnp.broadcast_to(jnp.exp(g), (LANES, G_DK)).T[:, 0:1]
        s_ref[h] = g_col * s_prev + _dot(ke.T.astype(BF16), v_h)

    x = x_ref[...] + _dot(y_s[...], wout_ref[...])
    out_ref[...] = _rmsnorm(x, gf_ref[...])


def _layer1(x1, g, wm, wg, w_alpha, b_alpha, head_g, w_out, gf, seq):
    t, d = x1.shape
    dk = G_HEADS * G_DK
    assert seq % TOKEN_TILE == 0 and TOKEN_TILE % G_CHUNK == 0
    tok = lambda i: (i, 0)
    const = lambda i: (0, 0)
    resident = lambda shape: pl.BlockSpec(shape, const, pipeline_mode=pl.Buffered(1))
    slots = lambda n, dt: pltpu.VMEM((TOKEN_TILE, n), dt)
    return pl.pallas_call(
        functools.partial(_layer1_body, tiles_per_seq=seq // TOKEN_TILE),
        grid=(t // TOKEN_TILE,),
        in_specs=[
            pl.BlockSpec((TOKEN_TILE, d), tok),
            pl.BlockSpec((1, d), const),
            resident((d, wm.shape[1])),
            pl.BlockSpec((d, G_RANK_PAD), const),
            pl.BlockSpec((G_RANK_PAD, dk), const),
            pl.BlockSpec((1, dk), const),
            pl.BlockSpec((1, D_MIX), const),
            resident((D_MIX, d)),
            pl.BlockSpec((1, d), const),
        ],
        out_specs=pl.BlockSpec((TOKEN_TILE, d), tok),
        out_shape=jax.ShapeDtypeStruct((t, d), F32),
        scratch_shapes=[
            slots(dk, BF16),
            slots(dk, BF16),
            slots(D_MIX, BF16),
            slots(D_MIX, BF16),
            slots(dk, F32),
            slots(D_MIX, BF16),
            pltpu.VMEM((G_HEADS, G_DK, G_DV), F32),
        ],
        compiler_params=_cparams(1),
        name="layer1",
    )(x1, g.reshape(1, d), wm, wg, w_alpha, b_alpha.reshape(1, dk), head_g.reshape(1, D_MIX), w_out,
      gf.reshape(1, d))


def kernel(x, norm_g, final_norm_g, ev_w_in, ev_conv_w, ev_conv_b, ev_i_bias, ev_f_bias, ev_head_g,
           s5_lam_re, s5_lam_im, s5_log_dt, s5_b_re, s5_b_im, s5_c_re, s5_c_im, s5_d, s5_glu_w,
           s5_glu_b, ev_w_out, od_w_in, gla_w_alpha, gla_b_alpha, gla_head_g, od_w_out):
    batch, seq, d = x.shape
    t = batch * seq
    xf = x.reshape(t, d)
    padc = lambda a: jnp.pad(a, ((0, 0), (0, LANES - a.shape[1])))

    wt = jnp.swapaxes(ev_w_in, 1, 2).reshape(ev_w_in.shape[2], d)
    g0 = 2 * M_HEADS * M_DK + 2 * M_HEADS * M_DV
    gi = g0 + M_HEADS
    gf = gi + M_HEADS
    half = D_MIX // 2
    w_qkvo, w_uz, wg = _cast_weights(wt, (0, gf), (g0, half + D_MIX), ((g0, M_HEADS), (gi, M_HEADS)))
    qk, v, o, u, z, gates = _proj_even(xf, norm_g[0], w_qkvo, w_uz, wg, ev_conv_w[0], ev_conv_b[0], seq)
    gbias = jnp.concatenate([padc(ev_i_bias), padc(ev_f_bias)], axis=1)
    hm = _mlstm(qk, v, gates, gbias, batch, seq)

    y3 = _s5(u.reshape(batch, seq, half), s5_lam_re[0], s5_lam_im[0], s5_log_dt[0], s5_b_re[0],
             s5_b_im[0], s5_c_re[0], s5_c_im[0], s5_d[0])
    yb = y3.reshape(t, half)
    x1 = _even_out(yb, hm, o, z, xf, ev_head_g[0], s5_glu_w[0].astype(BF16), s5_glu_b[0],
                   ev_w_out[0].astype(BF16))

    wt = jnp.swapaxes(od_w_in, 1, 2).reshape(od_w_in.shape[2], d)
    n_main = 2 * G_HEADS * G_DK + 2 * D_MIX
    wa = jnp.pad(gla_w_alpha[0], ((0, G_RANK_PAD - gla_w_alpha.shape[1]), (0, 0)))
    wm, wr = _cast_weights(wt, (0,), (n_main,), ((n_main, wt.shape[0] - n_main),),
                           scaled_cols=G_HEADS * G_DK, scale=G_DK ** -0.5)
    out = _layer1(x1, norm_g[1], wm, wr, wa, gla_b_alpha[0],
                  gla_head_g[0], od_w_out[0].astype(BF16), final_norm_g, seq)
    return out.reshape(batch, seq, d)
```

```python
import functools

import jax
import jax.numpy as jnp
from jax import lax
from jax.experimental import pallas as pl
from jax.experimental.pallas import tpu as pltpu

F32 = jnp.float32
BF16 = jnp.bfloat16

EPS = 1e-6
D_MODEL = 1024
D_MIX = 2 * D_MODEL
M_HEADS = 4
M_DK = 128
M_DV = 256
M_QK = 2 * M_HEADS * M_DK
CONV_WIDTH = 4
M_CHUNK = 256
M_SUB = 2
S5_GROUP = 16
S5_STATE = 64
S5_BLK = 8
S5_CHUNK = 1024
G_HEADS = 4
G_DK = 256
G_DV = 512
G_TAU = 16.0
G_CHUNK = 128
G_RANK_PAD = 128

LANES = 128
SUBLANES = 8
HALO = 16
S5_GPL = LANES // S5_GROUP
S5_SW = S5_GPL * S5_STATE
S5_XROWS = S5_CHUNK // S5_BLK + SUBLANES
TOKEN_TILE = 512
PROJ_TN = 256
VMEM_LIMIT = 56 * 1024 * 1024


def _cparams(n_grid):
    return pltpu.CompilerParams(
        dimension_semantics=("arbitrary",) * n_grid, vmem_limit_bytes=VMEM_LIMIT)


def _log_sigmoid(x):
    return jnp.minimum(x, 0.0) - jnp.log(1.0 + jnp.exp(-jnp.abs(x)))


def _silu(x):
    return x * jax.nn.sigmoid(x)


def _split_hi_lo(x):
    hi = x.astype(BF16)
    lo = (x - hi.astype(F32)).astype(BF16)
    return hi, lo


def _dot(a, b):
    return jnp.dot(a, b, preferred_element_type=F32)


def _dot_nt(a, b, precision=None):
    return lax.dot_general(a, b, (((1,), (1,)), ((), ())), precision=precision,
                           preferred_element_type=F32)


def _tri_ones(n, lower):
    row = lax.broadcasted_iota(jnp.int32, (n, n), 0)
    col = lax.broadcasted_iota(jnp.int32, (n, n), 1)
    keep = (row >= col) if lower else (row <= col)
    return jnp.where(keep, 1.0, 0.0).astype(BF16)


def _rmsnorm(x, g):
    return x * lax.rsqrt(jnp.mean(x * x, axis=-1, keepdims=True) + EPS) * g


def _project(hb, w_ref, out_refs, col0):
    off = col0
    for o_ref in out_refs:
        n = o_ref.shape[1]
        for j in range(0, n, PROJ_TN):
            o_ref[:, j:j + PROJ_TN] = _dot(hb, w_ref[:, off + j:off + j + PROJ_TN]).astype(o_ref.dtype)
        off += n


CAST_ROWS = 128


def _cast_body(wt_ref, *out_refs, starts, narrow, scaled_cols, scale):
    for o_ref, c0 in zip(out_refs[:-1], starts):
        n = o_ref.shape[1]
        val = wt_ref[c0:c0 + n, :]
        if scaled_cols and c0 == 0:
            col = lax.broadcasted_iota(jnp.int32, (n, 1), 0)
            val = val * jnp.where(col < scaled_cols, scale, 1.0)
        o_ref[...] = val.T.astype(BF16)
    lane = lax.broadcasted_iota(jnp.int32, (1, LANES), 1)
    for blk, (c0, n) in enumerate(narrow):
        start = (c0 // SUBLANES) * SUBLANES
        take = -(-(c0 - start + n) // SUBLANES) * SUBLANES
        rows = jnp.concatenate([wt_ref[start:start + take, :], jnp.zeros((LANES - take, CAST_ROWS), F32)], axis=0)
        slab = rows.T
        if c0 != start:
            slab = pltpu.roll(slab, LANES - (c0 - start), axis=1)
        out_refs[-1][:, blk * LANES:(blk + 1) * LANES] = jnp.where(lane < n, slab, 0.0).astype(BF16)


def _cast_weights(wt, starts, widths, narrow, scaled_cols=0, scale=1.0):
    cols, rows = wt.shape
    assert CAST_ROWS == LANES
    widths = tuple(widths) + (len(narrow) * LANES,)
    return pl.pallas_call(
        functools.partial(_cast_body, starts=starts, narrow=narrow, scaled_cols=scaled_cols, scale=scale),
        grid=(rows // CAST_ROWS,),
        in_specs=[pl.BlockSpec((cols, CAST_ROWS), lambda i: (0, i))],
        out_specs=[pl.BlockSpec((CAST_ROWS, n), lambda i: (i, 0)) for n in widths],
        out_shape=[jax.ShapeDtypeStruct((rows, n), BF16) for n in widths],
        compiler_params=_cparams(1),
        name="cast_weights",
    )(wt)


def _proj_even_body(x_ref, xh_ref, g_ref, w_ref, wb_ref, wg_ref, convw_ref, convb_ref,
                    qk_ref, v_ref, o_ref, u_ref, z_ref, gates_ref, ext_ref, *, tiles_per_seq):
    hb = _rmsnorm(x_ref[...], g_ref[...]).astype(BF16)
    hh = _rmsnorm(xh_ref[...], g_ref[...]).astype(BF16)
    seq_start = lax.rem(pl.program_id(0), tiles_per_seq) == 0
    base = HALO - (CONV_WIDTH - 1)
    lane = lax.broadcasted_iota(jnp.int32, (1, PROJ_TN), 1)
    for j in range(0, M_QK, PROJ_TN):
        cols = slice(j, j + PROJ_TN)
        wj = w_ref[:, cols]
        ext_ref[HALO:HALO + TOKEN_TILE, :] = _dot(hb, wj)
        ext_ref[0:HALO, :] = jnp.where(seq_start, 0.0, _dot(hh, wj))
        acc = convb_ref[:, cols] + convw_ref[0:1, cols] * ext_ref[base:base + TOKEN_TILE, :]
        for i in range(1, CONV_WIDTH):
            acc = acc + convw_ref[i:i + 1, cols] * ext_ref[base + i:base + i + TOKEN_TILE, :]
        scale = jnp.where(lane + j < M_HEADS * M_DK, M_DK ** -0.5, 1.0)
        qk_ref[:, cols] = (_silu(acc) * scale).astype(BF16)
    _project(hb, w_ref, (v_ref, o_ref), M_QK)
    _project(hb, wb_ref, (u_ref, z_ref), 0)
    gates_ref[...] = _dot(hb, wg_ref[...])


def _proj_even(xf, g, wm, wb, wg, conv_w, conv_b, seq):
    t, d = xf.shape
    half = D_MIX // 2
    assert seq % TOKEN_TILE == 0 and TOKEN_TILE % HALO == 0
    tok = lambda i: (i, 0)
    const = lambda i: (0, 0)
    per_halo = TOKEN_TILE // HALO
    widths = (M_QK, half, half, half, D_MIX)
    dtypes = (BF16, BF16, BF16, F32, BF16)
    out_shape = [jax.ShapeDtypeStruct((t, n), dt) for n, dt in zip(widths, dtypes)]
    ngate = wg.shape[1]
    out_shape.append(jax.ShapeDtypeStruct((t, ngate), F32))
    out_specs = [pl.BlockSpec((TOKEN_TILE, n), tok) for n in widths]
    out_specs.append(pl.BlockSpec((TOKEN_TILE, ngate), tok))
    return pl.pallas_call(
        functools.partial(_proj_even_body, tiles_per_seq=seq // TOKEN_TILE),
        grid=(t // TOKEN_TILE,),
        in_specs=[
            pl.BlockSpec((TOKEN_TILE, d), tok),
            pl.BlockSpec((HALO, d), lambda i: (jnp.maximum(i * per_halo - 1, 0), 0)),
            pl.BlockSpec((1, d), const),
            pl.BlockSpec((d, wm.shape[1]), const, pipeline_mode=pl.Buffered(1)),
            pl.BlockSpec((d, wb.shape[1]), const, pipeline_mode=pl.Buffered(1)),
            pl.BlockSpec((d, ngate), const),
            pl.BlockSpec((CONV_WIDTH, M_QK), const),
            pl.BlockSpec((1, M_QK), const),
        ],
        out_specs=out_specs,
        out_shape=out_shape,
        scratch_shapes=[pltpu.VMEM((HALO + TOKEN_TILE, PROJ_TN), F32)],
        compiler_params=_cparams(1),
        name="proj_even",
    )(xf, xf, g.reshape(1, d), wm, wb, wg, conv_w, conv_b.reshape(1, M_QK))


def _mlstm_body(qk_ref, v_ref, gates_ref, gbias_ref, out_ref, c_ref, n_ref, m_ref):
    @pl.when(pl.program_id(1) == 0)
    def _init():
        c_ref[...] = jnp.zeros_like(c_ref)
        n_ref[...] = jnp.zeros_like(n_ref)
        m_ref[...] = jnp.zeros_like(m_ref)

    for sub in range(M_SUB):
        _mlstm_chunk(slice(sub * M_CHUNK, (sub + 1) * M_CHUNK), qk_ref, v_ref, gates_ref, gbias_ref,
                     out_ref, c_ref, n_ref, m_ref)


def _mlstm_chunk(r, qk_ref, v_ref, gates_ref, gbias_ref, out_ref, c_ref, n_ref, m_ref):
    L = M_CHUNK
    gt = gates_ref[r, :] + gbias_ref[...]
    ipre = gt[:, 0:LANES]
    logf = _log_sigmoid(gt[:, LANES:2 * LANES]).astype(BF16)
    b = _dot(_tri_ones(L, True), logf)
    w = ipre - b
    rows = lax.broadcasted_iota(jnp.int32, (L, LANES), 0)
    cm = w
    k = 1
    while k < L:
        cm = jnp.maximum(cm, jnp.where(rows >= k, pltpu.roll(cm, k, axis=0), -jnp.inf))
        k *= 2

    sel_r = lax.broadcasted_iota(jnp.int32, (LANES, M_HEADS * LANES), 0)
    sel_c = lax.broadcasted_iota(jnp.int32, (LANES, M_HEADS * LANES), 1)
    spread = jnp.where(sel_r == lax.shift_right_logical(sel_c, 7), 1.0, 0.0).astype(BF16)

    def replicate(x):
        hi, lo = _split_hi_lo(x)
        return _dot(hi, spread) + _dot(lo, spread)

    b_rep = replicate(b)
    w_rep = replicate(w)
    cm_rep = replicate(cm)
    pick_r = lax.broadcasted_iota(jnp.int32, (M_HEADS * SUBLANES, LANES), 0)
    pick_c = lax.broadcasted_iota(jnp.int32, (M_HEADS * SUBLANES, LANES), 1)
    pick = jnp.where(lax.shift_right_logical(pick_r, 3) == pick_c, 1.0, 0.0).astype(BF16)
    w_hi, w_lo = _split_hi_lo(w)
    w_row = _dot_nt(pick, w_hi) + _dot_nt(pick, w_lo)

    trow = lax.broadcasted_iota(jnp.int32, (L, LANES), 0)
    tcol = lax.broadcasted_iota(jnp.int32, (L, LANES), 1)
    ones = jnp.ones((L, LANES), BF16)

    for h in range(M_HEADS):
        ks = slice(h * M_DK, (h + 1) * M_DK)
        ks2 = slice(M_HEADS * M_DK + h * M_DK, M_HEADS * M_DK + (h + 1) * M_DK)
        vs = slice(h * M_DV, (h + 1) * M_DV)
        hs = slice(h * LANES, (h + 1) * LANES)
        m_prev = m_ref[h, 0:1, :]
        c_prev = c_ref[h]
        n_prev = n_ref[h]
        big_m = jnp.maximum(m_prev, cm_rep[:, hs])
        w_inter = jnp.exp(m_prev - big_m)
        wr = w_row[h * SUBLANES:h * SUBLANES + 1, :]

        qb = qk_ref[r, ks]
        kb = qk_ref[r, ks2]
        v_h = v_ref[r, vs]
        s = _dot_nt(qb, kb)
        sc = jnp.concatenate(
            [jnp.where(trow >= tcol + j, jnp.exp(wr[:, j:j + LANES] - big_m), 0.0) * s[:, j:j + LANES]
             for j in range(0, L, LANES)], axis=1).astype(BF16)
        q_c = _dot(qb, c_prev.astype(BF16))
        den = _dot(sc, ones) + w_inter * _dot(qb, n_prev.astype(BF16))
        inv = 1.0 / jnp.maximum(jnp.abs(den), jnp.exp(-(b_rep[:, hs] + big_m)))
        num = _dot(sc, v_h)
        out_ref[r, vs] = jnp.concatenate(
            [(num[:, j:j + LANES] + w_inter * q_c[:, j:j + LANES]) * inv for j in range(0, M_DV, LANES)],
            axis=1).astype(BF16)

        g = b_rep[L - 1:L, hs]
        cm_last = cm_rep[L - 1:L, hs]
        m_last = big_m[L - 1:L, :]
        kw_t = (kb.astype(F32) * jnp.exp(w_rep[:, hs] - cm_last)).T.astype(BF16)
        s_prev = jnp.exp(m_prev - m_last)
        s_loc = jnp.exp(cm_last - m_last)
        c_ref[h] = (jnp.concatenate([s_prev] * (M_DV // LANES), axis=1) * c_prev
                    + jnp.concatenate([s_loc] * (M_DV // LANES), axis=1) * _dot(kw_t, v_h))
        n_ref[h] = s_prev * n_prev + s_loc * _dot(kw_t, ones)
        m_ref[h] = jnp.broadcast_to(g + m_last, (SUBLANES, LANES))


def _mlstm(qk, v, gates, gbias, batch, seq):
    t = batch * seq
    L = M_CHUNK * M_SUB
    nc = seq // L
    dv = M_HEADS * M_DV
    tok = lambda b, c: (b * nc + c, 0)
    const = lambda b, c: (0, 0)
    return pl.pallas_call(
        _mlstm_body,
        grid=(batch, nc),
        in_specs=[
            pl.BlockSpec((L, M_QK), tok),
            pl.BlockSpec((L, dv), tok),
            pl.BlockSpec((L, 2 * LANES), tok),
            pl.BlockSpec((1, 2 * LANES), const),
        ],
        out_specs=pl.BlockSpec((L, dv), tok),
        out_shape=jax.ShapeDtypeStruct((t, dv), BF16),
        scratch_shapes=[
            pltpu.VMEM((M_HEADS, M_DK, M_DV), F32),
            pltpu.VMEM((M_HEADS, M_DK, LANES), F32),
            pltpu.VMEM((M_HEADS, SUBLANES, LANES), F32),
        ],
        compiler_params=_cparams(2),
        name="mlstm",
    )(qk, v, gates, gbias)


def _gelu_tanh(x):
    return 0.5 * x * (1.0 + jnp.tanh(0.7978845608028654 * (x + 0.044715 * (x * x * x))))


def _s5_build_operators(lamr_ref, lami_ref, ldt_ref, btr_ref, bti_ref, ctr_ref, cti_ref, d_ref,
                        pbig_ref, qbig_ref, mbig_ref, a_ref):
    lr = lamr_ref[0]
    li = lami_ref[0]
    dt = jnp.exp(ldt_ref[0])
    zr = lr * dt
    th = li * dt
    er = jnp.exp(zr)
    ar = er * jnp.cos(th)
    ai = er * jnp.sin(th)
    den = lr * lr + li * li
    beta_r = ((ar - 1.0) * lr + ai * li) / den
    beta_i = (ai * lr - (ar - 1.0) * li) / den
    btr = btr_ref[0]
    bti = bti_ref[0]
    bbr = btr * beta_r - bti * beta_i
    bbi = btr * beta_i + bti * beta_r
    ctr = ctr_ref[0]
    cti = cti_ref[0]

    row_g = lax.shift_right_logical(lax.broadcasted_iota(jnp.int32, (LANES, S5_SW), 0), 4)
    lane_g = lax.shift_right_logical(lax.broadcasted_iota(jnp.int32, (LANES, S5_SW), 1), 6)
    same_group = row_g == lane_g

    def expand(x16):
        return jnp.where(same_group, jnp.concatenate([x16] * S5_GPL, axis=0), 0.0)

    def power(k):
        e = jnp.exp(float(k) * zr)
        return e * jnp.cos(float(k) * th), e * jnp.sin(float(k) * th)

    for s in range(S5_BLK):
        rows = slice(s * LANES, (s + 1) * LANES)
        pr, pi = power(S5_BLK - 1 - s)
        pbig_ref[rows, 0:S5_SW] = expand(pr * bbr - pi * bbi).astype(BF16)
        pbig_ref[rows, S5_SW:2 * S5_SW] = expand(pr * bbi + pi * bbr).astype(BF16)
        pr, pi = power(s + 1)
        qbig_ref[rows, 0:S5_SW] = expand(ctr * pr - cti * pi).astype(BF16)
        qbig_ref[rows, S5_SW:2 * S5_SW] = expand(-(ctr * pi + cti * pr)).astype(BF16)

    cb = jnp.concatenate([expand(ctr), expand(-cti)], axis=1)
    r128 = lax.broadcasted_iota(jnp.int32, (LANES, LANES), 0)
    c128 = lax.broadcasted_iota(jnp.int32, (LANES, LANES), 1)
    zero_blk = jnp.zeros((LANES, LANES), BF16)
    ab_lags = []
    for lag in range(S5_BLK):
        pr, pi = power(lag)
        ab_lags.append(jnp.concatenate([expand(pr * bbr - pi * bbi), expand(pr * bbi + pi * bbr)], axis=1))
    ab_hi, ab_lo = _split_hi_lo(jnp.concatenate(ab_lags, axis=0))
    cb_hi, cb_lo = _split_hi_lo(cb)
    v_lags = _dot_nt(ab_hi, cb_hi) + _dot_nt(ab_hi, cb_lo) + _dot_nt(ab_lo, cb_hi)
    for lag in range(S5_BLK):
        v = v_lags[lag * LANES:(lag + 1) * LANES, :]
        if lag == 0:
            v = v + jnp.where(r128 == c128, d_ref[0], 0.0)
        vb = v.astype(BF16)
        for s in range(S5_BLK - lag):
            t = s + lag
            mbig_ref[s * LANES:(s + 1) * LANES, t * LANES:(t + 1) * LANES] = vb
            if lag > 0:
                mbig_ref[t * LANES:(t + 1) * LANES, s * LANES:(s + 1) * LANES] = zero_blk

    pr, pi = power(S5_BLK)
    a_ref[:, 0:S5_SW] = jnp.broadcast_to(pr, (SUBLANES, S5_SW))
    a_ref[:, S5_SW:2 * S5_SW] = jnp.broadcast_to(pi, (SUBLANES, S5_SW))


def _s5_body(u_ref, lamr_ref, lami_ref, ldt_ref, btr_ref, bti_ref, ctr_ref, cti_ref, d_ref, y_ref,
             pbig_ref, qbig_ref, mbig_ref, a_ref, ucat_ref, x_ref, xp_ref, st_ref):
    nblk = S5_CHUNK // S5_BLK
    batch = u_ref.shape[0]

    @pl.when(pl.program_id(1) == 0)
    def _setup():
        _s5_build_operators(lamr_ref, lami_ref, ldt_ref, btr_ref, bti_ref, ctr_ref, cti_ref, d_ref,
                            pbig_ref, qbig_ref, mbig_ref, a_ref)
        st_ref[...] = jnp.zeros_like(st_ref)

    for b in range(batch):
        for s in range(S5_BLK):
            piece = u_ref[b, pl.ds(s, nblk, stride=S5_BLK), :]
            ucat_ref[b * nblk:(b + 1) * nblk, s * LANES:(s + 1) * LANES] = piece.astype(BF16)
    nslab = 2 * S5_SW // LANES
    half = nslab // 2
    xloc = _dot(ucat_ref[...], pbig_ref[...])
    for c in range(nslab):
        for b in range(batch):
            x_ref[c, b * S5_XROWS:b * S5_XROWS + nblk, :] = (
                xloc[b * nblk:(b + 1) * nblk, c * LANES:(c + 1) * LANES])

    lanes = lambda ref, c: ref[:, c * LANES:(c + 1) * LANES]
    ar = [lanes(a_ref, c) for c in range(half)]
    ai = [lanes(a_ref, half + c) for c in range(half)]
    xr = [lanes(st_ref, c) for c in range(half)]
    xi = [lanes(st_ref, half + c) for c in range(half)]
    for blk in range(nblk):
        r = pl.ds(blk, batch, stride=S5_XROWS)
        for c in range(half):
            xp_ref[c, r, :] = xr[c]
            xp_ref[half + c, r, :] = xi[c]
            nr = ar[c] * xr[c] - ai[c] * xi[c] + x_ref[c, r, :]
            ni = ar[c] * xi[c] + ai[c] * xr[c] + x_ref[half + c, r, :]
            xr[c], xi[c] = nr, ni
    for c in range(half):
        st_ref[:, c * LANES:(c + 1) * LANES] = xr[c]
        st_ref[:, (half + c) * LANES:(half + c + 1) * LANES] = xi[c]

    xpb = jnp.concatenate(
        [jnp.concatenate([xp_ref[c, b * S5_XROWS:b * S5_XROWS + nblk, :].astype(BF16)
                          for c in range(nslab)], axis=1) for b in range(batch)], axis=0)
    width = 2 * LANES
    for nb in range(S5_BLK // 2):
        kk = (2 * nb + 2) * LANES
        cols = slice(nb * width, (nb + 1) * width)
        y = _dot(ucat_ref[:, 0:kk], mbig_ref[0:kk, cols]) + _dot_nt(xpb, qbig_ref[cols, :])
        y = _gelu_tanh(y)
        for tt in range(2):
            t = 2 * nb + tt
            for b in range(batch):
                y_ref[b, pl.ds(t, nblk, stride=S5_BLK), :] = (
                    y[b * nblk:(b + 1) * nblk, tt * LANES:(tt + 1) * LANES])


def _s5(u3, lam_re, lam_im, log_dt, b_re, b_im, c_re, c_im, d_skip):
    batch, seq, width = u3.shape
    nlb = width // LANES
    assert batch == SUBLANES and seq % S5_CHUNK == 0
    lamr = lam_re.reshape(nlb, 1, S5_SW)
    lami = lam_im.reshape(nlb, 1, S5_SW)
    ldt = jnp.repeat(log_dt, S5_STATE).reshape(nlb, 1, S5_SW)
    bt = lambda b: b.reshape(nlb, S5_GPL, S5_STATE, S5_GROUP).transpose(0, 3, 1, 2).reshape(nlb, S5_GROUP, S5_SW)
    ct = lambda c: c.reshape(nlb, S5_GPL, S5_GROUP, S5_STATE).transpose(0, 2, 1, 3).reshape(nlb, S5_GROUP, S5_SW)
    par = lambda r, w: pl.BlockSpec((1, r, w), lambda i, j: (i, 0, 0))
    kdim = S5_BLK * LANES
    rows = (S5_CHUNK // S5_BLK) * batch
    io = pl.BlockSpec((batch, S5_CHUNK, LANES), lambda i, j: (0, j, i))
    return pl.pallas_call(
        _s5_body,
        grid=(nlb, seq // S5_CHUNK),
        in_specs=[io, par(1, S5_SW), par(1, S5_SW), par(1, S5_SW), par(S5_GROUP, S5_SW),
                  par(S5_GROUP, S5_SW), par(S5_GROUP, S5_SW), par(S5_GROUP, S5_SW), par(1, LANES)],
        out_specs=io,
        out_shape=jax.ShapeDtypeStruct((batch, seq, width), F32),
        scratch_shapes=[
            pltpu.VMEM((kdim, 2 * S5_SW), BF16),
            pltpu.VMEM((kdim, 2 * S5_SW), BF16),
            pltpu.VMEM((kdim, kdim), BF16),
            pltpu.VMEM((SUBLANES, 2 * S5_SW), F32),
            pltpu.VMEM((rows, kdim), BF16),
            pltpu.VMEM((2 * S5_SW // LANES, batch * S5_XROWS, LANES), F32),
            pltpu.VMEM((2 * S5_SW // LANES, batch * S5_XROWS, LANES), F32),
            pltpu.VMEM((SUBLANES, 2 * S5_SW), F32),
        ],
        compiler_params=_cparams(2),
        name="s5",
    )(u3, lamr, lami, ldt, bt(b_re), bt(b_im), ct(c_re), ct(c_im), d_skip.reshape(nlb, 1, LANES))


def _even_out_body(yb_ref, h_ref, o_ref, z_ref, x_ref, hg_ref, gluw_ref, glub_ref, wout_ref,
                   out_ref, y_ref):
    half = h_ref.shape[1]
    for h in range(M_HEADS):
        vs = slice(h * M_DV, (h + 1) * M_DV)
        og = (jax.nn.sigmoid(o_ref[:, vs]) * h_ref[:, vs]).astype(F32)
        y_ref[:, vs] = _rmsnorm(og, hg_ref[:, vs]).astype(BF16) * _silu(z_ref[:, vs])
    yg = yb_ref[...]
    s = _dot(yg.astype(BF16), gluw_ref[...]) + glub_ref[...]
    hb = yg * jax.nn.sigmoid(s)
    y_ref[:, half:2 * half] = hb.astype(BF16) * _silu(z_ref[:, half:2 * half])
    out_ref[...] = x_ref[...] + _dot(y_ref[...], wout_ref[...])


def _even_out(yb, hm, o, z, xf, head_g, glu_w, glu_b, w_out):
    t, d = xf.shape
    half = hm.shape[1]
    tok = lambda i: (i, 0)
    const = lambda i: (0, 0)
    return pl.pallas_call(
        _even_out_body,
        grid=(t // TOKEN_TILE,),
        in_specs=[
            pl.BlockSpec((TOKEN_TILE, half), tok),
            pl.BlockSpec((TOKEN_TILE, half), tok),
            pl.BlockSpec((TOKEN_TILE, half), tok),
            pl.BlockSpec((TOKEN_TILE, 2 * half), tok),
            pl.BlockSpec((TOKEN_TILE, d), tok),
            pl.BlockSpec((1, half), const),
            pl.BlockSpec((half, half), const),
            pl.BlockSpec((1, half), const),
            pl.BlockSpec((2 * half, d), const),
        ],
        out_specs=pl.BlockSpec((TOKEN_TILE, d), tok),
        out_shape=jax.ShapeDtypeStruct((t, d), F32),
        scratch_shapes=[pltpu.VMEM((TOKEN_TILE, 2 * half), BF16)],
        compiler_params=_cparams(1),
        name="even_out",
    )(yb, hm, o, z, xf, head_g.reshape(1, half), glu_w, glu_b.reshape(1, half), w_out)


def _proj_odd_body(x_ref, g_ref, w_ref, wg_ref, wa_ref, ba_ref, q_ref, k_ref, v_ref, z_ref, bc_ref):
    hb = _rmsnorm(x_ref[...], g_ref[...]).astype(BF16)
    _project(hb, w_ref, (q_ref, k_ref, v_ref, z_ref), 0)
    rb = _dot(hb, wg_ref[...]).astype(BF16)
    tril = _tri_ones(G_CHUNK, True)
    for j in range(0, bc_ref.shape[1], PROJ_TN):
        cols = slice(j, j + PROJ_TN)
        pre = _dot(rb, wa_ref[:, cols].astype(BF16)) + ba_ref[:, cols]
        la = (_log_sigmoid(pre) * (1.0 / G_TAU)).astype(BF16)
        for c in range(0, TOKEN_TILE, G_CHUNK):
            rows = slice(c, c + G_CHUNK)
            bc_ref[rows, cols] = _dot(tril, la[rows])


def _layer1_body(x_ref, g_ref, w_ref, wg_ref, wa_ref, ba_ref, hg_ref, wout_ref, gf_ref, out_ref,
                 q_s, k_s, v_s, z_s, bc_s, y_s, s_ref, *, tiles_per_seq):
    L = G_CHUNK

    @pl.when(lax.rem(pl.program_id(0), tiles_per_seq) == 0)
    def _init():
        s_ref[...] = jnp.zeros_like(s_ref)

    _proj_odd_body(x_ref, g_ref, w_ref, wg_ref, wa_ref, ba_ref, q_s, k_s, v_s, z_s, bc_s)

    row = lax.broadcasted_iota(jnp.int32, (L, L), 0)
    col = lax.broadcasted_iota(jnp.int32, (L, L), 1)
    causal = row >= col
    mid = L // 2 - 1
    for c, h in [(c, h) for c in range(0, TOKEN_TILE, L) for h in range(G_HEADS)]:
        r = slice(c, c + L)
        ks = slice(h * G_DK, (h + 1) * G_DK)
        vs = slice(h * G_DV, (h + 1) * G_DV)
        b = bc_s[r, ks]
        bm = b[mid:mid + 1, :]
        g = b[L - 1:L, :]
        e1 = jnp.exp(b - bm)
        qt = q_s[r, ks].astype(F32) * e1
        kt = k_s[r, ks].astype(F32) * (1.0 / e1)
        attn = jnp.where(causal, _dot_nt(qt.astype(BF16), kt.astype(BF16)), 0.0)
        v_h = v_s[r, vs]
        s_prev = s_ref[h]
        qi = qt * jnp.exp(bm)
        o = _dot(attn.astype(BF16), v_h) + _dot(qi.astype(BF16), s_prev.astype(BF16))
        y_s[r, vs] = (_rmsnorm(o, hg_ref[:, vs]) * _silu(z_s[r, vs].astype(F32))).astype(BF16)
        ke = kt * jnp.exp(g - bm)
        g_col = jnp.broadcast_to(jnp.exp(g), (LANES, G_DK)).T[:, 0:1]
        s_ref[h] = g_col * s_prev + _dot(ke.T.astype(BF16), v_h)

    x = x_ref[...] + _dot(y_s[...], wout_ref[...])
    out_ref[...] = _rmsnorm(x, gf_ref[...])


def _layer1(x1, g, wm, wg, w_alpha, b_alpha, head_g, w_out, gf, seq):
    t, d = x1.shape
    dk = G_HEADS * G_DK
    assert seq % TOKEN_TILE == 0 and TOKEN_TILE % G_CHUNK == 0
    tok = lambda i: (i, 0)
    const = lambda i: (0, 0)
    resident = lambda shape: pl.BlockSpec(shape, const, pipeline_mode=pl.Buffered(1))
    slots = lambda n, dt: pltpu.VMEM((TOKEN_TILE, n), dt)
    return pl.pallas_call(
        functools.partial(_layer1_body, tiles_per_seq=seq // TOKEN_TILE),
        grid=(t // TOKEN_TILE,),
        in_specs=[
            pl.BlockSpec((TOKEN_TILE, d), tok),
            pl.BlockSpec((1, d), const),
            resident((d, wm.shape[1])),
            pl.BlockSpec((d, G_RANK_PAD), const),
            pl.BlockSpec((G_RANK_PAD, dk), const),
            pl.BlockSpec((1, dk), const),
            pl.BlockSpec((1, D_MIX), const),
            resident((D_MIX, d)),
            pl.BlockSpec((1, d), const),
        ],
        out_specs=pl.BlockSpec((TOKEN_TILE, d), tok),
        out_shape=jax.ShapeDtypeStruct((t, d), F32),
        scratch_shapes=[
            slots(dk, BF16),
            slots(dk, BF16),
            slots(D_MIX, BF16),
            slots(D_MIX, BF16),
            slots(dk, F32),
            slots(D_MIX, BF16),
            pltpu.VMEM((G_HEADS, G_DK, G_DV), F32),
        ],
        compiler_params=_cparams(1),
        name="layer1",
    )(x1, g.reshape(1, d), wm, wg, w_alpha, b_alpha.reshape(1, dk), head_g.reshape(1, D_MIX), w_out,
      gf.reshape(1, d))


def kernel(x, norm_g, final_norm_g, ev_w_in, ev_conv_w, ev_conv_b, ev_i_bias, ev_f_bias, ev_head_g,
           s5_lam_re, s5_lam_im, s5_log_dt, s5_b_re, s5_b_im, s5_c_re, s5_c_im, s5_d, s5_glu_w,
           s5_glu_b, ev_w_out, od_w_in, gla_w_alpha, gla_b_alpha, gla_head_g, od_w_out):
    batch, seq, d = x.shape
    t = batch * seq
    xf = x.reshape(t, d)
    padc = lambda a: jnp.pad(a, ((0, 0), (0, LANES - a.shape[1])))

    wt = jnp.swapaxes(ev_w_in, 1, 2).reshape(ev_w_in.shape[2], d)
    g0 = 2 * M_HEADS * M_DK + 2 * M_HEADS * M_DV
    gi = g0 + M_HEADS
    gf = gi + M_HEADS
    half = D_MIX // 2
    w_qkvo, w_uz, wg = _cast_weights(wt, (0, gf), (g0, half + D_MIX), ((g0, M_HEADS), (gi, M_HEADS)))
    qk, v, o, u, z, gates = _proj_even(xf, norm_g[0], w_qkvo, w_uz, wg, ev_conv_w[0], ev_conv_b[0], seq)
    gbias = jnp.concatenate([padc(ev_i_bias), padc(ev_f_bias)], axis=1)
    hm = _mlstm(qk, v, gates, gbias, batch, seq)

    y3 = _s5(u.reshape(batch, seq, half), s5_lam_re[0], s5_lam_im[0], s5_log_dt[0], s5_b_re[0],
             s5_b_im[0], s5_c_re[0], s5_c_im[0], s5_d[0])
    yb = y3.reshape(t, half)
    x1 = _even_out(yb, hm, o, z, xf, ev_head_g[0], s5_glu_w[0].astype(BF16), s5_glu_b[0],
                   ev_w_out[0].astype(BF16))

    wt = jnp.swapaxes(od_w_in, 1, 2).reshape(od_w_in.shape[2], d)
    n_main = 2 * G_HEADS * G_DK + 2 * D_MIX
    wa = jnp.pad(gla_w_alpha[0], ((0, G_RANK_PAD - gla_w_alpha.shape[1]), (0, 0)))
    wm, wr = _cast_weights(wt, (0,), (n_main,), ((n_main, wt.shape[0] - n_main),),
                           scaled_cols=G_HEADS * G_DK, scale=G_DK ** -0.5)
    out = _layer1(x1, norm_g[1], wm, wr, wa, gla_b_alpha[0],
                  gla_head_g[0], od_w_out[0].astype(BF16), final_norm_g, seq)
    return out.reshape(batch, seq, d)
```

```python
import functools

import jax
import jax.numpy as jnp
from jax import lax
from jax.experimental import pallas as pl
from jax.experimental.pallas import tpu as pltpu

F32 = jnp.float32
BF16 = jnp.bfloat16

EPS = 1e-6
D_MODEL = 1024
D_MIX = 2 * D_MODEL
M_HEADS = 4
M_DK = 128
M_DV = 256
M_QK = 2 * M_HEADS * M_DK
CONV_WIDTH = 4
M_CHUNK = 256
M_SUB = 2
S5_GROUP = 16
S5_STATE = 64
S5_BLK = 8
S5_CHUNK = 1024
G_HEADS = 4
G_DK = 256
G_DV = 512
G_TAU = 16.0
G_CHUNK = 128
G_RANK_PAD = 128

LANES = 128
SUBLANES = 8
HALO = 16
S5_GPL = LANES // S5_GROUP
S5_SW = S5_GPL * S5_STATE
S5_XROWS = S5_CHUNK // S5_BLK + SUBLANES
TOKEN_TILE = 512
PROJ_TN = 256
VMEM_LIMIT = 56 * 1024 * 1024


def _cparams(n_grid):
    return pltpu.CompilerParams(
        dimension_semantics=("arbitrary",) * n_grid, vmem_limit_bytes=VMEM_LIMIT)


def _log_sigmoid(x):
    return jnp.minimum(x, 0.0) - jnp.log(1.0 + jnp.exp(-jnp.abs(x)))


def _silu(x):
    return x * jax.nn.sigmoid(x)


def _split_hi_lo(x):
    hi = x.astype(BF16)
    lo = (x - hi.astype(F32)).astype(BF16)
    return hi, lo


def _dot(a, b):
    return jnp.dot(a, b, preferred_element_type=F32)


def _dot_nt(a, b, precision=None):
    return lax.dot_general(a, b, (((1,), (1,)), ((), ())), precision=precision,
                           preferred_element_type=F32)


def _tri_ones(n, lower):
    row = lax.broadcasted_iota(jnp.int32, (n, n), 0)
    col = lax.broadcasted_iota(jnp.int32, (n, n), 1)
    keep = (row >= col) if lower else (row <= col)
    return jnp.where(keep, 1.0, 0.0).astype(BF16)


def _rmsnorm(x, g):
    return x * lax.rsqrt(jnp.mean(x * x, axis=-1, keepdims=True) + EPS) * g


def _project(hb, w_ref, out_refs, col0):
    off = col0
    for o_ref in out_refs:
        n = o_ref.shape[1]
        for j in range(0, n, PROJ_TN):
            o_ref[:, j:j + PROJ_TN] = _dot(hb, w_ref[:, off + j:off + j + PROJ_TN]).astype(o_ref.dtype)
        off += n


CAST_ROWS = 128


def _cast_body(wt_ref, *out_refs, starts, narrow, scaled_cols, scale):
    for o_ref, c0 in zip(out_refs[:-1], starts):
        n = o_ref.shape[1]
        val = wt_ref[c0:c0 + n, :]
        if scaled_cols and c0 == 0:
            col = lax.broadcasted_iota(jnp.int32, (n, 1), 0)
            val = val * jnp.where(col < scaled_cols, scale, 1.0)
        o_ref[...] = val.T.astype(BF16)
    lane = lax.broadcasted_iota(jnp.int32, (1, LANES), 1)
    for blk, (c0, n) in enumerate(narrow):
        start = (c0 // SUBLANES) * SUBLANES
        take = -(-(c0 - start + n) // SUBLANES) * SUBLANES
        rows = jnp.concatenate([wt_ref[start:start + take, :], jnp.zeros((LANES - take, CAST_ROWS), F32)], axis=0)
        slab = rows.T
        if c0 != start:
            slab = pltpu.roll(slab, LANES - (c0 - start), axis=1)
        out_refs[-1][:, blk * LANES:(blk + 1) * LANES] = jnp.where(lane < n, slab, 0.0).astype(BF16)


def _cast_weights(wt, starts, widths, narrow, scaled_cols=0, scale=1.0):
    cols, rows = wt.shape
    assert CAST_ROWS == LANES
    widths = tuple(widths) + (len(narrow) * LANES,)
    return pl.pallas_call(
        functools.partial(_cast_body, starts=starts, narrow=narrow, scaled_cols=scaled_cols, scale=scale),
        grid=(rows // CAST_ROWS,),
        in_specs=[pl.BlockSpec((cols, CAST_ROWS), lambda i: (0, i))],
        out_specs=[pl.BlockSpec((CAST_ROWS, n), lambda i: (i, 0)) for n in widths],
        out_shape=[jax.ShapeDtypeStruct((rows, n), BF16) for n in widths],
        compiler_params=_cparams(1),
        name="cast_weights",
    )(wt)


def _proj_even_body(x_ref, xh_ref, g_ref, w_ref, wb_ref, wg_ref, convw_ref, convb_ref,
                    qk_ref, v_ref, o_ref, u_ref, z_ref, gates_ref, ext_ref, *, tiles_per_seq):
    hb = _rmsnorm(x_ref[...], g_ref[...]).astype(BF16)
    hh = _rmsnorm(xh_ref[...], g_ref[...]).astype(BF16)
    seq_start = lax.rem(pl.program_id(0), tiles_per_seq) == 0
    base = HALO - (CONV_WIDTH - 1)
    lane = lax.broadcasted_iota(jnp.int32, (1, PROJ_TN), 1)
    for j in range(0, M_QK, PROJ_TN):
        cols = slice(j, j + PROJ_TN)
        wj = w_ref[:, cols]
        ext_ref[HALO:HALO + TOKEN_TILE, :] = _dot(hb, wj)
        ext_ref[0:HALO, :] = jnp.where(seq_start, 0.0, _dot(hh, wj))
        acc = convb_ref[:, cols] + convw_ref[0:1, cols] * ext_ref[base:base + TOKEN_TILE, :]
        for i in range(1, CONV_WIDTH):
            acc = acc + convw_ref[i:i + 1, cols] * ext_ref[base + i:base + i + TOKEN_TILE, :]
        scale = jnp.where(lane + j < M_HEADS * M_DK, M_DK ** -0.5, 1.0)
        qk_ref[:, cols] = (_silu(acc) * scale).astype(BF16)
    _project(hb, w_ref, (v_ref, o_ref), M_QK)
    _project(hb, wb_ref, (u_ref, z_ref), 0)
    gates_ref[...] = _dot(hb, wg_ref[...])


def _proj_even(xf, g, wm, wb, wg, conv_w, conv_b, seq):
    t, d = xf.shape
    half = D_MIX // 2
    assert seq % TOKEN_TILE == 0 and TOKEN_TILE % HALO == 0
    tok = lambda i: (i, 0)
    const = lambda i: (0, 0)
    per_halo = TOKEN_TILE // HALO
    widths = (M_QK, half, half, half, D_MIX)
    dtypes = (BF16, BF16, BF16, F32, BF16)
    out_shape = [jax.ShapeDtypeStruct((t, n), dt) for n, dt in zip(widths, dtypes)]
    ngate = wg.shape[1]
    out_shape.append(jax.ShapeDtypeStruct((t, ngate), F32))
    out_specs = [pl.BlockSpec((TOKEN_TILE, n), tok) for n in widths]
    out_specs.append(pl.BlockSpec((TOKEN_TILE, ngate), tok))
    return pl.pallas_call(
        functools.partial(_proj_even_body, tiles_per_seq=seq // TOKEN_TILE),
        grid=(t // TOKEN_TILE,),
        in_specs=[
            pl.BlockSpec((TOKEN_TILE, d), tok),
            pl.BlockSpec((HALO, d), lambda i: (jnp.maximum(i * per_halo - 1, 0), 0)),
            pl.BlockSpec((1, d), const),
            pl.BlockSpec((d, wm.shape[1]), const, pipeline_mode=pl.Buffered(1)),
            pl.BlockSpec((d, wb.shape[1]), const, pipeline_mode=pl.Buffered(1)),
            pl.BlockSpec((d, ngate), const),
            pl.BlockSpec((CONV_WIDTH, M_QK), const),
            pl.BlockSpec((1, M_QK), const),
        ],
        out_specs=out_specs,
        out_shape=out_shape,
        scratch_shapes=[pltpu.VMEM((HALO + TOKEN_TILE, PROJ_TN), F32)],
        compiler_params=_cparams(1),
        name="proj_even",
    )(xf, xf, g.reshape(1, d), wm, wb, wg, conv_w, conv_b.reshape(1, M_QK))


def _mlstm_body(qk_ref, v_ref, gates_ref, gbias_ref, out_ref, c_ref, n_ref, m_ref):
    @pl.when(pl.program_id(1) == 0)
    def _init():
        c_ref[...] = jnp.zeros_like(c_ref)
        n_ref[...] = jnp.zeros_like(n_ref)
        m_ref[...] = jnp.zeros_like(m_ref)

    for sub in range(M_SUB):
        _mlstm_chunk(slice(sub * M_CHUNK, (sub + 1) * M_CHUNK), qk_ref, v_ref, gates_ref, gbias_ref,
                     out_ref, c_ref, n_ref, m_ref)


def _mlstm_chunk(r, qk_ref, v_ref, gates_ref, gbias_ref, out_ref, c_ref, n_ref, m_ref):
    L = M_CHUNK
    gt = gates_ref[r, :] + gbias_ref[...]
    ipre = gt[:, 0:LANES]
    logf = _log_sigmoid(gt[:, LANES:2 * LANES]).astype(BF16)
    b = _dot(_tri_ones(L, True), logf)
    w = ipre - b
    rows = lax.broadcasted_iota(jnp.int32, (L, LANES), 0)
    cm = w
    k = 1
    while k < L:
        cm = jnp.maximum(cm, jnp.where(rows >= k, pltpu.roll(cm, k, axis=0), -jnp.inf))
        k *= 2

    sel_r = lax.broadcasted_iota(jnp.int32, (LANES, M_HEADS * LANES), 0)
    sel_c = lax.broadcasted_iota(jnp.int32, (LANES, M_HEADS * LANES), 1)
    spread = jnp.where(sel_r == lax.shift_right_logical(sel_c, 7), 1.0, 0.0).astype(BF16)

    def replicate(x):
        hi, lo = _split_hi_lo(x)
        return _dot(hi, spread) + _dot(lo, spread)

    b_rep = replicate(b)
    w_rep = replicate(w)
    cm_rep = replicate(cm)
    pick_r = lax.broadcasted_iota(jnp.int32, (M_HEADS * SUBLANES, LANES), 0)
    pick_c = lax.broadcasted_iota(jnp.int32, (M_HEADS * SUBLANES, LANES), 1)
    pick = jnp.where(lax.shift_right_logical(pick_r, 3) == pick_c, 1.0, 0.0).astype(BF16)
    w_hi, w_lo = _split_hi_lo(w)
    w_row = _dot_nt(pick, w_hi) + _dot_nt(pick, w_lo)

    trow = lax.broadcasted_iota(jnp.int32, (L, LANES), 0)
    tcol = lax.broadcasted_iota(jnp.int32, (L, LANES), 1)
    ones = jnp.ones((L, LANES), BF16)

    for h in range(M_HEADS):
        ks = slice(h * M_DK, (h + 1) * M_DK)
        ks2 = slice(M_HEADS * M_DK + h * M_DK, M_HEADS * M_DK + (h + 1) * M_DK)
        vs = slice(h * M_DV, (h + 1) * M_DV)
        hs = slice(h * LANES, (h + 1) * LANES)
        m_prev = m_ref[h, 0:1, :]
        c_prev = c_ref[h]
        n_prev = n_ref[h]
        big_m = jnp.maximum(m_prev, cm_rep[:, hs])
        w_inter = jnp.exp(m_prev - big_m)
        wr = w_row[h * SUBLANES:h * SUBLANES + 1, :]

        qb = qk_ref[r, ks]
        kb = qk_ref[r, ks2]
        v_h = v_ref[r, vs]
        s = _dot_nt(qb, kb)
        sc = jnp.concatenate(
            [jnp.where(trow >= tcol + j, jnp.exp(wr[:, j:j + LANES] - big_m), 0.0) * s[:, j:j + LANES]
             for j in range(0, L, LANES)], axis=1).astype(BF16)
        q_c = _dot(qb, c_prev.astype(BF16))
        den = _dot(sc, ones) + w_inter * _dot(qb, n_prev.astype(BF16))
        inv = 1.0 / jnp.maximum(jnp.abs(den), jnp.exp(-(b_rep[:, hs] + big_m)))
        num = _dot(sc, v_h)
        out_ref[r, vs] = jnp.concatenate(
            [(num[:, j:j + LANES] + w_inter * q_c[:, j:j + LANES]) * inv for j in range(0, M_DV, LANES)],
            axis=1).astype(BF16)

        g = b_rep[L - 1:L, hs]
        cm_last = cm_rep[L - 1:L, hs]
        m_last = big_m[L - 1:L, :]
        kw_t = (kb.astype(F32) * jnp.exp(w_rep[:, hs] - cm_last)).T.astype(BF16)
        s_prev = jnp.exp(m_prev - m_last)
        s_loc = jnp.exp(cm_last - m_last)
        c_ref[h] = (jnp.concatenate([s_prev] * (M_DV // LANES), axis=1) * c_prev
                    + jnp.concatenate([s_loc] * (M_DV // LANES), axis=1) * _dot(kw_t, v_h))
        n_ref[h] = s_prev * n_prev + s_loc * _dot(kw_t, ones)
        m_ref[h] = jnp.broadcast_to(g + m_last, (SUBLANES, LANES))


def _mlstm(qk, v, gates, gbias, batch, seq):
    t = batch * seq
    L = M_CHUNK * M_SUB
    nc = seq // L
    dv = M_HEADS * M_DV
    tok = lambda b, c: (b * nc + c, 0)
    const = lambda b, c: (0, 0)
    return pl.pallas_call(
        _mlstm_body,
        grid=(batch, nc),
        in_specs=[
            pl.BlockSpec((L, M_QK), tok),
            pl.BlockSpec((L, dv), tok),
            pl.BlockSpec((L, 2 * LANES), tok),
            pl.BlockSpec((1, 2 * LANES), const),
        ],
        out_specs=pl.BlockSpec((L, dv), tok),
        out_shape=jax.ShapeDtypeStruct((t, dv), BF16),
        scratch_shapes=[
            pltpu.VMEM((M_HEADS, M_DK, M_DV), F32),
            pltpu.VMEM((M_HEADS, M_DK, LANES), F32),
            pltpu.VMEM((M_HEADS, SUBLANES, LANES), F32),
        ],
        compiler_params=_cparams(2),
        name="mlstm",
    )(qk, v, gates, gbias)


def _gelu_tanh(x):
    return 0.5 * x * (1.0 + jnp.tanh(0.7978845608028654 * (x + 0.044715 * (x * x * x))))


def _s5_build_operators(lamr_ref, lami_ref, ldt_ref, btr_ref, bti_ref, ctr_ref, cti_ref, d_ref,
                        pbig_ref, qbig_ref, mbig_ref, a_ref):
    lr = lamr_ref[0]
    li = lami_ref[0]
    dt = jnp.exp(ldt_ref[0])
    zr = lr * dt
    th = li * dt
    er = jnp.exp(zr)
    ar = er * jnp.cos(th)
    ai = er * jnp.sin(th)
    den = lr * lr + li * li
    beta_r = ((ar - 1.0) * lr + ai * li) / den
    beta_i = (ai * lr - (ar - 1.0) * li) / den
    btr = btr_ref[0]
    bti = bti_ref[0]
    bbr = btr * beta_r - bti * beta_i
    bbi = btr * beta_i + bti * beta_r
    ctr = ctr_ref[0]
    cti = cti_ref[0]

    row_g = lax.shift_right_logical(lax.broadcasted_iota(jnp.int32, (LANES, S5_SW), 0), 4)
    lane_g = lax.shift_right_logical(lax.broadcasted_iota(jnp.int32, (LANES, S5_SW), 1), 6)
    same_group = row_g == lane_g

    def expand(x16):
        return jnp.where(same_group, jnp.concatenate([x16] * S5_GPL, axis=0), 0.0)

    def power(k):
        e = jnp.exp(float(k) * zr)
        return e * jnp.cos(float(k) * th), e * jnp.sin(float(k) * th)

    for s in range(S5_BLK):
        rows = slice(s * LANES, (s + 1) * LANES)
        pr, pi = power(S5_BLK - 1 - s)
        pbig_ref[rows, 0:S5_SW] = expand(pr * bbr - pi * bbi).astype(BF16)
        pbig_ref[rows, S5_SW:2 * S5_SW] = expand(pr * bbi + pi * bbr).astype(BF16)
        pr, pi = power(s + 1)
        qbig_ref[rows, 0:S5_SW] = expand(ctr * pr - cti * pi).astype(BF16)
        qbig_ref[rows, S5_SW:2 * S5_SW] = expand(-(ctr * pi + cti * pr)).astype(BF16)

    cb = jnp.concatenate([expand(ctr), expand(-cti)], axis=1)
    r128 = lax.broadcasted_iota(jnp.int32, (LANES, LANES), 0)
    c128 = lax.broadcasted_iota(jnp.int32, (LANES, LANES), 1)
    zero_blk = jnp.zeros((LANES, LANES), BF16)
    ab_lags = []
    for lag in range(S5_BLK):
        pr, pi = power(lag)
        ab_lags.append(jnp.concatenate([expand(pr * bbr - pi * bbi), expand(pr * bbi + pi * bbr)], axis=1))
    ab_hi, ab_lo = _split_hi_lo(jnp.concatenate(ab_lags, axis=0))
    cb_hi, cb_lo = _split_hi_lo(cb)
    v_lags = _dot_nt(ab_hi, cb_hi) + _dot_nt(ab_hi, cb_lo) + _dot_nt(ab_lo, cb_hi)
    for lag in range(S5_BLK):
        v = v_lags[lag * LANES:(lag + 1) * LANES, :]
        if lag == 0:
            v = v + jnp.where(r128 == c128, d_ref[0], 0.0)
        vb = v.astype(BF16)
        for s in range(S5_BLK - lag):
            t = s + lag
            mbig_ref[s * LANES:(s + 1) * LANES, t * LANES:(t + 1) * LANES] = vb
            if lag > 0:
                mbig_ref[t * LANES:(t + 1) * LANES, s * LANES:(s + 1) * LANES] = zero_blk

    pr, pi = power(S5_BLK)
    a_ref[:, 0:S5_SW] = jnp.broadcast_to(pr, (SUBLANES, S5_SW))
    a_ref[:, S5_SW:2 * S5_SW] = jnp.broadcast_to(pi, (SUBLANES, S5_SW))


def _s5_body(u_ref, lamr_ref, lami_ref, ldt_ref, btr_ref, bti_ref, ctr_ref, cti_ref, d_ref, y_ref,
             pbig_ref, qbig_ref, mbig_ref, a_ref, ucat_ref, x_ref, xp_ref, st_ref):
    nblk = S5_CHUNK // S5_BLK
    batch = u_ref.shape[0]

    @pl.when(pl.program_id(1) == 0)
    def _setup():
        _s5_build_operators(lamr_ref, lami_ref, ldt_ref, btr_ref, bti_ref, ctr_ref, cti_ref, d_ref,
                            pbig_ref, qbig_ref, mbig_ref, a_ref)
        st_ref[...] = jnp.zeros_like(st_ref)

    for b in range(batch):
        for s in range(S5_BLK):
            piece = u_ref[b, pl.ds(s, nblk, stride=S5_BLK), :]
            ucat_ref[b * nblk:(b + 1) * nblk, s * LANES:(s + 1) * LANES] = piece.astype(BF16)
    nslab = 2 * S5_SW // LANES
    half = nslab // 2
    xloc = _dot(ucat_ref[...], pbig_ref[...])
    for c in range(nslab):
        for b in range(batch):
            x_ref[c, b * S5_XROWS:b * S5_XROWS + nblk, :] = (
                xloc[b * nblk:(b + 1) * nblk, c * LANES:(c + 1) * LANES])

    lanes = lambda ref, c: ref[:, c * LANES:(c + 1) * LANES]
    ar = [lanes(a_ref, c) for c in range(half)]
    ai = [lanes(a_ref, half + c) for c in range(half)]
    xr = [lanes(st_ref, c) for c in range(half)]
    xi = [lanes(st_ref, half + c) for c in range(half)]
    for blk in range(nblk):
        r = pl.ds(blk, batch, stride=S5_XROWS)
        for c in range(half):
            xp_ref[c, r, :] = xr[c]
            xp_ref[half + c, r, :] = xi[c]
            nr = ar[c] * xr[c] - ai[c] * xi[c] + x_ref[c, r, :]
            ni = ar[c] * xi[c] + ai[c] * xr[c] + x_ref[half + c, r, :]
            xr[c], xi[c] = nr, ni
    for c in range(half):
        st_ref[:, c * LANES:(c + 1) * LANES] = xr[c]
        st_ref[:, (half + c) * LANES:(half + c + 1) * LANES] = xi[c]

    xpb = jnp.concatenate(
        [jnp.concatenate([xp_ref[c, b * S5_XROWS:b * S5_XROWS + nblk, :].astype(BF16)
                          for c in range(nslab)], axis=1) for b in range(batch)], axis=0)
    width = 2 * LANES
    for nb in range(S5_BLK // 2):
        kk = (2 * nb + 2) * LANES
        cols = slice(nb * width, (nb + 1) * width)
        y = _dot(ucat_ref[:, 0:kk], mbig_ref[0:kk, cols]) + _dot_nt(xpb, qbig_ref[cols, :])
        y = _gelu_tanh(y)
        for tt in range(2):
            t = 2 * nb + tt
            for b in range(batch):
                y_ref[b, pl.ds(t, nblk, stride=S5_BLK), :] = (
                    y[b * nblk:(b + 1) * nblk, tt * LANES:(tt + 1) * LANES])


def _s5(u3, lam_re, lam_im, log_dt, b_re, b_im, c_re, c_im, d_skip):
    batch, seq, width = u3.shape
    nlb = width // LANES
    assert batch == SUBLANES and seq % S5_CHUNK == 0
    lamr = lam_re.reshape(nlb, 1, S5_SW)
    lami = lam_im.reshape(nlb, 1, S5_SW)
    ldt = jnp.repeat(log_dt, S5_STATE).reshape(nlb, 1, S5_SW)
    bt = lambda b: b.reshape(nlb, S5_GPL, S5_STATE, S5_GROUP).transpose(0, 3, 1, 2).reshape(nlb, S5_GROUP, S5_SW)
    ct = lambda c: c.reshape(nlb, S5_GPL, S5_GROUP, S5_STATE).transpose(0, 2, 1, 3).reshape(nlb, S5_GROUP, S5_SW)
    par = lambda r, w: pl.BlockSpec((1, r, w), lambda i, j: (i, 0, 0))
    kdim = S5_BLK * LANES
    rows = (S5_CHUNK // S5_BLK) * batch
    io = pl.BlockSpec((batch, S5_CHUNK, LANES), lambda i, j: (0, j, i))
    return pl.pallas_call(
        _s5_body,
        grid=(nlb, seq // S5_CHUNK),
        in_specs=[io, par(1, S5_SW), par(1, S5_SW), par(1, S5_SW), par(S5_GROUP, S5_SW),
                  par(S5_GROUP, S5_SW), par(S5_GROUP, S5_SW), par(S5_GROUP, S5_SW), par(1, LANES)],
        out_specs=io,
        out_shape=jax.ShapeDtypeStruct((batch, seq, width), F32),
        scratch_shapes=[
            pltpu.VMEM((kdim, 2 * S5_SW), BF16),
            pltpu.VMEM((kdim, 2 * S5_SW), BF16),
            pltpu.VMEM((kdim, kdim), BF16),
            pltpu.VMEM((SUBLANES, 2 * S5_SW), F32),
            pltpu.VMEM((rows, kdim), BF16),
            pltpu.VMEM((2 * S5_SW // LANES, batch * S5_XROWS, LANES), F32),
            pltpu.VMEM((2 * S5_SW // LANES, batch * S5_XROWS, LANES), F32),
            pltpu.VMEM((SUBLANES, 2 * S5_SW), F32),
        ],
        compiler_params=_cparams(2),
        name="s5",
    )(u3, lamr, lami, ldt, bt(b_re), bt(b_im), ct(c_re), ct(c_im), d_skip.reshape(nlb, 1, LANES))


def _even_out_body(yb_ref, h_ref, o_ref, z_ref, x_ref, hg_ref, gluw_ref, glub_ref, wout_ref,
                   out_ref, y_ref):
    half = h_ref.shape[1]
    for h in range(M_HEADS):
        vs = slice(h * M_DV, (h + 1) * M_DV)
        og = (jax.nn.sigmoid(o_ref[:, vs]) * h_ref[:, vs]).astype(F32)
        y_ref[:, vs] = _rmsnorm(og, hg_ref[:, vs]).astype(BF16) * _silu(z_ref[:, vs])
    yg = yb_ref[...]
    s = _dot(yg.astype(BF16), gluw_ref[...]) + glub_ref[...]
    hb = yg * jax.nn.sigmoid(s)
    y_ref[:, half:2 * half] = hb.astype(BF16) * _silu(z_ref[:, half:2 * half])
    out_ref[...] = x_ref[...] + _dot(y_ref[...], wout_ref[...])


def _even_out(yb, hm, o, z, xf, head_g, glu_w, glu_b, w_out):
    t, d = xf.shape
    half = hm.shape[1]
    tok = lambda i: (i, 0)
    const = lambda i: (0, 0)
    return pl.pallas_call(
        _even_out_body,
        grid=(t // TOKEN_TILE,),
        in_specs=[
            pl.BlockSpec((TOKEN_TILE, half), tok),
            pl.BlockSpec((TOKEN_TILE, half), tok),
            pl.BlockSpec((TOKEN_TILE, half), tok),
            pl.BlockSpec((TOKEN_TILE, 2 * half), tok),
            pl.BlockSpec((TOKEN_TILE, d), tok),
            pl.BlockSpec((1, half), const),
            pl.BlockSpec((half, half), const),
            pl.BlockSpec((1, half), const),
            pl.BlockSpec((2 * half, d), const),
        ],
        out_specs=pl.BlockSpec((TOKEN_TILE, d), tok),
        out_shape=jax.ShapeDtypeStruct((t, d), F32),
        scratch_shapes=[pltpu.VMEM((TOKEN_TILE, 2 * half), BF16)],
        compiler_params=_cparams(1),
        name="even_out",
    )(yb, hm, o, z, xf, head_g.reshape(1, half), glu_w, glu_b.reshape(1, half), w_out)


def _proj_odd_body(x_ref, g_ref, w_ref, wg_ref, wa_ref, ba_ref, q_ref, k_ref, v_ref, z_ref, bc_ref):
    hb = _rmsnorm(x_ref[...], g_ref[...]).astype(BF16)
    _project(hb, w_ref, (q_ref, k_ref, v_ref, z_ref), 0)
    rb = _dot(hb, wg_ref[...]).astype(BF16)
    tril = _tri_ones(G_CHUNK, True)
    for j in range(0, bc_ref.shape[1], PROJ_TN):
        cols = slice(j, j + PROJ_TN)
        pre = _dot(rb, wa_ref[:, cols].astype(BF16)) + ba_ref[:, cols]
        la = (_log_sigmoid(pre) * (1.0 / G_TAU)).astype(BF16)
        for c in range(0, TOKEN_TILE, G_CHUNK):
            rows = slice(c, c + G_CHUNK)
            bc_ref[rows, cols] = _dot(tril, la[rows])


def _layer1_body(x_ref, g_ref, w_ref, wg_ref, wa_ref, ba_ref, hg_ref, wout_ref, gf_ref, out_ref,
                 q_s, k_s, v_s, z_s, bc_s, y_s, s_ref, *, tiles_per_seq):
    L = G_CHUNK

    @pl.when(lax.rem(pl.program_id(0), tiles_per_seq) == 0)
    def _init():
        s_ref[...] = jnp.zeros_like(s_ref)

    _proj_odd_body(x_ref, g_ref, w_ref, wg_ref, wa_ref, ba_ref, q_s, k_s, v_s, z_s, bc_s)

    row = lax.broadcasted_iota(jnp.int32, (L, L), 0)
    col = lax.broadcasted_iota(jnp.int32, (L, L), 1)
    causal = row >= col
    mid = L // 2 - 1

    for c, h in [(c, h) for c in range(0, TOKEN_TILE, 2 * L) for h in range(G_HEADS)]:
        ks = slice(h * G_DK, (h + 1) * G_DK)
        vs = slice(h * G_DV, (h + 1) * G_DV)
        r1, r2 = slice(c, c + L), slice(c + L, c + 2 * L)
        s_prev = s_ref[h]
        s_bf = s_prev.astype(BF16)
        v_p = v_s[c:c + 2 * L, vs]

        def prep(r):
            b = bc_s[r, ks]
            bm = b[mid:mid + 1, :]
            g = b[L - 1:L, :]
            e1 = jnp.exp(b - bm)
            qt = q_s[r, ks].astype(F32) * e1
            kt = k_s[r, ks].astype(F32) * (1.0 / e1)
            attn = jnp.where(causal, _dot_nt(qt.astype(BF16), kt.astype(BF16)), 0.0).astype(BF16)
            return attn, qt * jnp.exp(bm), kt * jnp.exp(g - bm), jnp.exp(g)

        a11, qi1, ke1, eg1 = prep(r1)
        a22, qi2, ke2, eg2 = prep(r2)
        qi2b = qi2.astype(BF16)
        a21 = _dot_nt(qi2b, ke1.astype(BF16)).astype(BF16)
        o1 = _dot(a11, v_p[0:L]) + _dot(qi1.astype(BF16), s_bf)
        o2 = _dot(jnp.concatenate([a21, a22], axis=1), v_p) + _dot((qi2 * eg1).astype(BF16), s_bf)
        y_s[r1, vs] = (_rmsnorm(o1, hg_ref[:, vs]) * _silu(z_s[r1, vs].astype(F32))).astype(BF16)
        y_s[r2, vs] = (_rmsnorm(o2, hg_ref[:, vs]) * _silu(z_s[r2, vs].astype(F32))).astype(BF16)
        ke_t = jnp.concatenate([(ke1 * eg2).T, ke2.T], axis=1).astype(BF16)
        g_col = jnp.broadcast_to(eg1 * eg2, (LANES, G_DK)).T[:, 0:1]
        s_ref[h] = g_col * s_prev + _dot(ke_t, v_p)

    x = x_ref[...] + _dot(y_s[...], wout_ref[...])
    out_ref[...] = _rmsnorm(x, gf_ref[...])


def _layer1(x1, g, wm, wg, w_alpha, b_alpha, head_g, w_out, gf, seq):
    t, d = x1.shape
    dk = G_HEADS * G_DK
    assert seq % TOKEN_TILE == 0 and TOKEN_TILE % G_CHUNK == 0
    tok = lambda i: (i, 0)
    const = lambda i: (0, 0)
    resident = lambda shape: pl.BlockSpec(shape, const, pipeline_mode=pl.Buffered(1))
    slots = lambda n, dt: pltpu.VMEM((TOKEN_TILE, n), dt)
    return pl.pallas_call(
        functools.partial(_layer1_body, tiles_per_seq=seq // TOKEN_TILE),
        grid=(t // TOKEN_TILE,),
        in_specs=[
            pl.BlockSpec((TOKEN_TILE, d), tok),
            pl.BlockSpec((1, d), const),
            resident((d, wm.shape[1])),
            pl.BlockSpec((d, G_RANK_PAD), const),
            pl.BlockSpec((G_RANK_PAD, dk), const),
            pl.BlockSpec((1, dk), const),
            pl.BlockSpec((1, D_MIX), const),
            resident((D_MIX, d)),
            pl.BlockSpec((1, d), const),
        ],
        out_specs=pl.BlockSpec((TOKEN_TILE, d), tok),
        out_shape=jax.ShapeDtypeStruct((t, d), F32),
        scratch_shapes=[
            slots(dk, BF16),
            slots(dk, BF16),
            slots(D_MIX, BF16),
            slots(D_MIX, BF16),
            slots(dk, F32),
            slots(D_MIX, BF16),
            pltpu.VMEM((G_HEADS, G_DK, G_DV), F32),
        ],
        compiler_params=_cparams(1),
        name="layer1",
    )(x1, g.reshape(1, d), wm, wg, w_alpha, b_alpha.reshape(1, dk), head_g.reshape(1, D_MIX), w_out,
      gf.reshape(1, d))


def kernel(x, norm_g, final_norm_g, ev_w_in, ev_conv_w, ev_conv_b, ev_i_bias, ev_f_bias, ev_head_g,
           s5_lam_re, s5_lam_im, s5_log_dt, s5_b_re, s5_b_im, s5_c_re, s5_c_im, s5_d, s5_glu_w,
           s5_glu_b, ev_w_out, od_w_in, gla_w_alpha, gla_b_alpha, gla_head_g, od_w_out):
    batch, seq, d = x.shape
    t = batch * seq
    xf = x.reshape(t, d)
    padc = lambda a: jnp.pad(a, ((0, 0), (0, LANES - a.shape[1])))

    wt = jnp.swapaxes(ev_w_in, 1, 2).reshape(ev_w_in.shape[2], d)
    g0 = 2 * M_HEADS * M_DK + 2 * M_HEADS * M_DV
    gi = g0 + M_HEADS
    gf = gi + M_HEADS
    half = D_MIX // 2
    w_qkvo, w_uz, wg = _cast_weights(wt, (0, gf), (g0, half + D_MIX), ((g0, M_HEADS), (gi, M_HEADS)))
    qk, v, o, u, z, gates = _proj_even(xf, norm_g[0], w_qkvo, w_uz, wg, ev_conv_w[0], ev_conv_b[0], seq)
    gbias = jnp.concatenate([padc(ev_i_bias), padc(ev_f_bias)], axis=1)
    hm = _mlstm(qk, v, gates, gbias, batch, seq)

    y3 = _s5(u.reshape(batch, seq, half), s5_lam_re[0], s5_lam_im[0], s5_log_dt[0], s5_b_re[0],
             s5_b_im[0], s5_c_re[0], s5_c_im[0], s5_d[0])
    yb = y3.reshape(t, half)
    x1 = _even_out(yb, hm, o, z, xf, ev_head_g[0], s5_glu_w[0].astype(BF16), s5_glu_b[0],
                   ev_w_out[0].astype(BF16))

    wt = jnp.swapaxes(od_w_in, 1, 2).reshape(od_w_in.shape[2], d)
    n_main = 2 * G_HEADS * G_DK + 2 * D_MIX
    wa = jnp.pad(gla_w_alpha[0], ((0, G_RANK_PAD - gla_w_alpha.shape[1]), (0, 0)))
    wm, wr = _cast_weights(wt, (0,), (n_main,), ((n_main, wt.shape[0] - n_main),),
                           scaled_cols=G_HEADS * G_DK, scale=G_DK ** -0.5)
    out = _layer1(x1, norm_g[1], wm, wr, wa, gla_b_alpha[0],
                  gla_head_g[0], od_w_out[0].astype(BF16), final_norm_g, seq)
    return out.reshape(batch, seq, d)
```

```python
import functools

import jax
import jax.numpy as jnp
from jax import lax
from jax.experimental import pallas as pl
from jax.experimental.pallas import tpu as pltpu

F32 = jnp.float32
BF16 = jnp.bfloat16

EPS = 1e-6
D_MODEL = 1024
D_MIX = 2 * D_MODEL
M_HEADS = 4
M_DK = 128
M_DV = 256
M_QK = 2 * M_HEADS * M_DK
CONV_WIDTH = 4
M_CHUNK = 256
M_SUB = 2
S5_GROUP = 16
S5_STATE = 64
S5_BLK = 8
S5_CHUNK = 1024
G_HEADS = 4
G_DK = 256
G_DV = 512
G_TAU = 16.0
G_CHUNK = 128
G_RANK_PAD = 128

LANES = 128
SUBLANES = 8
HALO = 16
S5_GPL = LANES // S5_GROUP
S5_SW = S5_GPL * S5_STATE
S5_XROWS = S5_CHUNK // S5_BLK + SUBLANES
TOKEN_TILE = 512
PROJ_TN = 256
VMEM_LIMIT = 56 * 1024 * 1024


def _cparams(n_grid):
    return pltpu.CompilerParams(
        dimension_semantics=("arbitrary",) * n_grid, vmem_limit_bytes=VMEM_LIMIT)


def _log_sigmoid(x):
    return jnp.minimum(x, 0.0) - jnp.log(1.0 + jnp.exp(-jnp.abs(x)))


def _silu(x):
    return x * jax.nn.sigmoid(x)


def _split_hi_lo(x):
    hi = x.astype(BF16)
    lo = (x - hi.astype(F32)).astype(BF16)
    return hi, lo


def _dot(a, b):
    return jnp.dot(a, b, preferred_element_type=F32)


def _dot_nt(a, b, precision=None):
    return lax.dot_general(a, b, (((1,), (1,)), ((), ())), precision=precision,
                           preferred_element_type=F32)


def _tri_ones(n, lower):
    row = lax.broadcasted_iota(jnp.int32, (n, n), 0)
    col = lax.broadcasted_iota(jnp.int32, (n, n), 1)
    keep = (row >= col) if lower else (row <= col)
    return jnp.where(keep, 1.0, 0.0).astype(BF16)


def _rmsnorm(x, g):
    return x * lax.rsqrt(jnp.mean(x * x, axis=-1, keepdims=True) + EPS) * g


def _project(hb, w_ref, out_refs, col0):
    off = col0
    for o_ref in out_refs:
        n = o_ref.shape[1]
        for j in range(0, n, PROJ_TN):
            o_ref[:, j:j + PROJ_TN] = _dot(hb, w_ref[:, off + j:off + j + PROJ_TN]).astype(o_ref.dtype)
        off += n


CAST_ROWS = 128


def _cast_body(wt_ref, *out_refs, starts, narrow, scaled_cols, scale):
    for o_ref, c0 in zip(out_refs[:-1], starts):
        n = o_ref.shape[1]
        val = wt_ref[c0:c0 + n, :]
        if scaled_cols and c0 == 0:
            col = lax.broadcasted_iota(jnp.int32, (n, 1), 0)
            val = val * jnp.where(col < scaled_cols, scale, 1.0)
        o_ref[...] = val.T.astype(BF16)
    lane = lax.broadcasted_iota(jnp.int32, (1, LANES), 1)
    for blk, (c0, n) in enumerate(narrow):
        start = (c0 // SUBLANES) * SUBLANES
        take = -(-(c0 - start + n) // SUBLANES) * SUBLANES
        rows = jnp.concatenate([wt_ref[start:start + take, :], jnp.zeros((LANES - take, CAST_ROWS), F32)], axis=0)
        slab = rows.T
        if c0 != start:
            slab = pltpu.roll(slab, LANES - (c0 - start), axis=1)
        out_refs[-1][:, blk * LANES:(blk + 1) * LANES] = jnp.where(lane < n, slab, 0.0).astype(BF16)


def _cast_weights(wt, starts, widths, narrow, scaled_cols=0, scale=1.0):
    cols, rows = wt.shape
    assert CAST_ROWS == LANES
    widths = tuple(widths) + (len(narrow) * LANES,)
    return pl.pallas_call(
        functools.partial(_cast_body, starts=starts, narrow=narrow, scaled_cols=scaled_cols, scale=scale),
        grid=(rows // CAST_ROWS,),
        in_specs=[pl.BlockSpec((cols, CAST_ROWS), lambda i: (0, i))],
        out_specs=[pl.BlockSpec((CAST_ROWS, n), lambda i: (i, 0)) for n in widths],
        out_shape=[jax.ShapeDtypeStruct((rows, n), BF16) for n in widths],
        compiler_params=_cparams(1),
        name="cast_weights",
    )(wt)


def _proj_even_body(x_ref, xh_ref, g_ref, w_ref, wb_ref, wg_ref, convw_ref, convb_ref,
                    qk_ref, v_ref, o_ref, u_ref, z_ref, gates_ref, ext_ref, *, tiles_per_seq):
    hb = _rmsnorm(x_ref[...], g_ref[...]).astype(BF16)
    hh = _rmsnorm(xh_ref[...], g_ref[...]).astype(BF16)
    seq_start = lax.rem(pl.program_id(0), tiles_per_seq) == 0
    base = HALO - (CONV_WIDTH - 1)
    lane = lax.broadcasted_iota(jnp.int32, (1, PROJ_TN), 1)
    for j in range(0, M_QK, PROJ_TN):
        cols = slice(j, j + PROJ_TN)
        wj = w_ref[:, cols]
        ext_ref[HALO:HALO + TOKEN_TILE, :] = _dot(hb, wj)
        ext_ref[0:HALO, :] = jnp.where(seq_start, 0.0, _dot(hh, wj))
        acc = convb_ref[:, cols] + convw_ref[0:1, cols] * ext_ref[base:base + TOKEN_TILE, :]
        for i in range(1, CONV_WIDTH):
            acc = acc + convw_ref[i:i + 1, cols] * ext_ref[base + i:base + i + TOKEN_TILE, :]
        scale = jnp.where(lane + j < M_HEADS * M_DK, M_DK ** -0.5, 1.0)
        qk_ref[:, cols] = (_silu(acc) * scale).astype(BF16)
    _project(hb, w_ref, (v_ref, o_ref), M_QK)
    _project(hb, wb_ref, (u_ref, z_ref), 0)
    gates_ref[...] = _dot(hb, wg_ref[...])


def _proj_even(xf, g, wm, wb, wg, conv_w, conv_b, seq):
    t, d = xf.shape
    half = D_MIX // 2
    assert seq % TOKEN_TILE == 0 and TOKEN_TILE % HALO == 0
    tok = lambda i: (i, 0)
    const = lambda i: (0, 0)
    per_halo = TOKEN_TILE // HALO
    widths = (M_QK, half, half, half, D_MIX)
    dtypes = (BF16, BF16, BF16, F32, BF16)
    out_shape = [jax.ShapeDtypeStruct((t, n), dt) for n, dt in zip(widths, dtypes)]
    ngate = wg.shape[1]
    out_shape.append(jax.ShapeDtypeStruct((t, ngate), F32))
    out_specs = [pl.BlockSpec((TOKEN_TILE, n), tok) for n in widths]
    out_specs.append(pl.BlockSpec((TOKEN_TILE, ngate), tok))
    return pl.pallas_call(
        functools.partial(_proj_even_body, tiles_per_seq=seq // TOKEN_TILE),
        grid=(t // TOKEN_TILE,),
        in_specs=[
            pl.BlockSpec((TOKEN_TILE, d), tok),
            pl.BlockSpec((HALO, d), lambda i: (jnp.maximum(i * per_halo - 1, 0), 0)),
            pl.BlockSpec((1, d), const),
            pl.BlockSpec((d, wm.shape[1]), const, pipeline_mode=pl.Buffered(1)),
            pl.BlockSpec((d, wb.shape[1]), const, pipeline_mode=pl.Buffered(1)),
            pl.BlockSpec((d, ngate), const),
            pl.BlockSpec((CONV_WIDTH, M_QK), const),
            pl.BlockSpec((1, M_QK), const),
        ],
        out_specs=out_specs,
        out_shape=out_shape,
        scratch_shapes=[pltpu.VMEM((HALO + TOKEN_TILE, PROJ_TN), F32)],
        compiler_params=_cparams(1),
        name="proj_even",
    )(xf, xf, g.reshape(1, d), wm, wb, wg, conv_w, conv_b.reshape(1, M_QK))


def _mlstm_body(qk_ref, v_ref, gates_ref, gbias_ref, out_ref, c_ref, n_ref, m_ref):
    @pl.when(pl.program_id(1) == 0)
    def _init():
        c_ref[...] = jnp.zeros_like(c_ref)
        n_ref[...] = jnp.zeros_like(n_ref)
        m_ref[...] = jnp.zeros_like(m_ref)

    for sub in range(M_SUB):
        _mlstm_chunk(slice(sub * M_CHUNK, (sub + 1) * M_CHUNK), qk_ref, v_ref, gates_ref, gbias_ref,
                     out_ref, c_ref, n_ref, m_ref)


def _mlstm_chunk(r, qk_ref, v_ref, gates_ref, gbias_ref, out_ref, c_ref, n_ref, m_ref):
    L = M_CHUNK
    gt = gates_ref[r, :] + gbias_ref[...]
    ipre = gt[:, 0:LANES]
    logf = _log_sigmoid(gt[:, LANES:2 * LANES]).astype(BF16)
    b = _dot(_tri_ones(L, True), logf)
    w = ipre - b
    rows = lax.broadcasted_iota(jnp.int32, (L, LANES), 0)
    cm = w
    k = 1
    while k < L:
        cm = jnp.maximum(cm, jnp.where(rows >= k, pltpu.roll(cm, k, axis=0), -jnp.inf))
        k *= 2

    def replicate(x):
        return jnp.concatenate([jnp.broadcast_to(x[:, h:h + 1], (L, LANES)) for h in range(M_HEADS)], axis=1)

    b_rep = replicate(b)
    w_rep = replicate(w)
    cm_rep = replicate(cm)
    pick_r = lax.broadcasted_iota(jnp.int32, (M_HEADS * SUBLANES, LANES), 0)
    pick_c = lax.broadcasted_iota(jnp.int32, (M_HEADS * SUBLANES, LANES), 1)
    pick = jnp.where(lax.shift_right_logical(pick_r, 3) == pick_c, 1.0, 0.0).astype(BF16)
    w_hi, w_lo = _split_hi_lo(w)
    w_row = _dot_nt(pick, w_hi) + _dot_nt(pick, w_lo)

    trow = lax.broadcasted_iota(jnp.int32, (L, LANES), 0)
    tcol = lax.broadcasted_iota(jnp.int32, (L, LANES), 1)
    ones = jnp.ones((L, LANES), BF16)

    for h in range(M_HEADS):
        ks = slice(h * M_DK, (h + 1) * M_DK)
        ks2 = slice(M_HEADS * M_DK + h * M_DK, M_HEADS * M_DK + (h + 1) * M_DK)
        vs = slice(h * M_DV, (h + 1) * M_DV)
        hs = slice(h * LANES, (h + 1) * LANES)
        m_prev = m_ref[h, 0:1, :]
        c_prev = c_ref[h]
        n_prev = n_ref[h]
        big_m = jnp.maximum(m_prev, cm_rep[:, hs])
        w_inter = jnp.exp(m_prev - big_m)
        wr = w_row[h * SUBLANES:h * SUBLANES + 1, :]

        qb = qk_ref[r, ks]
        kb = qk_ref[r, ks2]
        v_h = v_ref[r, vs]
        s = _dot_nt(qb, kb)
        sc = jnp.concatenate(
            [jnp.where(trow >= tcol + j, jnp.exp(wr[:, j:j + LANES] - big_m), 0.0) * s[:, j:j + LANES]
             for j in range(0, L, LANES)], axis=1).astype(BF16)
        q_c = _dot(qb, c_prev.astype(BF16))
        den = _dot(sc, ones) + w_inter * _dot(qb, n_prev.astype(BF16))
        inv = 1.0 / jnp.maximum(jnp.abs(den), jnp.exp(-(b_rep[:, hs] + big_m)))
        num = _dot(sc, v_h)
        out_ref[r, vs] = jnp.concatenate(
            [(num[:, j:j + LANES] + w_inter * q_c[:, j:j + LANES]) * inv for j in range(0, M_DV, LANES)],
            axis=1).astype(BF16)

        g = b_rep[L - 1:L, hs]
        cm_last = cm_rep[L - 1:L, hs]
        m_last = big_m[L - 1:L, :]
        kw_t = (kb.astype(F32) * jnp.exp(w_rep[:, hs] - cm_last)).T.astype(BF16)
        s_prev = jnp.exp(m_prev - m_last)
        s_loc = jnp.exp(cm_last - m_last)
        c_ref[h] = (jnp.concatenate([s_prev] * (M_DV // LANES), axis=1) * c_prev
                    + jnp.concatenate([s_loc] * (M_DV // LANES), axis=1) * _dot(kw_t, v_h))
        n_ref[h] = s_prev * n_prev + s_loc * _dot(kw_t, ones)
        m_ref[h] = jnp.broadcast_to(g + m_last, (SUBLANES, LANES))


def _mlstm(qk, v, gates, gbias, batch, seq):
    t = batch * seq
    L = M_CHUNK * M_SUB
    nc = seq // L
    dv = M_HEADS * M_DV
    tok = lambda b, c: (b * nc + c, 0)
    const = lambda b, c: (0, 0)
    return pl.pallas_call(
        _mlstm_body,
        grid=(batch, nc),
        in_specs=[
            pl.BlockSpec((L, M_QK), tok),
            pl.BlockSpec((L, dv), tok),
            pl.BlockSpec((L, 2 * LANES), tok),
            pl.BlockSpec((1, 2 * LANES), const),
        ],
        out_specs=pl.BlockSpec((L, dv), tok),
        out_shape=jax.ShapeDtypeStruct((t, dv), BF16),
        scratch_shapes=[
            pltpu.VMEM((M_HEADS, M_DK, M_DV), F32),
            pltpu.VMEM((M_HEADS, M_DK, LANES), F32),
            pltpu.VMEM((M_HEADS, SUBLANES, LANES), F32),
        ],
        compiler_params=_cparams(2),
        name="mlstm",
    )(qk, v, gates, gbias)


def _gelu_tanh(x):
    c = 0.7978845608028654
    half = 0.5 * x
    return half + half * jnp.tanh(x * (c + (c * 0.044715) * (x * x)))


def _s5_build_operators(lamr_ref, lami_ref, ldt_ref, btr_ref, bti_ref, ctr_ref, cti_ref, d_ref,
                        pbig_ref, qbig_ref, mbig_ref, a_ref):
    lr = lamr_ref[0]
    li = lami_ref[0]
    dt = jnp.exp(ldt_ref[0])
    zr = lr * dt
    th = li * dt
    er = jnp.exp(zr)
    ar = er * jnp.cos(th)
    ai = er * jnp.sin(th)
    den = lr * lr + li * li
    beta_r = ((ar - 1.0) * lr + ai * li) / den
    beta_i = (ai * lr - (ar - 1.0) * li) / den
    btr = btr_ref[0]
    bti = bti_ref[0]
    bbr = btr * beta_r - bti * beta_i
    bbi = btr * beta_i + bti * beta_r
    ctr = ctr_ref[0]
    cti = cti_ref[0]

    row_g = lax.shift_right_logical(lax.broadcasted_iota(jnp.int32, (LANES, S5_SW), 0), 4)
    lane_g = lax.shift_right_logical(lax.broadcasted_iota(jnp.int32, (LANES, S5_SW), 1), 6)
    same_group = row_g == lane_g

    def expand(x16):
        return jnp.where(same_group, jnp.concatenate([x16] * S5_GPL, axis=0), 0.0)

    def power(k):
        e = jnp.exp(float(k) * zr)
        return e * jnp.cos(float(k) * th), e * jnp.sin(float(k) * th)

    for s in range(S5_BLK):
        rows = slice(s * LANES, (s + 1) * LANES)
        pr, pi = power(S5_BLK - 1 - s)
        pbig_ref[rows, 0:S5_SW] = expand(pr * bbr - pi * bbi).astype(BF16)
        pbig_ref[rows, S5_SW:2 * S5_SW] = expand(pr * bbi + pi * bbr).astype(BF16)
        pr, pi = power(s + 1)
        qbig_ref[rows, 0:S5_SW] = expand(ctr * pr - cti * pi).astype(BF16)
        qbig_ref[rows, S5_SW:2 * S5_SW] = expand(-(ctr * pi + cti * pr)).astype(BF16)

    cb = jnp.concatenate([expand(ctr), expand(-cti)], axis=1)
    r128 = lax.broadcasted_iota(jnp.int32, (LANES, LANES), 0)
    c128 = lax.broadcasted_iota(jnp.int32, (LANES, LANES), 1)
    zero_blk = jnp.zeros((LANES, LANES), BF16)
    ab_lags = []
    for lag in range(S5_BLK):
        pr, pi = power(lag)
        ab_lags.append(jnp.concatenate([expand(pr * bbr - pi * bbi), expand(pr * bbi + pi * bbr)], axis=1))
    ab_hi, ab_lo = _split_hi_lo(jnp.concatenate(ab_lags, axis=0))
    cb_hi, cb_lo = _split_hi_lo(cb)
    v_lags = _dot_nt(ab_hi, cb_hi) + _dot_nt(ab_hi, cb_lo) + _dot_nt(ab_lo, cb_hi)
    for lag in range(S5_BLK):
        v = v_lags[lag * LANES:(lag + 1) * LANES, :]
        if lag == 0:
            v = v + jnp.where(r128 == c128, d_ref[0], 0.0)
        vb = v.astype(BF16)
        for s in range(S5_BLK - lag):
            t = s + lag
            mbig_ref[s * LANES:(s + 1) * LANES, t * LANES:(t + 1) * LANES] = vb
            if lag > 0:
                mbig_ref[t * LANES:(t + 1) * LANES, s * LANES:(s + 1) * LANES] = zero_blk

    pr, pi = power(S5_BLK)
    a_ref[:, 0:S5_SW] = jnp.broadcast_to(pr, (SUBLANES, S5_SW))
    a_ref[:, S5_SW:2 * S5_SW] = jnp.broadcast_to(pi, (SUBLANES, S5_SW))


def _s5_body(u_ref, lamr_ref, lami_ref, ldt_ref, btr_ref, bti_ref, ctr_ref, cti_ref, d_ref, y_ref,
             pbig_ref, qbig_ref, mbig_ref, a_ref, ucat_ref, x_ref, xp_ref, st_ref):
    nblk = S5_CHUNK // S5_BLK
    batch = u_ref.shape[0]

    @pl.when(pl.program_id(1) == 0)
    def _setup():
        _s5_build_operators(lamr_ref, lami_ref, ldt_ref, btr_ref, bti_ref, ctr_ref, cti_ref, d_ref,
                            pbig_ref, qbig_ref, mbig_ref, a_ref)
        st_ref[...] = jnp.zeros_like(st_ref)

    for b in range(batch):
        for s in range(S5_BLK):
            piece = u_ref[b, pl.ds(s, nblk, stride=S5_BLK), :]
            ucat_ref[b * nblk:(b + 1) * nblk, s * LANES:(s + 1) * LANES] = piece.astype(BF16)
    nslab = 2 * S5_SW // LANES
    half = nslab // 2
    xloc = _dot(ucat_ref[...], pbig_ref[...])
    for c in range(nslab):
        for b in range(batch):
            x_ref[c, b * S5_XROWS:b * S5_XROWS + nblk, :] = (
                xloc[b * nblk:(b + 1) * nblk, c * LANES:(c + 1) * LANES])

    lanes = lambda ref, c: ref[:, c * LANES:(c + 1) * LANES]
    ar = [lanes(a_ref, c) for c in range(half)]
    ai = [lanes(a_ref, half + c) for c in range(half)]
    xr = [lanes(st_ref, c) for c in range(half)]
    xi = [lanes(st_ref, half + c) for c in range(half)]
    for blk in range(nblk):
        r = pl.ds(blk, batch, stride=S5_XROWS)
        for c in range(half):
            xp_ref[c, r, :] = xr[c]
            xp_ref[half + c, r, :] = xi[c]
            nr = ar[c] * xr[c] - ai[c] * xi[c] + x_ref[c, r, :]
            ni = ar[c] * xi[c] + ai[c] * xr[c] + x_ref[half + c, r, :]
            xr[c], xi[c] = nr, ni
    for c in range(half):
        st_ref[:, c * LANES:(c + 1) * LANES] = xr[c]
        st_ref[:, (half + c) * LANES:(half + c + 1) * LANES] = xi[c]

    xpb = jnp.concatenate(
        [jnp.concatenate([xp_ref[c, b * S5_XROWS:b * S5_XROWS + nblk, :].astype(BF16)
                          for c in range(nslab)], axis=1) for b in range(batch)], axis=0)
    width = 2 * LANES
    for nb in range(S5_BLK // 2):
        kk = (2 * nb + 2) * LANES
        cols = slice(nb * width, (nb + 1) * width)
        y = _dot(ucat_ref[:, 0:kk], mbig_ref[0:kk, cols]) + _dot_nt(xpb, qbig_ref[cols, :])
        y = _gelu_tanh(y)
        for tt in range(2):
            t = 2 * nb + tt
            for b in range(batch):
                y_ref[b, pl.ds(t, nblk, stride=S5_BLK), :] = (
                    y[b * nblk:(b + 1) * nblk, tt * LANES:(tt + 1) * LANES])


def _s5(u3, lam_re, lam_im, log_dt, b_re, b_im, c_re, c_im, d_skip):
    batch, seq, width = u3.shape
    nlb = width // LANES
    assert batch == SUBLANES and seq % S5_CHUNK == 0
    lamr = lam_re.reshape(nlb, 1, S5_SW)
    lami = lam_im.reshape(nlb, 1, S5_SW)
    ldt = jnp.repeat(log_dt, S5_STATE).reshape(nlb, 1, S5_SW)
    bt = lambda b: b.reshape(nlb, S5_GPL, S5_STATE, S5_GROUP).transpose(0, 3, 1, 2).reshape(nlb, S5_GROUP, S5_SW)
    ct = lambda c: c.reshape(nlb, S5_GPL, S5_GROUP, S5_STATE).transpose(0, 2, 1, 3).reshape(nlb, S5_GROUP, S5_SW)
    par = lambda r, w: pl.BlockSpec((1, r, w), lambda i, j: (i, 0, 0))
    kdim = S5_BLK * LANES
    rows = (S5_CHUNK // S5_BLK) * batch
    io = pl.BlockSpec((batch, S5_CHUNK, LANES), lambda i, j: (0, j, i))
    return pl.pallas_call(
        _s5_body,
        grid=(nlb, seq // S5_CHUNK),
        in_specs=[io, par(1, S5_SW), par(1, S5_SW), par(1, S5_SW), par(S5_GROUP, S5_SW),
                  par(S5_GROUP, S5_SW), par(S5_GROUP, S5_SW), par(S5_GROUP, S5_SW), par(1, LANES)],
        out_specs=io,
        out_shape=jax.ShapeDtypeStruct((batch, seq, width), F32),
        scratch_shapes=[
            pltpu.VMEM((kdim, 2 * S5_SW), BF16),
            pltpu.VMEM((kdim, 2 * S5_SW), BF16),
            pltpu.VMEM((kdim, kdim), BF16),
            pltpu.VMEM((SUBLANES, 2 * S5_SW), F32),
            pltpu.VMEM((rows, kdim), BF16),
            pltpu.VMEM((2 * S5_SW // LANES, batch * S5_XROWS, LANES), F32),
            pltpu.VMEM((2 * S5_SW // LANES, batch * S5_XROWS, LANES), F32),
            pltpu.VMEM((SUBLANES, 2 * S5_SW), F32),
        ],
        compiler_params=_cparams(2),
        name="s5",
    )(u3, lamr, lami, ldt, bt(b_re), bt(b_im), ct(c_re), ct(c_im), d_skip.reshape(nlb, 1, LANES))


def _even_out_body(yb_ref, h_ref, o_ref, z_ref, x_ref, hg_ref, gluw_ref, glub_ref, wout_ref,
                   out_ref, y_ref):
    half = h_ref.shape[1]
    for h in range(M_HEADS):
        vs = slice(h * M_DV, (h + 1) * M_DV)
        og = (jax.nn.sigmoid(o_ref[:, vs]) * h_ref[:, vs]).astype(F32)
        y_ref[:, vs] = _rmsnorm(og, hg_ref[:, vs]).astype(BF16) * _silu(z_ref[:, vs])
    yg = yb_ref[...]
    s = _dot(yg.astype(BF16), gluw_ref[...]) + glub_ref[...]
    hb = yg * jax.nn.sigmoid(s)
    y_ref[:, half:2 * half] = hb.astype(BF16) * _silu(z_ref[:, half:2 * half])
    out_ref[...] = x_ref[...] + _dot(y_ref[...], wout_ref[...])


def _even_out(yb, hm, o, z, xf, head_g, glu_w, glu_b, w_out):
    t, d = xf.shape
    half = hm.shape[1]
    tok = lambda i: (i, 0)
    const = lambda i: (0, 0)
    return pl.pallas_call(
        _even_out_body,
        grid=(t // TOKEN_TILE,),
        in_specs=[
            pl.BlockSpec((TOKEN_TILE, half), tok),
            pl.BlockSpec((TOKEN_TILE, half), tok),
            pl.BlockSpec((TOKEN_TILE, half), tok),
            pl.BlockSpec((TOKEN_TILE, 2 * half), tok),
            pl.BlockSpec((TOKEN_TILE, d), tok),
            pl.BlockSpec((1, half), const),
            pl.BlockSpec((half, half), const),
            pl.BlockSpec((1, half), const),
            pl.BlockSpec((2 * half, d), const),
        ],
        out_specs=pl.BlockSpec((TOKEN_TILE, d), tok),
        out_shape=jax.ShapeDtypeStruct((t, d), F32),
        scratch_shapes=[pltpu.VMEM((TOKEN_TILE, 2 * half), BF16)],
        compiler_params=_cparams(1),
        name="even_out",
    )(yb, hm, o, z, xf, head_g.reshape(1, half), glu_w, glu_b.reshape(1, half), w_out)


def _proj_odd_body(x_ref, g_ref, w_ref, wg_ref, wa_ref, ba_ref, q_ref, k_ref, v_ref, z_ref, bc_ref):
    hb = _rmsnorm(x_ref[...], g_ref[...]).astype(BF16)
    _project(hb, w_ref, (q_ref, k_ref, v_ref, z_ref), 0)
    rb = _dot(hb, wg_ref[...]).astype(BF16)
    tril = _tri_ones(G_CHUNK, True)
    for j in range(0, bc_ref.shape[1], PROJ_TN):
        cols = slice(j, j + PROJ_TN)
        pre = _dot(rb, wa_ref[:, cols].astype(BF16)) + ba_ref[:, cols]
        la = (_log_sigmoid(pre) * (1.0 / G_TAU)).astype(BF16)
        for c in range(0, TOKEN_TILE, G_CHUNK):
            rows = slice(c, c + G_CHUNK)
            bc_ref[rows, cols] = _dot(tril, la[rows])


def _layer1_body(x_ref, g_ref, w_ref, wg_ref, wa_ref, ba_ref, hg_ref, wout_ref, gf_ref, out_ref,
                 q_s, k_s, v_s, z_s, bc_s, y_s, s_ref, *, tiles_per_seq):
    L = G_CHUNK

    @pl.when(lax.rem(pl.program_id(0), tiles_per_seq) == 0)
    def _init():
        s_ref[...] = jnp.zeros_like(s_ref)

    _proj_odd_body(x_ref, g_ref, w_ref, wg_ref, wa_ref, ba_ref, q_s, k_s, v_s, z_s, bc_s)

    row = lax.broadcasted_iota(jnp.int32, (L, L), 0)
    col = lax.broadcasted_iota(jnp.int32, (L, L), 1)
    causal = row >= col
    mid = L // 2 - 1

    for c, h in [(c, h) for c in range(0, TOKEN_TILE, 2 * L) for h in range(G_HEADS)]:
        ks = slice(h * G_DK, (h + 1) * G_DK)
        vs = slice(h * G_DV, (h + 1) * G_DV)
        r1, r2 = slice(c, c + L), slice(c + L, c + 2 * L)
        s_prev = s_ref[h]
        s_bf = s_prev.astype(BF16)
        v_p = v_s[c:c + 2 * L, vs]

        def prep(r):
            b = bc_s[r, ks]
            bm = b[mid:mid + 1, :]
            g = b[L - 1:L, :]
            e1 = jnp.exp(b - bm)
            qt = q_s[r, ks].astype(F32) * e1
            kt = k_s[r, ks].astype(F32) * (1.0 / e1)
            attn = jnp.where(causal, _dot_nt(qt.astype(BF16), kt.astype(BF16)), 0.0).astype(BF16)
            return attn, qt * jnp.exp(bm), kt * jnp.exp(g - bm), jnp.exp(g)

        a11, qi1, ke1, eg1 = prep(r1)
        a22, qi2, ke2, eg2 = prep(r2)
        qi2b = qi2.astype(BF16)
        a21 = _dot_nt(qi2b, ke1.astype(BF16)).astype(BF16)
        o1 = _dot(a11, v_p[0:L]) + _dot(qi1.astype(BF16), s_bf)
        o2 = _dot(jnp.concatenate([a21, a22], axis=1), v_p) + _dot((qi2 * eg1).astype(BF16), s_bf)
        y_s[r1, vs] = (_rmsnorm(o1, hg_ref[:, vs]) * _silu(z_s[r1, vs].astype(F32))).astype(BF16)
        y_s[r2, vs] = (_rmsnorm(o2, hg_ref[:, vs]) * _silu(z_s[r2, vs].astype(F32))).astype(BF16)
        ke_t = jnp.concatenate([(ke1 * eg2).T, ke2.T], axis=1).astype(BF16)
        g_col = jnp.broadcast_to(eg1 * eg2, (LANES, G_DK)).T[:, 0:1]
        s_ref[h] = g_col * s_prev + _dot(ke_t, v_p)

    x = x_ref[...] + _dot(y_s[...], wout_ref[...])
    out_ref[...] = _rmsnorm(x, gf_ref[...])


def _layer1(x1, g, wm, wg, w_alpha, b_alpha, head_g, w_out, gf, seq):
    t, d = x1.shape
    dk = G_HEADS * G_DK
    assert seq % TOKEN_TILE == 0 and TOKEN_TILE % G_CHUNK == 0
    tok = lambda i: (i, 0)
    const = lambda i: (0, 0)
    resident = lambda shape: pl.BlockSpec(shape, const, pipeline_mode=pl.Buffered(1))
    slots = lambda n, dt: pltpu.VMEM((TOKEN_TILE, n), dt)
    return pl.pallas_call(
        functools.partial(_layer1_body, tiles_per_seq=seq // TOKEN_TILE),
        grid=(t // TOKEN_TILE,),
        in_specs=[
            pl.BlockSpec((TOKEN_TILE, d), tok),
            pl.BlockSpec((1, d), const),
            resident((d, wm.shape[1])),
            pl.BlockSpec((d, G_RANK_PAD), const),
            pl.BlockSpec((G_RANK_PAD, dk), const),
            pl.BlockSpec((1, dk), const),
            pl.BlockSpec((1, D_MIX), const),
            resident((D_MIX, d)),
            pl.BlockSpec((1, d), const),
        ],
        out_specs=pl.BlockSpec((TOKEN_TILE, d), tok),
        out_shape=jax.ShapeDtypeStruct((t, d), F32),
        scratch_shapes=[
            slots(dk, BF16),
            slots(dk, BF16),
            slots(D_MIX, BF16),
            slots(D_MIX, BF16),
            slots(dk, F32),
            slots(D_MIX, BF16),
            pltpu.VMEM((G_HEADS, G_DK, G_DV), F32),
        ],
        compiler_params=_cparams(1),
        name="layer1",
    )(x1, g.reshape(1, d), wm, wg, w_alpha, b_alpha.reshape(1, dk), head_g.reshape(1, D_MIX), w_out,
      gf.reshape(1, d))


def kernel(x, norm_g, final_norm_g, ev_w_in, ev_conv_w, ev_conv_b, ev_i_bias, ev_f_bias, ev_head_g,
           s5_lam_re, s5_lam_im, s5_log_dt, s5_b_re, s5_b_im, s5_c_re, s5_c_im, s5_d, s5_glu_w,
           s5_glu_b, ev_w_out, od_w_in, gla_w_alpha, gla_b_alpha, gla_head_g, od_w_out):
    batch, seq, d = x.shape
    t = batch * seq
    xf = x.reshape(t, d)
    padc = lambda a: jnp.pad(a, ((0, 0), (0, LANES - a.shape[1])))

    wt = jnp.swapaxes(ev_w_in, 1, 2).reshape(ev_w_in.shape[2], d)
    g0 = 2 * M_HEADS * M_DK + 2 * M_HEADS * M_DV
    gi = g0 + M_HEADS
    gf = gi + M_HEADS
    half = D_MIX // 2
    w_qkvo, w_uz, wg = _cast_weights(wt, (0, gf), (g0, half + D_MIX), ((g0, M_HEADS), (gi, M_HEADS)))
    qk, v, o, u, z, gates = _proj_even(xf, norm_g[0], w_qkvo, w_uz, wg, ev_conv_w[0], ev_conv_b[0], seq)
    gbias = jnp.concatenate([padc(ev_i_bias), padc(ev_f_bias)], axis=1)
    hm = _mlstm(qk, v, gates, gbias, batch, seq)

    y3 = _s5(u.reshape(batch, seq, half), s5_lam_re[0], s5_lam_im[0], s5_log_dt[0], s5_b_re[0],
             s5_b_im[0], s5_c_re[0], s5_c_im[0], s5_d[0])
    yb = y3.reshape(t, half)
    x1 = _even_out(yb, hm, o, z, xf, ev_head_g[0], s5_glu_w[0].astype(BF16), s5_glu_b[0],
                   ev_w_out[0].astype(BF16))

    wt = jnp.swapaxes(od_w_in, 1, 2).reshape(od_w_in.shape[2], d)
    n_main = 2 * G_HEADS * G_DK + 2 * D_MIX
    wa = jnp.pad(gla_w_alpha[0], ((0, G_RANK_PAD - gla_w_alpha.shape[1]), (0, 0)))
    wm, wr = _cast_weights(wt, (0,), (n_main,), ((n_main, wt.shape[0] - n_main),),
                           scaled_cols=G_HEADS * G_DK, scale=G_DK ** -0.5)
    out = _layer1(x1, norm_g[1], wm, wr, wa, gla_b_alpha[0],
                  gla_head_g[0], od_w_out[0].astype(BF16), final_norm_g, seq)
    return out.reshape(batch, seq, d)
```

```python
import functools

import jax
import jax.numpy as jnp
from jax import lax
from jax.experimental import pallas as pl
from jax.experimental.pallas import tpu as pltpu

F32 = jnp.float32
BF16 = jnp.bfloat16

EPS = 1e-6
D_MODEL = 1024
D_MIX = 2 * D_MODEL
M_HEADS = 4
M_DK = 128
M_DV = 256
M_QK = 2 * M_HEADS * M_DK
CONV_WIDTH = 4
M_CHUNK = 256
M_SUB = 2
S5_GROUP = 16
S5_STATE = 64
S5_BLK = 8
S5_CHUNK = 1024
G_HEADS = 4
G_DK = 256
G_DV = 512
G_TAU = 16.0
G_CHUNK = 128
G_RANK_PAD = 128

LANES = 128
SUBLANES = 8
HALO = 16
S5_GPL = LANES // S5_GROUP
S5_SW = S5_GPL * S5_STATE
S5_XROWS = S5_CHUNK // S5_BLK + SUBLANES
TOKEN_TILE = 512
PROJ_TN = 256
VMEM_LIMIT = 56 * 1024 * 1024


def _cparams(n_grid):
    return pltpu.CompilerParams(
        dimension_semantics=("arbitrary",) * n_grid, vmem_limit_bytes=VMEM_LIMIT)


def _log_sigmoid(x):
    return jnp.minimum(x, 0.0) - jnp.log(1.0 + jnp.exp(-jnp.abs(x)))


def _silu(x):
    return x * jax.nn.sigmoid(x)


def _split_hi_lo(x):
    hi = x.astype(BF16)
    lo = (x - hi.astype(F32)).astype(BF16)
    return hi, lo


def _dot(a, b):
    return jnp.dot(a, b, preferred_element_type=F32)


def _dot_nt(a, b, precision=None):
    return lax.dot_general(a, b, (((1,), (1,)), ((), ())), precision=precision,
                           preferred_element_type=F32)


def _tri_ones(n, lower):
    row = lax.broadcasted_iota(jnp.int32, (n, n), 0)
    col = lax.broadcasted_iota(jnp.int32, (n, n), 1)
    keep = (row >= col) if lower else (row <= col)
    return jnp.where(keep, 1.0, 0.0).astype(BF16)


def _rmsnorm(x, g):
    return x * lax.rsqrt(jnp.mean(x * x, axis=-1, keepdims=True) + EPS) * g


def _project(hb, w_ref, out_refs, col0):
    off = col0
    for o_ref in out_refs:
        n = o_ref.shape[1]
        for j in range(0, n, PROJ_TN):
            o_ref[:, j:j + PROJ_TN] = _dot(hb, w_ref[:, off + j:off + j + PROJ_TN]).astype(o_ref.dtype)
        off += n


CAST_ROWS = 128


def _cast_body(wt_ref, *out_refs, starts, narrow, scaled_cols, scale):
    for o_ref, c0 in zip(out_refs[:-1], starts):
        n = o_ref.shape[1]
        val = wt_ref[c0:c0 + n, :]
        if scaled_cols and c0 == 0:
            col = lax.broadcasted_iota(jnp.int32, (n, 1), 0)
            val = val * jnp.where(col < scaled_cols, scale, 1.0)
        o_ref[...] = val.T.astype(BF16)
    lane = lax.broadcasted_iota(jnp.int32, (1, LANES), 1)
    for blk, (c0, n) in enumerate(narrow):
        start = (c0 // SUBLANES) * SUBLANES
        take = -(-(c0 - start + n) // SUBLANES) * SUBLANES
        rows = jnp.concatenate([wt_ref[start:start + take, :], jnp.zeros((LANES - take, CAST_ROWS), F32)], axis=0)
        slab = rows.T
        if c0 != start:
            slab = pltpu.roll(slab, LANES - (c0 - start), axis=1)
        out_refs[-1][:, blk * LANES:(blk + 1) * LANES] = jnp.where(lane < n, slab, 0.0).astype(BF16)


def _cast_weights(wt, starts, widths, narrow, scaled_cols=0, scale=1.0):
    cols, rows = wt.shape
    assert CAST_ROWS == LANES
    widths = tuple(widths) + (len(narrow) * LANES,)
    return pl.pallas_call(
        functools.partial(_cast_body, starts=starts, narrow=narrow, scaled_cols=scaled_cols, scale=scale),
        grid=(rows // CAST_ROWS,),
        in_specs=[pl.BlockSpec((cols, CAST_ROWS), lambda i: (0, i))],
        out_specs=[pl.BlockSpec((CAST_ROWS, n), lambda i: (i, 0)) for n in widths],
        out_shape=[jax.ShapeDtypeStruct((rows, n), BF16) for n in widths],
        compiler_params=_cparams(1),
        name="cast_weights",
    )(wt)


def _proj_even_body(x_ref, xh_ref, g_ref, w_ref, wb_ref, wg_ref, convw_ref, convb_ref,
                    qk_ref, v_ref, o_ref, u_ref, z_ref, gates_ref, ext_ref, *, tiles_per_seq):
    hb = _rmsnorm(x_ref[...], g_ref[...]).astype(BF16)
    hh = _rmsnorm(xh_ref[...], g_ref[...]).astype(BF16)
    seq_start = lax.rem(pl.program_id(0), tiles_per_seq) == 0
    base = HALO - (CONV_WIDTH - 1)
    lane = lax.broadcasted_iota(jnp.int32, (1, PROJ_TN), 1)
    for j in range(0, M_QK, PROJ_TN):
        cols = slice(j, j + PROJ_TN)
        wj = w_ref[:, cols]
        ext_ref[HALO:HALO + TOKEN_TILE, :] = _dot(hb, wj)
        ext_ref[0:HALO, :] = jnp.where(seq_start, 0.0, _dot(hh, wj))
        acc = convb_ref[:, cols] + convw_ref[0:1, cols] * ext_ref[base:base + TOKEN_TILE, :]
        for i in range(1, CONV_WIDTH):
            acc = acc + convw_ref[i:i + 1, cols] * ext_ref[base + i:base + i + TOKEN_TILE, :]
        scale = jnp.where(lane + j < M_HEADS * M_DK, M_DK ** -0.5, 1.0)
        qk_ref[:, cols] = (_silu(acc) * scale).astype(BF16)
    _project(hb, w_ref, (v_ref, o_ref), M_QK)
    _project(hb, wb_ref, (u_ref, z_ref), 0)
    gates_ref[...] = _dot(hb, wg_ref[...])


def _proj_even(xf, g, wm, wb, wg, conv_w, conv_b, seq):
    t, d = xf.shape
    half = D_MIX // 2
    assert seq % TOKEN_TILE == 0 and TOKEN_TILE % HALO == 0
    tok = lambda i: (i, 0)
    const = lambda i: (0, 0)
    per_halo = TOKEN_TILE // HALO
    widths = (M_QK, half, half, half, D_MIX)
    dtypes = (BF16, BF16, BF16, F32, BF16)
    out_shape = [jax.ShapeDtypeStruct((t, n), dt) for n, dt in zip(widths, dtypes)]
    ngate = wg.shape[1]
    out_shape.append(jax.ShapeDtypeStruct((t, ngate), F32))
    out_specs = [pl.BlockSpec((TOKEN_TILE, n), tok) for n in widths]
    out_specs.append(pl.BlockSpec((TOKEN_TILE, ngate), tok))
    return pl.pallas_call(
        functools.partial(_proj_even_body, tiles_per_seq=seq // TOKEN_TILE),
        grid=(t // TOKEN_TILE,),
        in_specs=[
            pl.BlockSpec((TOKEN_TILE, d), tok),
            pl.BlockSpec((HALO, d), lambda i: (jnp.maximum(i * per_halo - 1, 0), 0)),
            pl.BlockSpec((1, d), const),
            pl.BlockSpec((d, wm.shape[1]), const, pipeline_mode=pl.Buffered(1)),
            pl.BlockSpec((d, wb.shape[1]), const, pipeline_mode=pl.Buffered(1)),
            pl.BlockSpec((d, ngate), const),
            pl.BlockSpec((CONV_WIDTH, M_QK), const),
            pl.BlockSpec((1, M_QK), const),
        ],
        out_specs=out_specs,
        out_shape=out_shape,
        scratch_shapes=[pltpu.VMEM((HALO + TOKEN_TILE, PROJ_TN), F32)],
        compiler_params=_cparams(1),
        name="proj_even",
    )(xf, xf, g.reshape(1, d), wm, wb, wg, conv_w, conv_b.reshape(1, M_QK))


def _mlstm_body(qk_ref, v_ref, gates_ref, gbias_ref, out_ref, c_ref, n_ref, m_ref):
    @pl.when(pl.program_id(1) == 0)
    def _init():
        c_ref[...] = jnp.zeros_like(c_ref)
        n_ref[...] = jnp.zeros_like(n_ref)
        m_ref[...] = jnp.zeros_like(m_ref)

    for sub in range(M_SUB):
        _mlstm_chunk(slice(sub * M_CHUNK, (sub + 1) * M_CHUNK), qk_ref, v_ref, gates_ref, gbias_ref,
                     out_ref, c_ref, n_ref, m_ref)


def _mlstm_chunk(r, qk_ref, v_ref, gates_ref, gbias_ref, out_ref, c_ref, n_ref, m_ref):
    L = M_CHUNK
    gt = gates_ref[r, :] + gbias_ref[...]
    ipre = gt[:, 0:LANES]
    logf = _log_sigmoid(gt[:, LANES:2 * LANES]).astype(BF16)
    b = _dot(_tri_ones(L, True), logf)
    w = ipre - b
    rows = lax.broadcasted_iota(jnp.int32, (L, LANES), 0)
    cm = w
    k = 1
    while k < L:
        cm = jnp.maximum(cm, jnp.where(rows >= k, pltpu.roll(cm, k, axis=0), -jnp.inf))
        k *= 2

    def replicate(x):
        return jnp.concatenate([jnp.broadcast_to(x[:, h:h + 1], (L, LANES)) for h in range(M_HEADS)], axis=1)

    b_rep = replicate(b)
    w_rep = replicate(w)
    cm_rep = replicate(cm)
    pick_r = lax.broadcasted_iota(jnp.int32, (M_HEADS * SUBLANES, LANES), 0)
    pick_c = lax.broadcasted_iota(jnp.int32, (M_HEADS * SUBLANES, LANES), 1)
    pick = jnp.where(lax.shift_right_logical(pick_r, 3) == pick_c, 1.0, 0.0).astype(BF16)
    w_hi, w_lo = _split_hi_lo(w)
    w_row = _dot_nt(pick, w_hi) + _dot_nt(pick, w_lo)

    trow = lax.broadcasted_iota(jnp.int32, (L, LANES), 0)
    tcol = lax.broadcasted_iota(jnp.int32, (L, LANES), 1)
    ones = jnp.ones((L, LANES), BF16)

    for h in range(M_HEADS):
        ks = slice(h * M_DK, (h + 1) * M_DK)
        ks2 = slice(M_HEADS * M_DK + h * M_DK, M_HEADS * M_DK + (h + 1) * M_DK)
        vs = slice(h * M_DV, (h + 1) * M_DV)
        hs = slice(h * LANES, (h + 1) * LANES)
        m_prev = m_ref[h, 0:1, :]
        c_prev = c_ref[h]
        n_prev = n_ref[h]
        big_m = jnp.maximum(m_prev, cm_rep[:, hs])
        w_inter = jnp.exp(m_prev - big_m)
        wr = w_row[h * SUBLANES:h * SUBLANES + 1, :]

        qb = qk_ref[r, ks]
        kb = qk_ref[r, ks2]
        v_h = v_ref[r, vs]
        s = _dot_nt(qb, kb)
        sc = jnp.concatenate(
            [jnp.where(trow >= tcol + j, jnp.exp(wr[:, j:j + LANES] - big_m), 0.0) * s[:, j:j + LANES]
             for j in range(0, L, LANES)], axis=1).astype(BF16)
        q_c = _dot(qb, c_prev.astype(BF16))
        den = _dot(sc, ones) + w_inter * _dot(qb, n_prev.astype(BF16))
        inv = 1.0 / jnp.maximum(jnp.abs(den), jnp.exp(-(b_rep[:, hs] + big_m)))
        num = _dot(sc, v_h)
        out_ref[r, vs] = jnp.concatenate(
            [(num[:, j:j + LANES] + w_inter * q_c[:, j:j + LANES]) * inv for j in range(0, M_DV, LANES)],
            axis=1).astype(BF16)

        g = b_rep[L - 1:L, hs]
        cm_last = cm_rep[L - 1:L, hs]
        m_last = big_m[L - 1:L, :]
        kw_t = (kb.astype(F32) * jnp.exp(w_rep[:, hs] - cm_last)).T.astype(BF16)
        s_prev = jnp.exp(m_prev - m_last)
        s_loc = jnp.exp(cm_last - m_last)
        c_ref[h] = (jnp.concatenate([s_prev] * (M_DV // LANES), axis=1) * c_prev
                    + jnp.concatenate([s_loc] * (M_DV // LANES), axis=1) * _dot(kw_t, v_h))
        n_ref[h] = s_prev * n_prev + s_loc * _dot(kw_t, ones)
        m_ref[h] = jnp.broadcast_to(g + m_last, (SUBLANES, LANES))


def _mlstm(qk, v, gates, gbias, batch, seq):
    t = batch * seq
    L = M_CHUNK * M_SUB
    nc = seq // L
    dv = M_HEADS * M_DV
    tok = lambda b, c: (b * nc + c, 0)
    const = lambda b, c: (0, 0)
    return pl.pallas_call(
        _mlstm_body,
        grid=(batch, nc),
        in_specs=[
            pl.BlockSpec((L, M_QK), tok),
            pl.BlockSpec((L, dv), tok),
            pl.BlockSpec((L, 2 * LANES), tok),
            pl.BlockSpec((1, 2 * LANES), const),
        ],
        out_specs=pl.BlockSpec((L, dv), tok),
        out_shape=jax.ShapeDtypeStruct((t, dv), BF16),
        scratch_shapes=[
            pltpu.VMEM((M_HEADS, M_DK, M_DV), F32),
            pltpu.VMEM((M_HEADS, M_DK, LANES), F32),
            pltpu.VMEM((M_HEADS, SUBLANES, LANES), F32),
        ],
        compiler_params=_cparams(2),
        name="mlstm",
    )(qk, v, gates, gbias)


def _gelu_tanh(x):
    c = 0.7978845608028654
    half = 0.5 * x
    return half + half * jnp.tanh(x * (c + (c * 0.044715) * (x * x)))


def _s5_build_operators(lamr_ref, lami_ref, ldt_ref, btr_ref, bti_ref, ctr_ref, cti_ref, d_ref,
                        pbig_ref, qbig_ref, mbig_ref, a_ref):
    lr = lamr_ref[0]
    li = lami_ref[0]
    dt = jnp.exp(ldt_ref[0])
    zr = lr * dt
    th = li * dt
    er = jnp.exp(zr)
    ar = er * jnp.cos(th)
    ai = er * jnp.sin(th)
    den = lr * lr + li * li
    beta_r = ((ar - 1.0) * lr + ai * li) / den
    beta_i = (ai * lr - (ar - 1.0) * li) / den
    btr = btr_ref[0]
    bti = bti_ref[0]
    bbr = btr * beta_r - bti * beta_i
    bbi = btr * beta_i + bti * beta_r
    ctr = ctr_ref[0]
    cti = cti_ref[0]

    row_g = lax.shift_right_logical(lax.broadcasted_iota(jnp.int32, (LANES, S5_SW), 0), 4)
    lane_g = lax.shift_right_logical(lax.broadcasted_iota(jnp.int32, (LANES, S5_SW), 1), 6)
    same_group = row_g == lane_g

    def expand(x16):
        return jnp.where(same_group, jnp.concatenate([x16] * S5_GPL, axis=0), 0.0)

    def power(k):
        e = jnp.exp(float(k) * zr)
        return e * jnp.cos(float(k) * th), e * jnp.sin(float(k) * th)

    for s in range(S5_BLK):
        rows = slice(s * LANES, (s + 1) * LANES)
        pr, pi = power(S5_BLK - 1 - s)
        pbig_ref[rows, 0:S5_SW] = expand(pr * bbr - pi * bbi).astype(BF16)
        pbig_ref[rows, S5_SW:2 * S5_SW] = expand(pr * bbi + pi * bbr).astype(BF16)
        pr, pi = power(s + 1)
        qbig_ref[rows, 0:S5_SW] = expand(ctr * pr - cti * pi).astype(BF16)
        qbig_ref[rows, S5_SW:2 * S5_SW] = expand(-(ctr * pi + cti * pr)).astype(BF16)

    cb = jnp.concatenate([expand(ctr), expand(-cti)], axis=1)
    r128 = lax.broadcasted_iota(jnp.int32, (LANES, LANES), 0)
    c128 = lax.broadcasted_iota(jnp.int32, (LANES, LANES), 1)
    zero_blk = jnp.zeros((LANES, LANES), BF16)
    ab_lags = []
    for lag in range(S5_BLK):
        pr, pi = power(lag)
        ab_lags.append(jnp.concatenate([expand(pr * bbr - pi * bbi), expand(pr * bbi + pi * bbr)], axis=1))
    ab_hi, ab_lo = _split_hi_lo(jnp.concatenate(ab_lags, axis=0))
    cb_hi, cb_lo = _split_hi_lo(cb)
    v_lags = _dot_nt(ab_hi, cb_hi) + _dot_nt(ab_hi, cb_lo) + _dot_nt(ab_lo, cb_hi)
    for lag in range(S5_BLK):
        v = v_lags[lag * LANES:(lag + 1) * LANES, :]
        if lag == 0:
            v = v + jnp.where(r128 == c128, d_ref[0], 0.0)
        vb = v.astype(BF16)
        for s in range(S5_BLK - lag):
            t = s + lag
            mbig_ref[s * LANES:(s + 1) * LANES, t * LANES:(t + 1) * LANES] = vb
            if lag > 0:
                mbig_ref[t * LANES:(t + 1) * LANES, s * LANES:(s + 1) * LANES] = zero_blk

    pr, pi = power(S5_BLK)
    a_ref[:, 0:S5_SW] = jnp.broadcast_to(pr, (SUBLANES, S5_SW))
    a_ref[:, S5_SW:2 * S5_SW] = jnp.broadcast_to(pi, (SUBLANES, S5_SW))


def _s5_body(u_ref, lamr_ref, lami_ref, ldt_ref, btr_ref, bti_ref, ctr_ref, cti_ref, d_ref, y_ref,
             pbig_ref, qbig_ref, mbig_ref, a_ref, ucat_ref, x_ref, xp_ref, st_ref, yi_ref):
    nblk = S5_CHUNK // S5_BLK
    batch = u_ref.shape[0]

    @pl.when(pl.program_id(1) == 0)
    def _setup():
        _s5_build_operators(lamr_ref, lami_ref, ldt_ref, btr_ref, bti_ref, ctr_ref, cti_ref, d_ref,
                            pbig_ref, qbig_ref, mbig_ref, a_ref)
        st_ref[...] = jnp.zeros_like(st_ref)

    for b in range(batch):
        for s in range(S5_BLK):
            piece = u_ref[b, pl.ds(s, nblk, stride=S5_BLK), :]
            ucat_ref[b * nblk:(b + 1) * nblk, s * LANES:(s + 1) * LANES] = piece.astype(BF16)
    nslab = 2 * S5_SW // LANES
    half = nslab // 2
    xloc = _dot(ucat_ref[...], pbig_ref[...])
    for c in range(nslab):
        for b in range(batch):
            x_ref[c, b * S5_XROWS:b * S5_XROWS + nblk, :] = (
                xloc[b * nblk:(b + 1) * nblk, c * LANES:(c + 1) * LANES])

    width = 2 * LANES
    for nb in range(S5_BLK // 2):
        kk = (2 * nb + 2) * LANES
        cols = slice(nb * width, (nb + 1) * width)
        yi_ref[:, cols] = _dot(ucat_ref[:, 0:kk], mbig_ref[0:kk, cols])

    lanes = lambda ref, c: ref[:, c * LANES:(c + 1) * LANES]
    ar = [lanes(a_ref, c) for c in range(half)]
    ai = [lanes(a_ref, half + c) for c in range(half)]
    xr = [lanes(st_ref, c) for c in range(half)]
    xi = [lanes(st_ref, half + c) for c in range(half)]
    for blk in range(nblk):
        r = pl.ds(blk, batch, stride=S5_XROWS)
        for c in range(half):
            xp_ref[c, r, :] = xr[c]
            xp_ref[half + c, r, :] = xi[c]
            nr = ar[c] * xr[c] - ai[c] * xi[c] + x_ref[c, r, :]
            ni = ar[c] * xi[c] + ai[c] * xr[c] + x_ref[half + c, r, :]
            xr[c], xi[c] = nr, ni
    for c in range(half):
        st_ref[:, c * LANES:(c + 1) * LANES] = xr[c]
        st_ref[:, (half + c) * LANES:(half + c + 1) * LANES] = xi[c]

    xpb = jnp.concatenate(
        [jnp.concatenate([xp_ref[c, b * S5_XROWS:b * S5_XROWS + nblk, :].astype(BF16)
                          for c in range(nslab)], axis=1) for b in range(batch)], axis=0)
    for nb in range(S5_BLK // 2):
        cols = slice(nb * width, (nb + 1) * width)
        y = yi_ref[:, cols] + _dot_nt(xpb, qbig_ref[cols, :])
        y = _gelu_tanh(y)
        for tt in range(2):
            t = 2 * nb + tt
            for b in range(batch):
                y_ref[b, pl.ds(t, nblk, stride=S5_BLK), :] = (
                    y[b * nblk:(b + 1) * nblk, tt * LANES:(tt + 1) * LANES])


def _s5(u3, lam_re, lam_im, log_dt, b_re, b_im, c_re, c_im, d_skip):
    batch, seq, width = u3.shape
    nlb = width // LANES
    assert batch == SUBLANES and seq % S5_CHUNK == 0
    lamr = lam_re.reshape(nlb, 1, S5_SW)
    lami = lam_im.reshape(nlb, 1, S5_SW)
    ldt = jnp.repeat(log_dt, S5_STATE).reshape(nlb, 1, S5_SW)
    bt = lambda b: b.reshape(nlb, S5_GPL, S5_STATE, S5_GROUP).transpose(0, 3, 1, 2).reshape(nlb, S5_GROUP, S5_SW)
    ct = lambda c: c.reshape(nlb, S5_GPL, S5_GROUP, S5_STATE).transpose(0, 2, 1, 3).reshape(nlb, S5_GROUP, S5_SW)
    par = lambda r, w: pl.BlockSpec((1, r, w), lambda i, j: (i, 0, 0))
    kdim = S5_BLK * LANES
    rows = (S5_CHUNK // S5_BLK) * batch
    io = pl.BlockSpec((batch, S5_CHUNK, LANES), lambda i, j: (0, j, i))
    return pl.pallas_call(
        _s5_body,
        grid=(nlb, seq // S5_CHUNK),
        in_specs=[io, par(1, S5_SW), par(1, S5_SW), par(1, S5_SW), par(S5_GROUP, S5_SW),
                  par(S5_GROUP, S5_SW), par(S5_GROUP, S5_SW), par(S5_GROUP, S5_SW), par(1, LANES)],
        out_specs=io,
        out_shape=jax.ShapeDtypeStruct((batch, seq, width), F32),
        scratch_shapes=[
            pltpu.VMEM((kdim, 2 * S5_SW), BF16),
            pltpu.VMEM((kdim, 2 * S5_SW), BF16),
            pltpu.VMEM((kdim, kdim), BF16),
            pltpu.VMEM((SUBLANES, 2 * S5_SW), F32),
            pltpu.VMEM((rows, kdim), BF16),
            pltpu.VMEM((2 * S5_SW // LANES, batch * S5_XROWS, LANES), F32),
            pltpu.VMEM((2 * S5_SW // LANES, batch * S5_XROWS, LANES), F32),
            pltpu.VMEM((SUBLANES, 2 * S5_SW), F32),
            pltpu.VMEM((rows, kdim), F32),
        ],
        compiler_params=_cparams(2),
        name="s5",
    )(u3, lamr, lami, ldt, bt(b_re), bt(b_im), ct(c_re), ct(c_im), d_skip.reshape(nlb, 1, LANES))


def _even_out_body(yb_ref, h_ref, o_ref, z_ref, x_ref, hg_ref, gluw_ref, glub_ref, wout_ref,
                   out_ref, y_ref):
    half = h_ref.shape[1]
    for h in range(M_HEADS):
        vs = slice(h * M_DV, (h + 1) * M_DV)
        og = (jax.nn.sigmoid(o_ref[:, vs]) * h_ref[:, vs]).astype(F32)
        y_ref[:, vs] = _rmsnorm(og, hg_ref[:, vs]).astype(BF16) * _silu(z_ref[:, vs])
    yg = yb_ref[...]
    s = _dot(yg.astype(BF16), gluw_ref[...]) + glub_ref[...]
    hb = yg * jax.nn.sigmoid(s)
    y_ref[:, half:2 * half] = hb.astype(BF16) * _silu(z_ref[:, half:2 * half])
    out_ref[...] = x_ref[...] + _dot(y_ref[...], wout_ref[...])


def _even_out(yb, hm, o, z, xf, head_g, glu_w, glu_b, w_out):
    t, d = xf.shape
    half = hm.shape[1]
    tok = lambda i: (i, 0)
    const = lambda i: (0, 0)
    return pl.pallas_call(
        _even_out_body,
        grid=(t // TOKEN_TILE,),
        in_specs=[
            pl.BlockSpec((TOKEN_TILE, half), tok),
            pl.BlockSpec((TOKEN_TILE, half), tok),
            pl.BlockSpec((TOKEN_TILE, half), tok),
            pl.BlockSpec((TOKEN_TILE, 2 * half), tok),
            pl.BlockSpec((TOKEN_TILE, d), tok),
            pl.BlockSpec((1, half), const),
            pl.BlockSpec((half, half), const),
            pl.BlockSpec((1, half), const),
            pl.BlockSpec((2 * half, d), const),
        ],
        out_specs=pl.BlockSpec((TOKEN_TILE, d), tok),
        out_shape=jax.ShapeDtypeStruct((t, d), F32),
        scratch_shapes=[pltpu.VMEM((TOKEN_TILE, 2 * half), BF16)],
        compiler_params=_cparams(1),
        name="even_out",
    )(yb, hm, o, z, xf, head_g.reshape(1, half), glu_w, glu_b.reshape(1, half), w_out)


def _proj_odd_body(x_ref, g_ref, w_ref, wg_ref, wa_ref, ba_ref, q_ref, k_ref, v_ref, z_ref, bc_ref):
    hb = _rmsnorm(x_ref[...], g_ref[...]).astype(BF16)
    _project(hb, w_ref, (q_ref, k_ref, v_ref, z_ref), 0)
    rb = _dot(hb, wg_ref[...]).astype(BF16)
    tril = _tri_ones(G_CHUNK, True)
    for j in range(0, bc_ref.shape[1], PROJ_TN):
        cols = slice(j, j + PROJ_TN)
        pre = _dot(rb, wa_ref[:, cols].astype(BF16)) + ba_ref[:, cols]
        la = (_log_sigmoid(pre) * (1.0 / G_TAU)).astype(BF16)
        for c in range(0, TOKEN_TILE, G_CHUNK):
            rows = slice(c, c + G_CHUNK)
            bc_ref[rows, cols] = _dot(tril, la[rows])


def _layer1_body(x_ref, g_ref, w_ref, wg_ref, wa_ref, ba_ref, hg_ref, wout_ref, gf_ref, out_ref,
                 q_s, k_s, v_s, z_s, bc_s, y_s, s_ref, *, tiles_per_seq):
    L = G_CHUNK

    @pl.when(lax.rem(pl.program_id(0), tiles_per_seq) == 0)
    def _init():
        s_ref[...] = jnp.zeros_like(s_ref)

    _proj_odd_body(x_ref, g_ref, w_ref, wg_ref, wa_ref, ba_ref, q_s, k_s, v_s, z_s, bc_s)

    row = lax.broadcasted_iota(jnp.int32, (L, L), 0)
    col = lax.broadcasted_iota(jnp.int32, (L, L), 1)
    causal = row >= col
    mid = L // 2 - 1

    for c, h in [(c, h) for c in range(0, TOKEN_TILE, 2 * L) for h in range(G_HEADS)]:
        ks = slice(h * G_DK, (h + 1) * G_DK)
        vs = slice(h * G_DV, (h + 1) * G_DV)
        r1, r2 = slice(c, c + L), slice(c + L, c + 2 * L)
        s_prev = s_ref[h]
        s_bf = s_prev.astype(BF16)
        v_p = v_s[c:c + 2 * L, vs]

        def prep(r):
            b = bc_s[r, ks]
            bm = b[mid:mid + 1, :]
            g = b[L - 1:L, :]
            e1 = jnp.exp(b - bm)
            qt = q_s[r, ks].astype(F32) * e1
            kt = k_s[r, ks].astype(F32) * (1.0 / e1)
            attn = jnp.where(causal, _dot_nt(qt.astype(BF16), kt.astype(BF16)), 0.0).astype(BF16)
            return attn, qt * jnp.exp(bm), kt * jnp.exp(g - bm), jnp.exp(g)

        a11, qi1, ke1, eg1 = prep(r1)
        a22, qi2, ke2, eg2 = prep(r2)
        qi2b = qi2.astype(BF16)
        a21 = _dot_nt(qi2b, ke1.astype(BF16)).astype(BF16)
        o1 = _dot(a11, v_p[0:L]) + _dot(qi1.astype(BF16), s_bf)
        o2 = _dot(jnp.concatenate([a21, a22], axis=1), v_p) + _dot((qi2 * eg1).astype(BF16), s_bf)
        y_s[r1, vs] = (_rmsnorm(o1, hg_ref[:, vs]) * _silu(z_s[r1, vs].astype(F32))).astype(BF16)
        y_s[r2, vs] = (_rmsnorm(o2, hg_ref[:, vs]) * _silu(z_s[r2, vs].astype(F32))).astype(BF16)
        ke_t = jnp.concatenate([(ke1 * eg2).T, ke2.T], axis=1).astype(BF16)
        g_col = jnp.broadcast_to(eg1 * eg2, (LANES, G_DK)).T[:, 0:1]
        s_ref[h] = g_col * s_prev + _dot(ke_t, v_p)

    x = x_ref[...] + _dot(y_s[...], wout_ref[...])
    out_ref[...] = _rmsnorm(x, gf_ref[...])


def _layer1(x1, g, wm, wg, w_alpha, b_alpha, head_g, w_out, gf, seq):
    t, d = x1.shape
    dk = G_HEADS * G_DK
    assert seq % TOKEN_TILE == 0 and TOKEN_TILE % G_CHUNK == 0
    tok = lambda i: (i, 0)
    const = lambda i: (0, 0)
    resident = lambda shape: pl.BlockSpec(shape, const, pipeline_mode=pl.Buffered(1))
    slots = lambda n, dt: pltpu.VMEM((TOKEN_TILE, n), dt)
    return pl.pallas_call(
        functools.partial(_layer1_body, tiles_per_seq=seq // TOKEN_TILE),
        grid=(t // TOKEN_TILE,),
        in_specs=[
            pl.BlockSpec((TOKEN_TILE, d), tok),
            pl.BlockSpec((1, d), const),
            resident((d, wm.shape[1])),
            pl.BlockSpec((d, G_RANK_PAD), const),
            pl.BlockSpec((G_RANK_PAD, dk), const),
            pl.BlockSpec((1, dk), const),
            pl.BlockSpec((1, D_MIX), const),
            resident((D_MIX, d)),
            pl.BlockSpec((1, d), const),
        ],
        out_specs=pl.BlockSpec((TOKEN_TILE, d), tok),
        out_shape=jax.ShapeDtypeStruct((t, d), F32),
        scratch_shapes=[
            slots(dk, BF16),
            slots(dk, BF16),
            slots(D_MIX, BF16),
            slots(D_MIX, BF16),
            slots(dk, F32),
            slots(D_MIX, BF16),
            pltpu.VMEM((G_HEADS, G_DK, G_DV), F32),
        ],
        compiler_params=_cparams(1),
        name="layer1",
    )(x1, g.reshape(1, d), wm, wg, w_alpha, b_alpha.reshape(1, dk), head_g.reshape(1, D_MIX), w_out,
      gf.reshape(1, d))


def kernel(x, norm_g, final_norm_g, ev_w_in, ev_conv_w, ev_conv_b, ev_i_bias, ev_f_bias, ev_head_g,
           s5_lam_re, s5_lam_im, s5_log_dt, s5_b_re, s5_b_im, s5_c_re, s5_c_im, s5_d, s5_glu_w,
           s5_glu_b, ev_w_out, od_w_in, gla_w_alpha, gla_b_alpha, gla_head_g, od_w_out):
    batch, seq, d = x.shape
    t = batch * seq
    xf = x.reshape(t, d)
    padc = lambda a: jnp.pad(a, ((0, 0), (0, LANES - a.shape[1])))

    wt = jnp.swapaxes(ev_w_in, 1, 2).reshape(ev_w_in.shape[2], d)
    g0 = 2 * M_HEADS * M_DK + 2 * M_HEADS * M_DV
    gi = g0 + M_HEADS
    gf = gi + M_HEADS
    half = D_MIX // 2
    w_qkvo, w_uz, wg = _cast_weights(wt, (0, gf), (g0, half + D_MIX), ((g0, M_HEADS), (gi, M_HEADS)))
    qk, v, o, u, z, gates = _proj_even(xf, norm_g[0], w_qkvo, w_uz, wg, ev_conv_w[0], ev_conv_b[0], seq)
    gbias = jnp.concatenate([padc(ev_i_bias), padc(ev_f_bias)], axis=1)
    hm = _mlstm(qk, v, gates, gbias, batch, seq)

    y3 = _s5(u.reshape(batch, seq, half), s5_lam_re[0], s5_lam_im[0], s5_log_dt[0], s5_b_re[0],
             s5_b_im[0], s5_c_re[0], s5_c_im[0], s5_d[0])
    yb = y3.reshape(t, half)
    x1 = _even_out(yb, hm, o, z, xf, ev_head_g[0], s5_glu_w[0].astype(BF16), s5_glu_b[0],
                   ev_w_out[0].astype(BF16))

    wt = jnp.swapaxes(od_w_in, 1, 2).reshape(od_w_in.shape[2], d)
    n_main = 2 * G_HEADS * G_DK + 2 * D_MIX
    wa = jnp.pad(gla_w_alpha[0], ((0, G_RANK_PAD - gla_w_alpha.shape[1]), (0, 0)))
    wm, wr = _cast_weights(wt, (0,), (n_main,), ((n_main, wt.shape[0] - n_main),),
                           scaled_cols=G_HEADS * G_DK, scale=G_DK ** -0.5)
    out = _layer1(x1, norm_g[1], wm, wr, wa, gla_b_alpha[0],
                  gla_head_g[0], od_w_out[0].astype(BF16), final_norm_g, seq)
    return out.reshape(batch, seq, d)
```

```python
import functools

import jax
import jax.numpy as jnp
from jax import lax
from jax.experimental import pallas as pl
from jax.experimental.pallas import tpu as pltpu

F32 = jnp.float32
BF16 = jnp.bfloat16

EPS = 1e-6
D_MODEL = 1024
D_MIX = 2 * D_MODEL
M_HEADS = 4
M_DK = 128
M_DV = 256
M_QK = 2 * M_HEADS * M_DK
CONV_WIDTH = 4
M_CHUNK = 256
M_SUB = 2
S5_GROUP = 16
S5_STATE = 64
S5_BLK = 8
S5_CHUNK = 1024
G_HEADS = 4
G_DK = 256
G_DV = 512
G_TAU = 16.0
G_CHUNK = 128
G_RANK_PAD = 128

LANES = 128
SUBLANES = 8
HALO = 16
S5_GPL = LANES // S5_GROUP
S5_SW = S5_GPL * S5_STATE
S5_XROWS = S5_CHUNK // S5_BLK + SUBLANES
TOKEN_TILE = 512
PROJ_TN = 256
VMEM_LIMIT = 56 * 1024 * 1024


def _cparams(n_grid):
    return pltpu.CompilerParams(
        dimension_semantics=("arbitrary",) * n_grid, vmem_limit_bytes=VMEM_LIMIT)


def _log_sigmoid(x):
    return jnp.minimum(x, 0.0) - jnp.log(1.0 + jnp.exp(-jnp.abs(x)))


def _silu(x):
    return x * jax.nn.sigmoid(x)


def _split_hi_lo(x):
    hi = x.astype(BF16)
    lo = (x - hi.astype(F32)).astype(BF16)
    return hi, lo


def _dot(a, b):
    return jnp.dot(a, b, preferred_element_type=F32)


def _dot_nt(a, b, precision=None):
    return lax.dot_general(a, b, (((1,), (1,)), ((), ())), precision=precision,
                           preferred_element_type=F32)


def _tri_ones(n, lower):
    row = lax.broadcasted_iota(jnp.int32, (n, n), 0)
    col = lax.broadcasted_iota(jnp.int32, (n, n), 1)
    keep = (row >= col) if lower else (row <= col)
    return jnp.where(keep, 1.0, 0.0).astype(BF16)


def _rmsnorm(x, g):
    return x * lax.rsqrt(jnp.mean(x * x, axis=-1, keepdims=True) + EPS) * g


def _project(hb, w_ref, out_refs, col0):
    off = col0
    for o_ref in out_refs:
        n = o_ref.shape[1]
        for j in range(0, n, PROJ_TN):
            o_ref[:, j:j + PROJ_TN] = _dot(hb, w_ref[:, off + j:off + j + PROJ_TN]).astype(o_ref.dtype)
        off += n


CAST_ROWS = 128


def _cast_body(wt_ref, *out_refs, starts, narrow, scaled_cols, scale):
    for o_ref, c0 in zip(out_refs[:-1], starts):
        n = o_ref.shape[1]
        val = wt_ref[c0:c0 + n, :]
        if scaled_cols and c0 == 0:
            col = lax.broadcasted_iota(jnp.int32, (n, 1), 0)
            val = val * jnp.where(col < scaled_cols, scale, 1.0)
        o_ref[...] = val.T.astype(BF16)
    lane = lax.broadcasted_iota(jnp.int32, (1, LANES), 1)
    for blk, (c0, n) in enumerate(narrow):
        start = (c0 // SUBLANES) * SUBLANES
        take = -(-(c0 - start + n) // SUBLANES) * SUBLANES
        rows = jnp.concatenate([wt_ref[start:start + take, :], jnp.zeros((LANES - take, CAST_ROWS), F32)], axis=0)
        slab = rows.T
        if c0 != start:
            slab = pltpu.roll(slab, LANES - (c0 - start), axis=1)
        out_refs[-1][:, blk * LANES:(blk + 1) * LANES] = jnp.where(lane < n, slab, 0.0).astype(BF16)


def _cast_weights(wt, starts, widths, narrow, scaled_cols=0, scale=1.0):
    cols, rows = wt.shape
    assert CAST_ROWS == LANES
    widths = tuple(widths) + (len(narrow) * LANES,)
    return pl.pallas_call(
        functools.partial(_cast_body, starts=starts, narrow=narrow, scaled_cols=scaled_cols, scale=scale),
        grid=(rows // CAST_ROWS,),
        in_specs=[pl.BlockSpec((cols, CAST_ROWS), lambda i: (0, i))],
        out_specs=[pl.BlockSpec((CAST_ROWS, n), lambda i: (i, 0)) for n in widths],
        out_shape=[jax.ShapeDtypeStruct((rows, n), BF16) for n in widths],
        compiler_params=_cparams(1),
        name="cast_weights",
    )(wt)


def _proj_even_body(x_ref, xh_ref, g_ref, w_ref, wb_ref, wg_ref, convw_ref, convb_ref,
                    qk_ref, v_ref, o_ref, u_ref, z_ref, gates_ref, ext_ref, *, tiles_per_seq):
    hb = _rmsnorm(x_ref[...], g_ref[...]).astype(BF16)
    hh = _rmsnorm(xh_ref[...], g_ref[...]).astype(BF16)
    seq_start = lax.rem(pl.program_id(0), tiles_per_seq) == 0
    base = HALO - (CONV_WIDTH - 1)
    lane = lax.broadcasted_iota(jnp.int32, (1, PROJ_TN), 1)
    plain = []
    for wr, col0, o_refs in ((w_ref, M_QK, (v_ref, o_ref)), (wb_ref, 0, (u_ref, z_ref))):
        for o_r in o_refs:
            plain += [(wr, o_r, col0 + c, c) for c in range(0, o_r.shape[1], PROJ_TN)]
            col0 += o_r.shape[1]
    for j in range(0, M_QK, PROJ_TN):
        cols = slice(j, j + PROJ_TN)
        wj = w_ref[:, cols]
        ext_ref[HALO:HALO + TOKEN_TILE, :] = _dot(hb, wj)
        ext_ref[0:HALO, :] = jnp.where(seq_start, 0.0, _dot(hh, wj))
        for k in range(len(plain) * j // M_QK, len(plain) * (j + PROJ_TN) // M_QK):
            wr, dst, c_in, c_out = plain[k]
            dst[:, c_out:c_out + PROJ_TN] = _dot(hb, wr[:, c_in:c_in + PROJ_TN]).astype(dst.dtype)
        acc = convb_ref[:, cols] + convw_ref[0:1, cols] * ext_ref[base:base + TOKEN_TILE, :]
        for i in range(1, CONV_WIDTH):
            acc = acc + convw_ref[i:i + 1, cols] * ext_ref[base + i:base + i + TOKEN_TILE, :]
        scale = jnp.where(lane + j < M_HEADS * M_DK, M_DK ** -0.5, 1.0)
        qk_ref[:, cols] = (_silu(acc) * scale).astype(BF16)
    gates_ref[...] = _dot(hb, wg_ref[...])


def _proj_even(xf, g, wm, wb, wg, conv_w, conv_b, seq):
    t, d = xf.shape
    half = D_MIX // 2
    assert seq % TOKEN_TILE == 0 and TOKEN_TILE % HALO == 0
    tok = lambda i: (i, 0)
    const = lambda i: (0, 0)
    per_halo = TOKEN_TILE // HALO
    widths = (M_QK, half, half, half, D_MIX)
    dtypes = (BF16, BF16, BF16, F32, BF16)
    out_shape = [jax.ShapeDtypeStruct((t, n), dt) for n, dt in zip(widths, dtypes)]
    ngate = wg.shape[1]
    out_shape.append(jax.ShapeDtypeStruct((t, ngate), F32))
    out_specs = [pl.BlockSpec((TOKEN_TILE, n), tok) for n in widths]
    out_specs.append(pl.BlockSpec((TOKEN_TILE, ngate), tok))
    return pl.pallas_call(
        functools.partial(_proj_even_body, tiles_per_seq=seq // TOKEN_TILE),
        grid=(t // TOKEN_TILE,),
        in_specs=[
            pl.BlockSpec((TOKEN_TILE, d), tok),
            pl.BlockSpec((HALO, d), lambda i: (jnp.maximum(i * per_halo - 1, 0), 0)),
            pl.BlockSpec((1, d), const),
            pl.BlockSpec((d, wm.shape[1]), const, pipeline_mode=pl.Buffered(1)),
            pl.BlockSpec((d, wb.shape[1]), const, pipeline_mode=pl.Buffered(1)),
            pl.BlockSpec((d, ngate), const),
            pl.BlockSpec((CONV_WIDTH, M_QK), const),
            pl.BlockSpec((1, M_QK), const),
        ],
        out_specs=out_specs,
        out_shape=out_shape,
        scratch_shapes=[pltpu.VMEM((HALO + TOKEN_TILE, PROJ_TN), F32)],
        compiler_params=_cparams(1),
        name="proj_even",
    )(xf, xf, g.reshape(1, d), wm, wb, wg, conv_w, conv_b.reshape(1, M_QK))


def _mlstm_body(qk_ref, v_ref, gates_ref, gbias_ref, out_ref, c_ref, n_ref, m_ref):
    @pl.when(pl.program_id(1) == 0)
    def _init():
        c_ref[...] = jnp.zeros_like(c_ref)
        n_ref[...] = jnp.zeros_like(n_ref)
        m_ref[...] = jnp.zeros_like(m_ref)

    for sub in range(M_SUB):
        _mlstm_chunk(slice(sub * M_CHUNK, (sub + 1) * M_CHUNK), qk_ref, v_ref, gates_ref, gbias_ref,
                     out_ref, c_ref, n_ref, m_ref)


def _mlstm_chunk(r, qk_ref, v_ref, gates_ref, gbias_ref, out_ref, c_ref, n_ref, m_ref):
    L = M_CHUNK
    gt = gates_ref[r, :] + gbias_ref[...]
    ipre = gt[:, 0:LANES]
    logf = _log_sigmoid(gt[:, LANES:2 * LANES]).astype(BF16)
    b = _dot(_tri_ones(L, True), logf)
    w = ipre - b
    rows = lax.broadcasted_iota(jnp.int32, (L, LANES), 0)
    cm = w
    k = 1
    while k < L:
        cm = jnp.maximum(cm, jnp.where(rows >= k, pltpu.roll(cm, k, axis=0), -jnp.inf))
        k *= 2

    def replicate(x):
        return jnp.concatenate([jnp.broadcast_to(x[:, h:h + 1], (L, LANES)) for h in range(M_HEADS)], axis=1)

    b_rep = replicate(b)
    w_rep = replicate(w)
    cm_rep = replicate(cm)
    pick_r = lax.broadcasted_iota(jnp.int32, (M_HEADS * SUBLANES, LANES), 0)
    pick_c = lax.broadcasted_iota(jnp.int32, (M_HEADS * SUBLANES, LANES), 1)
    pick = jnp.where(lax.shift_right_logical(pick_r, 3) == pick_c, 1.0, 0.0).astype(BF16)
    w_hi, w_lo = _split_hi_lo(w)
    w_row = _dot_nt(pick, w_hi) + _dot_nt(pick, w_lo)

    trow = lax.broadcasted_iota(jnp.int32, (L, LANES), 0)
    tcol = lax.broadcasted_iota(jnp.int32, (L, LANES), 1)
    ones = jnp.ones((L, LANES), BF16)

    for h in range(M_HEADS):
        ks = slice(h * M_DK, (h + 1) * M_DK)
        ks2 = slice(M_HEADS * M_DK + h * M_DK, M_HEADS * M_DK + (h + 1) * M_DK)
        vs = slice(h * M_DV, (h + 1) * M_DV)
        hs = slice(h * LANES, (h + 1) * LANES)
        m_prev = m_ref[h, 0:1, :]
        c_prev = c_ref[h]
        n_prev = n_ref[h]
        big_m = jnp.maximum(m_prev, cm_rep[:, hs])
        w_inter = jnp.exp(m_prev - big_m)
        wr = w_row[h * SUBLANES:h * SUBLANES + 1, :]

        qb = qk_ref[r, ks]
        kb = qk_ref[r, ks2]
        v_h = v_ref[r, vs]
        s = _dot_nt(qb, kb)
        sc = jnp.concatenate(
            [jnp.where(trow >= tcol + j, jnp.exp(wr[:, j:j + LANES] - big_m), 0.0) * s[:, j:j + LANES]
             for j in range(0, L, LANES)], axis=1).astype(BF16)
        q_c = _dot(qb, c_prev.astype(BF16))
        den = _dot(sc, ones) + w_inter * _dot(qb, n_prev.astype(BF16))
        inv = 1.0 / jnp.maximum(jnp.abs(den), jnp.exp(-(b_rep[:, hs] + big_m)))
        num = _dot(sc, v_h)
        out_ref[r, vs] = jnp.concatenate(
            [(num[:, j:j + LANES] + w_inter * q_c[:, j:j + LANES]) * inv for j in range(0, M_DV, LANES)],
            axis=1).astype(BF16)

        g = b_rep[L - 1:L, hs]
        cm_last = cm_rep[L - 1:L, hs]
        m_last = big_m[L - 1:L, :]
        kw_t = (kb.astype(F32) * jnp.exp(w_rep[:, hs] - cm_last)).T.astype(BF16)
        s_prev = jnp.exp(m_prev - m_last)
        s_loc = jnp.exp(cm_last - m_last)
        c_ref[h] = (jnp.concatenate([s_prev] * (M_DV // LANES), axis=1) * c_prev
                    + jnp.concatenate([s_loc] * (M_DV // LANES), axis=1) * _dot(kw_t, v_h))
        n_ref[h] = s_prev * n_prev + s_loc * _dot(kw_t, ones)
        m_ref[h] = jnp.broadcast_to(g + m_last, (SUBLANES, LANES))


def _mlstm(qk, v, gates, gbias, batch, seq):
    t = batch * seq
    L = M_CHUNK * M_SUB
    nc = seq // L
    dv = M_HEADS * M_DV
    tok = lambda b, c: (b * nc + c, 0)
    const = lambda b, c: (0, 0)
    return pl.pallas_call(
        _mlstm_body,
        grid=(batch, nc),
        in_specs=[
            pl.BlockSpec((L, M_QK), tok),
            pl.BlockSpec((L, dv), tok),
            pl.BlockSpec((L, 2 * LANES), tok),
            pl.BlockSpec((1, 2 * LANES), const),
        ],
        out_specs=pl.BlockSpec((L, dv), tok),
        out_shape=jax.ShapeDtypeStruct((t, dv), BF16),
        scratch_shapes=[
            pltpu.VMEM((M_HEADS, M_DK, M_DV), F32),
            pltpu.VMEM((M_HEADS, M_DK, LANES), F32),
            pltpu.VMEM((M_HEADS, SUBLANES, LANES), F32),
        ],
        compiler_params=_cparams(2),
        name="mlstm",
    )(qk, v, gates, gbias)


def _gelu_tanh(x):
    c = 0.7978845608028654
    half = 0.5 * x
    return half + half * jnp.tanh(x * (c + (c * 0.044715) * (x * x)))


def _s5_build_operators(lamr_ref, lami_ref, ldt_ref, btr_ref, bti_ref, ctr_ref, cti_ref, d_ref,
                        pbig_ref, qbig_ref, mbig_ref, a_ref):
    lr = lamr_ref[0]
    li = lami_ref[0]
    dt = jnp.exp(ldt_ref[0])
    zr = lr * dt
    th = li * dt
    er = jnp.exp(zr)
    ar = er * jnp.cos(th)
    ai = er * jnp.sin(th)
    den = lr * lr + li * li
    beta_r = ((ar - 1.0) * lr + ai * li) / den
    beta_i = (ai * lr - (ar - 1.0) * li) / den
    btr = btr_ref[0]
    bti = bti_ref[0]
    bbr = btr * beta_r - bti * beta_i
    bbi = btr * beta_i + bti * beta_r
    ctr = ctr_ref[0]
    cti = cti_ref[0]

    row_g = lax.shift_right_logical(lax.broadcasted_iota(jnp.int32, (LANES, S5_SW), 0), 4)
    lane_g = lax.shift_right_logical(lax.broadcasted_iota(jnp.int32, (LANES, S5_SW), 1), 6)
    same_group = row_g == lane_g

    def expand(x16):
        return jnp.where(same_group, jnp.concatenate([x16] * S5_GPL, axis=0), 0.0)

    def power(k):
        e = jnp.exp(float(k) * zr)
        return e * jnp.cos(float(k) * th), e * jnp.sin(float(k) * th)

    for s in range(S5_BLK):
        rows = slice(s * LANES, (s + 1) * LANES)
        pr, pi = power(S5_BLK - 1 - s)
        pbig_ref[rows, 0:S5_SW] = expand(pr * bbr - pi * bbi).astype(BF16)
        pbig_ref[rows, S5_SW:2 * S5_SW] = expand(pr * bbi + pi * bbr).astype(BF16)
        pr, pi = power(s + 1)
        qbig_ref[rows, 0:S5_SW] = expand(ctr * pr - cti * pi).astype(BF16)
        qbig_ref[rows, S5_SW:2 * S5_SW] = expand(-(ctr * pi + cti * pr)).astype(BF16)

    cb = jnp.concatenate([expand(ctr), expand(-cti)], axis=1)
    r128 = lax.broadcasted_iota(jnp.int32, (LANES, LANES), 0)
    c128 = lax.broadcasted_iota(jnp.int32, (LANES, LANES), 1)
    zero_blk = jnp.zeros((LANES, LANES), BF16)
    ab_lags = []
    for lag in range(S5_BLK):
        pr, pi = power(lag)
        ab_lags.append(jnp.concatenate([expand(pr * bbr - pi * bbi), expand(pr * bbi + pi * bbr)], axis=1))
    ab_hi, ab_lo = _split_hi_lo(jnp.concatenate(ab_lags, axis=0))
    cb_hi, cb_lo = _split_hi_lo(cb)
    v_lags = _dot_nt(ab_hi, cb_hi) + _dot_nt(ab_hi, cb_lo) + _dot_nt(ab_lo, cb_hi)
    for lag in range(S5_BLK):
        v = v_lags[lag * LANES:(lag + 1) * LANES, :]
        if lag == 0:
            v = v + jnp.where(r128 == c128, d_ref[0], 0.0)
        vb = v.astype(BF16)
        for s in range(S5_BLK - lag):
            t = s + lag
            mbig_ref[s * LANES:(s + 1) * LANES, t * LANES:(t + 1) * LANES] = vb
            if lag > 0:
                mbig_ref[t * LANES:(t + 1) * LANES, s * LANES:(s + 1) * LANES] = zero_blk

    pr, pi = power(S5_BLK)
    a_ref[:, 0:S5_SW] = jnp.broadcast_to(pr, (SUBLANES, S5_SW))
    a_ref[:, S5_SW:2 * S5_SW] = jnp.broadcast_to(pi, (SUBLANES, S5_SW))


def _s5_body(u_ref, lamr_ref, lami_ref, ldt_ref, btr_ref, bti_ref, ctr_ref, cti_ref, d_ref, y_ref,
             pbig_ref, qbig_ref, mbig_ref, a_ref, ucat_ref, x_ref, xp_ref, st_ref):
    nblk = S5_CHUNK // S5_BLK
    batch = u_ref.shape[0]

    @pl.when(pl.program_id(1) == 0)
    def _setup():
        _s5_build_operators(lamr_ref, lami_ref, ldt_ref, btr_ref, bti_ref, ctr_ref, cti_ref, d_ref,
                            pbig_ref, qbig_ref, mbig_ref, a_ref)
        st_ref[...] = jnp.zeros_like(st_ref)

    for b in range(batch):
        for s in range(S5_BLK):
            piece = u_ref[b, pl.ds(s, nblk, stride=S5_BLK), :]
            ucat_ref[b * nblk:(b + 1) * nblk, s * LANES:(s + 1) * LANES] = piece.astype(BF16)
    nslab = 2 * S5_SW // LANES
    half = nslab // 2
    xloc = _dot(ucat_ref[...], pbig_ref[...])
    for c in range(nslab):
        for b in range(batch):
            x_ref[c, b * S5_XROWS:b * S5_XROWS + nblk, :] = (
                xloc[b * nblk:(b + 1) * nblk, c * LANES:(c + 1) * LANES])

    lanes = lambda ref, c: ref[:, c * LANES:(c + 1) * LANES]
    ar = [lanes(a_ref, c) for c in range(half)]
    ai = [lanes(a_ref, half + c) for c in range(half)]
    xr = [lanes(st_ref, c) for c in range(half)]
    xi = [lanes(st_ref, half + c) for c in range(half)]
    for blk in range(nblk):
        r = pl.ds(blk, batch, stride=S5_XROWS)
        for c in range(half):
            xp_ref[c, r, :] = xr[c]
            xp_ref[half + c, r, :] = xi[c]
            nr = ar[c] * xr[c] - ai[c] * xi[c] + x_ref[c, r, :]
            ni = ar[c] * xi[c] + ai[c] * xr[c] + x_ref[half + c, r, :]
            xr[c], xi[c] = nr, ni
    for c in range(half):
        st_ref[:, c * LANES:(c + 1) * LANES] = xr[c]
        st_ref[:, (half + c) * LANES:(half + c + 1) * LANES] = xi[c]

    xpb = jnp.concatenate(
        [jnp.concatenate([xp_ref[c, b * S5_XROWS:b * S5_XROWS + nblk, :].astype(BF16)
                          for c in range(nslab)], axis=1) for b in range(batch)], axis=0)
    width = 2 * LANES
    for nb in range(S5_BLK // 2):
        kk = (2 * nb + 2) * LANES
        cols = slice(nb * width, (nb + 1) * width)
        y = _dot(ucat_ref[:, 0:kk], mbig_ref[0:kk, cols]) + _dot_nt(xpb, qbig_ref[cols, :])
        y = _gelu_tanh(y)
        for tt in range(2):
            t = 2 * nb + tt
            for b in range(batch):
                y_ref[b, pl.ds(t, nblk, stride=S5_BLK), :] = (
                    y[b * nblk:(b + 1) * nblk, tt * LANES:(tt + 1) * LANES])


def _s5(u3, lam_re, lam_im, log_dt, b_re, b_im, c_re, c_im, d_skip):
    batch, seq, width = u3.shape
    nlb = width // LANES
    assert batch == SUBLANES and seq % S5_CHUNK == 0
    lamr = lam_re.reshape(nlb, 1, S5_SW)
    lami = lam_im.reshape(nlb, 1, S5_SW)
    ldt = jnp.repeat(log_dt, S5_STATE).reshape(nlb, 1, S5_SW)
    bt = lambda b: b.reshape(nlb, S5_GPL, S5_STATE, S5_GROUP).transpose(0, 3, 1, 2).reshape(nlb, S5_GROUP, S5_SW)
    ct = lambda c: c.reshape(nlb, S5_GPL, S5_GROUP, S5_STATE).transpose(0, 2, 1, 3).reshape(nlb, S5_GROUP, S5_SW)
    par = lambda r, w: pl.BlockSpec((1, r, w), lambda i, j: (i, 0, 0))
    kdim = S5_BLK * LANES
    rows = (S5_CHUNK // S5_BLK) * batch
    io = pl.BlockSpec((batch, S5_CHUNK, LANES), lambda i, j: (0, j, i))
    return pl.pallas_call(
        _s5_body,
        grid=(nlb, seq // S5_CHUNK),
        in_specs=[io, par(1, S5_SW), par(1, S5_SW), par(1, S5_SW), par(S5_GROUP, S5_SW),
                  par(S5_GROUP, S5_SW), par(S5_GROUP, S5_SW), par(S5_GROUP, S5_SW), par(1, LANES)],
        out_specs=io,
        out_shape=jax.ShapeDtypeStruct((batch, seq, width), F32),
        scratch_shapes=[
            pltpu.VMEM((kdim, 2 * S5_SW), BF16),
            pltpu.VMEM((kdim, 2 * S5_SW), BF16),
            pltpu.VMEM((kdim, kdim), BF16),
            pltpu.VMEM((SUBLANES, 2 * S5_SW), F32),
            pltpu.VMEM((rows, kdim), BF16),
            pltpu.VMEM((2 * S5_SW // LANES, batch * S5_XROWS, LANES), F32),
            pltpu.VMEM((2 * S5_SW // LANES, batch * S5_XROWS, LANES), F32),
            pltpu.VMEM((SUBLANES, 2 * S5_SW), F32),
        ],
        compiler_params=_cparams(2),
        name="s5",
    )(u3, lamr, lami, ldt, bt(b_re), bt(b_im), ct(c_re), ct(c_im), d_skip.reshape(nlb, 1, LANES))


def _even_out_body(yb_ref, h_ref, o_ref, z_ref, x_ref, hg_ref, gluw_ref, glub_ref, wout_ref,
                   out_ref):
    half = h_ref.shape[1]
    yg = yb_ref[...]
    s = _dot(yg.astype(BF16), gluw_ref[...]) + glub_ref[...]
    acc = x_ref[...]
    for h in range(M_HEADS):
        vs = slice(h * M_DV, (h + 1) * M_DV)
        og = (jax.nn.sigmoid(o_ref[:, vs]) * h_ref[:, vs]).astype(F32)
        y_h = _rmsnorm(og, hg_ref[:, vs]).astype(BF16) * _silu(z_ref[:, vs])
        acc = acc + _dot(y_h, wout_ref[vs, :])
    for j in range(0, half, M_DV):
        cs = slice(j, j + M_DV)
        zs = slice(half + j, half + j + M_DV)
        y_j = (yg[:, cs] * jax.nn.sigmoid(s[:, cs])).astype(BF16) * _silu(z_ref[:, zs])
        acc = acc + _dot(y_j, wout_ref[zs, :])
    out_ref[...] = acc


def _even_out(yb, hm, o, z, xf, head_g, glu_w, glu_b, w_out):
    t, d = xf.shape
    half = hm.shape[1]
    tok = lambda i: (i, 0)
    const = lambda i: (0, 0)
    return pl.pallas_call(
        _even_out_body,
        grid=(t // TOKEN_TILE,),
        in_specs=[
            pl.BlockSpec((TOKEN_TILE, half), tok),
            pl.BlockSpec((TOKEN_TILE, half), tok),
            pl.BlockSpec((TOKEN_TILE, half), tok),
            pl.BlockSpec((TOKEN_TILE, 2 * half), tok),
            pl.BlockSpec((TOKEN_TILE, d), tok),
            pl.BlockSpec((1, half), const),
            pl.BlockSpec((half, half), const),
            pl.BlockSpec((1, half), const),
            pl.BlockSpec((2 * half, d), const),
        ],
        out_specs=pl.BlockSpec((TOKEN_TILE, d), tok),
        out_shape=jax.ShapeDtypeStruct((t, d), F32),
        compiler_params=_cparams(1),
        name="even_out",
    )(yb, hm, o, z, xf, head_g.reshape(1, half), glu_w, glu_b.reshape(1, half), w_out)


def _proj_odd_body(x_ref, g_ref, w_ref, wg_ref, wa_ref, ba_ref, q_ref, k_ref, v_ref, z_ref, bc_ref):
    hb = _rmsnorm(x_ref[...], g_ref[...]).astype(BF16)
    _project(hb, w_ref, (q_ref, k_ref, v_ref, z_ref), 0)
    rb = _dot(hb, wg_ref[...]).astype(BF16)
    tril = _tri_ones(G_CHUNK, True)
    for j in range(0, bc_ref.shape[1], PROJ_TN):
        cols = slice(j, j + PROJ_TN)
        pre = _dot(rb, wa_ref[:, cols].astype(BF16)) + ba_ref[:, cols]
        la = (_log_sigmoid(pre) * (1.0 / G_TAU)).astype(BF16)
        for c in range(0, TOKEN_TILE, G_CHUNK):
            rows = slice(c, c + G_CHUNK)
            bc_ref[rows, cols] = _dot(tril, la[rows])


def _layer1_body(x_ref, g_ref, w_ref, wg_ref, wa_ref, ba_ref, hg_ref, wout_ref, gf_ref, out_ref,
                 q_s, k_s, v_s, z_s, bc_s, y_s, s_ref, *, tiles_per_seq):
    L = G_CHUNK

    @pl.when(lax.rem(pl.program_id(0), tiles_per_seq) == 0)
    def _init():
        s_ref[...] = jnp.zeros_like(s_ref)

    _proj_odd_body(x_ref, g_ref, w_ref, wg_ref, wa_ref, ba_ref, q_s, k_s, v_s, z_s, bc_s)

    row = lax.broadcasted_iota(jnp.int32, (L, L), 0)
    col = lax.broadcasted_iota(jnp.int32, (L, L), 1)
    causal = row >= col
    mid = L // 2 - 1

    for c, h in [(c, h) for c in range(0, TOKEN_TILE, 2 * L) for h in range(G_HEADS)]:
        ks = slice(h * G_DK, (h + 1) * G_DK)
        vs = slice(h * G_DV, (h + 1) * G_DV)
        r1, r2 = slice(c, c + L), slice(c + L, c + 2 * L)
        s_prev = s_ref[h]
        s_bf = s_prev.astype(BF16)
        v_p = v_s[c:c + 2 * L, vs]

        def prep(r):
            b = bc_s[r, ks]
            bm = b[mid:mid + 1, :]
            g = b[L - 1:L, :]
            e1 = jnp.exp(b - bm)
            qt = q_s[r, ks].astype(F32) * e1
            kt = k_s[r, ks].astype(F32) * (1.0 / e1)
            attn = jnp.where(causal, _dot_nt(qt.astype(BF16), kt.astype(BF16)), 0.0).astype(BF16)
            return attn, qt * jnp.exp(bm), kt * jnp.exp(g - bm), jnp.exp(g)

        a11, qi1, ke1, eg1 = prep(r1)
        a22, qi2, ke2, eg2 = prep(r2)
        qi2b = qi2.astype(BF16)
        a21 = _dot_nt(qi2b, ke1.astype(BF16)).astype(BF16)
        o1 = _dot(a11, v_p[0:L]) + _dot(qi1.astype(BF16), s_bf)
        o2 = _dot(jnp.concatenate([a21, a22], axis=1), v_p) + _dot((qi2 * eg1).astype(BF16), s_bf)
        y_s[r1, vs] = (_rmsnorm(o1, hg_ref[:, vs]) * _silu(z_s[r1, vs].astype(F32))).astype(BF16)
        y_s[r2, vs] = (_rmsnorm(o2, hg_ref[:, vs]) * _silu(z_s[r2, vs].astype(F32))).astype(BF16)
        ke_t = jnp.concatenate([(ke1 * eg2).T, ke2.T], axis=1).astype(BF16)
        g_col = jnp.broadcast_to(eg1 * eg2, (LANES, G_DK)).T[:, 0:1]
        s_ref[h] = g_col * s_prev + _dot(ke_t, v_p)

    x = x_ref[...] + _dot(y_s[...], wout_ref[...])
    out_ref[...] = _rmsnorm(x, gf_ref[...])


def _layer1(x1, g, wm, wg, w_alpha, b_alpha, head_g, w_out, gf, seq):
    t, d = x1.shape
    dk = G_HEADS * G_DK
    assert seq % TOKEN_TILE == 0 and TOKEN_TILE % G_CHUNK == 0
    tok = lambda i: (i, 0)
    const = lambda i: (0, 0)
    resident = lambda shape: pl.BlockSpec(shape, const, pipeline_mode=pl.Buffered(1))
    slots = lambda n, dt: pltpu.VMEM((TOKEN_TILE, n), dt)
    return pl.pallas_call(
        functools.partial(_layer1_body, tiles_per_seq=seq // TOKEN_TILE),
        grid=(t // TOKEN_TILE,),
        in_specs=[
            pl.BlockSpec((TOKEN_TILE, d), tok),
            pl.BlockSpec((1, d), const),
            resident((d, wm.shape[1])),
            pl.BlockSpec((d, G_RANK_PAD), const),
            pl.BlockSpec((G_RANK_PAD, dk), const),
            pl.BlockSpec((1, dk), const),
            pl.BlockSpec((1, D_MIX), const),
            resident((D_MIX, d)),
            pl.BlockSpec((1, d), const),
        ],
        out_specs=pl.BlockSpec((TOKEN_TILE, d), tok),
        out_shape=jax.ShapeDtypeStruct((t, d), F32),
        scratch_shapes=[
            slots(dk, BF16),
            slots(dk, BF16),
            slots(D_MIX, BF16),
            slots(D_MIX, BF16),
            slots(dk, F32),
            slots(D_MIX, BF16),
            pltpu.VMEM((G_HEADS, G_DK, G_DV), F32),
        ],
        compiler_params=_cparams(1),
        name="layer1",
    )(x1, g.reshape(1, d), wm, wg, w_alpha, b_alpha.reshape(1, dk), head_g.reshape(1, D_MIX), w_out,
      gf.reshape(1, d))


def kernel(x, norm_g, final_norm_g, ev_w_in, ev_conv_w, ev_conv_b, ev_i_bias, ev_f_bias, ev_head_g,
           s5_lam_re, s5_lam_im, s5_log_dt, s5_b_re, s5_b_im, s5_c_re, s5_c_im, s5_d, s5_glu_w,
           s5_glu_b, ev_w_out, od_w_in, gla_w_alpha, gla_b_alpha, gla_head_g, od_w_out):
    batch, seq, d = x.shape
    t = batch * seq
    xf = x.reshape(t, d)
    padc = lambda a: jnp.pad(a, ((0, 0), (0, LANES - a.shape[1])))

    wt = jnp.swapaxes(ev_w_in, 1, 2).reshape(ev_w_in.shape[2], d)
    g0 = 2 * M_HEADS * M_DK + 2 * M_HEADS * M_DV
    gi = g0 + M_HEADS
    gf = gi + M_HEADS
    half = D_MIX // 2
    w_qkvo, w_uz, wg = _cast_weights(wt, (0, gf), (g0, half + D_MIX), ((g0, M_HEADS), (gi, M_HEADS)))
    qk, v, o, u, z, gates = _proj_even(xf, norm_g[0], w_qkvo, w_uz, wg, ev_conv_w[0], ev_conv_b[0], seq)
    gbias = jnp.concatenate([padc(ev_i_bias), padc(ev_f_bias)], axis=1)
    hm = _mlstm(qk, v, gates, gbias, batch, seq)

    y3 = _s5(u.reshape(batch, seq, half), s5_lam_re[0], s5_lam_im[0], s5_log_dt[0], s5_b_re[0],
             s5_b_im[0], s5_c_re[0], s5_c_im[0], s5_d[0])
    yb = y3.reshape(t, half)
    x1 = _even_out(yb, hm, o, z, xf, ev_head_g[0], s5_glu_w[0].astype(BF16), s5_glu_b[0],
                   ev_w_out[0].astype(BF16))

    wt = jnp.swapaxes(od_w_in, 1, 2).reshape(od_w_in.shape[2], d)
    n_main = 2 * G_HEADS * G_DK + 2 * D_MIX
    wa = jnp.pad(gla_w_alpha[0], ((0, G_RANK_PAD - gla_w_alpha.shape[1]), (0, 0)))
    wm, wr = _cast_weights(wt, (0,), (n_main,), ((n_main, wt.shape[0] - n_main),),
                           scaled_cols=G_HEADS * G_DK, scale=G_DK ** -0.5)
    out = _layer1(x1, norm_g[1], wm, wr, wa, gla_b_alpha[0],
                  gla_head_g[0], od_w_out[0].astype(BF16), final_norm_g, seq)
    return out.reshape(batch, seq, d)
```

```python
import functools

import jax
import jax.numpy as jnp
from jax import lax
from jax.experimental import pallas as pl
from jax.experimental.pallas import tpu as pltpu

F32 = jnp.float32
BF16 = jnp.bfloat16

EPS = 1e-6
D_MODEL = 1024
D_MIX = 2 * D_MODEL
M_HEADS = 4
M_DK = 128
M_DV = 256
M_QK = 2 * M_HEADS * M_DK
CONV_WIDTH = 4
M_CHUNK = 256
M_SUB = 4
S5_GROUP = 16
S5_STATE = 64
S5_BLK = 8
S5_CHUNK = 1024
G_HEADS = 4
G_DK = 256
G_DV = 512
G_TAU = 16.0
G_CHUNK = 128
G_RANK_PAD = 128

LANES = 128
SUBLANES = 8
HALO = 16
S5_GPL = LANES // S5_GROUP
S5_SW = S5_GPL * S5_STATE
S5_XROWS = S5_CHUNK // S5_BLK + SUBLANES
TOKEN_TILE = 512
PROJ_TN = 256
VMEM_LIMIT = 56 * 1024 * 1024


def _cparams(n_grid):
    return pltpu.CompilerParams(
        dimension_semantics=("arbitrary",) * n_grid, vmem_limit_bytes=VMEM_LIMIT)


def _log_sigmoid(x):
    return jnp.minimum(x, 0.0) - jnp.log(1.0 + jnp.exp(-jnp.abs(x)))


def _silu(x):
    return x * jax.nn.sigmoid(x)


def _split_hi_lo(x):
    hi = x.astype(BF16)
    lo = (x - hi.astype(F32)).astype(BF16)
    return hi, lo


def _dot(a, b):
    return jnp.dot(a, b, preferred_element_type=F32)


def _dot_nt(a, b, precision=None):
    return lax.dot_general(a, b, (((1,), (1,)), ((), ())), precision=precision,
                           preferred_element_type=F32)


def _tri_ones(n, lower):
    row = lax.broadcasted_iota(jnp.int32, (n, n), 0)
    col = lax.broadcasted_iota(jnp.int32, (n, n), 1)
    keep = (row >= col) if lower else (row <= col)
    return jnp.where(keep, 1.0, 0.0).astype(BF16)


def _rmsnorm(x, g):
    return x * lax.rsqrt(jnp.mean(x * x, axis=-1, keepdims=True) + EPS) * g


def _project(hb, w_ref, out_refs, col0):
    off = col0
    for o_ref in out_refs:
        n = o_ref.shape[1]
        for j in range(0, n, PROJ_TN):
            o_ref[:, j:j + PROJ_TN] = _dot(hb, w_ref[:, off + j:off + j + PROJ_TN]).astype(o_ref.dtype)
        off += n


CAST_ROWS = 128


def _cast_body(wt_ref, *out_refs, starts, narrow, scaled_cols, scale):
    for o_ref, c0 in zip(out_refs[:-1], starts):
        n = o_ref.shape[1]
        val = wt_ref[c0:c0 + n, :]
        if scaled_cols and c0 == 0:
            col = lax.broadcasted_iota(jnp.int32, (n, 1), 0)
            val = val * jnp.where(col < scaled_cols, scale, 1.0)
        o_ref[...] = val.T.astype(BF16)
    lane = lax.broadcasted_iota(jnp.int32, (1, LANES), 1)
    for blk, (c0, n) in enumerate(narrow):
        start = (c0 // SUBLANES) * SUBLANES
        take = -(-(c0 - start + n) // SUBLANES) * SUBLANES
        rows = jnp.concatenate([wt_ref[start:start + take, :], jnp.zeros((LANES - take, CAST_ROWS), F32)], axis=0)
        slab = rows.T
        if c0 != start:
            slab = pltpu.roll(slab, LANES - (c0 - start), axis=1)
        out_refs[-1][:, blk * LANES:(blk + 1) * LANES] = jnp.where(lane < n, slab, 0.0).astype(BF16)


def _cast_weights(wt, starts, widths, narrow, scaled_cols=0, scale=1.0):
    cols, rows = wt.shape
    assert CAST_ROWS == LANES
    widths = tuple(widths) + (len(narrow) * LANES,)
    return pl.pallas_call(
        functools.partial(_cast_body, starts=starts, narrow=narrow, scaled_cols=scaled_cols, scale=scale),
        grid=(rows // CAST_ROWS,),
        in_specs=[pl.BlockSpec((cols, CAST_ROWS), lambda i: (0, i))],
        out_specs=[pl.BlockSpec((CAST_ROWS, n), lambda i: (i, 0)) for n in widths],
        out_shape=[jax.ShapeDtypeStruct((rows, n), BF16) for n in widths],
        compiler_params=_cparams(1),
        name="cast_weights",
    )(wt)


def _proj_even_body(x_ref, xh_ref, g_ref, w_ref, wb_ref, wg_ref, convw_ref, convb_ref,
                    qk_ref, v_ref, o_ref, u_ref, z_ref, gates_ref, ext_ref, *, tiles_per_seq):
    hb = _rmsnorm(x_ref[...], g_ref[...]).astype(BF16)
    hh = _rmsnorm(xh_ref[...], g_ref[...]).astype(BF16)
    seq_start = lax.rem(pl.program_id(0), tiles_per_seq) == 0
    base = HALO - (CONV_WIDTH - 1)
    lane = lax.broadcasted_iota(jnp.int32, (1, PROJ_TN), 1)
    plain = []
    for wr, col0, o_refs in ((w_ref, M_QK, (v_ref, o_ref)), (wb_ref, 0, (u_ref, z_ref))):
        for o_r in o_refs:
            plain += [(wr, o_r, col0 + c, c) for c in range(0, o_r.shape[1], PROJ_TN)]
            col0 += o_r.shape[1]
    for j in range(0, M_QK, PROJ_TN):
        cols = slice(j, j + PROJ_TN)
        wj = w_ref[:, cols]
        ext_ref[HALO:HALO + TOKEN_TILE, :] = _dot(hb, wj)
        ext_ref[0:HALO, :] = jnp.where(seq_start, 0.0, _dot(hh, wj))
        for k in range(len(plain) * j // M_QK, len(plain) * (j + PROJ_TN) // M_QK):
            wr, dst, c_in, c_out = plain[k]
            dst[:, c_out:c_out + PROJ_TN] = _dot(hb, wr[:, c_in:c_in + PROJ_TN]).astype(dst.dtype)
        acc = convb_ref[:, cols] + convw_ref[0:1, cols] * ext_ref[base:base + TOKEN_TILE, :]
        for i in range(1, CONV_WIDTH):
            acc = acc + convw_ref[i:i + 1, cols] * ext_ref[base + i:base + i + TOKEN_TILE, :]
        scale = jnp.where(lane + j < M_HEADS * M_DK, M_DK ** -0.5, 1.0)
        qk_ref[:, cols] = (_silu(acc) * scale).astype(BF16)
    gates_ref[...] = _dot(hb, wg_ref[...])


def _proj_even(xf, g, wm, wb, wg, conv_w, conv_b, seq):
    t, d = xf.shape
    half = D_MIX // 2
    assert seq % TOKEN_TILE == 0 and TOKEN_TILE % HALO == 0
    tok = lambda i: (i, 0)
    const = lambda i: (0, 0)
    per_halo = TOKEN_TILE // HALO
    widths = (M_QK, half, half, half, D_MIX)
    dtypes = (BF16, BF16, BF16, F32, BF16)
    out_shape = [jax.ShapeDtypeStruct((t, n), dt) for n, dt in zip(widths, dtypes)]
    ngate = wg.shape[1]
    out_shape.append(jax.ShapeDtypeStruct((t, ngate), F32))
    out_specs = [pl.BlockSpec((TOKEN_TILE, n), tok) for n in widths]
    out_specs.append(pl.BlockSpec((TOKEN_TILE, ngate), tok))
    return pl.pallas_call(
        functools.partial(_proj_even_body, tiles_per_seq=seq // TOKEN_TILE),
        grid=(t // TOKEN_TILE,),
        in_specs=[
            pl.BlockSpec((TOKEN_TILE, d), tok),
            pl.BlockSpec((HALO, d), lambda i: (jnp.maximum(i * per_halo - 1, 0), 0)),
            pl.BlockSpec((1, d), const),
            pl.BlockSpec((d, wm.shape[1]), const, pipeline_mode=pl.Buffered(1)),
            pl.BlockSpec((d, wb.shape[1]), const, pipeline_mode=pl.Buffered(1)),
            pl.BlockSpec((d, ngate), const),
            pl.BlockSpec((CONV_WIDTH, M_QK), const),
            pl.BlockSpec((1, M_QK), const),
        ],
        out_specs=out_specs,
        out_shape=out_shape,
        scratch_shapes=[pltpu.VMEM((HALO + TOKEN_TILE, PROJ_TN), F32)],
        compiler_params=_cparams(1),
        name="proj_even",
    )(xf, xf, g.reshape(1, d), wm, wb, wg, conv_w, conv_b.reshape(1, M_QK))


def _mlstm_body(qk_ref, v_ref, gates_ref, gbias_ref, out_ref, c_ref, n_ref, m_ref):
    @pl.when(pl.program_id(1) == 0)
    def _init():
        c_ref[...] = jnp.zeros_like(c_ref)
        n_ref[...] = jnp.zeros_like(n_ref)
        m_ref[...] = jnp.zeros_like(m_ref)

    for sub in range(M_SUB):
        _mlstm_chunk(slice(sub * M_CHUNK, (sub + 1) * M_CHUNK), qk_ref, v_ref, gates_ref, gbias_ref,
                     out_ref, c_ref, n_ref, m_ref)


def _mlstm_chunk(r, qk_ref, v_ref, gates_ref, gbias_ref, out_ref, c_ref, n_ref, m_ref):
    L = M_CHUNK
    gt = gates_ref[r, :] + gbias_ref[...]
    ipre = gt[:, 0:LANES]
    logf = _log_sigmoid(gt[:, LANES:2 * LANES]).astype(BF16)
    b = _dot(_tri_ones(L, True), logf)
    w = ipre - b
    rows = lax.broadcasted_iota(jnp.int32, (L, LANES), 0)
    cm = w
    k = 1
    while k < L:
        cm = jnp.maximum(cm, jnp.where(rows >= k, pltpu.roll(cm, k, axis=0), -jnp.inf))
        k *= 2

    def replicate(x):
        return jnp.concatenate([jnp.broadcast_to(x[:, h:h + 1], (L, LANES)) for h in range(M_HEADS)], axis=1)

    b_rep = replicate(b)
    w_rep = replicate(w)
    cm_rep = replicate(cm)
    pick_r = lax.broadcasted_iota(jnp.int32, (M_HEADS * SUBLANES, LANES), 0)
    pick_c = lax.broadcasted_iota(jnp.int32, (M_HEADS * SUBLANES, LANES), 1)
    pick = jnp.where(lax.shift_right_logical(pick_r, 3) == pick_c, 1.0, 0.0).astype(BF16)
    w_hi, w_lo = _split_hi_lo(w)
    w_row = _dot_nt(pick, w_hi) + _dot_nt(pick, w_lo)

    trow = lax.broadcasted_iota(jnp.int32, (L, LANES), 0)
    tcol = lax.broadcasted_iota(jnp.int32, (L, LANES), 1)
    ones = jnp.ones((L, LANES), BF16)

    for h in range(M_HEADS):
        ks = slice(h * M_DK, (h + 1) * M_DK)
        ks2 = slice(M_HEADS * M_DK + h * M_DK, M_HEADS * M_DK + (h + 1) * M_DK)
        vs = slice(h * M_DV, (h + 1) * M_DV)
        hs = slice(h * LANES, (h + 1) * LANES)
        m_prev = m_ref[h, 0:1, :]
        c_prev = c_ref[h]
        n_prev = n_ref[h]
        big_m = jnp.maximum(m_prev, cm_rep[:, hs])
        w_inter = jnp.exp(m_prev - big_m)
        wr = w_row[h * SUBLANES:h * SUBLANES + 1, :]

        qb = qk_ref[r, ks]
        kb = qk_ref[r, ks2]
        v_h = v_ref[r, vs]
        s = _dot_nt(qb, kb)
        sc = jnp.concatenate(
            [jnp.where(trow >= tcol + j, jnp.exp(wr[:, j:j + LANES] - big_m), 0.0) * s[:, j:j + LANES]
             for j in range(0, L, LANES)], axis=1).astype(BF16)
        q_c = _dot(qb, c_prev.astype(BF16))
        den = _dot(sc, ones) + w_inter * _dot(qb, n_prev.astype(BF16))
        inv = 1.0 / jnp.maximum(jnp.abs(den), jnp.exp(-(b_rep[:, hs] + big_m)))
        num = _dot(sc, v_h)
        out_ref[r, vs] = jnp.concatenate(
            [(num[:, j:j + LANES] + w_inter * q_c[:, j:j + LANES]) * inv for j in range(0, M_DV, LANES)],
            axis=1).astype(BF16)

        g = b_rep[L - 1:L, hs]
        cm_last = cm_rep[L - 1:L, hs]
        m_last = big_m[L - 1:L, :]
        kw_t = (kb.astype(F32) * jnp.exp(w_rep[:, hs] - cm_last)).T.astype(BF16)
        s_prev = jnp.exp(m_prev - m_last)
        s_loc = jnp.exp(cm_last - m_last)
        c_ref[h] = (jnp.concatenate([s_prev] * (M_DV // LANES), axis=1) * c_prev
                    + jnp.concatenate([s_loc] * (M_DV // LANES), axis=1) * _dot(kw_t, v_h))
        n_ref[h] = s_prev * n_prev + s_loc * _dot(kw_t, ones)
        m_ref[h] = jnp.broadcast_to(g + m_last, (SUBLANES, LANES))


def _mlstm(qk, v, gates, gbias, batch, seq):
    t = batch * seq
    L = M_CHUNK * M_SUB
    nc = seq // L
    dv = M_HEADS * M_DV
    tok = lambda b, c: (b * nc + c, 0)
    const = lambda b, c: (0, 0)
    return pl.pallas_call(
        _mlstm_body,
        grid=(batch, nc),
        in_specs=[
            pl.BlockSpec((L, M_QK), tok),
            pl.BlockSpec((L, dv), tok),
            pl.BlockSpec((L, 2 * LANES), tok),
            pl.BlockSpec((1, 2 * LANES), const),
        ],
        out_specs=pl.BlockSpec((L, dv), tok),
        out_shape=jax.ShapeDtypeStruct((t, dv), BF16),
        scratch_shapes=[
            pltpu.VMEM((M_HEADS, M_DK, M_DV), F32),
            pltpu.VMEM((M_HEADS, M_DK, LANES), F32),
            pltpu.VMEM((M_HEADS, SUBLANES, LANES), F32),
        ],
        compiler_params=_cparams(2),
        name="mlstm",
    )(qk, v, gates, gbias)


def _gelu_tanh(x):
    c = 0.7978845608028654
    half = 0.5 * x
    return half + half * jnp.tanh(x * (c + (c * 0.044715) * (x * x)))


def _s5_build_operators(lamr_ref, lami_ref, ldt_ref, btr_ref, bti_ref, ctr_ref, cti_ref, d_ref,
                        pbig_ref, qbig_ref, mbig_ref, a_ref):
    lr = lamr_ref[0]
    li = lami_ref[0]
    dt = jnp.exp(ldt_ref[0])
    zr = lr * dt
    th = li * dt
    er = jnp.exp(zr)
    ar = er * jnp.cos(th)
    ai = er * jnp.sin(th)
    den = lr * lr + li * li
    beta_r = ((ar - 1.0) * lr + ai * li) / den
    beta_i = (ai * lr - (ar - 1.0) * li) / den
    btr = btr_ref[0]
    bti = bti_ref[0]
    bbr = btr * beta_r - bti * beta_i
    bbi = btr * beta_i + bti * beta_r
    ctr = ctr_ref[0]
    cti = cti_ref[0]

    row_g = lax.shift_right_logical(lax.broadcasted_iota(jnp.int32, (LANES, S5_SW), 0), 4)
    lane_g = lax.shift_right_logical(lax.broadcasted_iota(jnp.int32, (LANES, S5_SW), 1), 6)
    same_group = row_g == lane_g

    def expand(x16):
        return jnp.where(same_group, jnp.concatenate([x16] * S5_GPL, axis=0), 0.0)

    def power(k):
        e = jnp.exp(float(k) * zr)
        return e * jnp.cos(float(k) * th), e * jnp.sin(float(k) * th)

    for s in range(S5_BLK):
        rows = slice(s * LANES, (s + 1) * LANES)
        pr, pi = power(S5_BLK - 1 - s)
        pbig_ref[rows, 0:S5_SW] = expand(pr * bbr - pi * bbi).astype(BF16)
        pbig_ref[rows, S5_SW:2 * S5_SW] = expand(pr * bbi + pi * bbr).astype(BF16)
        pr, pi = power(s + 1)
        qbig_ref[rows, 0:S5_SW] = expand(ctr * pr - cti * pi).astype(BF16)
        qbig_ref[rows, S5_SW:2 * S5_SW] = expand(-(ctr * pi + cti * pr)).astype(BF16)

    cb = jnp.concatenate([expand(ctr), expand(-cti)], axis=1)
    r128 = lax.broadcasted_iota(jnp.int32, (LANES, LANES), 0)
    c128 = lax.broadcasted_iota(jnp.int32, (LANES, LANES), 1)
    zero_blk = jnp.zeros((LANES, LANES), BF16)
    ab_lags = []
    for lag in range(S5_BLK):
        pr, pi = power(lag)
        ab_lags.append(jnp.concatenate([expand(pr * bbr - pi * bbi), expand(pr * bbi + pi * bbr)], axis=1))
    ab_hi, ab_lo = _split_hi_lo(jnp.concatenate(ab_lags, axis=0))
    cb_hi, cb_lo = _split_hi_lo(cb)
    v_lags = _dot_nt(ab_hi, cb_hi) + _dot_nt(ab_hi, cb_lo) + _dot_nt(ab_lo, cb_hi)
    for lag in range(S5_BLK):
        v = v_lags[lag * LANES:(lag + 1) * LANES, :]
        if lag == 0:
            v = v + jnp.where(r128 == c128, d_ref[0], 0.0)
        vb = v.astype(BF16)
        for s in range(S5_BLK - lag):
            t = s + lag
            mbig_ref[s * LANES:(s + 1) * LANES, t * LANES:(t + 1) * LANES] = vb
            if lag > 0:
                mbig_ref[t * LANES:(t + 1) * LANES, s * LANES:(s + 1) * LANES] = zero_blk

    pr, pi = power(S5_BLK)
    a_ref[:, 0:S5_SW] = jnp.broadcast_to(pr, (SUBLANES, S5_SW))
    a_ref[:, S5_SW:2 * S5_SW] = jnp.broadcast_to(pi, (SUBLANES, S5_SW))


def _s5_body(u_ref, lamr_ref, lami_ref, ldt_ref, btr_ref, bti_ref, ctr_ref, cti_ref, d_ref, y_ref,
             pbig_ref, qbig_ref, mbig_ref, a_ref, ucat_ref, x_ref, xp_ref, st_ref):
    nblk = S5_CHUNK // S5_BLK
    batch = u_ref.shape[0]

    @pl.when(pl.program_id(1) == 0)
    def _setup():
        _s5_build_operators(lamr_ref, lami_ref, ldt_ref, btr_ref, bti_ref, ctr_ref, cti_ref, d_ref,
                            pbig_ref, qbig_ref, mbig_ref, a_ref)
        st_ref[...] = jnp.zeros_like(st_ref)

    for b in range(batch):
        for s in range(S5_BLK):
            piece = u_ref[b, pl.ds(s, nblk, stride=S5_BLK), :]
            ucat_ref[b * nblk:(b + 1) * nblk, s * LANES:(s + 1) * LANES] = piece.astype(BF16)
    nslab = 2 * S5_SW // LANES
    half = nslab // 2
    xloc = _dot(ucat_ref[...], pbig_ref[...])
    for c in range(nslab):
        for b in range(batch):
            x_ref[c, b * S5_XROWS:b * S5_XROWS + nblk, :] = (
                xloc[b * nblk:(b + 1) * nblk, c * LANES:(c + 1) * LANES])

    lanes = lambda ref, c: ref[:, c * LANES:(c + 1) * LANES]
    ar = [lanes(a_ref, c) for c in range(half)]
    ai = [lanes(a_ref, half + c) for c in range(half)]
    xr = [lanes(st_ref, c) for c in range(half)]
    xi = [lanes(st_ref, half + c) for c in range(half)]
    for blk in range(nblk):
        r = pl.ds(blk, batch, stride=S5_XROWS)
        for c in range(half):
            xp_ref[c, r, :] = xr[c]
            xp_ref[half + c, r, :] = xi[c]
            nr = ar[c] * xr[c] - ai[c] * xi[c] + x_ref[c, r, :]
            ni = ar[c] * xi[c] + ai[c] * xr[c] + x_ref[half + c, r, :]
            xr[c], xi[c] = nr, ni
    for c in range(half):
        st_ref[:, c * LANES:(c + 1) * LANES] = xr[c]
        st_ref[:, (half + c) * LANES:(half + c + 1) * LANES] = xi[c]

    xpb = jnp.concatenate(
        [jnp.concatenate([xp_ref[c, b * S5_XROWS:b * S5_XROWS + nblk, :].astype(BF16)
                          for c in range(nslab)], axis=1) for b in range(batch)], axis=0)
    width = 2 * LANES
    for nb in range(S5_BLK // 2):
        kk = (2 * nb + 2) * LANES
        cols = slice(nb * width, (nb + 1) * width)
        y = _dot(ucat_ref[:, 0:kk], mbig_ref[0:kk, cols]) + _dot_nt(xpb, qbig_ref[cols, :])
        y = _gelu_tanh(y)
        for tt in range(2):
            t = 2 * nb + tt
            for b in range(batch):
                y_ref[b, pl.ds(t, nblk, stride=S5_BLK), :] = (
                    y[b * nblk:(b + 1) * nblk, tt * LANES:(tt + 1) * LANES])


def _s5(u3, lam_re, lam_im, log_dt, b_re, b_im, c_re, c_im, d_skip):
    batch, seq, width = u3.shape
    nlb = width // LANES
    assert batch == SUBLANES and seq % S5_CHUNK == 0
    lamr = lam_re.reshape(nlb, 1, S5_SW)
    lami = lam_im.reshape(nlb, 1, S5_SW)
    ldt = jnp.repeat(log_dt, S5_STATE).reshape(nlb, 1, S5_SW)
    bt = lambda b: b.reshape(nlb, S5_GPL, S5_STATE, S5_GROUP).transpose(0, 3, 1, 2).reshape(nlb, S5_GROUP, S5_SW)
    ct = lambda c: c.reshape(nlb, S5_GPL, S5_GROUP, S5_STATE).transpose(0, 2, 1, 3).reshape(nlb, S5_GROUP, S5_SW)
    par = lambda r, w: pl.BlockSpec((1, r, w), lambda i, j: (i, 0, 0))
    kdim = S5_BLK * LANES
    rows = (S5_CHUNK // S5_BLK) * batch
    io = pl.BlockSpec((batch, S5_CHUNK, LANES), lambda i, j: (0, j, i))
    return pl.pallas_call(
        _s5_body,
        grid=(nlb, seq // S5_CHUNK),
        in_specs=[io, par(1, S5_SW), par(1, S5_SW), par(1, S5_SW), par(S5_GROUP, S5_SW),
                  par(S5_GROUP, S5_SW), par(S5_GROUP, S5_SW), par(S5_GROUP, S5_SW), par(1, LANES)],
        out_specs=io,
        out_shape=jax.ShapeDtypeStruct((batch, seq, width), F32),
        scratch_shapes=[
            pltpu.VMEM((kdim, 2 * S5_SW), BF16),
            pltpu.VMEM((kdim, 2 * S5_SW), BF16),
            pltpu.VMEM((kdim, kdim), BF16),
            pltpu.VMEM((SUBLANES, 2 * S5_SW), F32),
            pltpu.VMEM((rows, kdim), BF16),
            pltpu.VMEM((2 * S5_SW // LANES, batch * S5_XROWS, LANES), F32),
            pltpu.VMEM((2 * S5_SW // LANES, batch * S5_XROWS, LANES), F32),
            pltpu.VMEM((SUBLANES, 2 * S5_SW), F32),
        ],
        compiler_params=_cparams(2),
        name="s5",
    )(u3, lamr, lami, ldt, bt(b_re), bt(b_im), ct(c_re), ct(c_im), d_skip.reshape(nlb, 1, LANES))


def _even_out_body(yb_ref, h_ref, o_ref, z_ref, x_ref, hg_ref, gluw_ref, glub_ref, wout_ref,
                   out_ref):
    half = h_ref.shape[1]
    yg = yb_ref[...]
    s = _dot(yg.astype(BF16), gluw_ref[...]) + glub_ref[...]
    acc = x_ref[...]
    for h in range(M_HEADS):
        vs = slice(h * M_DV, (h + 1) * M_DV)
        og = (jax.nn.sigmoid(o_ref[:, vs]) * h_ref[:, vs]).astype(F32)
        y_h = _rmsnorm(og, hg_ref[:, vs]).astype(BF16) * _silu(z_ref[:, vs])
        acc = acc + _dot(y_h, wout_ref[vs, :])
    for j in range(0, half, M_DV):
        cs = slice(j, j + M_DV)
        zs = slice(half + j, half + j + M_DV)
        y_j = (yg[:, cs] * jax.nn.sigmoid(s[:, cs])).astype(BF16) * _silu(z_ref[:, zs])
        acc = acc + _dot(y_j, wout_ref[zs, :])
    out_ref[...] = acc


def _even_out(yb, hm, o, z, xf, head_g, glu_w, glu_b, w_out):
    t, d = xf.shape
    half = hm.shape[1]
    tok = lambda i: (i, 0)
    const = lambda i: (0, 0)
    return pl.pallas_call(
        _even_out_body,
        grid=(t // TOKEN_TILE,),
        in_specs=[
            pl.BlockSpec((TOKEN_TILE, half), tok),
            pl.BlockSpec((TOKEN_TILE, half), tok),
            pl.BlockSpec((TOKEN_TILE, half), tok),
            pl.BlockSpec((TOKEN_TILE, 2 * half), tok),
            pl.BlockSpec((TOKEN_TILE, d), tok),
            pl.BlockSpec((1, half), const),
            pl.BlockSpec((half, half), const),
            pl.BlockSpec((1, half), const),
            pl.BlockSpec((2 * half, d), const),
        ],
        out_specs=pl.BlockSpec((TOKEN_TILE, d), tok),
        out_shape=jax.ShapeDtypeStruct((t, d), F32),
        compiler_params=_cparams(1),
        name="even_out",
    )(yb, hm, o, z, xf, head_g.reshape(1, half), glu_w, glu_b.reshape(1, half), w_out)


def _proj_odd_body(x_ref, g_ref, w_ref, wg_ref, wa_ref, ba_ref, q_ref, k_ref, v_ref, z_ref, bc_ref):
    hb = _rmsnorm(x_ref[...], g_ref[...]).astype(BF16)
    _project(hb, w_ref, (q_ref, k_ref, v_ref, z_ref), 0)
    rb = _dot(hb, wg_ref[...]).astype(BF16)
    tril = _tri_ones(G_CHUNK, True)
    for j in range(0, bc_ref.shape[1], PROJ_TN):
        cols = slice(j, j + PROJ_TN)
        pre = _dot(rb, wa_ref[:, cols].astype(BF16)) + ba_ref[:, cols]
        la = (_log_sigmoid(pre) * (1.0 / G_TAU)).astype(BF16)
        for c in range(0, TOKEN_TILE, G_CHUNK):
            rows = slice(c, c + G_CHUNK)
            bc_ref[rows, cols] = _dot(tril, la[rows])


def _layer1_body(x_ref, g_ref, w_ref, wg_ref, wa_ref, ba_ref, hg_ref, wout_ref, gf_ref, out_ref,
                 q_s, k_s, v_s, z_s, bc_s, y_s, s_ref, *, tiles_per_seq):
    L = G_CHUNK

    @pl.when(lax.rem(pl.program_id(0), tiles_per_seq) == 0)
    def _init():
        s_ref[...] = jnp.zeros_like(s_ref)

    _proj_odd_body(x_ref, g_ref, w_ref, wg_ref, wa_ref, ba_ref, q_s, k_s, v_s, z_s, bc_s)

    row = lax.broadcasted_iota(jnp.int32, (L, L), 0)
    col = lax.broadcasted_iota(jnp.int32, (L, L), 1)
    causal = row >= col
    mid = L // 2 - 1

    for c, h in [(c, h) for c in range(0, TOKEN_TILE, 2 * L) for h in range(G_HEADS)]:
        ks = slice(h * G_DK, (h + 1) * G_DK)
        vs = slice(h * G_DV, (h + 1) * G_DV)
        r1, r2 = slice(c, c + L), slice(c + L, c + 2 * L)
        s_prev = s_ref[h]
        s_bf = s_prev.astype(BF16)
        v_p = v_s[c:c + 2 * L, vs]

        def prep(r):
            b = bc_s[r, ks]
            bm = b[mid:mid + 1, :]
            g = b[L - 1:L, :]
            e1 = jnp.exp(b - bm)
            qt = q_s[r, ks].astype(F32) * e1
            kt = k_s[r, ks].astype(F32) * (1.0 / e1)
            attn = jnp.where(causal, _dot_nt(qt.astype(BF16), kt.astype(BF16)), 0.0).astype(BF16)
            return attn, qt * jnp.exp(bm), kt * jnp.exp(g - bm), jnp.exp(g)

        a11, qi1, ke1, eg1 = prep(r1)
        a22, qi2, ke2, eg2 = prep(r2)
        qi2b = qi2.astype(BF16)
        a21 = _dot_nt(qi2b, ke1.astype(BF16)).astype(BF16)
        o1 = _dot(a11, v_p[0:L]) + _dot(qi1.astype(BF16), s_bf)
        o2 = _dot(jnp.concatenate([a21, a22], axis=1), v_p) + _dot((qi2 * eg1).astype(BF16), s_bf)
        y_s[r1, vs] = (_rmsnorm(o1, hg_ref[:, vs]) * _silu(z_s[r1, vs].astype(F32))).astype(BF16)
        y_s[r2, vs] = (_rmsnorm(o2, hg_ref[:, vs]) * _silu(z_s[r2, vs].astype(F32))).astype(BF16)
        ke_t = jnp.concatenate([(ke1 * eg2).T, ke2.T], axis=1).astype(BF16)
        g_col = jnp.broadcast_to(eg1 * eg2, (LANES, G_DK)).T[:, 0:1]
        s_ref[h] = g_col * s_prev + _dot(ke_t, v_p)

    x = x_ref[...] + _dot(y_s[...], wout_ref[...])
    out_ref[...] = _rmsnorm(x, gf_ref[...])


def _layer1(x1, g, wm, wg, w_alpha, b_alpha, head_g, w_out, gf, seq):
    t, d = x1.shape
    dk = G_HEADS * G_DK
    assert seq % TOKEN_TILE == 0 and TOKEN_TILE % G_CHUNK == 0
    tok = lambda i: (i, 0)
    const = lambda i: (0, 0)
    resident = lambda shape: pl.BlockSpec(shape, const, pipeline_mode=pl.Buffered(1))
    slots = lambda n, dt: pltpu.VMEM((TOKEN_TILE, n), dt)
    return pl.pallas_call(
        functools.partial(_layer1_body, tiles_per_seq=seq // TOKEN_TILE),
        grid=(t // TOKEN_TILE,),
        in_specs=[
            pl.BlockSpec((TOKEN_TILE, d), tok),
            pl.BlockSpec((1, d), const),
            resident((d, wm.shape[1])),
            pl.BlockSpec((d, G_RANK_PAD), const),
            pl.BlockSpec((G_RANK_PAD, dk), const),
            pl.BlockSpec((1, dk), const),
            pl.BlockSpec((1, D_MIX), const),
            resident((D_MIX, d)),
            pl.BlockSpec((1, d), const),
        ],
        out_specs=pl.BlockSpec((TOKEN_TILE, d), tok),
        out_shape=jax.ShapeDtypeStruct((t, d), F32),
        scratch_shapes=[
            slots(dk, BF16),
            slots(dk, BF16),
            slots(D_MIX, BF16),
            slots(D_MIX, BF16),
            slots(dk, F32),
            slots(D_MIX, BF16),
            pltpu.VMEM((G_HEADS, G_DK, G_DV), F32),
        ],
        compiler_params=_cparams(1),
        name="layer1",
    )(x1, g.reshape(1, d), wm, wg, w_alpha, b_alpha.reshape(1, dk), head_g.reshape(1, D_MIX), w_out,
      gf.reshape(1, d))


def kernel(x, norm_g, final_norm_g, ev_w_in, ev_conv_w, ev_conv_b, ev_i_bias, ev_f_bias, ev_head_g,
           s5_lam_re, s5_lam_im, s5_log_dt, s5_b_re, s5_b_im, s5_c_re, s5_c_im, s5_d, s5_glu_w,
           s5_glu_b, ev_w_out, od_w_in, gla_w_alpha, gla_b_alpha, gla_head_g, od_w_out):
    batch, seq, d = x.shape
    t = batch * seq
    xf = x.reshape(t, d)
    padc = lambda a: jnp.pad(a, ((0, 0), (0, LANES - a.shape[1])))

    wt = jnp.swapaxes(ev_w_in, 1, 2).reshape(ev_w_in.shape[2], d)
    g0 = 2 * M_HEADS * M_DK + 2 * M_HEADS * M_DV
    gi = g0 + M_HEADS
    gf = gi + M_HEADS
    half = D_MIX // 2
    w_qkvo, w_uz, wg = _cast_weights(wt, (0, gf), (g0, half + D_MIX), ((g0, M_HEADS), (gi, M_HEADS)))
    qk, v, o, u, z, gates = _proj_even(xf, norm_g[0], w_qkvo, w_uz, wg, ev_conv_w[0], ev_conv_b[0], seq)
    gbias = jnp.concatenate([padc(ev_i_bias), padc(ev_f_bias)], axis=1)
    hm = _mlstm(qk, v, gates, gbias, batch, seq)

    y3 = _s5(u.reshape(batch, seq, half), s5_lam_re[0], s5_lam_im[0], s5_log_dt[0], s5_b_re[0],
             s5_b_im[0], s5_c_re[0], s5_c_im[0], s5_d[0])
    yb = y3.reshape(t, half)
    x1 = _even_out(yb, hm, o, z, xf, ev_head_g[0], s5_glu_w[0].astype(BF16), s5_glu_b[0],
                   ev_w_out[0].astype(BF16))

    wt = jnp.swapaxes(od_w_in, 1, 2).reshape(od_w_in.shape[2], d)
    n_main = 2 * G_HEADS * G_DK + 2 * D_MIX
    wa = jnp.pad(gla_w_alpha[0], ((0, G_RANK_PAD - gla_w_alpha.shape[1]), (0, 0)))
    wm, wr = _cast_weights(wt, (0,), (n_main,), ((n_main, wt.shape[0] - n_main),),
                           scaled_cols=G_HEADS * G_DK, scale=G_DK ** -0.5)
    out = _layer1(x1, norm_g[1], wm, wr, wa, gla_b_alpha[0],
                  gla_head_g[0], od_w_out[0].astype(BF16), final_norm_g, seq)
    return out.reshape(batch, seq, d)
```

```python
import functools

import jax
import jax.numpy as jnp
from jax import lax
from jax.experimental import pallas as pl
from jax.experimental.pallas import tpu as pltpu

F32 = jnp.float32
BF16 = jnp.bfloat16

EPS = 1e-6
D_MODEL = 1024
D_MIX = 2 * D_MODEL
M_HEADS = 4
M_DK = 128
M_DV = 256
M_QK = 2 * M_HEADS * M_DK
CONV_WIDTH = 4
M_CHUNK = 256
M_SUB = 4
S5_GROUP = 16
S5_STATE = 64
S5_BLK = 8
S5_CHUNK = 1024
G_HEADS = 4
G_DK = 256
G_DV = 512
G_TAU = 16.0
G_CHUNK = 128
G_RANK_PAD = 128

LANES = 128
SUBLANES = 8
HALO = 16
S5_GPL = LANES // S5_GROUP
S5_SW = S5_GPL * S5_STATE
S5_XROWS = S5_CHUNK // S5_BLK + SUBLANES
TOKEN_TILE = 512
PROJ_TN = 256
VMEM_LIMIT = 56 * 1024 * 1024


def _cparams(n_grid):
    return pltpu.CompilerParams(
        dimension_semantics=("arbitrary",) * n_grid, vmem_limit_bytes=VMEM_LIMIT)


def _log2(n):
    assert n & (n - 1) == 0
    return n.bit_length() - 1


def _log_sigmoid(x):
    return jnp.minimum(x, 0.0) - jnp.log(1.0 + jnp.exp(-jnp.abs(x)))


def _silu(x):
    return x * jax.nn.sigmoid(x)


def _split_hi_lo(x):
    hi = x.astype(BF16)
    lo = (x - hi.astype(F32)).astype(BF16)
    return hi, lo


def _dot(a, b):
    return jnp.dot(a, b, preferred_element_type=F32)


def _dot_nt(a, b, precision=None):
    return lax.dot_general(a, b, (((1,), (1,)), ((), ())), precision=precision,
                           preferred_element_type=F32)


def _tri_ones(n, lower):
    row = lax.broadcasted_iota(jnp.int32, (n, n), 0)
    col = lax.broadcasted_iota(jnp.int32, (n, n), 1)
    keep = (row >= col) if lower else (row <= col)
    return jnp.where(keep, 1.0, 0.0).astype(BF16)


def _rmsnorm(x, g):
    return x * lax.rsqrt(jnp.mean(x * x, axis=-1, keepdims=True) + EPS) * g


CAST_ROWS = 128


def _cast_body(wt_ref, *out_refs, starts, narrow, scaled_cols, scale):
    for o_ref, c0 in zip(out_refs[:-1], starts):
        n = o_ref.shape[1]
        val = wt_ref[c0:c0 + n, :]
        if scaled_cols and c0 == 0:
            col = lax.broadcasted_iota(jnp.int32, (n, 1), 0)
            val = val * jnp.where(col < scaled_cols, scale, 1.0)
        o_ref[...] = val.T.astype(BF16)
    lane = lax.broadcasted_iota(jnp.int32, (1, LANES), 1)
    for blk, (c0, n) in enumerate(narrow):
        start = (c0 // SUBLANES) * SUBLANES
        take = -(-(c0 - start + n) // SUBLANES) * SUBLANES
        rows = jnp.concatenate([wt_ref[start:start + take, :], jnp.zeros((LANES - take, CAST_ROWS), F32)], axis=0)
        slab = rows.T
        if c0 != start:
            slab = pltpu.roll(slab, LANES - (c0 - start), axis=1)
        out_refs[-1][:, blk * LANES:(blk + 1) * LANES] = jnp.where(lane < n, slab, 0.0).astype(BF16)


def _cast_weights(wt, starts, widths, narrow, scaled_cols=0, scale=1.0):
    cols, rows = wt.shape
    assert CAST_ROWS == LANES
    widths = tuple(widths) + (len(narrow) * LANES,)
    return pl.pallas_call(
        functools.partial(_cast_body, starts=starts, narrow=narrow, scaled_cols=scaled_cols, scale=scale),
        grid=(rows // CAST_ROWS,),
        in_specs=[pl.BlockSpec((cols, CAST_ROWS), lambda i: (0, i))],
        out_specs=[pl.BlockSpec((CAST_ROWS, n), lambda i: (i, 0)) for n in widths],
        out_shape=[jax.ShapeDtypeStruct((rows, n), BF16) for n in widths],
        compiler_params=_cparams(1),
        name="cast_weights",
    )(wt)


def _proj_even_body(x_ref, xh_ref, g_ref, w_ref, wb_ref, wg_ref, convw_ref, convb_ref,
                    qk_ref, v_ref, o_ref, u_ref, z_ref, gates_ref, ext_ref, *, tiles_per_seq):
    hb = _rmsnorm(x_ref[...], g_ref[...]).astype(BF16)
    hh = _rmsnorm(xh_ref[...], g_ref[...]).astype(BF16)
    seq_start = lax.rem(pl.program_id(0), tiles_per_seq) == 0
    base = HALO - (CONV_WIDTH - 1)
    lane = lax.broadcasted_iota(jnp.int32, (1, PROJ_TN), 1)
    plain = []
    for wr, col0, o_refs in ((w_ref, M_QK, (v_ref, o_ref)), (wb_ref, 0, (u_ref, z_ref))):
        for o_r in o_refs:
            plain += [(wr, o_r, col0 + c, c) for c in range(0, o_r.shape[1], PROJ_TN)]
            col0 += o_r.shape[1]
    for j in range(0, M_QK, PROJ_TN):
        cols = slice(j, j + PROJ_TN)
        wj = w_ref[:, cols]
        ext_ref[HALO:HALO + TOKEN_TILE, :] = _dot(hb, wj)
        ext_ref[0:HALO, :] = jnp.where(seq_start, 0.0, _dot(hh, wj))
        for k in range(len(plain) * j // M_QK, len(plain) * (j + PROJ_TN) // M_QK):
            wr, dst, c_in, c_out = plain[k]
            dst[:, c_out:c_out + PROJ_TN] = _dot(hb, wr[:, c_in:c_in + PROJ_TN]).astype(dst.dtype)
        acc = convb_ref[:, cols] + convw_ref[0:1, cols] * ext_ref[base:base + TOKEN_TILE, :]
        for i in range(1, CONV_WIDTH):
            acc = acc + convw_ref[i:i + 1, cols] * ext_ref[base + i:base + i + TOKEN_TILE, :]
        scale = jnp.where(lane + j < M_HEADS * M_DK, M_DK ** -0.5, 1.0)
        qk_ref[:, cols] = (_silu(acc) * scale).astype(BF16)
    gates_ref[...] = _dot(hb, wg_ref[...])


def _proj_even(xf, g, wm, wb, wg, conv_w, conv_b, seq):
    t, d = xf.shape
    half = D_MIX // 2
    assert seq % TOKEN_TILE == 0 and TOKEN_TILE % HALO == 0
    tok = lambda i: (i, 0)
    const = lambda i: (0, 0)
    per_halo = TOKEN_TILE // HALO
    widths = (M_QK, half, half, half, D_MIX)
    dtypes = (BF16, BF16, BF16, F32, BF16)
    out_shape = [jax.ShapeDtypeStruct((t, n), dt) for n, dt in zip(widths, dtypes)]
    ngate = wg.shape[1]
    out_shape.append(jax.ShapeDtypeStruct((t, ngate), F32))
    out_specs = [pl.BlockSpec((TOKEN_TILE, n), tok) for n in widths]
    out_specs.append(pl.BlockSpec((TOKEN_TILE, ngate), tok))
    return pl.pallas_call(
        functools.partial(_proj_even_body, tiles_per_seq=seq // TOKEN_TILE),
        grid=(t // TOKEN_TILE,),
        in_specs=[
            pl.BlockSpec((TOKEN_TILE, d), tok),
            pl.BlockSpec((HALO, d), lambda i: (jnp.maximum(i * per_halo - 1, 0), 0)),
            pl.BlockSpec((1, d), const),
            pl.BlockSpec((d, wm.shape[1]), const, pipeline_mode=pl.Buffered(1)),
            pl.BlockSpec((d, wb.shape[1]), const, pipeline_mode=pl.Buffered(1)),
            pl.BlockSpec((d, ngate), const),
            pl.BlockSpec((CONV_WIDTH, M_QK), const),
            pl.BlockSpec((1, M_QK), const),
        ],
        out_specs=out_specs,
        out_shape=out_shape,
        scratch_shapes=[pltpu.VMEM((HALO + TOKEN_TILE, PROJ_TN), F32)],
        compiler_params=_cparams(1),
        name="proj_even",
    )(xf, xf, g.reshape(1, d), wm, wb, wg, conv_w, conv_b.reshape(1, M_QK))


def _mlstm_body(qk_ref, v_ref, gates_ref, gbias_ref, out_ref, c_ref, n_ref, m_ref):
    @pl.when(pl.program_id(1) == 0)
    def _init():
        c_ref[...] = jnp.zeros_like(c_ref)
        n_ref[...] = jnp.zeros_like(n_ref)
        m_ref[...] = jnp.zeros_like(m_ref)

    for sub in range(M_SUB):
        _mlstm_chunk(slice(sub * M_CHUNK, (sub + 1) * M_CHUNK), qk_ref, v_ref, gates_ref, gbias_ref,
                     out_ref, c_ref, n_ref, m_ref)


def _mlstm_chunk(r, qk_ref, v_ref, gates_ref, gbias_ref, out_ref, c_ref, n_ref, m_ref):
    L = M_CHUNK
    gt = gates_ref[r, :] + gbias_ref[...]
    ipre = gt[:, 0:LANES]
    logf = _log_sigmoid(gt[:, LANES:2 * LANES]).astype(BF16)
    b = _dot(_tri_ones(L, True), logf)
    w = ipre - b
    rows = lax.broadcasted_iota(jnp.int32, (L, LANES), 0)
    cm = w
    k = 1
    while k < L:
        cm = jnp.maximum(cm, jnp.where(rows >= k, pltpu.roll(cm, k, axis=0), -jnp.inf))
        k *= 2

    def replicate(x):
        return jnp.concatenate([jnp.broadcast_to(x[:, h:h + 1], (L, LANES)) for h in range(M_HEADS)], axis=1)

    b_rep = replicate(b)
    w_rep = replicate(w)
    cm_rep = replicate(cm)
    pick_r = lax.broadcasted_iota(jnp.int32, (M_HEADS * SUBLANES, LANES), 0)
    pick_c = lax.broadcasted_iota(jnp.int32, (M_HEADS * SUBLANES, LANES), 1)
    pick = jnp.where(lax.shift_right_logical(pick_r, _log2(SUBLANES)) == pick_c, 1.0, 0.0).astype(BF16)
    w_hi, w_lo = _split_hi_lo(w)
    w_row = _dot_nt(pick, w_hi) + _dot_nt(pick, w_lo)

    trow = lax.broadcasted_iota(jnp.int32, (L, LANES), 0)
    tcol = lax.broadcasted_iota(jnp.int32, (L, LANES), 1)
    ones = jnp.ones((L, LANES), BF16)

    for h in range(M_HEADS):
        ks = slice(h * M_DK, (h + 1) * M_DK)
        ks2 = slice(M_HEADS * M_DK + h * M_DK, M_HEADS * M_DK + (h + 1) * M_DK)
        vs = slice(h * M_DV, (h + 1) * M_DV)
        hs = slice(h * LANES, (h + 1) * LANES)
        m_prev = m_ref[h, 0:1, :]
        c_prev = c_ref[h]
        n_prev = n_ref[h]
        big_m = jnp.maximum(m_prev, cm_rep[:, hs])
        w_inter = jnp.exp(m_prev - big_m)
        wr = w_row[h * SUBLANES:h * SUBLANES + 1, :]

        qb = qk_ref[r, ks]
        kb = qk_ref[r, ks2]
        v_h = v_ref[r, vs]
        s = _dot_nt(qb, kb)
        sc = jnp.concatenate(
            [jnp.where(trow >= tcol + j, jnp.exp(wr[:, j:j + LANES] - big_m), 0.0) * s[:, j:j + LANES]
             for j in range(0, L, LANES)], axis=1).astype(BF16)
        q_c = _dot(qb, c_prev.astype(BF16))
        den = _dot(sc, ones) + w_inter * _dot(qb, n_prev.astype(BF16))
        inv = 1.0 / jnp.maximum(jnp.abs(den), jnp.exp(-(b_rep[:, hs] + big_m)))
        num = _dot(sc, v_h)
        out_ref[r, vs] = jnp.concatenate(
            [(num[:, j:j + LANES] + w_inter * q_c[:, j:j + LANES]) * inv for j in range(0, M_DV, LANES)],
            axis=1).astype(BF16)

        g = b_rep[L - 1:L, hs]
        cm_last = cm_rep[L - 1:L, hs]
        m_last = big_m[L - 1:L, :]
        kw_t = (kb.astype(F32) * jnp.exp(w_rep[:, hs] - cm_last)).T.astype(BF16)
        s_prev = jnp.exp(m_prev - m_last)
        s_loc = jnp.exp(cm_last - m_last)
        c_ref[h] = (jnp.concatenate([s_prev] * (M_DV // LANES), axis=1) * c_prev
                    + jnp.concatenate([s_loc] * (M_DV // LANES), axis=1) * _dot(kw_t, v_h))
        n_ref[h] = s_prev * n_prev + s_loc * _dot(kw_t, ones)
        m_ref[h] = jnp.broadcast_to(g + m_last, (SUBLANES, LANES))


def _mlstm(qk, v, gates, gbias, batch, seq):
    t = batch * seq
    L = M_CHUNK * M_SUB
    nc = seq // L
    dv = M_HEADS * M_DV
    tok = lambda b, c: (b * nc + c, 0)
    const = lambda b, c: (0, 0)
    return pl.pallas_call(
        _mlstm_body,
        grid=(batch, nc),
        in_specs=[
            pl.BlockSpec((L, M_QK), tok),
            pl.BlockSpec((L, dv), tok),
            pl.BlockSpec((L, 2 * LANES), tok),
            pl.BlockSpec((1, 2 * LANES), const),
        ],
        out_specs=pl.BlockSpec((L, dv), tok),
        out_shape=jax.ShapeDtypeStruct((t, dv), BF16),
        scratch_shapes=[
            pltpu.VMEM((M_HEADS, M_DK, M_DV), F32),
            pltpu.VMEM((M_HEADS, M_DK, LANES), F32),
            pltpu.VMEM((M_HEADS, SUBLANES, LANES), F32),
        ],
        compiler_params=_cparams(2),
        name="mlstm",
    )(qk, v, gates, gbias)


def _gelu_tanh(x):
    c = 0.7978845608028654
    half = 0.5 * x
    return half + half * jnp.tanh(x * (c + (c * 0.044715) * (x * x)))


def _s5_build_operators(lamr_ref, lami_ref, ldt_ref, btr_ref, bti_ref, ctr_ref, cti_ref, d_ref,
                        pbig_ref, qbig_ref, mbig_ref, a_ref):
    lr = lamr_ref[0]
    li = lami_ref[0]
    dt = jnp.exp(ldt_ref[0])
    zr = lr * dt
    th = li * dt
    er = jnp.exp(zr)
    ar = er * jnp.cos(th)
    ai = er * jnp.sin(th)
    den = lr * lr + li * li
    beta_r = ((ar - 1.0) * lr + ai * li) / den
    beta_i = (ai * lr - (ar - 1.0) * li) / den
    btr = btr_ref[0]
    bti = bti_ref[0]
    bbr = btr * beta_r - bti * beta_i
    bbi = btr * beta_i + bti * beta_r
    ctr = ctr_ref[0]
    cti = cti_ref[0]

    row_g = lax.shift_right_logical(lax.broadcasted_iota(jnp.int32, (LANES, S5_SW), 0), _log2(S5_GROUP))
    lane_g = lax.shift_right_logical(lax.broadcasted_iota(jnp.int32, (LANES, S5_SW), 1), _log2(S5_STATE))
    same_group = row_g == lane_g

    def expand(x16):
        return jnp.where(same_group, jnp.concatenate([x16] * S5_GPL, axis=0), 0.0)

    def power(k):
        e = jnp.exp(float(k) * zr)
        return e * jnp.cos(float(k) * th), e * jnp.sin(float(k) * th)

    for s in range(S5_BLK):
        rows = slice(s * LANES, (s + 1) * LANES)
        pr, pi = power(S5_BLK - 1 - s)
        pbig_ref[rows, 0:S5_SW] = expand(pr * bbr - pi * bbi).astype(BF16)
        pbig_ref[rows, S5_SW:2 * S5_SW] = expand(pr * bbi + pi * bbr).astype(BF16)
        pr, pi = power(s + 1)
        qbig_ref[rows, 0:S5_SW] = expand(ctr * pr - cti * pi).astype(BF16)
        qbig_ref[rows, S5_SW:2 * S5_SW] = expand(-(ctr * pi + cti * pr)).astype(BF16)

    cb = jnp.concatenate([expand(ctr), expand(-cti)], axis=1)
    r128 = lax.broadcasted_iota(jnp.int32, (LANES, LANES), 0)
    c128 = lax.broadcasted_iota(jnp.int32, (LANES, LANES), 1)
    zero_blk = jnp.zeros((LANES, LANES), BF16)
    ab_lags = []
    for lag in range(S5_BLK):
        pr, pi = power(lag)
        ab_lags.append(jnp.concatenate([expand(pr * bbr - pi * bbi), expand(pr * bbi + pi * bbr)], axis=1))
    ab_hi, ab_lo = _split_hi_lo(jnp.concatenate(ab_lags, axis=0))
    cb_hi, cb_lo = _split_hi_lo(cb)
    v_lags = _dot_nt(ab_hi, cb_hi) + _dot_nt(ab_hi, cb_lo) + _dot_nt(ab_lo, cb_hi)
    for lag in range(S5_BLK):
        v = v_lags[lag * LANES:(lag + 1) * LANES, :]
        if lag == 0:
            v = v + jnp.where(r128 == c128, d_ref[0], 0.0)
        vb = v.astype(BF16)
        for s in range(S5_BLK - lag):
            t = s + lag
            mbig_ref[s * LANES:(s + 1) * LANES, t * LANES:(t + 1) * LANES] = vb
            if lag > 0:
                mbig_ref[t * LANES:(t + 1) * LANES, s * LANES:(s + 1) * LANES] = zero_blk

    pr, pi = power(S5_BLK)
    a_ref[:, 0:S5_SW] = jnp.broadcast_to(pr, (SUBLANES, S5_SW))
    a_ref[:, S5_SW:2 * S5_SW] = jnp.broadcast_to(pi, (SUBLANES, S5_SW))


def _s5_body(u_ref, lamr_ref, lami_ref, ldt_ref, btr_ref, bti_ref, ctr_ref, cti_ref, d_ref, y_ref,
             pbig_ref, qbig_ref, mbig_ref, a_ref, ucat_ref, x_ref, xp_ref, st_ref):
    nblk = S5_CHUNK // S5_BLK
    batch = u_ref.shape[0]

    @pl.when(pl.program_id(1) == 0)
    def _setup():
        _s5_build_operators(lamr_ref, lami_ref, ldt_ref, btr_ref, bti_ref, ctr_ref, cti_ref, d_ref,
                            pbig_ref, qbig_ref, mbig_ref, a_ref)
        st_ref[...] = jnp.zeros_like(st_ref)

    for b in range(batch):
        for s in range(S5_BLK):
            piece = u_ref[b, pl.ds(s, nblk, stride=S5_BLK), :]
            ucat_ref[b * nblk:(b + 1) * nblk, s * LANES:(s + 1) * LANES] = piece.astype(BF16)
    nslab = 2 * S5_SW // LANES
    half = nslab // 2
    xloc = _dot(ucat_ref[...], pbig_ref[...])
    for c in range(nslab):
        for b in range(batch):
            x_ref[c, b * S5_XROWS:b * S5_XROWS + nblk, :] = (
                xloc[b * nblk:(b + 1) * nblk, c * LANES:(c + 1) * LANES])

    lanes = lambda ref, c: ref[:, c * LANES:(c + 1) * LANES]
    ar = [lanes(a_ref, c) for c in range(half)]
    ai = [lanes(a_ref, half + c) for c in range(half)]
    xr = [lanes(st_ref, c) for c in range(half)]
    xi = [lanes(st_ref, half + c) for c in range(half)]
    for blk in range(nblk):
        r = pl.ds(blk, batch, stride=S5_XROWS)
        for c in range(half):
            xp_ref[c, r, :] = xr[c]
            xp_ref[half + c, r, :] = xi[c]
            nr = ar[c] * xr[c] - ai[c] * xi[c] + x_ref[c, r, :]
            ni = ar[c] * xi[c] + ai[c] * xr[c] + x_ref[half + c, r, :]
            xr[c], xi[c] = nr, ni
    for c in range(half):
        st_ref[:, c * LANES:(c + 1) * LANES] = xr[c]
        st_ref[:, (half + c) * LANES:(half + c + 1) * LANES] = xi[c]

    xpb = jnp.concatenate(
        [jnp.concatenate([xp_ref[c, b * S5_XROWS:b * S5_XROWS + nblk, :].astype(BF16)
                          for c in range(nslab)], axis=1) for b in range(batch)], axis=0)
    width = 2 * LANES
    for nb in range(S5_BLK // 2):
        kk = (2 * nb + 2) * LANES
        cols = slice(nb * width, (nb + 1) * width)
        y = _dot(ucat_ref[:, 0:kk], mbig_ref[0:kk, cols]) + _dot_nt(xpb, qbig_ref[cols, :])
        y = _gelu_tanh(y)
        for tt in range(2):
            t = 2 * nb + tt
            for b in range(batch):
                y_ref[b, pl.ds(t, nblk, stride=S5_BLK), :] = (
                    y[b * nblk:(b + 1) * nblk, tt * LANES:(tt + 1) * LANES])


def _s5(u3, lam_re, lam_im, log_dt, b_re, b_im, c_re, c_im, d_skip):
    batch, seq, width = u3.shape
    nlb = width // LANES
    assert batch == SUBLANES and seq % S5_CHUNK == 0
    lamr = lam_re.reshape(nlb, 1, S5_SW)
    lami = lam_im.reshape(nlb, 1, S5_SW)
    ldt = jnp.repeat(log_dt, S5_STATE).reshape(nlb, 1, S5_SW)
    bt = lambda b: b.reshape(nlb, S5_GPL, S5_STATE, S5_GROUP).transpose(0, 3, 1, 2).reshape(nlb, S5_GROUP, S5_SW)
    ct = lambda c: c.reshape(nlb, S5_GPL, S5_GROUP, S5_STATE).transpose(0, 2, 1, 3).reshape(nlb, S5_GROUP, S5_SW)
    par = lambda r, w: pl.BlockSpec((1, r, w), lambda i, j: (i, 0, 0))
    kdim = S5_BLK * LANES
    rows = (S5_CHUNK // S5_BLK) * batch
    io = pl.BlockSpec((batch, S5_CHUNK, LANES), lambda i, j: (0, j, i))
    return pl.pallas_call(
        _s5_body,
        grid=(nlb, seq // S5_CHUNK),
        in_specs=[io, par(1, S5_SW), par(1, S5_SW), par(1, S5_SW), par(S5_GROUP, S5_SW),
                  par(S5_GROUP, S5_SW), par(S5_GROUP, S5_SW), par(S5_GROUP, S5_SW), par(1, LANES)],
        out_specs=io,
        out_shape=jax.ShapeDtypeStruct((batch, seq, width), F32),
        scratch_shapes=[
            pltpu.VMEM((kdim, 2 * S5_SW), BF16),
            pltpu.VMEM((kdim, 2 * S5_SW), BF16),
            pltpu.VMEM((kdim, kdim), BF16),
            pltpu.VMEM((SUBLANES, 2 * S5_SW), F32),
            pltpu.VMEM((rows, kdim), BF16),
            pltpu.VMEM((2 * S5_SW // LANES, batch * S5_XROWS, LANES), F32),
            pltpu.VMEM((2 * S5_SW // LANES, batch * S5_XROWS, LANES), F32),
            pltpu.VMEM((SUBLANES, 2 * S5_SW), F32),
        ],
        compiler_params=_cparams(2),
        name="s5",
    )(u3, lamr, lami, ldt, bt(b_re), bt(b_im), ct(c_re), ct(c_im), d_skip.reshape(nlb, 1, LANES))


def _even_out_body(yb_ref, h_ref, o_ref, z_ref, x_ref, hg_ref, gluw_ref, glub_ref, wout_ref,
                   out_ref):
    half = h_ref.shape[1]
    yg = yb_ref[...]
    s = _dot(yg.astype(BF16), gluw_ref[...]) + glub_ref[...]
    acc = x_ref[...]
    for h in range(M_HEADS):
        vs = slice(h * M_DV, (h + 1) * M_DV)
        og = (jax.nn.sigmoid(o_ref[:, vs]) * h_ref[:, vs]).astype(F32)
        y_h = _rmsnorm(og, hg_ref[:, vs]).astype(BF16) * _silu(z_ref[:, vs])
        acc = acc + _dot(y_h, wout_ref[vs, :])
    for j in range(0, half, M_DV):
        cs = slice(j, j + M_DV)
        zs = slice(half + j, half + j + M_DV)
        y_j = (yg[:, cs] * jax.nn.sigmoid(s[:, cs])).astype(BF16) * _silu(z_ref[:, zs])
        acc = acc + _dot(y_j, wout_ref[zs, :])
    out_ref[...] = acc


def _even_out(yb, hm, o, z, xf, head_g, glu_w, glu_b, w_out):
    t, d = xf.shape
    half = hm.shape[1]
    tok = lambda i: (i, 0)
    const = lambda i: (0, 0)
    return pl.pallas_call(
        _even_out_body,
        grid=(t // TOKEN_TILE,),
        in_specs=[
            pl.BlockSpec((TOKEN_TILE, half), tok),
            pl.BlockSpec((TOKEN_TILE, half), tok),
            pl.BlockSpec((TOKEN_TILE, half), tok),
            pl.BlockSpec((TOKEN_TILE, 2 * half), tok),
            pl.BlockSpec((TOKEN_TILE, d), tok),
            pl.BlockSpec((1, half), const),
            pl.BlockSpec((half, half), const),
            pl.BlockSpec((1, half), const),
            pl.BlockSpec((2 * half, d), const),
        ],
        out_specs=pl.BlockSpec((TOKEN_TILE, d), tok),
        out_shape=jax.ShapeDtypeStruct((t, d), F32),
        compiler_params=_cparams(1),
        name="even_out",
    )(yb, hm, o, z, xf, head_g.reshape(1, half), glu_w, glu_b.reshape(1, half), w_out)


def _proj_odd_body(x_ref, g_ref, w_ref, wg_ref, wa_ref, ba_ref, q_ref, k_ref, v_ref, z_ref, bc_ref):
    hb = _rmsnorm(x_ref[...], g_ref[...]).astype(BF16)
    rb = _dot(hb, wg_ref[...]).astype(BF16)
    tril = _tri_ones(G_CHUNK, True)
    plain = []
    col0 = 0
    for o_r in (q_ref, k_ref, v_ref, z_ref):
        plain += [(o_r, col0 + c, c) for c in range(0, o_r.shape[1], PROJ_TN)]
        col0 += o_r.shape[1]
    width = bc_ref.shape[1]
    for j in range(0, width, PROJ_TN):
        for k in range(len(plain) * j // width, len(plain) * (j + PROJ_TN) // width):
            dst, c_in, c_out = plain[k]
            dst[:, c_out:c_out + PROJ_TN] = _dot(hb, w_ref[:, c_in:c_in + PROJ_TN]).astype(dst.dtype)
        cols = slice(j, j + PROJ_TN)
        pre = _dot(rb, wa_ref[:, cols].astype(BF16)) + ba_ref[:, cols]
        la = (_log_sigmoid(pre) * (1.0 / G_TAU)).astype(BF16)
        for c in range(0, TOKEN_TILE, G_CHUNK):
            rows = slice(c, c + G_CHUNK)
            bc_ref[rows, cols] = _dot(tril, la[rows])


def _layer1_body(x_ref, g_ref, w_ref, wg_ref, wa_ref, ba_ref, hg_ref, wout_ref, gf_ref, out_ref,
                 q_s, k_s, v_s, z_s, bc_s, y_s, s_ref, *, tiles_per_seq):
    L = G_CHUNK

    @pl.when(lax.rem(pl.program_id(0), tiles_per_seq) == 0)
    def _init():
        s_ref[...] = jnp.zeros_like(s_ref)

    _proj_odd_body(x_ref, g_ref, w_ref, wg_ref, wa_ref, ba_ref, q_s, k_s, v_s, z_s, bc_s)

    row = lax.broadcasted_iota(jnp.int32, (L, L), 0)
    col = lax.broadcasted_iota(jnp.int32, (L, L), 1)
    causal = row >= col
    mid = L // 2 - 1

    for c, h in [(c, h) for c in range(0, TOKEN_TILE, 2 * L) for h in range(G_HEADS)]:
        ks = slice(h * G_DK, (h + 1) * G_DK)
        vs = slice(h * G_DV, (h + 1) * G_DV)
        r1, r2 = slice(c, c + L), slice(c + L, c + 2 * L)
        s_prev = s_ref[h]
        s_bf = s_prev.astype(BF16)
        v_p = v_s[c:c + 2 * L, vs]

        def prep(r):
            b = bc_s[r, ks]
            bm = b[mid:mid + 1, :]
            g = b[L - 1:L, :]
            e1 = jnp.exp(b - bm)
            qt = q_s[r, ks].astype(F32) * e1
            kt = k_s[r, ks].astype(F32) * (1.0 / e1)
            attn = jnp.where(causal, _dot_nt(qt.astype(BF16), kt.astype(BF16)), 0.0).astype(BF16)
            return attn, qt * jnp.exp(bm), kt * jnp.exp(g - bm), jnp.exp(g)

        a11, qi1, ke1, eg1 = prep(r1)
        a22, qi2, ke2, eg2 = prep(r2)
        qi2b = qi2.astype(BF16)
        a21 = _dot_nt(qi2b, ke1.astype(BF16)).astype(BF16)
        o1 = _dot(a11, v_p[0:L]) + _dot(qi1.astype(BF16), s_bf)
        o2 = _dot(jnp.concatenate([a21, a22], axis=1), v_p) + _dot((qi2 * eg1).astype(BF16), s_bf)
        y_s[r1, vs] = (_rmsnorm(o1, hg_ref[:, vs]) * _silu(z_s[r1, vs].astype(F32))).astype(BF16)
        y_s[r2, vs] = (_rmsnorm(o2, hg_ref[:, vs]) * _silu(z_s[r2, vs].astype(F32))).astype(BF16)
        ke_t = jnp.concatenate([(ke1 * eg2).T, ke2.T], axis=1).astype(BF16)
        g_col = jnp.broadcast_to(eg1 * eg2, (LANES, G_DK)).T[:, 0:1]
        s_ref[h] = g_col * s_prev + _dot(ke_t, v_p)

    x = x_ref[...] + _dot(y_s[...], wout_ref[...])
    out_ref[...] = _rmsnorm(x, gf_ref[...])


def _layer1(x1, g, wm, wg, w_alpha, b_alpha, head_g, w_out, gf, seq):
    t, d = x1.shape
    dk = G_HEADS * G_DK
    assert seq % TOKEN_TILE == 0 and TOKEN_TILE % G_CHUNK == 0
    tok = lambda i: (i, 0)
    const = lambda i: (0, 0)
    resident = lambda shape: pl.BlockSpec(shape, const, pipeline_mode=pl.Buffered(1))
    slots = lambda n, dt: pltpu.VMEM((TOKEN_TILE, n), dt)
    return pl.pallas_call(
        functools.partial(_layer1_body, tiles_per_seq=seq // TOKEN_TILE),
        grid=(t // TOKEN_TILE,),
        in_specs=[
            pl.BlockSpec((TOKEN_TILE, d), tok),
            pl.BlockSpec((1, d), const),
            resident((d, wm.shape[1])),
            pl.BlockSpec((d, G_RANK_PAD), const),
            pl.BlockSpec((G_RANK_PAD, dk), const),
            pl.BlockSpec((1, dk), const),
            pl.BlockSpec((1, D_MIX), const),
            resident((D_MIX, d)),
            pl.BlockSpec((1, d), const),
        ],
        out_specs=pl.BlockSpec((TOKEN_TILE, d), tok),
        out_shape=jax.ShapeDtypeStruct((t, d), F32),
        scratch_shapes=[
            slots(dk, BF16),
            slots(dk, BF16),
            slots(D_MIX, BF16),
            slots(D_MIX, BF16),
            slots(dk, F32),
            slots(D_MIX, BF16),
            pltpu.VMEM((G_HEADS, G_DK, G_DV), F32),
        ],
        compiler_params=_cparams(1),
        name="layer1",
    )(x1, g.reshape(1, d), wm, wg, w_alpha, b_alpha.reshape(1, dk), head_g.reshape(1, D_MIX), w_out,
      gf.reshape(1, d))


def kernel(x, norm_g, final_norm_g, ev_w_in, ev_conv_w, ev_conv_b, ev_i_bias, ev_f_bias, ev_head_g,
           s5_lam_re, s5_lam_im, s5_log_dt, s5_b_re, s5_b_im, s5_c_re, s5_c_im, s5_d, s5_glu_w,
           s5_glu_b, ev_w_out, od_w_in, gla_w_alpha, gla_b_alpha, gla_head_g, od_w_out):
    batch, seq, d = x.shape
    t = batch * seq
    xf = x.reshape(t, d)
    padc = lambda a: jnp.pad(a, ((0, 0), (0, LANES - a.shape[1])))

    wt = jnp.swapaxes(ev_w_in, 1, 2).reshape(ev_w_in.shape[2], d)
    g0 = 2 * M_HEADS * M_DK + 2 * M_HEADS * M_DV
    gi = g0 + M_HEADS
    gf = gi + M_HEADS
    half = D_MIX // 2
    w_qkvo, w_uz, wg = _cast_weights(wt, (0, gf), (g0, half + D_MIX), ((g0, M_HEADS), (gi, M_HEADS)))
    qk, v, o, u, z, gates = _proj_even(xf, norm_g[0], w_qkvo, w_uz, wg, ev_conv_w[0], ev_conv_b[0], seq)
    gbias = jnp.concatenate([padc(ev_i_bias), padc(ev_f_bias)], axis=1)
    hm = _mlstm(qk, v, gates, gbias, batch, seq)

    y3 = _s5(u.reshape(batch, seq, half), s5_lam_re[0], s5_lam_im[0], s5_log_dt[0], s5_b_re[0],
             s5_b_im[0], s5_c_re[0], s5_c_im[0], s5_d[0])
    yb = y3.reshape(t, half)
    x1 = _even_out(yb, hm, o, z, xf, ev_head_g[0], s5_glu_w[0].astype(BF16), s5_glu_b[0],
                   ev_w_out[0].astype(BF16))

    wt = jnp.swapaxes(od_w_in, 1, 2).reshape(od_w_in.shape[2], d)
    n_main = 2 * G_HEADS * G_DK + 2 * D_MIX
    wa = jnp.pad(gla_w_alpha[0], ((0, G_RANK_PAD - gla_w_alpha.shape[1]), (0, 0)))
    assert _log2(G_DK) % 2 == 0
    wm, wr = _cast_weights(wt, (0,), (n_main,), ((n_main, wt.shape[0] - n_main),),
                           scaled_cols=G_HEADS * G_DK, scale=G_DK ** -0.5)
    out = _layer1(x1, norm_g[1], wm, wr, wa, gla_b_alpha[0],
                  gla_head_g[0], od_w_out[0].astype(BF16), final_norm_g, seq)
    return out.reshape(batch, seq, d)
```

```python
import functools

import jax
import jax.numpy as jnp
from jax import lax
from jax.experimental import pallas as pl
from jax.experimental.pallas import tpu as pltpu

F32 = jnp.float32
BF16 = jnp.bfloat16

EPS = 1e-6
D_MODEL = 1024
D_MIX = 2 * D_MODEL
M_HEADS = 4
M_DK = 128
M_DV = 256
M_QK = 2 * M_HEADS * M_DK
CONV_WIDTH = 4
M_CHUNK = 256
M_SUB = 4
S5_GROUP = 16
S5_STATE = 64
S5_BLK = 8
S5_CHUNK = 1024
G_HEADS = 4
G_DK = 256
G_DV = 512
G_TAU = 16.0
G_CHUNK = 128
G_RANK_PAD = 128

LANES = 128
SUBLANES = 8
HALO = 16
S5_GPL = LANES // S5_GROUP
S5_SW = S5_GPL * S5_STATE
S5_XROWS = S5_CHUNK // S5_BLK + SUBLANES
TOKEN_TILE = 512
PROJ_TN = 256
VMEM_LIMIT = 56 * 1024 * 1024


def _cparams(n_grid):
    return pltpu.CompilerParams(
        dimension_semantics=("arbitrary",) * n_grid, vmem_limit_bytes=VMEM_LIMIT)


def _log2(n):
    assert n & (n - 1) == 0
    return n.bit_length() - 1


def _log_sigmoid(x):
    return jnp.minimum(x, 0.0) - jnp.log(1.0 + jnp.exp(-jnp.abs(x)))


def _silu(x):
    return x * jax.nn.sigmoid(x)


def _split_hi_lo(x):
    hi = x.astype(BF16)
    lo = (x - hi.astype(F32)).astype(BF16)
    return hi, lo


def _dot(a, b):
    return jnp.dot(a, b, preferred_element_type=F32)


def _dot_nt(a, b, precision=None):
    return lax.dot_general(a, b, (((1,), (1,)), ((), ())), precision=precision,
                           preferred_element_type=F32)


def _tri_ones(n, lower):
    row = lax.broadcasted_iota(jnp.int32, (n, n), 0)
    col = lax.broadcasted_iota(jnp.int32, (n, n), 1)
    keep = (row >= col) if lower else (row <= col)
    return jnp.where(keep, 1.0, 0.0).astype(BF16)


def _rmsnorm(x, g):
    return x * lax.rsqrt(jnp.mean(x * x, axis=-1, keepdims=True) + EPS) * g


CAST_ROWS = 128


def _cast_body(wt_ref, *out_refs, starts, narrow, scaled_cols, scale):
    for o_ref, c0 in zip(out_refs[:-1], starts):
        n = o_ref.shape[1]
        val = wt_ref[c0:c0 + n, :]
        if scaled_cols and c0 == 0:
            col = lax.broadcasted_iota(jnp.int32, (n, 1), 0)
            val = val * jnp.where(col < scaled_cols, scale, 1.0)
        o_ref[...] = val.T.astype(BF16)
    lane = lax.broadcasted_iota(jnp.int32, (1, LANES), 1)
    for blk, (c0, n) in enumerate(narrow):
        start = (c0 // SUBLANES) * SUBLANES
        take = -(-(c0 - start + n) // SUBLANES) * SUBLANES
        rows = jnp.concatenate([wt_ref[start:start + take, :], jnp.zeros((LANES - take, CAST_ROWS), F32)], axis=0)
        slab = rows.T
        if c0 != start:
            slab = pltpu.roll(slab, LANES - (c0 - start), axis=1)
        out_refs[-1][:, blk * LANES:(blk + 1) * LANES] = jnp.where(lane < n, slab, 0.0).astype(BF16)


def _cast_weights(wt, starts, widths, narrow, scaled_cols=0, scale=1.0):
    cols, rows = wt.shape
    assert CAST_ROWS == LANES
    widths = tuple(widths) + (len(narrow) * LANES,)
    return pl.pallas_call(
        functools.partial(_cast_body, starts=starts, narrow=narrow, scaled_cols=scaled_cols, scale=scale),
        grid=(rows // CAST_ROWS,),
        in_specs=[pl.BlockSpec((cols, CAST_ROWS), lambda i: (0, i))],
        out_specs=[pl.BlockSpec((CAST_ROWS, n), lambda i: (i, 0)) for n in widths],
        out_shape=[jax.ShapeDtypeStruct((rows, n), BF16) for n in widths],
        compiler_params=_cparams(1),
        name="cast_weights",
    )(wt)


def _proj_even_body(x_ref, xh_ref, g_ref, w_ref, wb_ref, wg_ref, convw_ref, convb_ref,
                    qk_ref, v_ref, o_ref, u_ref, z_ref, gates_ref, ext_ref, *, tiles_per_seq):
    hb = _rmsnorm(x_ref[...], g_ref[...]).astype(BF16)
    hh = _rmsnorm(xh_ref[...], g_ref[...]).astype(BF16)
    seq_start = lax.rem(pl.program_id(0), tiles_per_seq) == 0
    base = HALO - (CONV_WIDTH - 1)
    lane = lax.broadcasted_iota(jnp.int32, (1, PROJ_TN), 1)
    plain = []
    for wr, col0, o_refs in ((w_ref, M_QK, (v_ref, o_ref)), (wb_ref, 0, (u_ref, z_ref))):
        for o_r in o_refs:
            plain += [(wr, o_r, col0 + c, c) for c in range(0, o_r.shape[1], PROJ_TN)]
            col0 += o_r.shape[1]
    for j in range(0, M_QK, PROJ_TN):
        cols = slice(j, j + PROJ_TN)
        wj = w_ref[:, cols]
        ext_ref[HALO:HALO + TOKEN_TILE, :] = _dot(hb, wj)
        ext_ref[0:HALO, :] = jnp.where(seq_start, 0.0, _dot(hh, wj))
        for k in range(len(plain) * j // M_QK, len(plain) * (j + PROJ_TN) // M_QK):
            wr, dst, c_in, c_out = plain[k]
            dst[:, c_out:c_out + PROJ_TN] = _dot(hb, wr[:, c_in:c_in + PROJ_TN]).astype(dst.dtype)
        last = CONV_WIDTH - 1
        acc = convb_ref[:, cols] + convw_ref[last:last + 1, cols] * ext_ref[HALO:HALO + TOKEN_TILE, :]
        for i in range(last):
            acc = acc + convw_ref[i:i + 1, cols] * ext_ref[base + i:base + i + TOKEN_TILE, :]
        scale = jnp.where(lane + j < M_HEADS * M_DK, M_DK ** -0.5, 1.0)
        qk_ref[:, cols] = (_silu(acc) * scale).astype(BF16)
    gates_ref[...] = _dot(hb, wg_ref[...])


def _proj_even(xf, g, wm, wb, wg, conv_w, conv_b, seq):
    t, d = xf.shape
    half = D_MIX // 2
    assert seq % TOKEN_TILE == 0 and TOKEN_TILE % HALO == 0
    tok = lambda i: (i, 0)
    const = lambda i: (0, 0)
    per_halo = TOKEN_TILE // HALO
    widths = (M_QK, half, half, half, D_MIX)
    dtypes = (BF16, BF16, BF16, F32, BF16)
    out_shape = [jax.ShapeDtypeStruct((t, n), dt) for n, dt in zip(widths, dtypes)]
    ngate = wg.shape[1]
    out_shape.append(jax.ShapeDtypeStruct((t, ngate), F32))
    out_specs = [pl.BlockSpec((TOKEN_TILE, n), tok) for n in widths]
    out_specs.append(pl.BlockSpec((TOKEN_TILE, ngate), tok))
    return pl.pallas_call(
        functools.partial(_proj_even_body, tiles_per_seq=seq // TOKEN_TILE),
        grid=(t // TOKEN_TILE,),
        in_specs=[
            pl.BlockSpec((TOKEN_TILE, d), tok),
            pl.BlockSpec((HALO, d), lambda i: (jnp.maximum(i * per_halo - 1, 0), 0)),
            pl.BlockSpec((1, d), const),
            pl.BlockSpec((d, wm.shape[1]), const, pipeline_mode=pl.Buffered(1)),
            pl.BlockSpec((d, wb.shape[1]), const, pipeline_mode=pl.Buffered(1)),
            pl.BlockSpec((d, ngate), const),
            pl.BlockSpec((CONV_WIDTH, M_QK), const),
            pl.BlockSpec((1, M_QK), const),
        ],
        out_specs=out_specs,
        out_shape=out_shape,
        scratch_shapes=[pltpu.VMEM((HALO + TOKEN_TILE, PROJ_TN), F32)],
        compiler_params=_cparams(1),
        name="proj_even",
    )(xf, xf, g.reshape(1, d), wm, wb, wg, conv_w, conv_b.reshape(1, M_QK))


def _mlstm_body(qk_ref, v_ref, gates_ref, gbias_ref, out_ref, c_ref, n_ref, m_ref):
    @pl.when(pl.program_id(1) == 0)
    def _init():
        c_ref[...] = jnp.zeros_like(c_ref)
        n_ref[...] = jnp.zeros_like(n_ref)
        m_ref[...] = jnp.zeros_like(m_ref)

    for sub in range(M_SUB):
        _mlstm_chunk(slice(sub * M_CHUNK, (sub + 1) * M_CHUNK), qk_ref, v_ref, gates_ref, gbias_ref,
                     out_ref, c_ref, n_ref, m_ref)


def _mlstm_chunk(r, qk_ref, v_ref, gates_ref, gbias_ref, out_ref, c_ref, n_ref, m_ref):
    L = M_CHUNK
    gt = gates_ref[r, :] + gbias_ref[...]
    ipre = gt[:, 0:LANES]
    logf = _log_sigmoid(gt[:, LANES:2 * LANES]).astype(BF16)
    b = _dot(_tri_ones(L, True), logf)
    w = ipre - b
    rows = lax.broadcasted_iota(jnp.int32, (L, LANES), 0)
    cm = w
    k = 1
    while k < L:
        cm = jnp.maximum(cm, jnp.where(rows >= k, pltpu.roll(cm, k, axis=0), -jnp.inf))
        k *= 2

    def replicate(x):
        return jnp.concatenate([jnp.broadcast_to(x[:, h:h + 1], (L, LANES)) for h in range(M_HEADS)], axis=1)

    b_rep = replicate(b)
    w_rep = replicate(w)
    cm_rep = replicate(cm)
    pick_r = lax.broadcasted_iota(jnp.int32, (M_HEADS * SUBLANES, LANES), 0)
    pick_c = lax.broadcasted_iota(jnp.int32, (M_HEADS * SUBLANES, LANES), 1)
    pick = jnp.where(lax.shift_right_logical(pick_r, _log2(SUBLANES)) == pick_c, 1.0, 0.0).astype(BF16)
    w_hi, w_lo = _split_hi_lo(w)
    w_row = _dot_nt(pick, w_hi) + _dot_nt(pick, w_lo)

    trow = lax.broadcasted_iota(jnp.int32, (L, LANES), 0)
    tcol = lax.broadcasted_iota(jnp.int32, (L, LANES), 1)
    ones = jnp.ones((L, LANES), BF16)

    for h in range(M_HEADS):
        ks = slice(h * M_DK, (h + 1) * M_DK)
        ks2 = slice(M_HEADS * M_DK + h * M_DK, M_HEADS * M_DK + (h + 1) * M_DK)
        vs = slice(h * M_DV, (h + 1) * M_DV)
        hs = slice(h * LANES, (h + 1) * LANES)
        m_prev = m_ref[h, 0:1, :]
        c_prev = c_ref[h]
        n_prev = n_ref[h]
        big_m = jnp.maximum(m_prev, cm_rep[:, hs])
        w_inter = jnp.exp(m_prev - big_m)
        wr = w_row[h * SUBLANES:h * SUBLANES + 1, :]

        qb = qk_ref[r, ks]
        kb = qk_ref[r, ks2]
        v_h = v_ref[r, vs]
        s = _dot_nt(qb, kb)
        sc = jnp.concatenate(
            [jnp.where(trow >= tcol + j, jnp.exp(wr[:, j:j + LANES] - big_m), 0.0) * s[:, j:j + LANES]
             for j in range(0, L, LANES)], axis=1).astype(BF16)
        q_c = _dot(qb, c_prev.astype(BF16))
        den = _dot(sc, ones) + w_inter * _dot(qb, n_prev.astype(BF16))
        inv = 1.0 / jnp.maximum(jnp.abs(den), jnp.exp(-(b_rep[:, hs] + big_m)))
        num = _dot(sc, v_h)
        out_ref[r, vs] = jnp.concatenate(
            [(num[:, j:j + LANES] + w_inter * q_c[:, j:j + LANES]) * inv for j in range(0, M_DV, LANES)],
            axis=1).astype(BF16)

        g = b_rep[L - 1:L, hs]
        cm_last = cm_rep[L - 1:L, hs]
        m_last = big_m[L - 1:L, :]
        kw_t = (kb.astype(F32) * jnp.exp(w_rep[:, hs] - cm_last)).T.astype(BF16)
        s_prev = jnp.exp(m_prev - m_last)
        s_loc = jnp.exp(cm_last - m_last)
        c_ref[h] = (jnp.concatenate([s_prev] * (M_DV // LANES), axis=1) * c_prev
                    + jnp.concatenate([s_loc] * (M_DV // LANES), axis=1) * _dot(kw_t, v_h))
        n_ref[h] = s_prev * n_prev + s_loc * _dot(kw_t, ones)
        m_ref[h] = jnp.broadcast_to(g + m_last, (SUBLANES, LANES))


def _mlstm(qk, v, gates, gbias, batch, seq):
    t = batch * seq
    L = M_CHUNK * M_SUB
    nc = seq // L
    dv = M_HEADS * M_DV
    tok = lambda b, c: (b * nc + c, 0)
    const = lambda b, c: (0, 0)
    return pl.pallas_call(
        _mlstm_body,
        grid=(batch, nc),
        in_specs=[
            pl.BlockSpec((L, M_QK), tok),
            pl.BlockSpec((L, dv), tok),
            pl.BlockSpec((L, 2 * LANES), tok),
            pl.BlockSpec((1, 2 * LANES), const),
        ],
        out_specs=pl.BlockSpec((L, dv), tok),
        out_shape=jax.ShapeDtypeStruct((t, dv), BF16),
        scratch_shapes=[
            pltpu.VMEM((M_HEADS, M_DK, M_DV), F32),
            pltpu.VMEM((M_HEADS, M_DK, LANES), F32),
            pltpu.VMEM((M_HEADS, SUBLANES, LANES), F32),
        ],
        compiler_params=_cparams(2),
        name="mlstm",
    )(qk, v, gates, gbias)


def _gelu_tanh(x):
    c = 0.7978845608028654
    half = 0.5 * x
    return half + half * jnp.tanh(x * (c + (c * 0.044715) * (x * x)))


def _s5_build_operators(lamr_ref, lami_ref, ldt_ref, btr_ref, bti_ref, ctr_ref, cti_ref, d_ref,
                        pbig_ref, qbig_ref, mbig_ref, a_ref):
    lr = lamr_ref[0]
    li = lami_ref[0]
    dt = jnp.exp(ldt_ref[0])
    zr = lr * dt
    th = li * dt
    er = jnp.exp(zr)
    ar = er * jnp.cos(th)
    ai = er * jnp.sin(th)
    den = lr * lr + li * li
    beta_r = ((ar - 1.0) * lr + ai * li) / den
    beta_i = (ai * lr - (ar - 1.0) * li) / den
    btr = btr_ref[0]
    bti = bti_ref[0]
    bbr = btr * beta_r - bti * beta_i
    bbi = btr * beta_i + bti * beta_r
    ctr = ctr_ref[0]
    cti = cti_ref[0]

    row_g = lax.shift_right_logical(lax.broadcasted_iota(jnp.int32, (LANES, S5_SW), 0), _log2(S5_GROUP))
    lane_g = lax.shift_right_logical(lax.broadcasted_iota(jnp.int32, (LANES, S5_SW), 1), _log2(S5_STATE))
    same_group = row_g == lane_g

    def expand(x16):
        return jnp.where(same_group, jnp.concatenate([x16] * S5_GPL, axis=0), 0.0)

    def power(k):
        e = jnp.exp(float(k) * zr)
        return e * jnp.cos(float(k) * th), e * jnp.sin(float(k) * th)

    for s in range(S5_BLK):
        rows = slice(s * LANES, (s + 1) * LANES)
        pr, pi = power(S5_BLK - 1 - s)
        pbig_ref[rows, 0:S5_SW] = expand(pr * bbr - pi * bbi).astype(BF16)
        pbig_ref[rows, S5_SW:2 * S5_SW] = expand(pr * bbi + pi * bbr).astype(BF16)
        pr, pi = power(s + 1)
        qbig_ref[rows, 0:S5_SW] = expand(ctr * pr - cti * pi).astype(BF16)
        qbig_ref[rows, S5_SW:2 * S5_SW] = expand(-(ctr * pi + cti * pr)).astype(BF16)

    cb = jnp.concatenate([expand(ctr), expand(-cti)], axis=1)
    r128 = lax.broadcasted_iota(jnp.int32, (LANES, LANES), 0)
    c128 = lax.broadcasted_iota(jnp.int32, (LANES, LANES), 1)
    zero_blk = jnp.zeros((LANES, LANES), BF16)
    ab_lags = []
    for lag in range(S5_BLK):
        pr, pi = power(lag)
        ab_lags.append(jnp.concatenate([expand(pr * bbr - pi * bbi), expand(pr * bbi + pi * bbr)], axis=1))
    ab_hi, ab_lo = _split_hi_lo(jnp.concatenate(ab_lags, axis=0))
    cb_hi, cb_lo = _split_hi_lo(cb)
    v_lags = _dot_nt(ab_hi, cb_hi) + _dot_nt(ab_hi, cb_lo) + _dot_nt(ab_lo, cb_hi)
    for lag in range(S5_BLK):
        v = v_lags[lag * LANES:(lag + 1) * LANES, :]
        if lag == 0:
            v = v + jnp.where(r128 == c128, d_ref[0], 0.0)
        vb = v.astype(BF16)
        for s in range(S5_BLK - lag):
            t = s + lag
            mbig_ref[s * LANES:(s + 1) * LANES, t * LANES:(t + 1) * LANES] = vb
            if lag > 0:
                mbig_ref[t * LANES:(t + 1) * LANES, s * LANES:(s + 1) * LANES] = zero_blk

    pr, pi = power(S5_BLK)
    a_ref[:, 0:S5_SW] = jnp.broadcast_to(pr, (SUBLANES, S5_SW))
    a_ref[:, S5_SW:2 * S5_SW] = jnp.broadcast_to(pi, (SUBLANES, S5_SW))


def _s5_body(u_ref, lamr_ref, lami_ref, ldt_ref, btr_ref, bti_ref, ctr_ref, cti_ref, d_ref, y_ref,
             pbig_ref, qbig_ref, mbig_ref, a_ref, ucat_ref, x_ref, xp_ref, st_ref):
    nblk = S5_CHUNK // S5_BLK
    batch = u_ref.shape[0]

    @pl.when(pl.program_id(1) == 0)
    def _setup():
        _s5_build_operators(lamr_ref, lami_ref, ldt_ref, btr_ref, bti_ref, ctr_ref, cti_ref, d_ref,
                            pbig_ref, qbig_ref, mbig_ref, a_ref)
        st_ref[...] = jnp.zeros_like(st_ref)

    for b in range(batch):
        for s in range(S5_BLK):
            piece = u_ref[b, pl.ds(s, nblk, stride=S5_BLK), :]
            ucat_ref[b * nblk:(b + 1) * nblk, s * LANES:(s + 1) * LANES] = piece.astype(BF16)
    nslab = 2 * S5_SW // LANES
    half = nslab // 2
    xloc = _dot(ucat_ref[...], pbig_ref[...])
    for c in range(nslab):
        for b in range(batch):
            x_ref[c, b * S5_XROWS:b * S5_XROWS + nblk, :] = (
                xloc[b * nblk:(b + 1) * nblk, c * LANES:(c + 1) * LANES])

    lanes = lambda ref, c: ref[:, c * LANES:(c + 1) * LANES]
    ar = [lanes(a_ref, c) for c in range(half)]
    ai = [lanes(a_ref, half + c) for c in range(half)]
    xr = [lanes(st_ref, c) for c in range(half)]
    xi = [lanes(st_ref, half + c) for c in range(half)]
    for blk in range(nblk):
        r = pl.ds(blk, batch, stride=S5_XROWS)
        for c in range(half):
            xp_ref[c, r, :] = xr[c]
            xp_ref[half + c, r, :] = xi[c]
            nr = ar[c] * xr[c] - ai[c] * xi[c] + x_ref[c, r, :]
            ni = ar[c] * xi[c] + ai[c] * xr[c] + x_ref[half + c, r, :]
            xr[c], xi[c] = nr, ni
    for c in range(half):
        st_ref[:, c * LANES:(c + 1) * LANES] = xr[c]
        st_ref[:, (half + c) * LANES:(half + c + 1) * LANES] = xi[c]

    xpb = jnp.concatenate(
        [jnp.concatenate([xp_ref[c, b * S5_XROWS:b * S5_XROWS + nblk, :].astype(BF16)
                          for c in range(nslab)], axis=1) for b in range(batch)], axis=0)
    width = 2 * LANES
    for nb in range(S5_BLK // 2):
        kk = (2 * nb + 2) * LANES
        cols = slice(nb * width, (nb + 1) * width)
        y = _dot(ucat_ref[:, 0:kk], mbig_ref[0:kk, cols]) + _dot_nt(xpb, qbig_ref[cols, :])
        y = _gelu_tanh(y)
        for tt in range(2):
            t = 2 * nb + tt
            for b in range(batch):
                y_ref[b, pl.ds(t, nblk, stride=S5_BLK), :] = (
                    y[b * nblk:(b + 1) * nblk, tt * LANES:(tt + 1) * LANES])


def _s5(u3, lam_re, lam_im, log_dt, b_re, b_im, c_re, c_im, d_skip):
    batch, seq, width = u3.shape
    nlb = width // LANES
    assert batch == SUBLANES and seq % S5_CHUNK == 0
    lamr = lam_re.reshape(nlb, 1, S5_SW)
    lami = lam_im.reshape(nlb, 1, S5_SW)
    ldt = jnp.repeat(log_dt, S5_STATE).reshape(nlb, 1, S5_SW)
    bt = lambda b: b.reshape(nlb, S5_GPL, S5_STATE, S5_GROUP).transpose(0, 3, 1, 2).reshape(nlb, S5_GROUP, S5_SW)
    ct = lambda c: c.reshape(nlb, S5_GPL, S5_GROUP, S5_STATE).transpose(0, 2, 1, 3).reshape(nlb, S5_GROUP, S5_SW)
    par = lambda r, w: pl.BlockSpec((1, r, w), lambda i, j: (i, 0, 0))
    kdim = S5_BLK * LANES
    rows = (S5_CHUNK // S5_BLK) * batch
    io = pl.BlockSpec((batch, S5_CHUNK, LANES), lambda i, j: (0, j, i))
    return pl.pallas_call(
        _s5_body,
        grid=(nlb, seq // S5_CHUNK),
        in_specs=[io, par(1, S5_SW), par(1, S5_SW), par(1, S5_SW), par(S5_GROUP, S5_SW),
                  par(S5_GROUP, S5_SW), par(S5_GROUP, S5_SW), par(S5_GROUP, S5_SW), par(1, LANES)],
        out_specs=io,
        out_shape=jax.ShapeDtypeStruct((batch, seq, width), F32),
        scratch_shapes=[
            pltpu.VMEM((kdim, 2 * S5_SW), BF16),
            pltpu.VMEM((kdim, 2 * S5_SW), BF16),
            pltpu.VMEM((kdim, kdim), BF16),
            pltpu.VMEM((SUBLANES, 2 * S5_SW), F32),
            pltpu.VMEM((rows, kdim), BF16),
            pltpu.VMEM((2 * S5_SW // LANES, batch * S5_XROWS, LANES), F32),
            pltpu.VMEM((2 * S5_SW // LANES, batch * S5_XROWS, LANES), F32),
            pltpu.VMEM((SUBLANES, 2 * S5_SW), F32),
        ],
        compiler_params=_cparams(2),
        name="s5",
    )(u3, lamr, lami, ldt, bt(b_re), bt(b_im), ct(c_re), ct(c_im), d_skip.reshape(nlb, 1, LANES))


def _even_out_body(yb_ref, h_ref, o_ref, z_ref, x_ref, hg_ref, gluw_ref, glub_ref, wout_ref,
                   out_ref):
    half = h_ref.shape[1]
    yg = yb_ref[...]
    s = _dot(yg.astype(BF16), gluw_ref[...]) + glub_ref[...]
    acc = x_ref[...]
    for h in range(M_HEADS):
        vs = slice(h * M_DV, (h + 1) * M_DV)
        og = (jax.nn.sigmoid(o_ref[:, vs]) * h_ref[:, vs]).astype(F32)
        y_h = _rmsnorm(og, hg_ref[:, vs]).astype(BF16) * _silu(z_ref[:, vs])
        acc = acc + _dot(y_h, wout_ref[vs, :])
    for j in range(0, half, M_DV):
        cs = slice(j, j + M_DV)
        zs = slice(half + j, half + j + M_DV)
        y_j = (yg[:, cs] * jax.nn.sigmoid(s[:, cs])).astype(BF16) * _silu(z_ref[:, zs])
        acc = acc + _dot(y_j, wout_ref[zs, :])
    out_ref[...] = acc


def _even_out(yb, hm, o, z, xf, head_g, glu_w, glu_b, w_out):
    t, d = xf.shape
    half = hm.shape[1]
    tok = lambda i: (i, 0)
    const = lambda i: (0, 0)
    return pl.pallas_call(
        _even_out_body,
        grid=(t // TOKEN_TILE,),
        in_specs=[
            pl.BlockSpec((TOKEN_TILE, half), tok),
            pl.BlockSpec((TOKEN_TILE, half), tok),
            pl.BlockSpec((TOKEN_TILE, half), tok),
            pl.BlockSpec((TOKEN_TILE, 2 * half), tok),
            pl.BlockSpec((TOKEN_TILE, d), tok),
            pl.BlockSpec((1, half), const),
            pl.BlockSpec((half, half), const),
            pl.BlockSpec((1, half), const),
            pl.BlockSpec((2 * half, d), const),
        ],
        out_specs=pl.BlockSpec((TOKEN_TILE, d), tok),
        out_shape=jax.ShapeDtypeStruct((t, d), F32),
        compiler_params=_cparams(1),
        name="even_out",
    )(yb, hm, o, z, xf, head_g.reshape(1, half), glu_w, glu_b.reshape(1, half), w_out)


def _proj_odd_body(x_ref, g_ref, w_ref, wg_ref, wa_ref, ba_ref, q_ref, k_ref, v_ref, z_ref, bc_ref):
    hb = _rmsnorm(x_ref[...], g_ref[...]).astype(BF16)
    rb = _dot(hb, wg_ref[...]).astype(BF16)
    tril = _tri_ones(G_CHUNK, True)
    plain = []
    col0 = 0
    for o_r in (q_ref, k_ref, v_ref, z_ref):
        plain += [(o_r, col0 + c, c) for c in range(0, o_r.shape[1], PROJ_TN)]
        col0 += o_r.shape[1]
    width = bc_ref.shape[1]
    for j in range(0, width, PROJ_TN):
        for k in range(len(plain) * j // width, len(plain) * (j + PROJ_TN) // width):
            dst, c_in, c_out = plain[k]
            dst[:, c_out:c_out + PROJ_TN] = _dot(hb, w_ref[:, c_in:c_in + PROJ_TN]).astype(dst.dtype)
        cols = slice(j, j + PROJ_TN)
        pre = _dot(rb, wa_ref[:, cols].astype(BF16)) + ba_ref[:, cols]
        la = (_log_sigmoid(pre) * (1.0 / G_TAU)).astype(BF16)
        for c in range(0, TOKEN_TILE, G_CHUNK):
            rows = slice(c, c + G_CHUNK)
            bc_ref[rows, cols] = _dot(tril, la[rows])


def _layer1_body(x_ref, g_ref, w_ref, wg_ref, wa_ref, ba_ref, hg_ref, wout_ref, gf_ref, out_ref,
                 q_s, k_s, v_s, z_s, bc_s, y_s, s_ref, *, tiles_per_seq):
    L = G_CHUNK

    @pl.when(lax.rem(pl.program_id(0), tiles_per_seq) == 0)
    def _init():
        s_ref[...] = jnp.zeros_like(s_ref)

    _proj_odd_body(x_ref, g_ref, w_ref, wg_ref, wa_ref, ba_ref, q_s, k_s, v_s, z_s, bc_s)

    row = lax.broadcasted_iota(jnp.int32, (L, L), 0)
    col = lax.broadcasted_iota(jnp.int32, (L, L), 1)
    causal = row >= col
    mid = L // 2 - 1

    for c, h in [(c, h) for c in range(0, TOKEN_TILE, 2 * L) for h in range(G_HEADS)]:
        ks = slice(h * G_DK, (h + 1) * G_DK)
        vs = slice(h * G_DV, (h + 1) * G_DV)
        r1, r2 = slice(c, c + L), slice(c + L, c + 2 * L)
        s_prev = s_ref[h]
        s_bf = s_prev.astype(BF16)
        v_p = v_s[c:c + 2 * L, vs]

        def prep(r):
            b = bc_s[r, ks]
            bm = b[mid:mid + 1, :]
            g = b[L - 1:L, :]
            e1 = jnp.exp(b - bm)
            qt = q_s[r, ks].astype(F32) * e1
            kt = k_s[r, ks].astype(F32) * (1.0 / e1)
            attn = jnp.where(causal, _dot_nt(qt.astype(BF16), kt.astype(BF16)), 0.0).astype(BF16)
            return attn, qt * jnp.exp(bm), kt * jnp.exp(g - bm), jnp.exp(g)

        a11, qi1, ke1, eg1 = prep(r1)
        a22, qi2, ke2, eg2 = prep(r2)
        qi2b = qi2.astype(BF16)
        a21 = _dot_nt(qi2b, ke1.astype(BF16)).astype(BF16)
        o1 = _dot(a11, v_p[0:L]) + _dot(qi1.astype(BF16), s_bf)
        o2 = _dot(jnp.concatenate([a21, a22], axis=1), v_p) + _dot((qi2 * eg1).astype(BF16), s_bf)
        y_s[r1, vs] = (_rmsnorm(o1, hg_ref[:, vs]) * _silu(z_s[r1, vs].astype(F32))).astype(BF16)
        y_s[r2, vs] = (_rmsnorm(o2, hg_ref[:, vs]) * _silu(z_s[r2, vs].astype(F32))).astype(BF16)
        ke_t = jnp.concatenate([(ke1 * eg2).T, ke2.T], axis=1).astype(BF16)
        g_col = jnp.broadcast_to(eg1 * eg2, (LANES, G_DK)).T[:, 0:1]
        s_ref[h] = g_col * s_prev + _dot(ke_t, v_p)

    x = x_ref[...] + _dot(y_s[...], wout_ref[...])
    out_ref[...] = _rmsnorm(x, gf_ref[...])


def _layer1(x1, g, wm, wg, w_alpha, b_alpha, head_g, w_out, gf, seq):
    t, d = x1.shape
    dk = G_HEADS * G_DK
    assert seq % TOKEN_TILE == 0 and TOKEN_TILE % G_CHUNK == 0
    tok = lambda i: (i, 0)
    const = lambda i: (0, 0)
    resident = lambda shape: pl.BlockSpec(shape, const, pipeline_mode=pl.Buffered(1))
    slots = lambda n, dt: pltpu.VMEM((TOKEN_TILE, n), dt)
    return pl.pallas_call(
        functools.partial(_layer1_body, tiles_per_seq=seq // TOKEN_TILE),
        grid=(t // TOKEN_TILE,),
        in_specs=[
            pl.BlockSpec((TOKEN_TILE, d), tok),
            pl.BlockSpec((1, d), const),
            resident((d, wm.shape[1])),
            pl.BlockSpec((d, G_RANK_PAD), const),
            pl.BlockSpec((G_RANK_PAD, dk), const),
            pl.BlockSpec((1, dk), const),
            pl.BlockSpec((1, D_MIX), const),
            resident((D_MIX, d)),
            pl.BlockSpec((1, d), const),
        ],
        out_specs=pl.BlockSpec((TOKEN_TILE, d), tok),
        out_shape=jax.ShapeDtypeStruct((t, d), F32),
        scratch_shapes=[
            slots(dk, BF16),
            slots(dk, BF16),
            slots(D_MIX, BF16),
            slots(D_MIX, BF16),
            slots(dk, F32),
            slots(D_MIX, BF16),
            pltpu.VMEM((G_HEADS, G_DK, G_DV), F32),
        ],
        compiler_params=_cparams(1),
        name="layer1",
    )(x1, g.reshape(1, d), wm, wg, w_alpha, b_alpha.reshape(1, dk), head_g.reshape(1, D_MIX), w_out,
      gf.reshape(1, d))


def kernel(x, norm_g, final_norm_g, ev_w_in, ev_conv_w, ev_conv_b, ev_i_bias, ev_f_bias, ev_head_g,
           s5_lam_re, s5_lam_im, s5_log_dt, s5_b_re, s5_b_im, s5_c_re, s5_c_im, s5_d, s5_glu_w,
           s5_glu_b, ev_w_out, od_w_in, gla_w_alpha, gla_b_alpha, gla_head_g, od_w_out):
    batch, seq, d = x.shape
    t = batch * seq
    xf = x.reshape(t, d)
    padc = lambda a: jnp.pad(a, ((0, 0), (0, LANES - a.shape[1])))

    wt = jnp.swapaxes(ev_w_in, 1, 2).reshape(ev_w_in.shape[2], d)
    g0 = 2 * M_HEADS * M_DK + 2 * M_HEADS * M_DV
    gi = g0 + M_HEADS
    gf = gi + M_HEADS
    half = D_MIX // 2
    w_qkvo, w_uz, wg = _cast_weights(wt, (0, gf), (g0, half + D_MIX), ((g0, M_HEADS), (gi, M_HEADS)))
    qk, v, o, u, z, gates = _proj_even(xf, norm_g[0], w_qkvo, w_uz, wg, ev_conv_w[0], ev_conv_b[0], seq)
    gbias = jnp.concatenate([padc(ev_i_bias), padc(ev_f_bias)], axis=1)
    hm = _mlstm(qk, v, gates, gbias, batch, seq)

    y3 = _s5(u.reshape(batch, seq, half), s5_lam_re[0], s5_lam_im[0], s5_log_dt[0], s5_b_re[0],
             s5_b_im[0], s5_c_re[0], s5_c_im[0], s5_d[0])
    yb = y3.reshape(t, half)
    x1 = _even_out(yb, hm, o, z, xf, ev_head_g[0], s5_glu_w[0].astype(BF16), s5_glu_b[0],
                   ev_w_out[0].astype(BF16))

    wt = jnp.swapaxes(od_w_in, 1, 2).reshape(od_w_in.shape[2], d)
    n_main = 2 * G_HEADS * G_DK + 2 * D_MIX
    wa = jnp.pad(gla_w_alpha[0], ((0, G_RANK_PAD - gla_w_alpha.shape[1]), (0, 0)))
    assert _log2(G_DK) % 2 == 0
    wm, wr = _cast_weights(wt, (0,), (n_main,), ((n_main, wt.shape[0] - n_main),),
                           scaled_cols=G_HEADS * G_DK, scale=G_DK ** -0.5)
    out = _layer1(x1, norm_g[1], wm, wr, wa, gla_b_alpha[0],
                  gla_head_g[0], od_w_out[0].astype(BF16), final_norm_g, seq)
    return out.reshape(batch, seq, d)
```

```python
import functools

import jax
import jax.numpy as jnp
from jax import lax
from jax.experimental import pallas as pl
from jax.experimental.pallas import tpu as pltpu

F32 = jnp.float32
BF16 = jnp.bfloat16

EPS = 1e-6
D_MODEL = 1024
D_MIX = 2 * D_MODEL
M_HEADS = 4
M_DK = 128
M_DV = 256
M_QK = 2 * M_HEADS * M_DK
CONV_WIDTH = 4
M_CHUNK = 256
M_SUB = 4
S5_GROUP = 16
S5_STATE = 64
S5_BLK = 8
S5_CHUNK = 1024
G_HEADS = 4
G_DK = 256
G_DV = 512
G_TAU = 16.0
G_CHUNK = 128
G_RANK_PAD = 128

LANES = 128
SUBLANES = 8
HALO = 16
S5_GPL = LANES // S5_GROUP
S5_SW = S5_GPL * S5_STATE
S5_XROWS = S5_CHUNK // S5_BLK + SUBLANES
TOKEN_TILE = 512
PROJ_TN = 256
VMEM_LIMIT = 56 * 1024 * 1024


def _cparams(n_grid):
    return pltpu.CompilerParams(
        dimension_semantics=("arbitrary",) * n_grid, vmem_limit_bytes=VMEM_LIMIT)


def _log2(n):
    assert n & (n - 1) == 0
    return n.bit_length() - 1


def _log_sigmoid(x):
    return jnp.minimum(x, 0.0) - jnp.log(1.0 + jnp.exp(-jnp.abs(x)))


def _silu(x):
    return x * jax.nn.sigmoid(x)


def _split_hi_lo(x):
    hi = x.astype(BF16)
    lo = (x - hi.astype(F32)).astype(BF16)
    return hi, lo


def _dot(a, b):
    return jnp.dot(a, b, preferred_element_type=F32)


def _dot_nt(a, b, precision=None):
    return lax.dot_general(a, b, (((1,), (1,)), ((), ())), precision=precision,
                           preferred_element_type=F32)


def _tri_ones(n, lower):
    row = lax.broadcasted_iota(jnp.int32, (n, n), 0)
    col = lax.broadcasted_iota(jnp.int32, (n, n), 1)
    keep = (row >= col) if lower else (row <= col)
    return jnp.where(keep, 1.0, 0.0).astype(BF16)


def _rmsnorm(x, g):
    return x * lax.rsqrt(jnp.mean(x * x, axis=-1, keepdims=True) + EPS) * g


CAST_ROWS = 256


def _cast_body(wt_ref, *out_refs, starts, narrow, scaled_cols, scale):
    for o_ref, c0 in zip(out_refs[:-1], starts):
        n = o_ref.shape[1]
        val = wt_ref[c0:c0 + n, :]
        if scaled_cols and c0 == 0:
            col = lax.broadcasted_iota(jnp.int32, (n, 1), 0)
            val = val * jnp.where(col < scaled_cols, scale, 1.0)
        o_ref[...] = val.T.astype(BF16)
    lane = lax.broadcasted_iota(jnp.int32, (1, LANES), 1)
    for blk, (c0, n) in enumerate(narrow):
        start = (c0 // SUBLANES) * SUBLANES
        take = -(-(c0 - start + n) // SUBLANES) * SUBLANES
        rows = jnp.concatenate([wt_ref[start:start + take, :], jnp.zeros((LANES - take, CAST_ROWS), F32)], axis=0)
        slab = rows.T
        if c0 != start:
            slab = pltpu.roll(slab, LANES - (c0 - start), axis=1)
        out_refs[-1][:, blk * LANES:(blk + 1) * LANES] = jnp.where(lane < n, slab, 0.0).astype(BF16)


def _cast_weights(wt, starts, widths, narrow, scaled_cols=0, scale=1.0):
    cols, rows = wt.shape
    assert rows % CAST_ROWS == 0 and CAST_ROWS % LANES == 0
    widths = tuple(widths) + (len(narrow) * LANES,)
    return pl.pallas_call(
        functools.partial(_cast_body, starts=starts, narrow=narrow, scaled_cols=scaled_cols, scale=scale),
        grid=(rows // CAST_ROWS,),
        in_specs=[pl.BlockSpec((cols, CAST_ROWS), lambda i: (0, i))],
        out_specs=[pl.BlockSpec((CAST_ROWS, n), lambda i: (i, 0)) for n in widths],
        out_shape=[jax.ShapeDtypeStruct((rows, n), BF16) for n in widths],
        compiler_params=_cparams(1),
        name="cast_weights",
    )(wt)


def _proj_even_body(x_ref, xh_ref, g_ref, w_ref, wb_ref, wg_ref, convw_ref, convb_ref,
                    qk_ref, v_ref, o_ref, u_ref, z_ref, gates_ref, ext_ref, *, tiles_per_seq):
    hb = _rmsnorm(x_ref[...], g_ref[...]).astype(BF16)
    hh = _rmsnorm(xh_ref[...], g_ref[...]).astype(BF16)
    seq_start = lax.rem(pl.program_id(0), tiles_per_seq) == 0
    base = HALO - (CONV_WIDTH - 1)
    lane = lax.broadcasted_iota(jnp.int32, (1, PROJ_TN), 1)
    plain = []
    for wr, col0, o_refs in ((w_ref, M_QK, (v_ref, o_ref)), (wb_ref, 0, (u_ref, z_ref))):
        for o_r in o_refs:
            plain += [(wr, o_r, col0 + c, c) for c in range(0, o_r.shape[1], PROJ_TN)]
            col0 += o_r.shape[1]
    for j in range(0, M_QK, PROJ_TN):
        cols = slice(j, j + PROJ_TN)
        wj = w_ref[:, cols]
        ext_ref[HALO:HALO + TOKEN_TILE, :] = _dot(hb, wj)
        ext_ref[0:HALO, :] = jnp.where(seq_start, 0.0, _dot(hh, wj))
        for k in range(len(plain) * j // M_QK, len(plain) * (j + PROJ_TN) // M_QK):
            wr, dst, c_in, c_out = plain[k]
            dst[:, c_out:c_out + PROJ_TN] = _dot(hb, wr[:, c_in:c_in + PROJ_TN]).astype(dst.dtype)
        last = CONV_WIDTH - 1
        acc = convb_ref[:, cols] + convw_ref[last:last + 1, cols] * ext_ref[HALO:HALO + TOKEN_TILE, :]
        for i in range(last):
            acc = acc + convw_ref[i:i + 1, cols] * ext_ref[base + i:base + i + TOKEN_TILE, :]
        scale = jnp.where(lane + j < M_HEADS * M_DK, M_DK ** -0.5, 1.0)
        qk_ref[:, cols] = (_silu(acc) * scale).astype(BF16)
    gates_ref[...] = _dot(hb, wg_ref[...])


def _proj_even(xf, g, wm, wb, wg, conv_w, conv_b, seq):
    t, d = xf.shape
    half = D_MIX // 2
    assert seq % TOKEN_TILE == 0 and TOKEN_TILE % HALO == 0
    tok = lambda i: (i, 0)
    const = lambda i: (0, 0)
    per_halo = TOKEN_TILE // HALO
    widths = (M_QK, half, half, half, D_MIX)
    dtypes = (BF16, BF16, BF16, F32, BF16)
    out_shape = [jax.ShapeDtypeStruct((t, n), dt) for n, dt in zip(widths, dtypes)]
    ngate = wg.shape[1]
    out_shape.append(jax.ShapeDtypeStruct((t, ngate), F32))
    out_specs = [pl.BlockSpec((TOKEN_TILE, n), tok) for n in widths]
    out_specs.append(pl.BlockSpec((TOKEN_TILE, ngate), tok))
    return pl.pallas_call(
        functools.partial(_proj_even_body, tiles_per_seq=seq // TOKEN_TILE),
        grid=(t // TOKEN_TILE,),
        in_specs=[
            pl.BlockSpec((TOKEN_TILE, d), tok),
            pl.BlockSpec((HALO, d), lambda i: (jnp.maximum(i * per_halo - 1, 0), 0)),
            pl.BlockSpec((1, d), const),
            pl.BlockSpec((d, wm.shape[1]), const, pipeline_mode=pl.Buffered(1)),
            pl.BlockSpec((d, wb.shape[1]), const, pipeline_mode=pl.Buffered(1)),
            pl.BlockSpec((d, ngate), const),
            pl.BlockSpec((CONV_WIDTH, M_QK), const),
            pl.BlockSpec((1, M_QK), const),
        ],
        out_specs=out_specs,
        out_shape=out_shape,
        scratch_shapes=[pltpu.VMEM((HALO + TOKEN_TILE, PROJ_TN), F32)],
        compiler_params=_cparams(1),
        name="proj_even",
    )(xf, xf, g.reshape(1, d), wm, wb, wg, conv_w, conv_b.reshape(1, M_QK))


def _mlstm_body(qk_ref, v_ref, gates_ref, gbias_ref, out_ref, c_ref, n_ref, m_ref):
    @pl.when(pl.program_id(1) == 0)
    def _init():
        c_ref[...] = jnp.zeros_like(c_ref)
        n_ref[...] = jnp.zeros_like(n_ref)
        m_ref[...] = jnp.zeros_like(m_ref)

    for sub in range(M_SUB):
        _mlstm_chunk(slice(sub * M_CHUNK, (sub + 1) * M_CHUNK), qk_ref, v_ref, gates_ref, gbias_ref,
                     out_ref, c_ref, n_ref, m_ref)


def _mlstm_chunk(r, qk_ref, v_ref, gates_ref, gbias_ref, out_ref, c_ref, n_ref, m_ref):
    L = M_CHUNK
    gt = gates_ref[r, :] + gbias_ref[...]
    ipre = gt[:, 0:LANES]
    logf = _log_sigmoid(gt[:, LANES:2 * LANES]).astype(BF16)
    b = _dot(_tri_ones(L, True), logf)
    w = ipre - b
    rows = lax.broadcasted_iota(jnp.int32, (L, LANES), 0)
    cm = w
    k = 1
    while k < L:
        cm = jnp.maximum(cm, jnp.where(rows >= k, pltpu.roll(cm, k, axis=0), -jnp.inf))
        k *= 2

    def replicate(x):
        return jnp.concatenate([jnp.broadcast_to(x[:, h:h + 1], (L, LANES)) for h in range(M_HEADS)], axis=1)

    b_rep = replicate(b)
    w_rep = replicate(w)
    cm_rep = replicate(cm)
    pick_r = lax.broadcasted_iota(jnp.int32, (M_HEADS * SUBLANES, LANES), 0)
    pick_c = lax.broadcasted_iota(jnp.int32, (M_HEADS * SUBLANES, LANES), 1)
    pick = jnp.where(lax.shift_right_logical(pick_r, _log2(SUBLANES)) == pick_c, 1.0, 0.0).astype(BF16)
    w_hi, w_lo = _split_hi_lo(w)
    w_row = _dot_nt(pick, w_hi) + _dot_nt(pick, w_lo)

    trow = lax.broadcasted_iota(jnp.int32, (L, LANES), 0)
    tcol = lax.broadcasted_iota(jnp.int32, (L, LANES), 1)
    ones = jnp.ones((L, LANES), BF16)

    for h in range(M_HEADS):
        ks = slice(h * M_DK, (h + 1) * M_DK)
        ks2 = slice(M_HEADS * M_DK + h * M_DK, M_HEADS * M_DK + (h + 1) * M_DK)
        vs = slice(h * M_DV, (h + 1) * M_DV)
        hs = slice(h * LANES, (h + 1) * LANES)
        m_prev = m_ref[h, 0:1, :]
        c_prev = c_ref[h]
        n_prev = n_ref[h]
        big_m = jnp.maximum(m_prev, cm_rep[:, hs])
        w_inter = jnp.exp(m_prev - big_m)
        wr = w_row[h * SUBLANES:h * SUBLANES + 1, :]

        qb = qk_ref[r, ks]
        kb = qk_ref[r, ks2]
        v_h = v_ref[r, vs]
        s = _dot_nt(qb, kb)
        sc = jnp.concatenate(
            [jnp.where(trow >= tcol + j, jnp.exp(wr[:, j:j + LANES] - big_m), 0.0) * s[:, j:j + LANES]
             for j in range(0, L, LANES)], axis=1).astype(BF16)
        q_c = _dot(qb, c_prev.astype(BF16))
        den = _dot(sc, ones) + w_inter * _dot(qb, n_prev.astype(BF16))
        inv = 1.0 / jnp.maximum(jnp.abs(den), jnp.exp(-(b_rep[:, hs] + big_m)))
        num = _dot(sc, v_h)
        out_ref[r, vs] = jnp.concatenate(
            [(num[:, j:j + LANES] + w_inter * q_c[:, j:j + LANES]) * inv for j in range(0, M_DV, LANES)],
            axis=1).astype(BF16)

        g = b_rep[L - 1:L, hs]
        cm_last = cm_rep[L - 1:L, hs]
        m_last = big_m[L - 1:L, :]
        kw_t = (kb.astype(F32) * jnp.exp(w_rep[:, hs] - cm_last)).T.astype(BF16)
        s_prev = jnp.exp(m_prev - m_last)
        s_loc = jnp.exp(cm_last - m_last)
        c_ref[h] = (jnp.concatenate([s_prev] * (M_DV // LANES), axis=1) * c_prev
                    + jnp.concatenate([s_loc] * (M_DV // LANES), axis=1) * _dot(kw_t, v_h))
        n_ref[h] = s_prev * n_prev + s_loc * _dot(kw_t, ones)
        m_ref[h] = jnp.broadcast_to(g + m_last, (SUBLANES, LANES))


def _mlstm(qk, v, gates, gbias, batch, seq):
    t = batch * seq
    L = M_CHUNK * M_SUB
    nc = seq // L
    dv = M_HEADS * M_DV
    tok = lambda b, c: (b * nc + c, 0)
    const = lambda b, c: (0, 0)
    return pl.pallas_call(
        _mlstm_body,
        grid=(batch, nc),
        in_specs=[
            pl.BlockSpec((L, M_QK), tok),
            pl.BlockSpec((L, dv), tok),
            pl.BlockSpec((L, 2 * LANES), tok),
            pl.BlockSpec((1, 2 * LANES), const),
        ],
        out_specs=pl.BlockSpec((L, dv), tok),
        out_shape=jax.ShapeDtypeStruct((t, dv), BF16),
        scratch_shapes=[
            pltpu.VMEM((M_HEADS, M_DK, M_DV), F32),
            pltpu.VMEM((M_HEADS, M_DK, LANES), F32),
            pltpu.VMEM((M_HEADS, SUBLANES, LANES), F32),
        ],
        compiler_params=_cparams(2),
        name="mlstm",
    )(qk, v, gates, gbias)


def _gelu_tanh(x):
    c = 0.7978845608028654
    half = 0.5 * x
    return half + half * jnp.tanh(x * (c + (c * 0.044715) * (x * x)))


def _s5_build_operators(lamr_ref, lami_ref, ldt_ref, btr_ref, bti_ref, ctr_ref, cti_ref, d_ref,
                        pbig_ref, qbig_ref, mbig_ref, a_ref):
    lr = lamr_ref[0]
    li = lami_ref[0]
    dt = jnp.exp(ldt_ref[0])
    zr = lr * dt
    th = li * dt
    er = jnp.exp(zr)
    ar = er * jnp.cos(th)
    ai = er * jnp.sin(th)
    den = lr * lr + li * li
    beta_r = ((ar - 1.0) * lr + ai * li) / den
    beta_i = (ai * lr - (ar - 1.0) * li) / den
    btr = btr_ref[0]
    bti = bti_ref[0]
    bbr = btr * beta_r - bti * beta_i
    bbi = btr * beta_i + bti * beta_r
    ctr = ctr_ref[0]
    cti = cti_ref[0]

    row_g = lax.shift_right_logical(lax.broadcasted_iota(jnp.int32, (LANES, S5_SW), 0), _log2(S5_GROUP))
    lane_g = lax.shift_right_logical(lax.broadcasted_iota(jnp.int32, (LANES, S5_SW), 1), _log2(S5_STATE))
    same_group = row_g == lane_g

    def expand(x16):
        return jnp.where(same_group, jnp.concatenate([x16] * S5_GPL, axis=0), 0.0)

    def power(k):
        e = jnp.exp(float(k) * zr)
        return e * jnp.cos(float(k) * th), e * jnp.sin(float(k) * th)

    for s in range(S5_BLK):
        rows = slice(s * LANES, (s + 1) * LANES)
        pr, pi = power(S5_BLK - 1 - s)
        pbig_ref[rows, 0:S5_SW] = expand(pr * bbr - pi * bbi).astype(BF16)
        pbig_ref[rows, S5_SW:2 * S5_SW] = expand(pr * bbi + pi * bbr).astype(BF16)
        pr, pi = power(s + 1)
        qbig_ref[rows, 0:S5_SW] = expand(ctr * pr - cti * pi).astype(BF16)
        qbig_ref[rows, S5_SW:2 * S5_SW] = expand(-(ctr * pi + cti * pr)).astype(BF16)

    cb = jnp.concatenate([expand(ctr), expand(-cti)], axis=1)
    r128 = lax.broadcasted_iota(jnp.int32, (LANES, LANES), 0)
    c128 = lax.broadcasted_iota(jnp.int32, (LANES, LANES), 1)
    zero_blk = jnp.zeros((LANES, LANES), BF16)
    ab_lags = []
    for lag in range(S5_BLK):
        pr, pi = power(lag)
        ab_lags.append(jnp.concatenate([expand(pr * bbr - pi * bbi), expand(pr * bbi + pi * bbr)], axis=1))
    ab_hi, ab_lo = _split_hi_lo(jnp.concatenate(ab_lags, axis=0))
    cb_hi, cb_lo = _split_hi_lo(cb)
    v_lags = _dot_nt(ab_hi, cb_hi) + _dot_nt(ab_hi, cb_lo) + _dot_nt(ab_lo, cb_hi)
    for lag in range(S5_BLK):
        v = v_lags[lag * LANES:(lag + 1) * LANES, :]
        if lag == 0:
            v = v + jnp.where(r128 == c128, d_ref[0], 0.0)
        vb = v.astype(BF16)
        for s in range(S5_BLK - lag):
            t = s + lag
            mbig_ref[s * LANES:(s + 1) * LANES, t * LANES:(t + 1) * LANES] = vb
            if lag > 0:
                mbig_ref[t * LANES:(t + 1) * LANES, s * LANES:(s + 1) * LANES] = zero_blk

    pr, pi = power(S5_BLK)
    a_ref[:, 0:S5_SW] = jnp.broadcast_to(pr, (SUBLANES, S5_SW))
    a_ref[:, S5_SW:2 * S5_SW] = jnp.broadcast_to(pi, (SUBLANES, S5_SW))


def _s5_body(u_ref, lamr_ref, lami_ref, ldt_ref, btr_ref, bti_ref, ctr_ref, cti_ref, d_ref, y_ref,
             pbig_ref, qbig_ref, mbig_ref, a_ref, ucat_ref, x_ref, xp_ref, st_ref):
    nblk = S5_CHUNK // S5_BLK
    batch = u_ref.shape[0]

    @pl.when(pl.program_id(1) == 0)
    def _setup():
        _s5_build_operators(lamr_ref, lami_ref, ldt_ref, btr_ref, bti_ref, ctr_ref, cti_ref, d_ref,
                            pbig_ref, qbig_ref, mbig_ref, a_ref)
        st_ref[...] = jnp.zeros_like(st_ref)

    for b in range(batch):
        for s in range(S5_BLK):
            piece = u_ref[b, pl.ds(s, nblk, stride=S5_BLK), :]
            ucat_ref[b * nblk:(b + 1) * nblk, s * LANES:(s + 1) * LANES] = piece.astype(BF16)
    nslab = 2 * S5_SW // LANES
    half = nslab // 2
    xloc = _dot(ucat_ref[...], pbig_ref[...])
    for c in range(nslab):
        for b in range(batch):
            x_ref[c, b * S5_XROWS:b * S5_XROWS + nblk, :] = (
                xloc[b * nblk:(b + 1) * nblk, c * LANES:(c + 1) * LANES])

    lanes = lambda ref, c: ref[:, c * LANES:(c + 1) * LANES]
    ar = [lanes(a_ref, c) for c in range(half)]
    ai = [lanes(a_ref, half + c) for c in range(half)]
    xr = [lanes(st_ref, c) for c in range(half)]
    xi = [lanes(st_ref, half + c) for c in range(half)]
    for blk in range(nblk):
        r = pl.ds(blk, batch, stride=S5_XROWS)
        for c in range(half):
            xp_ref[c, r, :] = xr[c]
            xp_ref[half + c, r, :] = xi[c]
            nr = ar[c] * xr[c] - ai[c] * xi[c] + x_ref[c, r, :]
            ni = ar[c] * xi[c] + ai[c] * xr[c] + x_ref[half + c, r, :]
            xr[c], xi[c] = nr, ni
    for c in range(half):
        st_ref[:, c * LANES:(c + 1) * LANES] = xr[c]
        st_ref[:, (half + c) * LANES:(half + c + 1) * LANES] = xi[c]

    xpb = jnp.concatenate(
        [jnp.concatenate([xp_ref[c, b * S5_XROWS:b * S5_XROWS + nblk, :].astype(BF16)
                          for c in range(nslab)], axis=1) for b in range(batch)], axis=0)
    width = 2 * LANES
    for nb in range(S5_BLK // 2):
        kk = (2 * nb + 2) * LANES
        cols = slice(nb * width, (nb + 1) * width)
        y = _dot(ucat_ref[:, 0:kk], mbig_ref[0:kk, cols]) + _dot_nt(xpb, qbig_ref[cols, :])
        y = _gelu_tanh(y)
        for tt in range(2):
            t = 2 * nb + tt
            for b in range(batch):
                y_ref[b, pl.ds(t, nblk, stride=S5_BLK), :] = (
                    y[b * nblk:(b + 1) * nblk, tt * LANES:(tt + 1) * LANES])


def _s5(u3, lam_re, lam_im, log_dt, b_re, b_im, c_re, c_im, d_skip):
    batch, seq, width = u3.shape
    nlb = width // LANES
    assert batch == SUBLANES and seq % S5_CHUNK == 0
    lamr = lam_re.reshape(nlb, 1, S5_SW)
    lami = lam_im.reshape(nlb, 1, S5_SW)
    ldt = jnp.repeat(log_dt, S5_STATE).reshape(nlb, 1, S5_SW)
    bt = lambda b: b.reshape(nlb, S5_GPL, S5_STATE, S5_GROUP).transpose(0, 3, 1, 2).reshape(nlb, S5_GROUP, S5_SW)
    ct = lambda c: c.reshape(nlb, S5_GPL, S5_GROUP, S5_STATE).transpose(0, 2, 1, 3).reshape(nlb, S5_GROUP, S5_SW)
    par = lambda r, w: pl.BlockSpec((1, r, w), lambda i, j: (i, 0, 0))
    kdim = S5_BLK * LANES
    rows = (S5_CHUNK // S5_BLK) * batch
    io = pl.BlockSpec((batch, S5_CHUNK, LANES), lambda i, j: (0, j, i))
    return pl.pallas_call(
        _s5_body,
        grid=(nlb, seq // S5_CHUNK),
        in_specs=[io, par(1, S5_SW), par(1, S5_SW), par(1, S5_SW), par(S5_GROUP, S5_SW),
                  par(S5_GROUP, S5_SW), par(S5_GROUP, S5_SW), par(S5_GROUP, S5_SW), par(1, LANES)],
        out_specs=io,
        out_shape=jax.ShapeDtypeStruct((batch, seq, width), F32),
        scratch_shapes=[
            pltpu.VMEM((kdim, 2 * S5_SW), BF16),
            pltpu.VMEM((kdim, 2 * S5_SW), BF16),
            pltpu.VMEM((kdim, kdim), BF16),
            pltpu.VMEM((SUBLANES, 2 * S5_SW), F32),
            pltpu.VMEM((rows, kdim), BF16),
            pltpu.VMEM((2 * S5_SW // LANES, batch * S5_XROWS, LANES), F32),
            pltpu.VMEM((2 * S5_SW // LANES, batch * S5_XROWS, LANES), F32),
            pltpu.VMEM((SUBLANES, 2 * S5_SW), F32),
        ],
        compiler_params=_cparams(2),
        name="s5",
    )(u3, lamr, lami, ldt, bt(b_re), bt(b_im), ct(c_re), ct(c_im), d_skip.reshape(nlb, 1, LANES))


def _even_out_body(yb_ref, h_ref, o_ref, z_ref, x_ref, hg_ref, gluw_ref, glub_ref, wout_ref,
                   out_ref):
    half = h_ref.shape[1]
    yg = yb_ref[...]
    s = _dot(yg.astype(BF16), gluw_ref[...]) + glub_ref[...]
    acc = x_ref[...]
    for h in range(M_HEADS):
        vs = slice(h * M_DV, (h + 1) * M_DV)
        og = (jax.nn.sigmoid(o_ref[:, vs]) * h_ref[:, vs]).astype(F32)
        y_h = _rmsnorm(og, hg_ref[:, vs]).astype(BF16) * _silu(z_ref[:, vs])
        acc = acc + _dot(y_h, wout_ref[vs, :])
    for j in range(0, half, M_DV):
        cs = slice(j, j + M_DV)
        zs = slice(half + j, half + j + M_DV)
        y_j = (yg[:, cs] * jax.nn.sigmoid(s[:, cs])).astype(BF16) * _silu(z_ref[:, zs])
        acc = acc + _dot(y_j, wout_ref[zs, :])
    out_ref[...] = acc


def _even_out(yb, hm, o, z, xf, head_g, glu_w, glu_b, w_out):
    t, d = xf.shape
    half = hm.shape[1]
    tok = lambda i: (i, 0)
    const = lambda i: (0, 0)
    return pl.pallas_call(
        _even_out_body,
        grid=(t // TOKEN_TILE,),
        in_specs=[
            pl.BlockSpec((TOKEN_TILE, half), tok),
            pl.BlockSpec((TOKEN_TILE, half), tok),
            pl.BlockSpec((TOKEN_TILE, half), tok),
            pl.BlockSpec((TOKEN_TILE, 2 * half), tok),
            pl.BlockSpec((TOKEN_TILE, d), tok),
            pl.BlockSpec((1, half), const),
            pl.BlockSpec((half, half), const),
            pl.BlockSpec((1, half), const),
            pl.BlockSpec((2 * half, d), const),
        ],
        out_specs=pl.BlockSpec((TOKEN_TILE, d), tok),
        out_shape=jax.ShapeDtypeStruct((t, d), F32),
        compiler_params=_cparams(1),
        name="even_out",
    )(yb, hm, o, z, xf, head_g.reshape(1, half), glu_w, glu_b.reshape(1, half), w_out)


def _proj_odd_body(x_ref, g_ref, w_ref, wg_ref, wa_ref, ba_ref, q_ref, k_ref, v_ref, z_ref, bc_ref):
    hb = _rmsnorm(x_ref[...], g_ref[...]).astype(BF16)
    rb = _dot(hb, wg_ref[...]).astype(BF16)
    tril = _tri_ones(G_CHUNK, True)
    plain = []
    col0 = 0
    for o_r in (q_ref, k_ref, v_ref, z_ref):
        plain += [(o_r, col0 + c, c) for c in range(0, o_r.shape[1], PROJ_TN)]
        col0 += o_r.shape[1]
    width = bc_ref.shape[1]
    for j in range(0, width, PROJ_TN):
        for k in range(len(plain) * j // width, len(plain) * (j + PROJ_TN) // width):
            dst, c_in, c_out = plain[k]
            dst[:, c_out:c_out + PROJ_TN] = _dot(hb, w_ref[:, c_in:c_in + PROJ_TN]).astype(dst.dtype)
        cols = slice(j, j + PROJ_TN)
        pre = _dot(rb, wa_ref[:, cols].astype(BF16)) + ba_ref[:, cols]
        la = (_log_sigmoid(pre) * (1.0 / G_TAU)).astype(BF16)
        for c in range(0, TOKEN_TILE, G_CHUNK):
            rows = slice(c, c + G_CHUNK)
            bc_ref[rows, cols] = _dot(tril, la[rows])


def _layer1_body(x_ref, g_ref, w_ref, wg_ref, wa_ref, ba_ref, hg_ref, wout_ref, gf_ref, out_ref,
                 q_s, k_s, v_s, z_s, bc_s, y_s, s_ref, *, tiles_per_seq):
    L = G_CHUNK

    @pl.when(lax.rem(pl.program_id(0), tiles_per_seq) == 0)
    def _init():
        s_ref[...] = jnp.zeros_like(s_ref)

    _proj_odd_body(x_ref, g_ref, w_ref, wg_ref, wa_ref, ba_ref, q_s, k_s, v_s, z_s, bc_s)

    row = lax.broadcasted_iota(jnp.int32, (L, L), 0)
    col = lax.broadcasted_iota(jnp.int32, (L, L), 1)
    causal = row >= col
    mid = L // 2 - 1

    for c, h in [(c, h) for c in range(0, TOKEN_TILE, 2 * L) for h in range(G_HEADS)]:
        ks = slice(h * G_DK, (h + 1) * G_DK)
        vs = slice(h * G_DV, (h + 1) * G_DV)
        r1, r2 = slice(c, c + L), slice(c + L, c + 2 * L)
        s_prev = s_ref[h]
        s_bf = s_prev.astype(BF16)
        v_p = v_s[c:c + 2 * L, vs]

        def prep(r):
            b = bc_s[r, ks]
            bm = b[mid:mid + 1, :]
            g = b[L - 1:L, :]
            e1 = jnp.exp(b - bm)
            qt = q_s[r, ks].astype(F32) * e1
            kt = k_s[r, ks].astype(F32) * (1.0 / e1)
            attn = jnp.where(causal, _dot_nt(qt.astype(BF16), kt.astype(BF16)), 0.0).astype(BF16)
            return attn, qt * jnp.exp(bm), kt * jnp.exp(g - bm), jnp.exp(g)

        a11, qi1, ke1, eg1 = prep(r1)
        a22, qi2, ke2, eg2 = prep(r2)
        qi2b = qi2.astype(BF16)
        a21 = _dot_nt(qi2b, ke1.astype(BF16)).astype(BF16)
        o1 = _dot(a11, v_p[0:L]) + _dot(qi1.astype(BF16), s_bf)
        o2 = _dot(jnp.concatenate([a21, a22], axis=1), v_p) + _dot((qi2 * eg1).astype(BF16), s_bf)
        y_s[r1, vs] = (_rmsnorm(o1, hg_ref[:, vs]) * _silu(z_s[r1, vs].astype(F32))).astype(BF16)
        y_s[r2, vs] = (_rmsnorm(o2, hg_ref[:, vs]) * _silu(z_s[r2, vs].astype(F32))).astype(BF16)
        ke_t = jnp.concatenate([(ke1 * eg2).T, ke2.T], axis=1).astype(BF16)
        g_col = jnp.broadcast_to(eg1 * eg2, (LANES, G_DK)).T[:, 0:1]
        s_ref[h] = g_col * s_prev + _dot(ke_t, v_p)

    x = x_ref[...] + _dot(y_s[...], wout_ref[...])
    out_ref[...] = _rmsnorm(x, gf_ref[...])


def _layer1(x1, g, wm, wg, w_alpha, b_alpha, head_g, w_out, gf, seq):
    t, d = x1.shape
    dk = G_HEADS * G_DK
    assert seq % TOKEN_TILE == 0 and TOKEN_TILE % G_CHUNK == 0
    tok = lambda i: (i, 0)
    const = lambda i: (0, 0)
    resident = lambda shape: pl.BlockSpec(shape, const, pipeline_mode=pl.Buffered(1))
    slots = lambda n, dt: pltpu.VMEM((TOKEN_TILE, n), dt)
    return pl.pallas_call(
        functools.partial(_layer1_body, tiles_per_seq=seq // TOKEN_TILE),
        grid=(t // TOKEN_TILE,),
        in_specs=[
            pl.BlockSpec((TOKEN_TILE, d), tok),
            pl.BlockSpec((1, d), const),
            resident((d, wm.shape[1])),
            pl.BlockSpec((d, G_RANK_PAD), const),
            pl.BlockSpec((G_RANK_PAD, dk), const),
            pl.BlockSpec((1, dk), const),
            pl.BlockSpec((1, D_MIX), const),
            resident((D_MIX, d)),
            pl.BlockSpec((1, d), const),
        ],
        out_specs=pl.BlockSpec((TOKEN_TILE, d), tok),
        out_shape=jax.ShapeDtypeStruct((t, d), F32),
        scratch_shapes=[
            slots(dk, BF16),
            slots(dk, BF16),
            slots(D_MIX, BF16),
            slots(D_MIX, BF16),
            slots(dk, F32),
            slots(D_MIX, BF16),
            pltpu.VMEM((G_HEADS, G_DK, G_DV), F32),
        ],
        compiler_params=_cparams(1),
        name="layer1",
    )(x1, g.reshape(1, d), wm, wg, w_alpha, b_alpha.reshape(1, dk), head_g.reshape(1, D_MIX), w_out,
      gf.reshape(1, d))


def kernel(x, norm_g, final_norm_g, ev_w_in, ev_conv_w, ev_conv_b, ev_i_bias, ev_f_bias, ev_head_g,
           s5_lam_re, s5_lam_im, s5_log_dt, s5_b_re, s5_b_im, s5_c_re, s5_c_im, s5_d, s5_glu_w,
           s5_glu_b, ev_w_out, od_w_in, gla_w_alpha, gla_b_alpha, gla_head_g, od_w_out):
    batch, seq, d = x.shape
    t = batch * seq
    xf = x.reshape(t, d)
    padc = lambda a: jnp.pad(a, ((0, 0), (0, LANES - a.shape[1])))

    wt = jnp.swapaxes(ev_w_in, 1, 2).reshape(ev_w_in.shape[2], d)
    g0 = 2 * M_HEADS * M_DK + 2 * M_HEADS * M_DV
    gi = g0 + M_HEADS
    gf = gi + M_HEADS
    half = D_MIX // 2
    w_qkvo, w_uz, wg = _cast_weights(wt, (0, gf), (g0, half + D_MIX), ((g0, M_HEADS), (gi, M_HEADS)))
    qk, v, o, u, z, gates = _proj_even(xf, norm_g[0], w_qkvo, w_uz, wg, ev_conv_w[0], ev_conv_b[0], seq)
    gbias = jnp.concatenate([padc(ev_i_bias), padc(ev_f_bias)], axis=1)
    hm = _mlstm(qk, v, gates, gbias, batch, seq)

    y3 = _s5(u.reshape(batch, seq, half), s5_lam_re[0], s5_lam_im[0], s5_log_dt[0], s5_b_re[0],
             s5_b_im[0], s5_c_re[0], s5_c_im[0], s5_d[0])
    yb = y3.reshape(t, half)
    x1 = _even_out(yb, hm, o, z, xf, ev_head_g[0], s5_glu_w[0].astype(BF16), s5_glu_b[0],
                   ev_w_out[0].astype(BF16))

    wt = jnp.swapaxes(od_w_in, 1, 2).reshape(od_w_in.shape[2], d)
    n_main = 2 * G_HEADS * G_DK + 2 * D_MIX
    wa = jnp.pad(gla_w_alpha[0], ((0, G_RANK_PAD - gla_w_alpha.shape[1]), (0, 0)))
    assert _log2(G_DK) % 2 == 0
    wm, wr = _cast_weights(wt, (0,), (n_main,), ((n_main, wt.shape[0] - n_main),),
                           scaled_cols=G_HEADS * G_DK, scale=G_DK ** -0.5)
    out = _layer1(x1, norm_g[1], wm, wr, wa, gla_b_alpha[0],
                  gla_head_g[0], od_w_out[0].astype(BF16), final_norm_g, seq)
    return out.reshape(batch, seq, d)
```

```python
import functools

import jax
import jax.numpy as jnp
from jax import lax
from jax.experimental import pallas as pl
from jax.experimental.pallas import tpu as pltpu

F32 = jnp.float32
BF16 = jnp.bfloat16

EPS = 1e-6
D_MODEL = 1024
D_MIX = 2 * D_MODEL
M_HEADS = 4
M_DK = 128
M_DV = 256
M_QK = 2 * M_HEADS * M_DK
CONV_WIDTH = 4
M_CHUNK = 256
M_SUB = 4
S5_GROUP = 16
S5_STATE = 64
S5_BLK = 8
S5_CHUNK = 1024
G_HEADS = 4
G_DK = 256
G_DV = 512
G_TAU = 16.0
G_CHUNK = 128
G_RANK_PAD = 128

LANES = 128
SUBLANES = 8
HALO = 16
S5_GPL = LANES // S5_GROUP
S5_SW = S5_GPL * S5_STATE
S5_XROWS = S5_CHUNK // S5_BLK + SUBLANES
TOKEN_TILE = 512
PROJ_TN = 256
VMEM_LIMIT = 56 * 1024 * 1024


def _cparams(n_grid):
    return pltpu.CompilerParams(
        dimension_semantics=("arbitrary",) * n_grid, vmem_limit_bytes=VMEM_LIMIT)


def _log2(n):
    assert n & (n - 1) == 0
    return n.bit_length() - 1


def _log_sigmoid(x):
    return jnp.minimum(x, 0.0) - jnp.log(1.0 + jnp.exp(-jnp.abs(x)))


def _silu(x):
    return x * jax.nn.sigmoid(x)


def _split_hi_lo(x):
    hi = x.astype(BF16)
    lo = (x - hi.astype(F32)).astype(BF16)
    return hi, lo


def _dot(a, b):
    return jnp.dot(a, b, preferred_element_type=F32)


def _dot_nt(a, b, precision=None):
    return lax.dot_general(a, b, (((1,), (1,)), ((), ())), precision=precision,
                           preferred_element_type=F32)


def _tri_ones(n, lower):
    row = lax.broadcasted_iota(jnp.int32, (n, n), 0)
    col = lax.broadcasted_iota(jnp.int32, (n, n), 1)
    keep = (row >= col) if lower else (row <= col)
    return jnp.where(keep, 1.0, 0.0).astype(BF16)


def _rmsnorm(x, g):
    return x * lax.rsqrt(jnp.mean(x * x, axis=-1, keepdims=True) + EPS) * g


CAST_ROWS = 256


def _cast_body(wt_ref, *out_refs, starts, narrow, scaled_cols, scale):
    for o_ref, c0 in zip(out_refs[:-1], starts):
        n = o_ref.shape[1]
        val = wt_ref[c0:c0 + n, :]
        if scaled_cols and c0 == 0:
            col = lax.broadcasted_iota(jnp.int32, (n, 1), 0)
            val = val * jnp.where(col < scaled_cols, scale, 1.0)
        o_ref[...] = val.T.astype(BF16)
    lane = lax.broadcasted_iota(jnp.int32, (1, LANES), 1)
    for blk, (c0, n) in enumerate(narrow):
        start = (c0 // SUBLANES) * SUBLANES
        take = -(-(c0 - start + n) // SUBLANES) * SUBLANES
        rows = jnp.concatenate([wt_ref[start:start + take, :], jnp.zeros((LANES - take, CAST_ROWS), F32)], axis=0)
        slab = rows.T
        if c0 != start:
            slab = pltpu.roll(slab, LANES - (c0 - start), axis=1)
        out_refs[-1][:, blk * LANES:(blk + 1) * LANES] = jnp.where(lane < n, slab, 0.0).astype(BF16)


def _cast_weights(wt, starts, widths, narrow, scaled_cols=0, scale=1.0):
    cols, rows = wt.shape
    assert rows % CAST_ROWS == 0 and CAST_ROWS % LANES == 0
    widths = tuple(widths) + (len(narrow) * LANES,)
    return pl.pallas_call(
        functools.partial(_cast_body, starts=starts, narrow=narrow, scaled_cols=scaled_cols, scale=scale),
        grid=(rows // CAST_ROWS,),
        in_specs=[pl.BlockSpec((cols, CAST_ROWS), lambda i: (0, i))],
        out_specs=[pl.BlockSpec((CAST_ROWS, n), lambda i: (i, 0)) for n in widths],
        out_shape=[jax.ShapeDtypeStruct((rows, n), BF16) for n in widths],
        compiler_params=_cparams(1),
        name="cast_weights",
    )(wt)


def _proj_even_body(x_ref, xh_ref, g_ref, w_ref, wb_ref, wg_ref, convw_ref, convb_ref,
                    qk_ref, v_ref, o_ref, u_ref, z_ref, gates_ref, ext_ref, *, tiles_per_seq):
    hb = _rmsnorm(x_ref[...], g_ref[...]).astype(BF16)
    hh = _rmsnorm(xh_ref[...], g_ref[...]).astype(BF16)
    seq_start = lax.rem(pl.program_id(0), tiles_per_seq) == 0
    base = HALO - (CONV_WIDTH - 1)
    lane = lax.broadcasted_iota(jnp.int32, (1, PROJ_TN), 1)
    plain = []
    for wr, col0, o_refs in ((w_ref, M_QK, (v_ref, o_ref)), (wb_ref, 0, (u_ref, z_ref))):
        for o_r in o_refs:
            plain += [(wr, o_r, col0 + c, c) for c in range(0, o_r.shape[1], PROJ_TN)]
            col0 += o_r.shape[1]
    for j in range(0, M_QK, PROJ_TN):
        cols = slice(j, j + PROJ_TN)
        wj = w_ref[:, cols]
        ext_ref[HALO:HALO + TOKEN_TILE, :] = _dot(hb, wj)
        ext_ref[0:HALO, :] = jnp.where(seq_start, 0.0, _dot(hh, wj))
        for k in range(len(plain) * j // M_QK, len(plain) * (j + PROJ_TN) // M_QK):
            wr, dst, c_in, c_out = plain[k]
            dst[:, c_out:c_out + PROJ_TN] = _dot(hb, wr[:, c_in:c_in + PROJ_TN]).astype(dst.dtype)
        last = CONV_WIDTH - 1
        acc = convb_ref[:, cols] + convw_ref[last:last + 1, cols] * ext_ref[HALO:HALO + TOKEN_TILE, :]
        for i in range(last):
            acc = acc + convw_ref[i:i + 1, cols] * ext_ref[base + i:base + i + TOKEN_TILE, :]
        scale = jnp.where(lane + j < M_HEADS * M_DK, M_DK ** -0.5, 1.0)
        qk_ref[:, cols] = (_silu(acc) * scale).astype(BF16)
    gates_ref[...] = _dot(hb, wg_ref[...])


def _proj_even(xf, g, wm, wb, wg, conv_w, conv_b, seq):
    t, d = xf.shape
    half = D_MIX // 2
    assert seq % TOKEN_TILE == 0 and TOKEN_TILE % HALO == 0
    tok = lambda i: (i, 0)
    const = lambda i: (0, 0)
    per_halo = TOKEN_TILE // HALO
    widths = (M_QK, half, half, half, D_MIX)
    dtypes = (BF16, BF16, BF16, F32, BF16)
    out_shape = [jax.ShapeDtypeStruct((t, n), dt) for n, dt in zip(widths, dtypes)]
    ngate = wg.shape[1]
    out_shape.append(jax.ShapeDtypeStruct((t, ngate), F32))
    out_specs = [pl.BlockSpec((TOKEN_TILE, n), tok) for n in widths]
    out_specs.append(pl.BlockSpec((TOKEN_TILE, ngate), tok))
    return pl.pallas_call(
        functools.partial(_proj_even_body, tiles_per_seq=seq // TOKEN_TILE),
        grid=(t // TOKEN_TILE,),
        in_specs=[
            pl.BlockSpec((TOKEN_TILE, d), tok),
            pl.BlockSpec((HALO, d), lambda i: (jnp.maximum(i * per_halo - 1, 0), 0)),
            pl.BlockSpec((1, d), const),
            pl.BlockSpec((d, wm.shape[1]), const, pipeline_mode=pl.Buffered(1)),
            pl.BlockSpec((d, wb.shape[1]), const, pipeline_mode=pl.Buffered(1)),
            pl.BlockSpec((d, ngate), const),
            pl.BlockSpec((CONV_WIDTH, M_QK), const),
            pl.BlockSpec((1, M_QK), const),
        ],
        out_specs=out_specs,
        out_shape=out_shape,
        scratch_shapes=[pltpu.VMEM((HALO + TOKEN_TILE, PROJ_TN), F32)],
        compiler_params=_cparams(1),
        name="proj_even",
    )(xf, xf, g.reshape(1, d), wm, wb, wg, conv_w, conv_b.reshape(1, M_QK))


def _mlstm_body(qk_ref, v_ref, gates_ref, gbias_ref, out_ref, c_ref, n_ref, m_ref):
    @pl.when(pl.program_id(1) == 0)
    def _init():
        c_ref[...] = jnp.zeros_like(c_ref)
        n_ref[...] = jnp.zeros_like(n_ref)
        m_ref[...] = jnp.zeros_like(m_ref)

    for sub in range(M_SUB):
        _mlstm_chunk(slice(sub * M_CHUNK, (sub + 1) * M_CHUNK), qk_ref, v_ref, gates_ref, gbias_ref,
                     out_ref, c_ref, n_ref, m_ref)


def _mlstm_chunk(r, qk_ref, v_ref, gates_ref, gbias_ref, out_ref, c_ref, n_ref, m_ref):
    L = M_CHUNK
    gt = gates_ref[r, :] + gbias_ref[...]
    ipre = gt[:, 0:LANES]
    logf = _log_sigmoid(gt[:, LANES:2 * LANES]).astype(BF16)
    b = _dot(_tri_ones(L, True), logf)
    w = ipre - b
    rows = lax.broadcasted_iota(jnp.int32, (L, LANES), 0)
    cm = w
    k = 1
    while k < L:
        cm = jnp.maximum(cm, jnp.where(rows >= k, pltpu.roll(cm, k, axis=0), -jnp.inf))
        k *= 2

    def replicate(x):
        return jnp.concatenate([jnp.broadcast_to(x[:, h:h + 1], (L, LANES)) for h in range(M_HEADS)], axis=1)

    b_rep = replicate(b)
    w_rep = replicate(w)
    cm_rep = replicate(cm)
    pick_r = lax.broadcasted_iota(jnp.int32, (M_HEADS * SUBLANES, LANES), 0)
    pick_c = lax.broadcasted_iota(jnp.int32, (M_HEADS * SUBLANES, LANES), 1)
    pick = jnp.where(lax.shift_right_logical(pick_r, _log2(SUBLANES)) == pick_c, 1.0, 0.0).astype(BF16)
    w_hi, w_lo = _split_hi_lo(w)
    w_row = _dot_nt(pick, w_hi) + _dot_nt(pick, w_lo)

    trow = lax.broadcasted_iota(jnp.int32, (L, LANES), 0)
    tcol = lax.broadcasted_iota(jnp.int32, (L, LANES), 1)
    ones = jnp.ones((L, LANES), BF16)

    for h in range(M_HEADS):
        ks = slice(h * M_DK, (h + 1) * M_DK)
        ks2 = slice(M_HEADS * M_DK + h * M_DK, M_HEADS * M_DK + (h + 1) * M_DK)
        vs = slice(h * M_DV, (h + 1) * M_DV)
        hs = slice(h * LANES, (h + 1) * LANES)
        m_prev = m_ref[h, 0:1, :]
        c_prev = c_ref[h]
        n_prev = n_ref[h]
        big_m = jnp.maximum(m_prev, cm_rep[:, hs])
        w_inter = jnp.exp(m_prev - big_m)
        wr = w_row[h * SUBLANES:h * SUBLANES + 1, :]

        qb = qk_ref[r, ks]
        kb = qk_ref[r, ks2]
        v_h = v_ref[r, vs]
        s = _dot_nt(qb, kb)
        sc = jnp.concatenate(
            [jnp.where(trow >= tcol + j, jnp.exp(wr[:, j:j + LANES] - big_m), 0.0) * s[:, j:j + LANES]
             for j in range(0, L, LANES)], axis=1).astype(BF16)
        q_c = _dot(qb, c_prev.astype(BF16))
        den = _dot(sc, ones) + w_inter * _dot(qb, n_prev.astype(BF16))
        inv = 1.0 / jnp.maximum(jnp.abs(den), jnp.exp(-(b_rep[:, hs] + big_m)))
        num = _dot(sc, v_h)
        out_ref[r, vs] = jnp.concatenate(
            [(num[:, j:j + LANES] + w_inter * q_c[:, j:j + LANES]) * inv for j in range(0, M_DV, LANES)],
            axis=1).astype(BF16)

        g = b_rep[L - 1:L, hs]
        cm_last = cm_rep[L - 1:L, hs]
        m_last = big_m[L - 1:L, :]
        kw_t = (kb.astype(F32) * jnp.exp(w_rep[:, hs] - cm_last)).T.astype(BF16)
        s_prev = jnp.exp(m_prev - m_last)
        s_loc = jnp.exp(cm_last - m_last)
        c_ref[h] = (jnp.concatenate([s_prev] * (M_DV // LANES), axis=1) * c_prev
                    + jnp.concatenate([s_loc] * (M_DV // LANES), axis=1) * _dot(kw_t, v_h))
        n_ref[h] = s_prev * n_prev + s_loc * _dot(kw_t, ones)
        m_ref[h] = jnp.broadcast_to(g + m_last, (SUBLANES, LANES))


def _mlstm(qk, v, gates, gbias, batch, seq):
    t = batch * seq
    L = M_CHUNK * M_SUB
    nc = seq // L
    dv = M_HEADS * M_DV
    tok = lambda b, c: (b * nc + c, 0)
    const = lambda b, c: (0, 0)
    return pl.pallas_call(
        _mlstm_body,
        grid=(batch, nc),
        in_specs=[
            pl.BlockSpec((L, M_QK), tok),
            pl.BlockSpec((L, dv), tok),
            pl.BlockSpec((L, 2 * LANES), tok),
            pl.BlockSpec((1, 2 * LANES), const),
        ],
        out_specs=pl.BlockSpec((L, dv), tok),
        out_shape=jax.ShapeDtypeStruct((t, dv), BF16),
        scratch_shapes=[
            pltpu.VMEM((M_HEADS, M_DK, M_DV), F32),
            pltpu.VMEM((M_HEADS, M_DK, LANES), F32),
            pltpu.VMEM((M_HEADS, SUBLANES, LANES), F32),
        ],
        compiler_params=_cparams(2),
        name="mlstm",
    )(qk, v, gates, gbias)


def _gelu_tanh(x):
    c = 0.7978845608028654
    half = 0.5 * x
    return half + half * jnp.tanh(x * (c + (c * 0.044715) * (x * x)))


def _s5_build_operators(lamr_ref, lami_ref, ldt_ref, btr_ref, bti_ref, ctr_ref, cti_ref, d_ref,
                        pbig_ref, qbig_ref, mbig_ref, a_ref):
    lr = lamr_ref[0]
    li = lami_ref[0]
    dt = jnp.exp(ldt_ref[0])
    zr = lr * dt
    th = li * dt
    er = jnp.exp(zr)
    ar = er * jnp.cos(th)
    ai = er * jnp.sin(th)
    den = lr * lr + li * li
    beta_r = ((ar - 1.0) * lr + ai * li) / den
    beta_i = (ai * lr - (ar - 1.0) * li) / den
    btr = btr_ref[0]
    bti = bti_ref[0]
    bbr = btr * beta_r - bti * beta_i
    bbi = btr * beta_i + bti * beta_r
    ctr = ctr_ref[0]
    cti = cti_ref[0]

    row_g = lax.shift_right_logical(lax.broadcasted_iota(jnp.int32, (LANES, S5_SW), 0), _log2(S5_GROUP))
    lane_g = lax.shift_right_logical(lax.broadcasted_iota(jnp.int32, (LANES, S5_SW), 1), _log2(S5_STATE))
    same_group = row_g == lane_g

    def expand(x16):
        return jnp.where(same_group, jnp.concatenate([x16] * S5_GPL, axis=0), 0.0)

    def power(k):
        e = jnp.exp(float(k) * zr)
        return e * jnp.cos(float(k) * th), e * jnp.sin(float(k) * th)

    for s in range(S5_BLK):
        rows = slice(s * LANES, (s + 1) * LANES)
        pr, pi = power(S5_BLK - 1 - s)
        pbig_ref[rows, 0:S5_SW] = expand(pr * bbr - pi * bbi).astype(BF16)
        pbig_ref[rows, S5_SW:2 * S5_SW] = expand(pr * bbi + pi * bbr).astype(BF16)
        pr, pi = power(s + 1)
        qbig_ref[rows, 0:S5_SW] = expand(ctr * pr - cti * pi).astype(BF16)
        qbig_ref[rows, S5_SW:2 * S5_SW] = expand(-(ctr * pi + cti * pr)).astype(BF16)

    cb = jnp.concatenate([expand(ctr), expand(-cti)], axis=1)
    r128 = lax.broadcasted_iota(jnp.int32, (LANES, LANES), 0)
    c128 = lax.broadcasted_iota(jnp.int32, (LANES, LANES), 1)
    zero_blk = jnp.zeros((LANES, LANES), BF16)
    ab_lags = []
    for lag in range(S5_BLK):
        pr, pi = power(lag)
        ab_lags.append(jnp.concatenate([expand(pr * bbr - pi * bbi), expand(pr * bbi + pi * bbr)], axis=1))
    ab_hi, ab_lo = _split_hi_lo(jnp.concatenate(ab_lags, axis=0))
    cb_hi, cb_lo = _split_hi_lo(cb)
    v_lags = _dot_nt(ab_hi, cb_hi) + _dot_nt(ab_hi, cb_lo) + _dot_nt(ab_lo, cb_hi)
    for lag in range(S5_BLK):
        v = v_lags[lag * LANES:(lag + 1) * LANES, :]
        if lag == 0:
            v = v + jnp.where(r128 == c128, d_ref[0], 0.0)
        vb = v.astype(BF16)
        for s in range(S5_BLK - lag):
            t = s + lag
            mbig_ref[s * LANES:(s + 1) * LANES, t * LANES:(t + 1) * LANES] = vb
            if lag > 0:
                mbig_ref[t * LANES:(t + 1) * LANES, s * LANES:(s + 1) * LANES] = zero_blk

    pr, pi = power(S5_BLK)
    a_ref[:, 0:S5_SW] = jnp.broadcast_to(pr, (SUBLANES, S5_SW))
    a_ref[:, S5_SW:2 * S5_SW] = jnp.broadcast_to(pi, (SUBLANES, S5_SW))


def _s5_body(u_ref, lamr_ref, lami_ref, ldt_ref, btr_ref, bti_ref, ctr_ref, cti_ref, d_ref, y_ref,
             pbig_ref, qbig_ref, mbig_ref, a_ref, ucat_ref, x_ref, xp_ref, st_ref):
    nblk = S5_CHUNK // S5_BLK
    batch = u_ref.shape[0]

    @pl.when(pl.program_id(1) == 0)
    def _setup():
        _s5_build_operators(lamr_ref, lami_ref, ldt_ref, btr_ref, bti_ref, ctr_ref, cti_ref, d_ref,
                            pbig_ref, qbig_ref, mbig_ref, a_ref)
        st_ref[...] = jnp.zeros_like(st_ref)

    for b in range(batch):
        for s in range(S5_BLK):
            piece = u_ref[b, pl.ds(s, nblk, stride=S5_BLK), :]
            ucat_ref[b * nblk:(b + 1) * nblk, s * LANES:(s + 1) * LANES] = piece.astype(BF16)
    nslab = 2 * S5_SW // LANES
    half = nslab // 2
    xloc = _dot(ucat_ref[...], pbig_ref[...])
    for c in range(nslab):
        for b in range(batch):
            x_ref[c, b * S5_XROWS:b * S5_XROWS + nblk, :] = (
                xloc[b * nblk:(b + 1) * nblk, c * LANES:(c + 1) * LANES])

    lanes = lambda ref, c: ref[:, c * LANES:(c + 1) * LANES]
    ar = [lanes(a_ref, c) for c in range(half)]
    ai = [lanes(a_ref, half + c) for c in range(half)]
    xr = [lanes(st_ref, c) for c in range(half)]
    xi = [lanes(st_ref, half + c) for c in range(half)]
    for blk in range(nblk):
        r = pl.ds(blk, batch, stride=S5_XROWS)
        for c in range(half):
            xp_ref[c, r, :] = xr[c]
            xp_ref[half + c, r, :] = xi[c]
            nr = ar[c] * xr[c] - ai[c] * xi[c] + x_ref[c, r, :]
            ni = ar[c] * xi[c] + ai[c] * xr[c] + x_ref[half + c, r, :]
            xr[c], xi[c] = nr, ni
    for c in range(half):
        st_ref[:, c * LANES:(c + 1) * LANES] = xr[c]
        st_ref[:, (half + c) * LANES:(half + c + 1) * LANES] = xi[c]

    xpb = jnp.concatenate(
        [jnp.concatenate([xp_ref[c, b * S5_XROWS:b * S5_XROWS + nblk, :].astype(BF16)
                          for c in range(nslab)], axis=1) for b in range(batch)], axis=0)
    width = 2 * LANES
    for nb in range(S5_BLK // 2):
        kk = (2 * nb + 2) * LANES
        cols = slice(nb * width, (nb + 1) * width)
        y = _dot(ucat_ref[:, 0:kk], mbig_ref[0:kk, cols]) + _dot_nt(xpb, qbig_ref[cols, :])
        y = _gelu_tanh(y)
        for tt in range(2):
            t = 2 * nb + tt
            for b in range(batch):
                y_ref[b, pl.ds(t, nblk, stride=S5_BLK), :] = (
                    y[b * nblk:(b + 1) * nblk, tt * LANES:(tt + 1) * LANES])


def _s5(u3, lam_re, lam_im, log_dt, b_re, b_im, c_re, c_im, d_skip):
    batch, seq, width = u3.shape
    nlb = width // LANES
    assert batch == SUBLANES and seq % S5_CHUNK == 0
    lamr = lam_re.reshape(nlb, 1, S5_SW)
    lami = lam_im.reshape(nlb, 1, S5_SW)
    ldt = jnp.repeat(log_dt, S5_STATE).reshape(nlb, 1, S5_SW)
    bt = lambda b: b.reshape(nlb, S5_GPL, S5_STATE, S5_GROUP).transpose(0, 3, 1, 2).reshape(nlb, S5_GROUP, S5_SW)
    ct = lambda c: c.reshape(nlb, S5_GPL, S5_GROUP, S5_STATE).transpose(0, 2, 1, 3).reshape(nlb, S5_GROUP, S5_SW)
    par = lambda r, w: pl.BlockSpec((1, r, w), lambda i, j: (i, 0, 0))
    kdim = S5_BLK * LANES
    rows = (S5_CHUNK // S5_BLK) * batch
    io = pl.BlockSpec((batch, S5_CHUNK, LANES), lambda i, j: (0, j, i))
    return pl.pallas_call(
        _s5_body,
        grid=(nlb, seq // S5_CHUNK),
        in_specs=[io, par(1, S5_SW), par(1, S5_SW), par(1, S5_SW), par(S5_GROUP, S5_SW),
                  par(S5_GROUP, S5_SW), par(S5_GROUP, S5_SW), par(S5_GROUP, S5_SW), par(1, LANES)],
        out_specs=io,
        out_shape=jax.ShapeDtypeStruct((batch, seq, width), F32),
        scratch_shapes=[
            pltpu.VMEM((kdim, 2 * S5_SW), BF16),
            pltpu.VMEM((kdim, 2 * S5_SW), BF16),
            pltpu.VMEM((kdim, kdim), BF16),
            pltpu.VMEM((SUBLANES, 2 * S5_SW), F32),
            pltpu.VMEM((rows, kdim), BF16),
            pltpu.VMEM((2 * S5_SW // LANES, batch * S5_XROWS, LANES), F32),
            pltpu.VMEM((2 * S5_SW // LANES, batch * S5_XROWS, LANES), F32),
            pltpu.VMEM((SUBLANES, 2 * S5_SW), F32),
        ],
        compiler_params=_cparams(2),
        name="s5",
    )(u3, lamr, lami, ldt, bt(b_re), bt(b_im), ct(c_re), ct(c_im), d_skip.reshape(nlb, 1, LANES))


def _even_out_body(yb_ref, h_ref, o_ref, z_ref, x_ref, hg_ref, gluw_ref, glub_ref, wout_ref, gn_ref,
                   out_ref, hn_ref):
    half = h_ref.shape[1]
    yg = yb_ref[...]
    s = _dot(yg.astype(BF16), gluw_ref[...]) + glub_ref[...]
    acc = x_ref[...]
    for h in range(M_HEADS):
        vs = slice(h * M_DV, (h + 1) * M_DV)
        og = (jax.nn.sigmoid(o_ref[:, vs]) * h_ref[:, vs]).astype(F32)
        y_h = _rmsnorm(og, hg_ref[:, vs]).astype(BF16) * _silu(z_ref[:, vs])
        acc = acc + _dot(y_h, wout_ref[vs, :])
    for j in range(0, half, M_DV):
        cs = slice(j, j + M_DV)
        zs = slice(half + j, half + j + M_DV)
        y_j = (yg[:, cs] * jax.nn.sigmoid(s[:, cs])).astype(BF16) * _silu(z_ref[:, zs])
        acc = acc + _dot(y_j, wout_ref[zs, :])
    out_ref[...] = acc
    hn_ref[...] = _rmsnorm(acc, gn_ref[...]).astype(BF16)


def _even_out(yb, hm, o, z, xf, head_g, glu_w, glu_b, w_out, g_next):
    t, d = xf.shape
    half = hm.shape[1]
    tok = lambda i: (i, 0)
    const = lambda i: (0, 0)
    return pl.pallas_call(
        _even_out_body,
        grid=(t // TOKEN_TILE,),
        in_specs=[
            pl.BlockSpec((TOKEN_TILE, half), tok),
            pl.BlockSpec((TOKEN_TILE, half), tok),
            pl.BlockSpec((TOKEN_TILE, half), tok),
            pl.BlockSpec((TOKEN_TILE, 2 * half), tok),
            pl.BlockSpec((TOKEN_TILE, d), tok),
            pl.BlockSpec((1, half), const),
            pl.BlockSpec((half, half), const),
            pl.BlockSpec((1, half), const),
            pl.BlockSpec((2 * half, d), const),
            pl.BlockSpec((1, d), const),
        ],
        out_specs=[pl.BlockSpec((TOKEN_TILE, d), tok), pl.BlockSpec((TOKEN_TILE, d), tok)],
        out_shape=[jax.ShapeDtypeStruct((t, d), F32), jax.ShapeDtypeStruct((t, d), BF16)],
        compiler_params=_cparams(1),
        name="even_out",
    )(yb, hm, o, z, xf, head_g.reshape(1, half), glu_w, glu_b.reshape(1, half), w_out, g_next.reshape(1, d))


def _proj_odd_body(hn_ref, w_ref, wg_ref, wa_ref, ba_ref, q_ref, k_ref, v_ref, z_ref, bc_ref):
    hb = hn_ref[...]
    rb = _dot(hb, wg_ref[...]).astype(BF16)
    tril = _tri_ones(G_CHUNK, True)
    plain = []
    col0 = 0
    for o_r in (q_ref, k_ref, v_ref, z_ref):
        plain += [(o_r, col0 + c, c) for c in range(0, o_r.shape[1], PROJ_TN)]
        col0 += o_r.shape[1]
    width = bc_ref.shape[1]
    for j in range(0, width, PROJ_TN):
        for k in range(len(plain) * j // width, len(plain) * (j + PROJ_TN) // width):
            dst, c_in, c_out = plain[k]
            dst[:, c_out:c_out + PROJ_TN] = _dot(hb, w_ref[:, c_in:c_in + PROJ_TN]).astype(dst.dtype)
        cols = slice(j, j + PROJ_TN)
        pre = _dot(rb, wa_ref[:, cols].astype(BF16)) + ba_ref[:, cols]
        la = (_log_sigmoid(pre) * (1.0 / G_TAU)).astype(BF16)
        for c in range(0, TOKEN_TILE, G_CHUNK):
            rows = slice(c, c + G_CHUNK)
            bc_ref[rows, cols] = _dot(tril, la[rows])


def _layer1_body(x_ref, hn_ref, w_ref, wg_ref, wa_ref, ba_ref, hg_ref, wout_ref, gf_ref, out_ref,
                 q_s, k_s, v_s, z_s, bc_s, y_s, s_ref, *, tiles_per_seq):
    L = G_CHUNK

    @pl.when(lax.rem(pl.program_id(0), tiles_per_seq) == 0)
    def _init():
        s_ref[...] = jnp.zeros_like(s_ref)

    _proj_odd_body(hn_ref, w_ref, wg_ref, wa_ref, ba_ref, q_s, k_s, v_s, z_s, bc_s)

    row = lax.broadcasted_iota(jnp.int32, (L, L), 0)
    col = lax.broadcasted_iota(jnp.int32, (L, L), 1)
    causal = row >= col
    mid = L // 2 - 1

    for c, h in [(c, h) for c in range(0, TOKEN_TILE, 2 * L) for h in range(G_HEADS)]:
        ks = slice(h * G_DK, (h + 1) * G_DK)
        vs = slice(h * G_DV, (h + 1) * G_DV)
        r1, r2 = slice(c, c + L), slice(c + L, c + 2 * L)
        s_prev = s_ref[h]
        s_bf = s_prev.astype(BF16)
        v_p = v_s[c:c + 2 * L, vs]

        def prep(r):
            b = bc_s[r, ks]
            bm = b[mid:mid + 1, :]
            g = b[L - 1:L, :]
            e1 = jnp.exp(b - bm)
            qt = q_s[r, ks].astype(F32) * e1
            kt = k_s[r, ks].astype(F32) * (1.0 / e1)
            attn = jnp.where(causal, _dot_nt(qt.astype(BF16), kt.astype(BF16)), 0.0).astype(BF16)
            return attn, qt * jnp.exp(bm), kt * jnp.exp(g - bm), jnp.exp(g)

        a11, qi1, ke1, eg1 = prep(r1)
        a22, qi2, ke2, eg2 = prep(r2)
        qi2b = qi2.astype(BF16)
        a21 = _dot_nt(qi2b, ke1.astype(BF16)).astype(BF16)
        o1 = _dot(a11, v_p[0:L]) + _dot(qi1.astype(BF16), s_bf)
        o2 = _dot(jnp.concatenate([a21, a22], axis=1), v_p) + _dot((qi2 * eg1).astype(BF16), s_bf)
        y_s[r1, vs] = (_rmsnorm(o1, hg_ref[:, vs]) * _silu(z_s[r1, vs].astype(F32))).astype(BF16)
        y_s[r2, vs] = (_rmsnorm(o2, hg_ref[:, vs]) * _silu(z_s[r2, vs].astype(F32))).astype(BF16)
        ke_t = jnp.concatenate([(ke1 * eg2).T, ke2.T], axis=1).astype(BF16)
        g_col = jnp.broadcast_to(eg1 * eg2, (LANES, G_DK)).T[:, 0:1]
        s_ref[h] = g_col * s_prev + _dot(ke_t, v_p)

    x = x_ref[...] + _dot(y_s[...], wout_ref[...])
    out_ref[...] = _rmsnorm(x, gf_ref[...])


def _layer1(x1, hn, wm, wg, w_alpha, b_alpha, head_g, w_out, gf, seq):
    t, d = x1.shape
    dk = G_HEADS * G_DK
    assert seq % TOKEN_TILE == 0 and TOKEN_TILE % G_CHUNK == 0
    tok = lambda i: (i, 0)
    const = lambda i: (0, 0)
    resident = lambda shape: pl.BlockSpec(shape, const, pipeline_mode=pl.Buffered(1))
    slots = lambda n, dt: pltpu.VMEM((TOKEN_TILE, n), dt)
    return pl.pallas_call(
        functools.partial(_layer1_body, tiles_per_seq=seq // TOKEN_TILE),
        grid=(t // TOKEN_TILE,),
        in_specs=[
            pl.BlockSpec((TOKEN_TILE, d), tok),
            pl.BlockSpec((TOKEN_TILE, d), tok),
            resident((d, wm.shape[1])),
            pl.BlockSpec((d, G_RANK_PAD), const),
            pl.BlockSpec((G_RANK_PAD, dk), const),
            pl.BlockSpec((1, dk), const),
            pl.BlockSpec((1, D_MIX), const),
            resident((D_MIX, d)),
            pl.BlockSpec((1, d), const),
        ],
        out_specs=pl.BlockSpec((TOKEN_TILE, d), tok),
        out_shape=jax.ShapeDtypeStruct((t, d), F32),
        scratch_shapes=[
            slots(dk, BF16),
            slots(dk, BF16),
            slots(D_MIX, BF16),
            slots(D_MIX, BF16),
            slots(dk, F32),
            slots(D_MIX, BF16),
            pltpu.VMEM((G_HEADS, G_DK, G_DV), F32),
        ],
        compiler_params=_cparams(1),
        name="layer1",
    )(x1, hn, wm, wg, w_alpha, b_alpha.reshape(1, dk), head_g.reshape(1, D_MIX), w_out, gf.reshape(1, d))


def kernel(x, norm_g, final_norm_g, ev_w_in, ev_conv_w, ev_conv_b, ev_i_bias, ev_f_bias, ev_head_g,
           s5_lam_re, s5_lam_im, s5_log_dt, s5_b_re, s5_b_im, s5_c_re, s5_c_im, s5_d, s5_glu_w,
           s5_glu_b, ev_w_out, od_w_in, gla_w_alpha, gla_b_alpha, gla_head_g, od_w_out):
    batch, seq, d = x.shape
    t = batch * seq
    xf = x.reshape(t, d)
    padc = lambda a: jnp.pad(a, ((0, 0), (0, LANES - a.shape[1])))

    wt = jnp.swapaxes(ev_w_in, 1, 2).reshape(ev_w_in.shape[2], d)
    g0 = 2 * M_HEADS * M_DK + 2 * M_HEADS * M_DV
    gi = g0 + M_HEADS
    gf = gi + M_HEADS
    half = D_MIX // 2
    w_qkvo, w_uz, wg = _cast_weights(wt, (0, gf), (g0, half + D_MIX), ((g0, M_HEADS), (gi, M_HEADS)))
    qk, v, o, u, z, gates = _proj_even(xf, norm_g[0], w_qkvo, w_uz, wg, ev_conv_w[0], ev_conv_b[0], seq)
    gbias = jnp.concatenate([padc(ev_i_bias), padc(ev_f_bias)], axis=1)
    hm = _mlstm(qk, v, gates, gbias, batch, seq)

    y3 = _s5(u.reshape(batch, seq, half), s5_lam_re[0], s5_lam_im[0], s5_log_dt[0], s5_b_re[0],
             s5_b_im[0], s5_c_re[0], s5_c_im[0], s5_d[0])
    yb = y3.reshape(t, half)
    x1, hn1 = _even_out(yb, hm, o, z, xf, ev_head_g[0], s5_glu_w[0].astype(BF16), s5_glu_b[0],
                        ev_w_out[0].astype(BF16), norm_g[1])

    wt = jnp.swapaxes(od_w_in, 1, 2).reshape(od_w_in.shape[2], d)
    n_main = 2 * G_HEADS * G_DK + 2 * D_MIX
    wa = jnp.pad(gla_w_alpha[0], ((0, G_RANK_PAD - gla_w_alpha.shape[1]), (0, 0)))
    assert _log2(G_DK) % 2 == 0
    wm, wr = _cast_weights(wt, (0,), (n_main,), ((n_main, wt.shape[0] - n_main),),
                           scaled_cols=G_HEADS * G_DK, scale=G_DK ** -0.5)
    out = _layer1(x1, hn1, wm, wr, wa, gla_b_alpha[0],
                  gla_head_g[0], od_w_out[0].astype(BF16), final_norm_g, seq)
    return out.reshape(batch, seq, d)
```
